```python
import math
import jax, jax.numpy as jnp
from jax import lax
import numpy as np

D_MODEL = 1024
BATCH = 2
SEQ = 8192
DEPTH = 2

M_MLSTM = 512
MLSTM_HEADS = 4
MLSTM_HEAD_DIM = M_MLSTM // MLSTM_HEADS
CHUNK = 128
CONV_WIDTH = 5
FORGET_BIAS_LO = 3.0
FORGET_BIAS_HI = 6.0
M_FOURIER = 256
FOURIER_GROUPS = 4
FOURIER_GROUP_DIM = M_FOURIER // FOURIER_GROUPS
M_S5 = 256
S5_GROUP = 16
S5_GROUPS = M_S5 // S5_GROUP
S5_STATE = 64
DT_MIN = 1e-3
DT_MAX = 1e-1
N_BRANCHES = 3
D_FF = 4 * D_MODEL
EPS = 1e-6

OFF_Q = 0
OFF_K = OFF_Q + M_MLSTM
OFF_V = OFF_K + M_MLSTM
OFF_O = OFF_V + M_MLSTM
OFF_IG = OFF_O + M_MLSTM
OFF_FG = OFF_IG + 2 * MLSTM_HEADS
OFF_FOURIER = OFF_FG + 2 * MLSTM_HEADS
OFF_S5 = OFF_FOURIER + M_FOURIER
OFF_GATE = OFF_S5 + M_S5
N_IN = OFF_GATE + N_BRANCHES * D_MODEL

kernel_name = 'hybrid_mlstm_fnet_s5_encoder'

F32 = jnp.float32


def rmsnorm(x, g):
    xf = x.astype(F32)
    y = xf * lax.rsqrt(jnp.mean(xf * xf, axis=-1, keepdims=True) + EPS)
    return y * g.astype(F32)


def centred_dwconv(x, w, b):
    xf = x.astype(F32)
    y = lax.conv_general_dilated(xf, w.astype(F32)[:, None, :], window_strides=(1,), padding='SAME',
                                 dimension_numbers=('NWC', 'WIO', 'NWC'), feature_group_count=xf.shape[-1])
    return y + b.astype(F32)


def mlstm_direction(q, k, v, ig, fg):
    bsz, s, nh, dh = q.shape
    nc = s // CHUNK

    def chunk(t):
        return t.reshape(bsz, nc, CHUNK, nh, -1).transpose(0, 3, 1, 2, 4)

    def chunk_g(t):
        return t.reshape(bsz, nc, CHUNK, nh).transpose(0, 3, 1, 2)

    q = chunk(q) * (dh ** -0.5)
    k = chunk(k)
    v = chunk(v)
    ig = chunk_g(ig)
    logf = jax.nn.log_sigmoid(chunk_g(fg))
    b = jnp.cumsum(logf, axis=-1)
    g = b[..., -1]
    a = g[..., None] - b + ig
    a_max = jnp.max(a, axis=-1)
    w = jnp.exp(a - a_max[..., None])
    c_loc = jnp.einsum('bhcl,bhcle,bhclk->bhcek', w, v, k)
    n_loc = jnp.einsum('bhcl,bhclk->bhck', w, k)

    def step(carry, inp):
        c_st, n_st, m_st = carry
        cl, nl, am, gc = inp
        m_new = jnp.maximum(gc + m_st, am)
        s_old = jnp.exp(gc + m_st - m_new)
        s_new = jnp.exp(am - m_new)
        c_new = s_old[..., None, None] * c_st + s_new[..., None, None] * cl
        n_new = s_old[..., None] * n_st + s_new[..., None] * nl
        return (c_new, n_new, m_new), (c_st, n_st, m_st)

    init = (jnp.zeros((bsz, nh, dh, dh), F32), jnp.zeros((bsz, nh, dh), F32), jnp.zeros((bsz, nh), F32))
    xs = (jnp.moveaxis(c_loc, 2, 0), jnp.moveaxis(n_loc, 2, 0), jnp.moveaxis(a_max, 2, 0), jnp.moveaxis(g, 2, 0))
    _, (c_prev, n_prev, m_prev) = lax.scan(step, init, xs)
    c_prev = jnp.moveaxis(c_prev, 0, 2)
    n_prev = jnp.moveaxis(n_prev, 0, 2)
    m_prev = jnp.moveaxis(m_prev, 0, 2)

    mask = jnp.tril(jnp.ones((CHUNK, CHUNK), dtype=bool))
    d_log = jnp.where(mask, b[..., :, None] - b[..., None, :] + ig[..., None, :], -jnp.inf)
    inter_log = b + m_prev[..., None]
    m_t = jnp.maximum(inter_log, jnp.max(d_log, axis=-1))
    scores = jnp.einsum('bhcld,bhcjd->bhclj', q, k) * jnp.exp(d_log - m_t[..., None])
    inter_w = jnp.exp(inter_log - m_t)
    num = jnp.einsum('bhclj,bhcjd->bhcld', scores, v) + inter_w[..., None] * jnp.einsum('bhcek,bhclk->bhcle', c_prev, q)
    den = jnp.sum(scores, axis=-1) + inter_w * jnp.einsum('bhclk,bhck->bhcl', q, n_prev)
    h = num / jnp.maximum(jnp.abs(den), jnp.exp(-m_t))[..., None]
    return h.transpose(0, 2, 3, 1, 4).reshape(bsz, s, nh, dh)


def mlstm_branch(z_qk, z_v, z_o, z_ig, z_fg, conv_w, conv_b, norm_g):
    bsz, s, _ = z_v.shape
    nh, dh = MLSTM_HEADS, MLSTM_HEAD_DIM
    qk = jax.nn.silu(centred_dwconv(z_qk, conv_w, conv_b))
    q = qk[..., :M_MLSTM].reshape(bsz, s, nh, dh)
    k = qk[..., M_MLSTM:].reshape(bsz, s, nh, dh)
    v = z_v.astype(F32).reshape(bsz, s, nh, dh)
    ig = z_ig.astype(F32)
    fg = z_fg.astype(F32)
    flip = lambda t: jnp.flip(t, axis=1)
    h_fwd = mlstm_direction(q, k, v, ig[..., :nh], fg[..., :nh])
    h_bwd = flip(mlstm_direction(flip(q), flip(k), flip(v), flip(ig[..., nh:]), flip(fg[..., nh:])))
    h = h_fwd + h_bwd
    mu = jnp.mean(h, axis=-1, keepdims=True)
    var = jnp.mean(jnp.square(h - mu), axis=-1, keepdims=True)
    h = ((h - mu) * lax.rsqrt(var + EPS)).reshape(bsz, s, M_MLSTM) * norm_g.astype(F32)
    return h * jax.nn.sigmoid(z_o.astype(F32))


def fourier_branch(z_f):
    bsz, s, _ = z_f.shape
    u = z_f.astype(F32).reshape(bsz, s, FOURIER_GROUPS, FOURIER_GROUP_DIM)
    y = jnp.real(jnp.fft.fft2(u, axes=(1, 3), norm='ortho'))
    return y.reshape(bsz, s, M_FOURIER)


def s5_direction(u, lam_re, lam_im, log_dt, b_re, b_im, c_re, c_im, reverse):
    lam_re = lam_re.astype(F32)
    lam_im = lam_im.astype(F32)
    dt = jnp.exp(log_dt.astype(F32))[:, None]
    mag = jnp.exp(lam_re * dt)
    lb_re = mag * jnp.cos(lam_im * dt)
    lb_im = mag * jnp.sin(lam_im * dt)
    den = lam_re * lam_re + lam_im * lam_im
    f_re = ((lb_re - 1.0) * lam_re + lb_im * lam_im) / den
    f_im = (lb_im * lam_re - (lb_re - 1.0) * lam_im) / den
    b_re = b_re.astype(F32)
    b_im = b_im.astype(F32)
    bb_re = f_re[..., None] * b_re - f_im[..., None] * b_im
    bb_im = f_re[..., None] * b_im + f_im[..., None] * b_re
    bu_re = jnp.einsum('gpc,bsgc->bsgp', bb_re, u)
    bu_im = jnp.einsum('gpc,bsgc->bsgp', bb_im, u)
    a_re = jnp.broadcast_to(lb_re, bu_re.shape)
    a_im = jnp.broadcast_to(lb_im, bu_im.shape)

    def combine(e1, e2):
        a1r, a1i, x1r, x1i = e1
        a2r, a2i, x2r, x2i = e2
        return (a1r * a2r - a1i * a2i, a1r * a2i + a1i * a2r,
                a2r * x1r - a2i * x1i + x2r, a2r * x1i + a2i * x1r + x2i)

    _, _, x_re, x_im = lax.associative_scan(combine, (a_re, a_im, bu_re, bu_im), axis=1, reverse=reverse)
    return jnp.einsum('gcp,bsgp->bsgc', c_re.astype(F32), x_re) - jnp.einsum('gcp,bsgp->bsgc', c_im.astype(F32), x_im)


def s5_branch(z_s, lam_re, lam_im, log_dt, b_re, b_im, c_re, c_im, d_skip, w_glu, b_glu):
    bsz, s, _ = z_s.shape
    u = z_s.astype(F32).reshape(bsz, s, S5_GROUPS, S5_GROUP)
    y = (d_skip.astype(F32) * u
         + s5_direction(u, lam_re[0], lam_im[0], log_dt[0], b_re[0], b_im[0], c_re[0], c_im[0], False)
         + s5_direction(u, lam_re[1], lam_im[1], log_dt[1], b_re[1], b_im[1], c_re[1], c_im[1], True))
    y = jax.nn.gelu(y).reshape(bsz, s, M_S5)
    zz = y @ w_glu.astype(F32) + b_glu.astype(F32)
    return zz[..., :D_MODEL] * jax.nn.sigmoid(zz[..., D_MODEL:])


def setup_inputs(seed: int = 0) -> dict:
    key = jax.random.key(seed)
    ks = jax.random.split(key, 32)

    def nrm(k, shape, scale):
        return jax.random.normal(k, shape, F32) * scale

    fg_bias = jnp.tile(jnp.linspace(FORGET_BIAS_LO, FORGET_BIAS_HI, MLSTM_HEADS, dtype=F32), 2)[None, :]
    b_in = jnp.concatenate([
        nrm(ks[6], (DEPTH, OFF_FG), 0.02),
        fg_bias + nrm(ks[7], (DEPTH, 2 * MLSTM_HEADS), 0.1),
        nrm(ks[8], (DEPTH, N_IN - OFF_FOURIER), 0.02)], axis=-1)
    lam_im0 = jnp.pi * jnp.arange(S5_STATE, dtype=F32)
    return {
        'x': jax.random.normal(ks[0], (BATCH, SEQ, D_MODEL), F32),
        'g_mix_pre': 1.0 + nrm(ks[1], (DEPTH, D_MODEL), 0.02),
        'g_mix_post': 1.0 + nrm(ks[2], (DEPTH, D_MODEL), 0.02),
        'g_ffn_pre': 1.0 + nrm(ks[3], (DEPTH, D_MODEL), 0.02),
        'g_ffn_post': 1.0 + nrm(ks[4], (DEPTH, D_MODEL), 0.02),
        'w_in': nrm(ks[5], (DEPTH, D_MODEL, N_IN), D_MODEL ** -0.5),
        'b_in': b_in,
        'conv_w': nrm(ks[9], (DEPTH, CONV_WIDTH, 2 * M_MLSTM), CONV_WIDTH ** -0.5),
        'conv_b': nrm(ks[10], (DEPTH, 2 * M_MLSTM), 0.02),
        'mlstm_norm_g': 1.0 + nrm(ks[11], (DEPTH, M_MLSTM), 0.02),
        'w_up_mlstm': nrm(ks[12], (DEPTH, M_MLSTM, D_MODEL), M_MLSTM ** -0.5),
        'w_up_fourier': nrm(ks[13], (DEPTH, M_FOURIER, D_MODEL), M_FOURIER ** -0.5),
        's5_lam_re': -0.5 + nrm(ks[14], (DEPTH, 2, S5_GROUPS, S5_STATE), 0.01),
        's5_lam_im': lam_im0 + nrm(ks[15], (DEPTH, 2, S5_GROUPS, S5_STATE), 0.01),
        's5_log_dt': jax.random.uniform(ks[16], (DEPTH, 2, S5_GROUPS), F32, math.log(DT_MIN), math.log(DT_MAX)),
        's5_b_re': nrm(ks[17], (DEPTH, 2, S5_GROUPS, S5_STATE, S5_GROUP), (2 * S5_GROUP) ** -0.5),
        's5_b_im': nrm(ks[18], (DEPTH, 2, S5_GROUPS, S5_STATE, S5_GROUP), (2 * S5_GROUP) ** -0.5),
        's5_c_re': nrm(ks[19], (DEPTH, 2, S5_GROUPS, S5_GROUP, S5_STATE), S5_STATE ** -0.5),
        's5_c_im': nrm(ks[20], (DEPTH, 2, S5_GROUPS, S5_GROUP, S5_STATE), S5_STATE ** -0.5),
        's5_d': nrm(ks[21], (DEPTH, S5_GROUPS, S5_GROUP), 1.0),
        'w_glu': nrm(ks[22], (DEPTH, M_S5, 2 * D_MODEL), M_S5 ** -0.5),
        'b_glu': nrm(ks[23], (DEPTH, 2 * D_MODEL), 0.02),
        'w_out': nrm(ks[24], (DEPTH, D_MODEL, D_MODEL), D_MODEL ** -0.5),
        'w_ffn1': nrm(ks[25], (DEPTH, D_MODEL, D_FF), D_MODEL ** -0.5),
        'w_ffn2': nrm(ks[26], (DEPTH, D_FF, D_MODEL), D_FF ** -0.5),
    }


def reference(x, g_mix_pre, g_mix_post, g_ffn_pre, g_ffn_post, w_in, b_in, conv_w, conv_b, mlstm_norm_g,
              w_up_mlstm, w_up_fourier, s5_lam_re, s5_lam_im, s5_log_dt, s5_b_re, s5_b_im, s5_c_re, s5_c_im,
              s5_d, w_glu, b_glu, w_out, w_ffn1, w_ffn2):
    bsz, s, _ = x.shape
    for l in range(DEPTH):
        h = rmsnorm(x, g_mix_pre[l])
        z = h @ w_in[l].astype(F32) + b_in[l].astype(F32)
        y_m = mlstm_branch(z[..., OFF_Q:OFF_V], z[..., OFF_V:OFF_O], z[..., OFF_O:OFF_IG],
                           z[..., OFF_IG:OFF_FG], z[..., OFF_FG:OFF_FOURIER],
                           conv_w[l], conv_b[l], mlstm_norm_g[l]) @ w_up_mlstm[l].astype(F32)
        y_f = fourier_branch(z[..., OFF_FOURIER:OFF_S5]) @ w_up_fourier[l].astype(F32)
        y_s = s5_branch(z[..., OFF_S5:OFF_GATE], s5_lam_re[l], s5_lam_im[l], s5_log_dt[l], s5_b_re[l], s5_b_im[l],
                        s5_c_re[l], s5_c_im[l], s5_d[l], w_glu[l], b_glu[l])
        gates = jax.nn.sigmoid(z[..., OFF_GATE:]).reshape(bsz, s, N_BRANCHES, D_MODEL)
        mixed = gates[..., 0, :] * y_m + gates[..., 1, :] * y_f + gates[..., 2, :] * y_s
        x = x + rmsnorm(mixed @ w_out[l].astype(F32), g_mix_post[l]).astype(x.dtype)
        h = rmsnorm(x, g_ffn_pre[l])
        f = jnp.square(jax.nn.relu(h @ w_ffn1[l].astype(F32))) @ w_ffn2[l].astype(F32)
        x = x + rmsnorm(f, g_ffn_post[l]).astype(x.dtype)
    return x
```

```python
import functools
import math

import numpy as np
import jax
import jax.numpy as jnp
from jax import lax
from jax.experimental import pallas as pl
from jax.experimental.pallas import tpu as pltpu

F32 = jnp.float32
BF16 = jnp.bfloat16
HIGHEST = lax.Precision.HIGHEST

LANES = 128
D_MODEL = 1024
M_MLSTM = 512
HEADS = 4
HEAD_DIM = 128
CHUNK = 128
CONV_WIDTH = 5
CONV_HALO = 8
M_FOURIER = 256
FOURIER_GROUP_DIM = 64
M_S5 = 256
S5_GROUP = 16
S5_GROUPS = 16
S5_STATE = 64
S5_CHUNK = 32
S5_TILE = 4096
N_BRANCHES = 3
D_FF = 4 * D_MODEL
EPS = 1e-6

OFF_Q = 0
OFF_V = 2 * M_MLSTM
OFF_O = 3 * M_MLSTM
OFF_IG = 4 * M_MLSTM
OFF_FOURIER = OFF_IG + 4 * HEADS
OFF_S5 = OFF_FOURIER + M_FOURIER
OFF_GATE = OFF_S5 + M_S5

DFT_N2 = 64
DFT_STEP = 8

VMEM_LIMIT = 56 * 1024 * 1024

NT_DIMS = (((1,), (1,)), ((), ()))
TN_DIMS = (((0,), (0,)), ((), ()))


def _params(*sem):
    return pltpu.CompilerParams(dimension_semantics=sem, vmem_limit_bytes=VMEM_LIMIT)


def _rms(x, g):
    return x * lax.rsqrt(jnp.mean(x * x, axis=-1, keepdims=True) + EPS) * g


def _sigmoid(x):
    return 1.0 / (1.0 + jnp.exp(-x))


def _dot(a, b):
    return jnp.dot(a, b, preferred_element_type=F32)


def _dot_hi(a, b):
    return jnp.dot(a, b, precision=HIGHEST, preferred_element_type=F32)


def _inproj_kernel(x_ref, g_ref, wm_ref, bm_ref, wg_ref, bg_ref, wgt_ref, bgt_ref,
                   zqk_ref, zv_ref, zo_ref, zf_ref, zs_ref, zg_ref, zgt_ref):
    h = _rms(x_ref[...], g_ref[...]).astype(BF16)

    def proj(lo, hi):
        return _dot(h, wm_ref[:, lo:hi]) + bm_ref[:, lo:hi]

    zqk_ref[...] = proj(OFF_Q, OFF_V).astype(BF16)
    zv_ref[...] = proj(OFF_V, OFF_O).astype(BF16)
    zo_ref[...] = proj(OFF_O, OFF_IG).astype(BF16)
    zf_ref[...] = proj(OFF_IG, OFF_IG + M_FOURIER)
    zs_ref[...] = proj(OFF_IG + M_FOURIER, OFF_IG + M_FOURIER + M_S5)
    zg_ref[...] = _dot(h, wg_ref[...]) + bg_ref[...]
    zgt_ref[...] = lax.dot_general(wgt_ref[...], h, NT_DIMS, preferred_element_type=F32) + bgt_ref[...]


def _inproj(x2, g, w_in, b_in, tile):
    t = x2.shape[0]
    n_gate = 4 * HEADS
    wm = jnp.concatenate([w_in[:, :OFF_IG], w_in[:, OFF_FOURIER:OFF_GATE]], axis=1).astype(BF16)
    bm = jnp.concatenate([b_in[:OFF_IG], b_in[OFF_FOURIER:OFF_GATE]])[None, :]
    wg = w_in[:, OFF_IG:OFF_FOURIER]
    bg = b_in[OFF_IG:OFF_FOURIER]
    n_main = wm.shape[1]
    const = lambda shape: pl.BlockSpec(shape, lambda i: (0, 0))
    rows = lambda width: pl.BlockSpec((tile, width), lambda i: (i, 0))
    cols = lambda height: pl.BlockSpec((height, tile), lambda i: (0, i))
    return pl.pallas_call(
        _inproj_kernel,
        grid=(t // tile,),
        in_specs=[rows(D_MODEL), const((1, D_MODEL)), const((D_MODEL, n_main)), const((1, n_main)),
                  const((D_MODEL, n_gate)), const((1, n_gate)), const((n_gate, D_MODEL)), const((n_gate, 1))],
        out_specs=[rows(2 * M_MLSTM), rows(M_MLSTM), rows(M_MLSTM), rows(M_FOURIER), rows(M_S5),
                   rows(n_gate), cols(n_gate)],
        out_shape=[jax.ShapeDtypeStruct((t, 2 * M_MLSTM), BF16), jax.ShapeDtypeStruct((t, M_MLSTM), BF16),
                   jax.ShapeDtypeStruct((t, M_MLSTM), BF16), jax.ShapeDtypeStruct((t, M_FOURIER), F32),
                   jax.ShapeDtypeStruct((t, M_S5), F32), jax.ShapeDtypeStruct((t, n_gate), F32),
                   jax.ShapeDtypeStruct((n_gate, t), F32)],
        compiler_params=_params("parallel"),
        name="inproj",
    )(x2, g[None, :], wm, bm, wg.astype(BF16), bg[None, :], wg.T.astype(BF16), bg[:, None])


def _log_sigmoid(x):
    return jnp.minimum(x, 0.0) - jnp.log1p(jnp.exp(-jnp.abs(x)))


def _mlstm_kernel(qk_f, qkp_f, qkn_f, v_f, g_f, gt_f, qk_b, qkp_b, qkn_b, v_b, g_b, gt_b, cw_ref, cb_ref,
                  hf_ref, hb_ref, state_sc, m_sc, ext_sc):
    c = pl.program_id(1)
    nc = pl.num_programs(1)
    L = CHUNK
    pad = CONV_WIDTH // 2

    @pl.when(c == 0)
    def _():
        state_sc[...] = jnp.zeros_like(state_sc)
        m_sc[...] = jnp.zeros_like(m_sc)

    row = lax.broadcasted_iota(jnp.int32, (L, L), 0)
    col = lax.broadcasted_iota(jnp.int32, (L, L), 1)
    lower = row >= col
    upper = row <= col
    lower_f = lower.astype(F32)
    upper_f = upper.astype(F32)
    ones_v = jnp.ones((L, HEAD_DIM), F32)

    streams = ((0, c, qk_f, qkp_f, qkn_f, v_f, g_f, gt_f, hf_ref),
               (1, nc - 1 - c, qk_b, qkp_b, qkn_b, v_b, g_b, gt_b, hb_ref))
    for d, chunk_idx, qk_ref, qkp_ref, qkn_ref, v_ref, g_ref, gt_ref, out_ref in streams:
        prev = jnp.where(chunk_idx == 0, 0.0, qkp_ref[...].astype(F32))
        nxt = jnp.where(chunk_idx == nc - 1, 0.0, qkn_ref[...].astype(F32))
        ext_sc[0:CONV_HALO, :] = prev
        ext_sc[CONV_HALO:CONV_HALO + L, :] = qk_ref[...].astype(F32)
        ext_sc[CONV_HALO + L:, :] = nxt
        acc = cb_ref[...] + cw_ref[0:1, :] * ext_sc[CONV_HALO - pad:CONV_HALO - pad + L, :]
        for j in range(1, CONV_WIDTH):
            acc = acc + cw_ref[j:j + 1, :] * ext_sc[CONV_HALO - pad + j:CONV_HALO - pad + j + L, :]
        qk = acc * _sigmoid(acc)

        gates = g_ref[...]
        gates_t = gt_ref[...]
        lf_cols = _log_sigmoid(gates[:, 2 * HEADS:])
        lf_rows = _log_sigmoid(gates_t[2 * HEADS:, :])
        if d == 0:
            b_cols = _dot_hi(lower_f, lf_cols)
            b_rows = _dot_hi(lf_rows, upper_f)
            mask = lower
        else:
            b_cols = _dot_hi(upper_f, lf_cols)
            b_rows = _dot_hi(lf_rows, lower_f)
            mask = upper

        for hd in range(HEADS):
            k_idx = d * HEADS + hd
            lo = hd * HEAD_DIM
            q = qk[:, lo:lo + HEAD_DIM] * (HEAD_DIM ** -0.5)
            k = qk[:, M_MLSTM + lo:M_MLSTM + lo + HEAD_DIM].astype(BF16)
            v = v_ref[:, lo:lo + HEAD_DIM]
            b_c = jnp.broadcast_to(b_cols[:, k_idx:k_idx + 1], (L, L))
            ig_c = jnp.broadcast_to(gates[:, k_idx:k_idx + 1], (L, L))
            b_r = b_rows[k_idx:k_idx + 1, :]
            ig_r = gates_t[k_idx:k_idx + 1, :]
            g_tot = b_c[L - 1:L, :] if d == 0 else b_c[0:1, :]
            m_prev = m_sc[k_idx:k_idx + 1, :]

            d_log = jnp.where(mask, b_c - b_r + ig_r, -1e30)
            inter_log = b_c + m_prev
            m_t = jnp.maximum(inter_log, jnp.max(d_log, axis=1, keepdims=True))
            scores = lax.dot_general(q.astype(BF16), k, NT_DIMS, preferred_element_type=F32) * jnp.exp(d_log - m_t)
            inter_w = jnp.exp(inter_log - m_t)
            st = state_sc[k_idx]
            v_ext = jnp.concatenate([v, ones_v.astype(BF16)], axis=1)
            res = _dot(scores.astype(BF16), v_ext) + _dot((inter_w * q).astype(BF16), st.astype(BF16))
            num = res[:, :HEAD_DIM]
            den = res[:, HEAD_DIM:]
            out_ref[:, lo:lo + HEAD_DIM] = num / jnp.maximum(jnp.abs(den), jnp.exp(-m_t))

            a_c = g_tot - b_c + ig_c
            a_max = jnp.max(a_c, axis=0, keepdims=True)
            w_c = jnp.exp(a_c - a_max)
            vw = jnp.concatenate([v.astype(F32) * w_c, w_c], axis=1).astype(BF16)
            st_loc = lax.dot_general(k, vw, TN_DIMS, preferred_element_type=F32)
            m_new = jnp.maximum(g_tot + m_prev, a_max)
            s_old = jnp.exp(g_tot + m_prev - m_new)
            s_new = jnp.exp(a_max - m_new)
            s_old2 = jnp.concatenate([s_old, s_old], axis=1)
            s_new2 = jnp.concatenate([s_new, s_new], axis=1)
            state_sc[k_idx] = s_old2 * st + s_new2 * st_loc
            m_sc[k_idx:k_idx + 1, :] = m_new


def _mlstm(zqk, zv, zg, zgt, conv_w, conv_b, bsz, seq):
    t = bsz * seq
    nc = seq // CHUNK
    hpc = CHUNK // CONV_HALO
    n_halo = t // CONV_HALO
    n_gate = 4 * HEADS

    def fwd(b, c):
        return b * nc + c

    def bwd(b, c):
        return b * nc + nc - 1 - c

    def specs(pos):
        return [
            pl.BlockSpec((CHUNK, 2 * M_MLSTM), lambda b, c: (pos(b, c), 0)),
            pl.BlockSpec((CONV_HALO, 2 * M_MLSTM), lambda b, c: (jnp.maximum(pos(b, c) * hpc - 1, 0), 0)),
            pl.BlockSpec((CONV_HALO, 2 * M_MLSTM), lambda b, c: (jnp.minimum((pos(b, c) + 1) * hpc, n_halo - 1), 0)),
            pl.BlockSpec((CHUNK, M_MLSTM), lambda b, c: (pos(b, c), 0)),
            pl.BlockSpec((CHUNK, n_gate), lambda b, c: (pos(b, c), 0)),
            pl.BlockSpec((n_gate, CHUNK), lambda b, c: (0, pos(b, c))),
        ]

    const = lambda shape: pl.BlockSpec(shape, lambda b, c: (0, 0))
    out_shape = jax.ShapeDtypeStruct((t, M_MLSTM), F32)
    return pl.pallas_call(
        _mlstm_kernel,
        grid=(bsz, nc),
        in_specs=specs(fwd) + specs(bwd) + [const((CONV_WIDTH, 2 * M_MLSTM)), const((1, 2 * M_MLSTM))],
        out_specs=[pl.BlockSpec((CHUNK, M_MLSTM), lambda b, c: (fwd(b, c), 0)),
                   pl.BlockSpec((CHUNK, M_MLSTM), lambda b, c: (bwd(b, c), 0))],
        out_shape=[out_shape, out_shape],
        scratch_shapes=[pltpu.VMEM((2 * HEADS, HEAD_DIM, 2 * HEAD_DIM), F32),
                        pltpu.VMEM((2 * HEADS, HEAD_DIM), F32),
                        pltpu.VMEM((CHUNK + 2 * CONV_HALO, 2 * M_MLSTM), F32)],
        compiler_params=_params("arbitrary", "arbitrary"),
        name="mlstm",
    )(zqk, zqk, zqk, zv, zg, zgt, zqk, zqk, zqk, zv, zg, zgt, conv_w, conv_b[None, :])


def _dft_constants(seq):
    n1, n2 = seq // DFT_N2, DFT_N2
    k1 = np.arange(n1)[:, None, None]
    s2 = np.arange(n2)[None, None, :]
    s1 = np.arange(n1)[None, :, None]
    ang = -2.0 * np.pi * ((k1 * (n2 * s1 + s2)) % seq) / seq
    stage1 = np.concatenate([np.cos(ang), np.sin(ang)], axis=0)
    stage1 = np.ascontiguousarray(stage1.transpose(2, 0, 1))
    a2 = 2.0 * np.pi * np.outer(np.arange(n2), np.arange(n2)) / n2
    c2, sn2 = np.cos(a2), np.sin(a2)
    stage2 = np.block([[c2, sn2], [-sn2, c2]])
    ag = 2.0 * np.pi * np.outer(np.arange(FOURIER_GROUP_DIM), np.arange(FOURIER_GROUP_DIM)) / FOURIER_GROUP_DIM
    scale = 1.0 / math.sqrt(seq * FOURIER_GROUP_DIM)
    eye = np.eye(LANES // FOURIER_GROUP_DIM)
    group = np.stack([np.kron(eye, np.cos(ag)), np.kron(eye, np.sin(ag))]) * scale
    as_bf16 = lambda a: jnp.asarray(a, F32).astype(BF16)
    return as_bf16(stage1), as_bf16(stage2), as_bf16(group)


def _fourier_a_kernel(u_ref, w_ref, br_ref, bi_ref, *, n1):
    for j in range(DFT_STEP):
        res = _dot(w_ref[j], u_ref[:, j, :].astype(BF16))
        br_ref[:, j, :] = res[:n1]
        bi_ref[:, j, :] = res[n1:]


def _fourier_c_kernel(br_ref, bi_ref, w2_ref, wg_ref, y_ref):
    n2 = DFT_N2
    stacked = jnp.concatenate(
        [jnp.concatenate([br_ref[j * n2:(j + 1) * n2, :], bi_ref[j * n2:(j + 1) * n2, :]], axis=0)
         for j in range(DFT_STEP)], axis=1).astype(BF16)
    z = _dot(w2_ref[...], stacked).astype(BF16)
    zr = jnp.concatenate([z[:n2, j * LANES:(j + 1) * LANES] for j in range(DFT_STEP)], axis=0)
    zi = jnp.concatenate([z[n2:, j * LANES:(j + 1) * LANES] for j in range(DFT_STEP)], axis=0)
    y = _dot(zr, wg_ref[0]) + _dot(zi, wg_ref[1])
    for j in range(DFT_STEP):
        y_ref[:, j, :] = y[j * n2:(j + 1) * n2, :]


def _fourier(zf, bsz, seq):
    n1, n2 = seq // DFT_N2, DFT_N2
    stage1, stage2, group = _dft_constants(seq)
    halves = M_FOURIER // LANES
    blk_a = pl.BlockSpec((None, n1, DFT_STEP, LANES), lambda b, j, h: (b, 0, j, h))
    b_shape = jax.ShapeDtypeStruct((bsz, n1, n2, M_FOURIER), F32)
    br, bi = pl.pallas_call(
        functools.partial(_fourier_a_kernel, n1=n1),
        grid=(bsz, n2 // DFT_STEP, halves),
        in_specs=[blk_a, pl.BlockSpec((DFT_STEP, 2 * n1, n1), lambda b, j, h: (j, 0, 0))],
        out_specs=[blk_a, blk_a],
        out_shape=[b_shape, b_shape],
        compiler_params=_params("parallel", "parallel", "parallel"),
        name="fourier_stage1",
    )(zf.reshape(bsz, n1, n2, M_FOURIER), stage1)
    blk_c = pl.BlockSpec((None, DFT_STEP * n2, LANES), lambda b, j, h: (b, j, h))
    y = pl.pallas_call(
        _fourier_c_kernel,
        grid=(bsz, n1 // DFT_STEP, halves),
        in_specs=[blk_c, blk_c, pl.BlockSpec((2 * n2, 2 * n2), lambda b, j, h: (0, 0)),
                  pl.BlockSpec((2, LANES, LANES), lambda b, j, h: (0, 0, 0))],
        out_specs=pl.BlockSpec((None, n2, DFT_STEP, LANES), lambda b, j, h: (b, 0, j, h)),
        out_shape=jax.ShapeDtypeStruct((bsz, n2, n1, M_FOURIER), F32),
        compiler_params=_params("parallel", "parallel", "parallel"),
        name="fourier_stage2",
    )(br.reshape(bsz, n1 * n2, M_FOURIER), bi.reshape(bsz, n1 * n2, M_FOURIER), stage2, group)
    return y.reshape(bsz * seq, M_FOURIER)


def _s5_prep_kernel(lamc_re, lamc_im, ldt_ref, lam4_re, lam4_im, ldt4_ref, b_re, b_im, c_re, c_im, c4_re, c4_im,
                    d_ref, toep_ref, bend_ref, cout_ref, laml_ref):
    L, P, G = S5_CHUNK, S5_STATE, S5_GROUP
    W = 2 * L * G
    shift = G.bit_length() - 1

    def cmul(ar, ai, br, bi):
        return ar * br - ai * bi, ar * bi + ai * br

    def power(lre, lim, dt, tau):
        mag = jnp.exp(tau * (lre * dt))
        ang = tau * (lim * dt)
        return mag * jnp.cos(ang), mag * jnp.sin(ang)

    def zoh_factor(lre, lim, dt):
        lbr, lbi = power(lre, lim, dt, 1.0)
        den = lre * lre + lim * lim
        return ((lbr - 1.0) * lre + lbi * lim) / den, (lbi * lre - (lbr - 1.0) * lim) / den

    lane_g = lax.broadcasted_iota(jnp.int32, (G, W), 1)
    row_g = lax.broadcasted_iota(jnp.int32, (G, W), 0)
    rep = ((lane_g & (G - 1)) == row_g).astype(F32)
    blk = lax.broadcasted_iota(jnp.int32, (P, W), 1) >> shift
    lag = (L - 1) - blk

    d_tiled = _dot_hi(jnp.broadcast_to(d_ref[0], (G, G)), rep)
    gen = jnp.where(((lane_g & (G - 1)) == row_g) & ((lane_g >> shift) == L - 1), d_tiled, 0.0)
    for d in range(2):
        dt = jnp.exp(ldt_ref[d, 0])
        lr, li = lamc_re[d, 0], lamc_im[d, 0]
        fr, fi = zoh_factor(lr, li, dt)
        bbr, bbi = cmul(fr, fi, b_re[d, 0], b_im[d, 0])
        btr, bti = _dot_hi(bbr, rep), _dot_hi(bbi, rep)
        if d == 0:
            expo, valid = jnp.maximum(lag, 0), lag >= 0
        else:
            expo, valid = jnp.maximum(-lag, 0), (lag <= 0) & (blk < 2 * L - 1)
        pr, pi = power(lr, li, dt, expo.astype(F32))
        mr, mi = cmul(jnp.where(valid, pr, 0.0), jnp.where(valid, pi, 0.0), btr, bti)
        gen = gen + _dot_hi(c_re[d, 0], mr) - _dot_hi(c_im[d, 0], mi)
        half = L * G
        step = lax.broadcasted_iota(jnp.int32, (P, half), 1) >> shift
        es = ((L - 1) - step) if d == 0 else step
        qr, qi = power(lr, li, dt, es.astype(F32))
        xr, xi = cmul(qr, qi, btr[:, :half], bti[:, :half])
        bend_ref[0, 2 * d] = xr.astype(BF16)
        bend_ref[0, 2 * d + 1] = xi.astype(BF16)

    for t in range(L):
        a = (L - 1 - t) * G
        toep_ref[0, t * G:(t + 1) * G, :] = gen[:, a:a + L * G].astype(BF16)

    lane4 = lax.broadcasted_iota(jnp.int32, (L, 4 * P), 1)
    step4 = lax.broadcasted_iota(jnp.int32, (L, 4 * P), 0)
    is_fwd = lane4 < 2 * P
    dt4 = jnp.exp(ldt4_ref[0])
    pr, pi = power(lam4_re[0], lam4_im[0], dt4, jnp.where(is_fwd, step4 + 1, L - step4).astype(F32))
    cr, ci = c4_re[0], c4_im[0]
    plane_bit = P.bit_length() - 1
    re_c = ((lax.broadcasted_iota(jnp.int32, (G, 4 * P), 1) >> plane_bit) & 1) == 0
    for t in range(L):
        re_part, im_part = cmul(cr, ci, pr[t:t + 1], pi[t:t + 1])
        cout_ref[0, t * G:(t + 1) * G, :] = jnp.where(re_c, re_part, -im_part).astype(BF16)
    re_1 = ((lax.broadcasted_iota(jnp.int32, (1, 4 * P), 1) >> plane_bit) & 1) == 0
    lr, li = power(lam4_re[0], lam4_im[0], dt4, float(L))
    laml_ref[0] = jnp.where(re_1, lr, li)


def _s5_prep(lam_re, lam_im, log_dt, b_re, b_im, c_re, c_im, d_skip):
    ng, P, G, L = S5_GROUPS, S5_STATE, S5_GROUP, S5_CHUNK
    tile4 = lambda a: jnp.concatenate([a[0], a[0], a[1], a[1]], axis=-1)
    ldt4 = jnp.repeat(tile4(log_dt[..., None]), P, axis=-1)[:, None, :]
    spec = lambda *tail: pl.BlockSpec((2, 1) + tail, lambda g: (0, g) + (0,) * len(tail))
    per_g = lambda *tail: pl.BlockSpec((1,) + tail, lambda g: (g,) + (0,) * len(tail))
    return pl.pallas_call(
        _s5_prep_kernel,
        grid=(ng,),
        in_specs=[spec(P, 1), spec(P, 1), spec(1, 1), per_g(1, 4 * P), per_g(1, 4 * P), per_g(1, 4 * P),
                  spec(P, G), spec(P, G), spec(G, P), spec(G, P), per_g(G, 4 * P), per_g(G, 4 * P), per_g(1, G)],
        out_specs=[per_g(L * G, L * G), per_g(4, P, L * G), per_g(L * G, 4 * P), per_g(1, 4 * P)],
        out_shape=[jax.ShapeDtypeStruct((ng, L * G, L * G), BF16),
                   jax.ShapeDtypeStruct((ng, 4, P, L * G), BF16),
                   jax.ShapeDtypeStruct((ng, L * G, 4 * P), BF16),
                   jax.ShapeDtypeStruct((ng, 1, 4 * P), F32)],
        compiler_params=_params("parallel"),
        name="s5_prep",
    )(lam_re[..., None], lam_im[..., None], log_dt[:, :, None, None],
      tile4(lam_re)[:, None, :], tile4(lam_im)[:, None, :], ldt4,
      b_re, b_im, c_re, c_im, tile4(c_re), tile4(c_im), d_skip[:, None, :])


def _s5_sums_kernel(lo_ref, hi_ref, bend_ref, ut_ref, et_ref, *, nchunk):
    L, P, G = S5_CHUNK, S5_STATE, S5_GROUP
    per_half = LANES // G
    for half, z_ref in enumerate((lo_ref, hi_ref)):
        for s in range(L):
            zt = z_ref[pl.ds(s, nchunk, stride=L), :].T.astype(BF16)
            for gl in range(per_half):
                ut_ref[half * per_half + gl, s * G:(s + 1) * G, :] = zt[gl * G:(gl + 1) * G, :]
    for g in range(S5_GROUPS):
        for plane in range(4):
            row = plane * S5_GROUPS * P + g * P
            et_ref[row:row + P, :] = _dot(bend_ref[g, plane], ut_ref[g])


def _s5_scan_kernel(et_ref, lam_ref, xt_ref, e_sc, x_sc, *, bsz, steps):
    width = S5_GROUPS * S5_STATE
    for plane in range(4):
        e_sc[plane] = et_ref[plane * width:(plane + 1) * width, :].T
    ar_f, ai_f, ar_b, ai_b = lam_ref[0], lam_ref[1], lam_ref[2], lam_ref[3]
    zero = jnp.zeros_like(ar_f)
    for b in range(bsz):
        def body(i, carry):
            xr, xi, yr, yi = carry
            rf = b * steps + i
            rb = b * steps + steps - 1 - i
            x_sc[0, pl.ds(rf, 1), :] = xr
            x_sc[1, pl.ds(rf, 1), :] = xi
            x_sc[2, pl.ds(rb, 1), :] = yr
            x_sc[3, pl.ds(rb, 1), :] = yi
            xr, xi = (ar_f * xr - ai_f * xi + e_sc[0, pl.ds(rf, 1), :],
                      ar_f * xi + ai_f * xr + e_sc[1, pl.ds(rf, 1), :])
            yr, yi = (ar_b * yr - ai_b * yi + e_sc[2, pl.ds(rb, 1), :],
                      ar_b * yi + ai_b * yr + e_sc[3, pl.ds(rb, 1), :])
            return xr, xi, yr, yi
        lax.fori_loop(0, steps, body, (zero, zero, zero, zero))
    for plane in range(4):
        xt_ref[plane * width:(plane + 1) * width, :] = x_sc[plane].T


def _gelu_tanh(x):
    return 0.5 * x * (1.0 + jnp.tanh(math.sqrt(2.0 / math.pi) * (x + 0.044715 * (x * x * x))))


def _s5_out_kernel(ut_ref, xt_ref, toep_ref, cout_ref, lo_ref, hi_ref, yt_sc, *, nchunk):
    L, P, G = S5_CHUNK, S5_STATE, S5_GROUP
    width = S5_GROUPS * P
    for g in range(S5_GROUPS):
        xg = jnp.concatenate([xt_ref[plane * width + g * P:plane * width + (g + 1) * P, :] for plane in range(4)],
                             axis=0).astype(BF16)
        yt = _gelu_tanh(_dot(toep_ref[g], ut_ref[g]) + _dot(cout_ref[g], xg))
        for t in range(L):
            yt_sc[t, g * G:(g + 1) * G, :] = yt[t * G:(t + 1) * G, :]
    for t in range(L):
        y = yt_sc[t].T
        lo_ref[pl.ds(t, nchunk, stride=L), :] = y[:, :LANES]
        hi_ref[pl.ds(t, nchunk, stride=L), :] = y[:, LANES:]


def _s5(zs, prep, bsz, seq):
    ng, P, G, L = S5_GROUPS, S5_STATE, S5_GROUP, S5_CHUNK
    toep, bend, cout, laml = prep
    laml = laml.reshape(ng, 4, P).transpose(1, 0, 2).reshape(4, 1, ng * P)
    t = bsz * seq
    tile = min(S5_TILE, t)
    nct = tile // L
    nchunk = t // L
    ut_spec = pl.BlockSpec((ng, L * G, nct), lambda i: (0, 0, i))
    plane_spec = pl.BlockSpec((4 * ng * P, nct), lambda i: (0, i))
    plane_shape = jax.ShapeDtypeStruct((4 * ng * P, nchunk), F32)
    whole = lambda a: pl.BlockSpec(a.shape, lambda i: (0,) * a.ndim)
    ut, et = pl.pallas_call(
        functools.partial(_s5_sums_kernel, nchunk=nct),
        grid=(t // tile,),
        in_specs=[pl.BlockSpec((tile, LANES), lambda i: (i, 0)), pl.BlockSpec((tile, LANES), lambda i: (i, 1)),
                  whole(bend)],
        out_specs=[ut_spec, plane_spec],
        out_shape=[jax.ShapeDtypeStruct((ng, L * G, nchunk), BF16), plane_shape],
        compiler_params=_params("parallel"),
        name="s5_chunk_sums",
    )(zs, zs, bend)
    xt = pl.pallas_call(
        functools.partial(_s5_scan_kernel, bsz=bsz, steps=seq // L),
        out_shape=plane_shape,
        scratch_shapes=[pltpu.VMEM((4, nchunk, ng * P), F32), pltpu.VMEM((4, nchunk, ng * P), F32)],
        compiler_params=pltpu.CompilerParams(vmem_limit_bytes=VMEM_LIMIT),
        name="s5_state_scan",
    )(et, laml)
    half_spec = pl.BlockSpec((tile, LANES), lambda i: (i, 0))
    half_shape = jax.ShapeDtypeStruct((t, LANES), F32)
    return pl.pallas_call(
        functools.partial(_s5_out_kernel, nchunk=nct),
        grid=(t // tile,),
        in_specs=[ut_spec, plane_spec, whole(toep), whole(cout)],
        out_specs=[half_spec, half_spec],
        out_shape=[half_shape, half_shape],
        scratch_shapes=[pltpu.VMEM((L, M_S5, nct), F32)],
        compiler_params=_params("parallel"),
        name="s5_outputs",
    )(ut, xt, toep, cout)


def _merge_kernel(x_ref, hf_ref, hb_ref, zo_ref, yf_ref, ys_lo_ref, ys_hi_ref, gpre_ref, wgate_ref, bgate_ref,
                  ng_ref, wm_ref, wf_ref, wglu_ref, bglu_ref, wout_ref, gpost_ref, o_ref):
    x = x_ref[...]
    h = _rms(x, gpre_ref[...]).astype(BF16)
    hs = hf_ref[...] + hb_ref[...]
    parts = []
    for hd in range(HEADS):
        blk = hs[:, hd * HEAD_DIM:(hd + 1) * HEAD_DIM]
        mu = jnp.mean(blk, axis=-1, keepdims=True)
        cen = blk - mu
        var = jnp.mean(cen * cen, axis=-1, keepdims=True)
        parts.append(cen * lax.rsqrt(var + EPS))
    hm = jnp.concatenate(parts, axis=1) * ng_ref[...] * _sigmoid(zo_ref[...].astype(F32))
    y_m = _dot(hm.astype(BF16), wm_ref[...])
    y_f = _dot(yf_ref[...].astype(BF16), wf_ref[...])
    zz = (_dot(ys_lo_ref[...].astype(BF16), wglu_ref[:LANES, :]) + _dot(ys_hi_ref[...].astype(BF16), wglu_ref[LANES:, :])
          + bglu_ref[...])
    y_s = zz[:, :D_MODEL] * _sigmoid(zz[:, D_MODEL:])

    def gate(i):
        lo = i * D_MODEL
        return _sigmoid(_dot(h, wgate_ref[:, lo:lo + D_MODEL]) + bgate_ref[:, lo:lo + D_MODEL])

    mixed = gate(0) * y_m + gate(1) * y_f + gate(2) * y_s
    o_ref[...] = x + _rms(_dot(mixed.astype(BF16), wout_ref[...]), gpost_ref[...])


def _merge(x2, hf, hb, zo, yf, ys_lo, ys_hi, g_pre, w_gate, b_gate, norm_g, w_up_m, w_up_f, w_glu, b_glu, w_out,
           g_post, tile):
    t = x2.shape[0]
    rows = lambda width: pl.BlockSpec((tile, width), lambda i: (i, 0))
    const = lambda a: pl.BlockSpec(a.shape, lambda i: (0, 0))
    args = [g_pre[None, :], w_gate.astype(BF16), b_gate[None, :], norm_g[None, :], w_up_m.astype(BF16),
            w_up_f.astype(BF16), w_glu.astype(BF16), b_glu[None, :], w_out.astype(BF16), g_post[None, :]]
    return pl.pallas_call(
        _merge_kernel,
        grid=(t // tile,),
        in_specs=[rows(D_MODEL), rows(M_MLSTM), rows(M_MLSTM), rows(M_MLSTM), rows(M_FOURIER),
                  rows(LANES), rows(LANES)] + [const(a) for a in args],
        out_specs=rows(D_MODEL),
        out_shape=jax.ShapeDtypeStruct((t, D_MODEL), F32),
        compiler_params=_params("parallel"),
        name="merge",
    )(x2, hf, hb, zo, yf, ys_lo, ys_hi, *args)


def _ffn_kernel(x_ref, gpre_ref, w1_ref, w2_ref, gpost_ref, o_ref, *, n_split):
    x = x_ref[...]
    h = _rms(x, gpre_ref[...]).astype(BF16)
    width = D_FF // n_split
    f = None
    for j in range(n_split):
        a = jnp.maximum(_dot(h, w1_ref[:, j * width:(j + 1) * width]), 0.0)
        part = _dot((a * a).astype(BF16), w2_ref[j * width:(j + 1) * width, :])
        f = part if f is None else f + part
    o_ref[...] = x + _rms(f, gpost_ref[...])


def _ffn(x2, g_pre, w1, w2, g_post, tile):
    t = x2.shape[0]
    rows = pl.BlockSpec((tile, D_MODEL), lambda i: (i, 0))
    const = lambda shape: pl.BlockSpec(shape, lambda i: (0, 0))
    return pl.pallas_call(
        functools.partial(_ffn_kernel, n_split=4),
        grid=(t // tile,),
        in_specs=[rows, const((1, D_MODEL)), const((D_MODEL, D_FF)), const((D_FF, D_MODEL)), const((1, D_MODEL))],
        out_specs=rows,
        out_shape=jax.ShapeDtypeStruct((t, D_MODEL), F32),
        compiler_params=_params("parallel"),
        name="ffn",
    )(x2, g_pre[None, :], w1.astype(BF16), w2.astype(BF16), g_post[None, :])


def kernel(x, g_mix_pre, g_mix_post, g_ffn_pre, g_ffn_post, w_in, b_in, conv_w, conv_b, mlstm_norm_g, w_up_mlstm, w_up_fourier, s5_lam_re, s5_lam_im, s5_log_dt, s5_b_re, s5_b_im, s5_c_re, s5_c_im, s5_d, w_glu, b_glu, w_out, w_ffn1, w_ffn2):
    bsz, seq, _ = x.shape
    depth = w_in.shape[0]
    t = bsz * seq
    tile = min(512, t)
    x2 = x.reshape(t, D_MODEL)
    for l in range(depth):
        zqk, zv, zo, zf, zs, zg, zgt = _inproj(x2, g_mix_pre[l], w_in[l], b_in[l], tile)
        hf, hb = _mlstm(zqk, zv, zg, zgt, conv_w[l], conv_b[l], bsz, seq)
        yf = _fourier(zf, bsz, seq)
        prep = _s5_prep(s5_lam_re[l], s5_lam_im[l], s5_log_dt[l], s5_b_re[l], s5_b_im[l], s5_c_re[l], s5_c_im[l],
                        s5_d[l])
        ys_lo, ys_hi = _s5(zs, prep, bsz, seq)
        x2 = _merge(x2, hf, hb, zo, yf, ys_lo, ys_hi, g_mix_pre[l], w_in[l][:, OFF_GATE:], b_in[l][OFF_GATE:],
                    mlstm_norm_g[l], w_up_mlstm[l], w_up_fourier[l], w_glu[l], b_glu[l], w_out[l],
                    g_mix_post[l], tile)
        x2 = _ffn(x2, g_ffn_pre[l], w_ffn1[l], w_ffn2[l], g_ffn_post[l], tile)
    return x2.reshape(bsz, seq, D_MODEL)
```

```python
import functools
import math

import numpy as np
import jax
import jax.numpy as jnp
from jax import lax
from jax.experimental import pallas as pl
from jax.experimental.pallas import tpu as pltpu

F32 = jnp.float32
BF16 = jnp.bfloat16
HIGHEST = lax.Precision.HIGHEST

LANES = 128
D_MODEL = 1024
M_MLSTM = 512
HEADS = 4
HEAD_DIM = 128
CHUNK = 128
CONV_WIDTH = 5
CONV_HALO = 8
M_FOURIER = 256
FOURIER_GROUP_DIM = 64
M_S5 = 256
S5_GROUP = 16
S5_GROUPS = 16
S5_STATE = 64
S5_CHUNK = 32
S5_TILE = 4096
N_BRANCHES = 3
D_FF = 4 * D_MODEL
EPS = 1e-6

OFF_Q = 0
OFF_V = 2 * M_MLSTM
OFF_O = 3 * M_MLSTM
OFF_IG = 4 * M_MLSTM
OFF_FOURIER = OFF_IG + 4 * HEADS
OFF_S5 = OFF_FOURIER + M_FOURIER
OFF_GATE = OFF_S5 + M_S5

DFT_N2 = 64
DFT_STEP = 8

VMEM_LIMIT = 56 * 1024 * 1024

NT_DIMS = (((1,), (1,)), ((), ()))
TN_DIMS = (((0,), (0,)), ((), ()))


def _params(*sem):
    return pltpu.CompilerParams(dimension_semantics=sem, vmem_limit_bytes=VMEM_LIMIT)


def _rms(x, g):
    return x * lax.rsqrt(jnp.mean(x * x, axis=-1, keepdims=True) + EPS) * g


def _sigmoid(x):
    return 1.0 / (1.0 + jnp.exp(-x))


def _dot(a, b):
    return jnp.dot(a, b, preferred_element_type=F32)


def _dot_hi(a, b):
    return jnp.dot(a, b, precision=HIGHEST, preferred_element_type=F32)


def _split3(a):
    hi = a.astype(BF16)
    rest = a - hi.astype(F32)
    mid = rest.astype(BF16)
    return hi, mid, (rest - mid.astype(F32)).astype(BF16)


def _dot_sel(a, sel):
    hi, mid, lo = _split3(a)
    return _dot(hi, sel) + _dot(mid, sel) + _dot(lo, sel)


def _sel_dot(sel, a):
    hi, mid, lo = _split3(a)
    return _dot(sel, hi) + _dot(sel, mid) + _dot(sel, lo)


def _inproj_kernel(x_ref, xp_ref, xn_ref, g_ref, wm_ref, bm_ref, wg_ref, bg_ref, wgt_ref, bgt_ref, cw_ref, cb_ref,
                   qk_ref, zv_ref, zo_ref, zf_ref, zs_ref, zg_ref, zgt_ref, ext_sc, *, tiles_per_seq):
    i = pl.program_id(0)
    tile = x_ref.shape[0]
    pad = CONV_WIDTH // 2
    h = _rms(x_ref[...], g_ref[...]).astype(BF16)

    def proj(hh, lo, hi):
        return _dot(hh, wm_ref[:, lo:hi]) + bm_ref[:, lo:hi]

    h_halo = _rms(jnp.concatenate([xp_ref[...], xn_ref[...]], axis=0), g_ref[...]).astype(BF16)
    z_halo = proj(h_halo, OFF_Q, OFF_V)
    first = (i % tiles_per_seq) == 0
    last = (i % tiles_per_seq) == tiles_per_seq - 1
    ext_sc[0:CONV_HALO, :] = jnp.where(first, 0.0, z_halo[:CONV_HALO])
    ext_sc[CONV_HALO:CONV_HALO + tile, :] = proj(h, OFF_Q, OFF_V)
    ext_sc[CONV_HALO + tile:, :] = jnp.where(last, 0.0, z_halo[CONV_HALO:])
    acc = cb_ref[...] + cw_ref[0:1, :] * ext_sc[CONV_HALO - pad:CONV_HALO - pad + tile, :]
    for j in range(1, CONV_WIDTH):
        acc = acc + cw_ref[j:j + 1, :] * ext_sc[CONV_HALO - pad + j:CONV_HALO - pad + j + tile, :]
    qk = acc * _sigmoid(acc)
    qk_ref[:, :M_MLSTM] = (qk[:, :M_MLSTM] * (HEAD_DIM ** -0.5)).astype(BF16)
    qk_ref[:, M_MLSTM:] = qk[:, M_MLSTM:].astype(BF16)

    zv_ref[...] = proj(h, OFF_V, OFF_O).astype(BF16)
    zo_ref[...] = proj(h, OFF_O, OFF_IG).astype(BF16)
    zf_ref[...] = proj(h, OFF_IG, OFF_IG + M_FOURIER)
    zs_ref[...] = proj(h, OFF_IG + M_FOURIER, OFF_IG + M_FOURIER + M_S5)
    zg_ref[...] = _dot(h, wg_ref[...]) + bg_ref[...]
    zgt_ref[...] = lax.dot_general(wgt_ref[...], h, NT_DIMS, preferred_element_type=F32) + bgt_ref[...]


def _inproj(x2, g, w_in, b_in, conv_w, conv_b, tile, seq):
    t = x2.shape[0]
    n_gate = 4 * HEADS
    wm = jnp.concatenate([w_in[:, :OFF_IG], w_in[:, OFF_FOURIER:OFF_GATE]], axis=1).astype(BF16)
    bm = jnp.concatenate([b_in[:OFF_IG], b_in[OFF_FOURIER:OFF_GATE]])[None, :]
    wg = w_in[:, OFF_IG:OFF_FOURIER]
    bg = b_in[OFF_IG:OFF_FOURIER]
    n_main = wm.shape[1]
    hpt = tile // CONV_HALO
    n_halo = t // CONV_HALO
    const = lambda shape: pl.BlockSpec(shape, lambda i: (0, 0))
    rows = lambda width: pl.BlockSpec((tile, width), lambda i: (i, 0))
    cols = lambda height: pl.BlockSpec((height, tile), lambda i: (0, i))
    return pl.pallas_call(
        functools.partial(_inproj_kernel, tiles_per_seq=seq // tile),
        grid=(t // tile,),
        in_specs=[rows(D_MODEL),
                  pl.BlockSpec((CONV_HALO, D_MODEL), lambda i: (jnp.maximum(i * hpt - 1, 0), 0)),
                  pl.BlockSpec((CONV_HALO, D_MODEL), lambda i: (jnp.minimum((i + 1) * hpt, n_halo - 1), 0)),
                  const((1, D_MODEL)), const((D_MODEL, n_main)), const((1, n_main)),
                  const((D_MODEL, n_gate)), const((1, n_gate)), const((n_gate, D_MODEL)), const((n_gate, 1)),
                  const((CONV_WIDTH, 2 * M_MLSTM)), const((1, 2 * M_MLSTM))],
        out_specs=[rows(2 * M_MLSTM), rows(M_MLSTM), rows(M_MLSTM), rows(M_FOURIER), rows(M_S5),
                   rows(n_gate), cols(n_gate)],
        out_shape=[jax.ShapeDtypeStruct((t, 2 * M_MLSTM), BF16), jax.ShapeDtypeStruct((t, M_MLSTM), BF16),
                   jax.ShapeDtypeStruct((t, M_MLSTM), BF16), jax.ShapeDtypeStruct((t, M_FOURIER), F32),
                   jax.ShapeDtypeStruct((t, M_S5), F32), jax.ShapeDtypeStruct((t, n_gate), F32),
                   jax.ShapeDtypeStruct((n_gate, t), F32)],
        scratch_shapes=[pltpu.VMEM((tile + 2 * CONV_HALO, 2 * M_MLSTM), F32)],
        compiler_params=_params("parallel"),
        name="inproj",
    )(x2, x2, x2, g[None, :], wm, bm, wg.astype(BF16), bg[None, :], wg.T.astype(BF16), bg[:, None],
      conv_w, conv_b[None, :])


def _log_sigmoid(x):
    return jnp.minimum(x, 0.0) - jnp.log1p(jnp.exp(-jnp.abs(x)))


def _mlstm_kernel(qk_f, v_f, g_f, gt_f, qk_b, v_b, g_b, gt_b, hf_ref, hb_ref, state_sc, m_sc):
    c = pl.program_id(1)
    L = CHUNK

    @pl.when(c == 0)
    def _():
        state_sc[...] = jnp.zeros_like(state_sc)
        m_sc[...] = jnp.zeros_like(m_sc)

    row = lax.broadcasted_iota(jnp.int32, (L, L), 0)
    col = lax.broadcasted_iota(jnp.int32, (L, L), 1)
    lower = row >= col
    upper = row <= col
    lower_f = jnp.where(lower, 1.0, 0.0).astype(BF16)
    upper_f = jnp.where(upper, 1.0, 0.0).astype(BF16)
    ones_v = jnp.ones((L, HEAD_DIM), F32)

    streams = ((0, qk_f, v_f, g_f, gt_f, hf_ref), (1, qk_b, v_b, g_b, gt_b, hb_ref))
    for d, qk_ref, v_ref, g_ref, gt_ref, out_ref in streams:
        gates = g_ref[...]
        gates_t = gt_ref[...]
        lf_cols = _log_sigmoid(gates[:, 2 * HEADS:])
        lf_rows = _log_sigmoid(gates_t[2 * HEADS:, :])
        if d == 0:
            b_cols = _sel_dot(lower_f, lf_cols)
            b_rows = _dot_sel(lf_rows, upper_f)
            mask = lower
        else:
            b_cols = _sel_dot(upper_f, lf_cols)
            b_rows = _dot_sel(lf_rows, lower_f)
            mask = upper

        for hd in range(HEADS):
            k_idx = d * HEADS + hd
            lo = hd * HEAD_DIM
            q = qk_ref[:, lo:lo + HEAD_DIM]
            k = qk_ref[:, M_MLSTM + lo:M_MLSTM + lo + HEAD_DIM]
            v = v_ref[:, lo:lo + HEAD_DIM]
            b_c = jnp.broadcast_to(b_cols[:, k_idx:k_idx + 1], (L, L))
            ig_c = jnp.broadcast_to(gates[:, k_idx:k_idx + 1], (L, L))
            b_r = b_rows[k_idx:k_idx + 1, :]
            ig_r = gates_t[k_idx:k_idx + 1, :]
            g_tot = b_c[L - 1:L, :] if d == 0 else b_c[0:1, :]
            m_prev = m_sc[k_idx:k_idx + 1, :]

            d_log = jnp.where(mask, b_c - b_r + ig_r, -1e30)
            inter_log = b_c + m_prev
            m_t = jnp.maximum(inter_log, jnp.max(d_log, axis=1, keepdims=True))
            scores = lax.dot_general(q, k, NT_DIMS, preferred_element_type=F32) * jnp.exp(d_log - m_t)
            inter_w = jnp.exp(inter_log - m_t)
            st = state_sc[k_idx]
            v_ext = jnp.concatenate([v, ones_v.astype(BF16)], axis=1)
            res = _dot(scores.astype(BF16), v_ext) + _dot((inter_w * q.astype(F32)).astype(BF16), st.astype(BF16))
            num = res[:, :HEAD_DIM]
            den = res[:, HEAD_DIM:]
            out_ref[:, lo:lo + HEAD_DIM] = num / jnp.maximum(jnp.abs(den), jnp.exp(-m_t))

            a_c = g_tot - b_c + ig_c
            a_max = jnp.max(a_c, axis=0, keepdims=True)
            w_c = jnp.exp(a_c - a_max)
            vw = jnp.concatenate([v.astype(F32) * w_c, w_c], axis=1).astype(BF16)
            st_loc = lax.dot_general(k, vw, TN_DIMS, preferred_element_type=F32)
            m_new = jnp.maximum(g_tot + m_prev, a_max)
            s_old = jnp.exp(g_tot + m_prev - m_new)
            s_new = jnp.exp(a_max - m_new)
            s_old2 = jnp.concatenate([s_old, s_old], axis=1)
            s_new2 = jnp.concatenate([s_new, s_new], axis=1)
            state_sc[k_idx] = s_old2 * st + s_new2 * st_loc
            m_sc[k_idx:k_idx + 1, :] = m_new


def _mlstm(qk, zv, zg, zgt, bsz, seq):
    t = bsz * seq
    nc = seq // CHUNK
    n_gate = 4 * HEADS

    def fwd(b, c):
        return b * nc + c

    def bwd(b, c):
        return b * nc + nc - 1 - c

    def specs(pos):
        return [pl.BlockSpec((CHUNK, 2 * M_MLSTM), lambda b, c: (pos(b, c), 0)),
                pl.BlockSpec((CHUNK, M_MLSTM), lambda b, c: (pos(b, c), 0)),
                pl.BlockSpec((CHUNK, n_gate), lambda b, c: (pos(b, c), 0)),
                pl.BlockSpec((n_gate, CHUNK), lambda b, c: (0, pos(b, c)))]

    out_shape = jax.ShapeDtypeStruct((t, M_MLSTM), F32)
    return pl.pallas_call(
        _mlstm_kernel,
        grid=(bsz, nc),
        in_specs=specs(fwd) + specs(bwd),
        out_specs=[pl.BlockSpec((CHUNK, M_MLSTM), lambda b, c: (fwd(b, c), 0)),
                   pl.BlockSpec((CHUNK, M_MLSTM), lambda b, c: (bwd(b, c), 0))],
        out_shape=[out_shape, out_shape],
        scratch_shapes=[pltpu.VMEM((2 * HEADS, HEAD_DIM, 2 * HEAD_DIM), F32),
                        pltpu.VMEM((2 * HEADS, HEAD_DIM), F32)],
        compiler_params=_params("arbitrary", "arbitrary"),
        name="mlstm",
    )(qk, zv, zg, zgt, qk, zv, zg, zgt)


def _dft_constants(seq):
    n1, n2 = seq // DFT_N2, DFT_N2
    k1 = np.arange(n1)[:, None, None]
    s2 = np.arange(n2)[None, None, :]
    s1 = np.arange(n1)[None, :, None]
    ang = -2.0 * np.pi * ((k1 * (n2 * s1 + s2)) % seq) / seq
    stage1 = np.concatenate([np.cos(ang), np.sin(ang)], axis=0)
    stage1 = np.ascontiguousarray(stage1.transpose(2, 0, 1))
    a2 = 2.0 * np.pi * np.outer(np.arange(n2), np.arange(n2)) / n2
    c2, sn2 = np.cos(a2), np.sin(a2)
    stage2 = np.block([[c2, sn2], [-sn2, c2]])
    ag = 2.0 * np.pi * np.outer(np.arange(FOURIER_GROUP_DIM), np.arange(FOURIER_GROUP_DIM)) / FOURIER_GROUP_DIM
    scale = 1.0 / math.sqrt(seq * FOURIER_GROUP_DIM)
    eye = np.eye(LANES // FOURIER_GROUP_DIM)
    group = np.stack([np.kron(eye, np.cos(ag)), np.kron(eye, np.sin(ag))]) * scale
    as_bf16 = lambda a: jnp.asarray(a, F32).astype(BF16)
    return as_bf16(stage1), as_bf16(stage2), as_bf16(group)


def _fourier_a_kernel(u_ref, w_ref, br_ref, bi_ref, u_sc, br_sc, bi_sc, *, n1):
    rows = n1 * DFT_STEP
    u_sc[...] = u_ref[...].reshape(rows, LANES)
    for j in range(DFT_STEP):
        pick = pl.ds(j, n1, stride=DFT_STEP)
        res = _dot(w_ref[j], u_sc[pick, :].astype(BF16))
        br_sc[pick, :] = res[:n1]
        bi_sc[pick, :] = res[n1:]
    br_ref[...] = br_sc[...].reshape(n1, DFT_STEP, LANES)
    bi_ref[...] = bi_sc[...].reshape(n1, DFT_STEP, LANES)


def _fourier_c_kernel(br_ref, bi_ref, w2_ref, wg_ref, y_ref, y_sc):
    n2 = DFT_N2
    stacked = jnp.concatenate(
        [jnp.concatenate([br_ref[j * n2:(j + 1) * n2, :], bi_ref[j * n2:(j + 1) * n2, :]], axis=0)
         for j in range(DFT_STEP)], axis=1).astype(BF16)
    z = _dot(w2_ref[...], stacked).astype(BF16)
    zr = jnp.concatenate([z[:n2, j * LANES:(j + 1) * LANES] for j in range(DFT_STEP)], axis=0)
    zi = jnp.concatenate([z[n2:, j * LANES:(j + 1) * LANES] for j in range(DFT_STEP)], axis=0)
    y = _dot(zr, wg_ref[0]) + _dot(zi, wg_ref[1])
    for j in range(DFT_STEP):
        y_sc[pl.ds(j, n2, stride=DFT_STEP), :] = y[j * n2:(j + 1) * n2, :]
    y_ref[...] = y_sc[...].reshape(n2, DFT_STEP, LANES)


def _fourier(zf, bsz, seq):
    n1, n2 = seq // DFT_N2, DFT_N2
    stage1, stage2, group = _dft_constants(seq)
    halves = M_FOURIER // LANES
    blk_a = pl.BlockSpec((None, n1, DFT_STEP, LANES), lambda b, j, h: (b, 0, j, h))
    b_shape = jax.ShapeDtypeStruct((bsz, n1, n2, M_FOURIER), F32)
    br, bi = pl.pallas_call(
        functools.partial(_fourier_a_kernel, n1=n1),
        grid=(bsz, n2 // DFT_STEP, halves),
        in_specs=[blk_a, pl.BlockSpec((DFT_STEP, 2 * n1, n1), lambda b, j, h: (j, 0, 0))],
        out_specs=[blk_a, blk_a],
        out_shape=[b_shape, b_shape],
        scratch_shapes=[pltpu.VMEM((n1 * DFT_STEP, LANES), F32)] * 3,
        compiler_params=_params("parallel", "parallel", "parallel"),
        name="fourier_stage1",
    )(zf.reshape(bsz, n1, n2, M_FOURIER), stage1)
    blk_c = pl.BlockSpec((None, DFT_STEP * n2, LANES), lambda b, j, h: (b, j, h))
    y = pl.pallas_call(
        _fourier_c_kernel,
        grid=(bsz, n1 // DFT_STEP, halves),
        in_specs=[blk_c, blk_c, pl.BlockSpec((2 * n2, 2 * n2), lambda b, j, h: (0, 0)),
                  pl.BlockSpec((2, LANES, LANES), lambda b, j, h: (0, 0, 0))],
        out_specs=pl.BlockSpec((None, n2, DFT_STEP, LANES), lambda b, j, h: (b, 0, j, h)),
        out_shape=jax.ShapeDtypeStruct((bsz, n2, n1, M_FOURIER), F32),
        scratch_shapes=[pltpu.VMEM((n2 * DFT_STEP, LANES), F32)],
        compiler_params=_params("parallel", "parallel", "parallel"),
        name="fourier_stage2",
    )(br.reshape(bsz, n1 * n2, M_FOURIER), bi.reshape(bsz, n1 * n2, M_FOURIER), stage2, group)
    return y.reshape(bsz * seq, M_FOURIER)


def _s5_prep_kernel(lamc_re, lamc_im, ldt_ref, lam4_re, lam4_im, ldt4_ref, b_re, b_im, c_re, c_im, c4_re, c4_im,
                    d_ref, toep_ref, bend_ref, cout_ref, laml_ref):
    L, P, G = S5_CHUNK, S5_STATE, S5_GROUP
    W = 2 * L * G
    shift = G.bit_length() - 1

    def cmul(ar, ai, br, bi):
        return ar * br - ai * bi, ar * bi + ai * br

    def power(lre, lim, dt, tau):
        mag = jnp.exp(tau * (lre * dt))
        ang = tau * (lim * dt)
        return mag * jnp.cos(ang), mag * jnp.sin(ang)

    def zoh_factor(lre, lim, dt):
        lbr, lbi = power(lre, lim, dt, 1.0)
        den = lre * lre + lim * lim
        return ((lbr - 1.0) * lre + lbi * lim) / den, (lbi * lre - (lbr - 1.0) * lim) / den

    lane_g = lax.broadcasted_iota(jnp.int32, (G, W), 1)
    row_g = lax.broadcasted_iota(jnp.int32, (G, W), 0)
    rep = jnp.where((lane_g & (G - 1)) == row_g, 1.0, 0.0).astype(BF16)
    expo_rows = lax.broadcasted_iota(jnp.int32, (LANES, W), 0)
    lag_w = (L - 1) - (lax.broadcasted_iota(jnp.int32, (LANES, W), 1) >> shift)
    expo_half = lax.broadcasted_iota(jnp.int32, (LANES, L * G), 0)
    step_half = lax.broadcasted_iota(jnp.int32, (LANES, L * G), 1) >> shift
    expo_tab = jnp.minimum(lax.broadcasted_iota(jnp.int32, (P, LANES), 1), L).astype(F32)

    d_tiled = _dot_sel(jnp.broadcast_to(d_ref[0], (G, G)), rep)
    gen = jnp.where(((lane_g & (G - 1)) == row_g) & ((lane_g >> shift) == L - 1), d_tiled, 0.0)
    for d in range(2):
        dt = jnp.exp(ldt_ref[d, 0])
        lr, li = lamc_re[d, 0], lamc_im[d, 0]
        fr, fi = zoh_factor(lr, li, dt)
        bbr, bbi = cmul(fr, fi, b_re[d, 0], b_im[d, 0])
        btr, bti = _dot_sel(bbr, rep), _dot_sel(bbi, rep)
        tab_r, tab_i = power(lr, li, dt, expo_tab)
        if d == 0:
            pick = (expo_rows == lag_w) & (lag_w >= 0)
        else:
            pick = (expo_rows == -lag_w) & (lag_w <= 0) & (lag_w > -L)
        spread = jnp.where(pick, 1.0, 0.0).astype(BF16)
        mr, mi = cmul(_dot_sel(tab_r, spread), _dot_sel(tab_i, spread), btr, bti)
        gen = gen + _dot_hi(c_re[d, 0], mr) - _dot_hi(c_im[d, 0], mi)
        half = L * G
        spread_s = jnp.where(expo_half == (((L - 1) - step_half) if d == 0 else step_half), 1.0, 0.0).astype(BF16)
        xr, xi = cmul(_dot_sel(tab_r, spread_s), _dot_sel(tab_i, spread_s), btr[:, :half], bti[:, :half])
        bend_ref[0, 2 * d] = xr.astype(BF16)
        bend_ref[0, 2 * d + 1] = xi.astype(BF16)

    for t in range(L):
        a = (L - 1 - t) * G
        toep_ref[0, t * G:(t + 1) * G, :] = gen[:, a:a + L * G].astype(BF16)

    lane4 = lax.broadcasted_iota(jnp.int32, (L, 4 * P), 1)
    step4 = lax.broadcasted_iota(jnp.int32, (L, 4 * P), 0)
    is_fwd = lane4 < 2 * P
    dt4 = jnp.exp(ldt4_ref[0])
    pr, pi = power(lam4_re[0], lam4_im[0], dt4, jnp.where(is_fwd, step4 + 1, L - step4).astype(F32))
    cr, ci = c4_re[0], c4_im[0]
    plane_bit = P.bit_length() - 1
    re_c = ((lax.broadcasted_iota(jnp.int32, (G, 4 * P), 1) >> plane_bit) & 1) == 0
    for t in range(L):
        re_part, im_part = cmul(cr, ci, pr[t:t + 1], pi[t:t + 1])
        cout_ref[0, t * G:(t + 1) * G, :] = jnp.where(re_c, re_part, -im_part).astype(BF16)
    re_1 = ((lax.broadcasted_iota(jnp.int32, (1, 4 * P), 1) >> plane_bit) & 1) == 0
    lr, li = power(lam4_re[0], lam4_im[0], dt4, float(L))
    laml_ref[0] = jnp.where(re_1, lr, li)


def _s5_prep(lam_re, lam_im, log_dt, b_re, b_im, c_re, c_im, d_skip):
    ng, P, G, L = S5_GROUPS, S5_STATE, S5_GROUP, S5_CHUNK
    tile4 = lambda a: jnp.concatenate([a[0], a[0], a[1], a[1]], axis=-1)
    ldt4 = jnp.repeat(tile4(log_dt[..., None]), P, axis=-1)[:, None, :]
    spec = lambda *tail: pl.BlockSpec((2, 1) + tail, lambda g: (0, g) + (0,) * len(tail))
    per_g = lambda *tail: pl.BlockSpec((1,) + tail, lambda g: (g,) + (0,) * len(tail))
    return pl.pallas_call(
        _s5_prep_kernel,
        grid=(ng,),
        in_specs=[spec(P, 1), spec(P, 1), spec(1, 1), per_g(1, 4 * P), per_g(1, 4 * P), per_g(1, 4 * P),
                  spec(P, G), spec(P, G), spec(G, P), spec(G, P), per_g(G, 4 * P), per_g(G, 4 * P), per_g(1, G)],
        out_specs=[per_g(L * G, L * G), per_g(4, P, L * G), per_g(L * G, 4 * P), per_g(1, 4 * P)],
        out_shape=[jax.ShapeDtypeStruct((ng, L * G, L * G), BF16),
                   jax.ShapeDtypeStruct((ng, 4, P, L * G), BF16),
                   jax.ShapeDtypeStruct((ng, L * G, 4 * P), BF16),
                   jax.ShapeDtypeStruct((ng, 1, 4 * P), F32)],
        compiler_params=_params("parallel"),
        name="s5_prep",
    )(lam_re[..., None], lam_im[..., None], log_dt[:, :, None, None],
      tile4(lam_re)[:, None, :], tile4(lam_im)[:, None, :], ldt4,
      b_re, b_im, c_re, c_im, tile4(c_re), tile4(c_im), d_skip[:, None, :])


def _s5_sums_kernel(lo_ref, hi_ref, bend_ref, ut_ref, et_ref, *, nchunk):
    L, P, G = S5_CHUNK, S5_STATE, S5_GROUP
    per_half = LANES // G
    for half, z_ref in enumerate((lo_ref, hi_ref)):
        for s in range(L):
            zt = z_ref[pl.ds(s, nchunk, stride=L), :].T.astype(BF16)
            for gl in range(per_half):
                ut_ref[half * per_half + gl, s * G:(s + 1) * G, :] = zt[gl * G:(gl + 1) * G, :]
    for g in range(S5_GROUPS):
        for plane in range(4):
            row = plane * S5_GROUPS * P + g * P
            et_ref[row:row + P, :] = _dot(bend_ref[g, plane], ut_ref[g])


def _s5_scan_kernel(et_ref, lam_ref, xt_ref, e_sc, x_sc, *, bsz, steps):
    width = S5_GROUPS * S5_STATE
    for plane in range(4):
        e_sc[plane] = et_ref[plane * width:(plane + 1) * width, :].T
    ar_f, ai_f, ar_b, ai_b = lam_ref[0], lam_ref[1], lam_ref[2], lam_ref[3]
    zero = jnp.zeros_like(ar_f)
    for b in range(bsz):
        def body(i, carry):
            xr, xi, yr, yi = carry
            rf = b * steps + i
            rb = b * steps + steps - 1 - i
            x_sc[0, pl.ds(rf, 1), :] = xr
            x_sc[1, pl.ds(rf, 1), :] = xi
            x_sc[2, pl.ds(rb, 1), :] = yr
            x_sc[3, pl.ds(rb, 1), :] = yi
            xr, xi = (ar_f * xr - ai_f * xi + e_sc[0, pl.ds(rf, 1), :],
                      ar_f * xi + ai_f * xr + e_sc[1, pl.ds(rf, 1), :])
            yr, yi = (ar_b * yr - ai_b * yi + e_sc[2, pl.ds(rb, 1), :],
                      ar_b * yi + ai_b * yr + e_sc[3, pl.ds(rb, 1), :])
            return xr, xi, yr, yi
        lax.fori_loop(0, steps, body, (zero, zero, zero, zero))
    for plane in range(4):
        xt_ref[plane * width:(plane + 1) * width, :] = x_sc[plane].T


def _gelu_tanh(x):
    return 0.5 * x * (1.0 + jnp.tanh(math.sqrt(2.0 / math.pi) * (x + 0.044715 * (x * x * x))))


def _s5_out_kernel(ut_ref, xt_ref, toep_ref, cout_ref, lo_ref, hi_ref, yt_sc, *, nchunk):
    L, P, G = S5_CHUNK, S5_STATE, S5_GROUP
    width = S5_GROUPS * P
    for g in range(S5_GROUPS):
        xg = jnp.concatenate([xt_ref[plane * width + g * P:plane * width + (g + 1) * P, :] for plane in range(4)],
                             axis=0).astype(BF16)
        yt = _gelu_tanh(_dot(toep_ref[g], ut_ref[g]) + _dot(cout_ref[g], xg))
        for t in range(L):
            yt_sc[t, g * G:(g + 1) * G, :] = yt[t * G:(t + 1) * G, :]
    for t in range(L):
        y = yt_sc[t].T
        lo_ref[pl.ds(t, nchunk, stride=L), :] = y[:, :LANES]
        hi_ref[pl.ds(t, nchunk, stride=L), :] = y[:, LANES:]


def _s5(zs, prep, bsz, seq):
    ng, P, G, L = S5_GROUPS, S5_STATE, S5_GROUP, S5_CHUNK
    toep, bend, cout, laml = prep
    laml = laml.reshape(ng, 4, P).transpose(1, 0, 2).reshape(4, 1, ng * P)
    t = bsz * seq
    tile = min(S5_TILE, t)
    nct = tile // L
    nchunk = t // L
    ut_spec = pl.BlockSpec((ng, L * G, nct), lambda i: (0, 0, i))
    plane_spec = pl.BlockSpec((4 * ng * P, nct), lambda i: (0, i))
    plane_shape = jax.ShapeDtypeStruct((4 * ng * P, nchunk), F32)
    whole = lambda a: pl.BlockSpec(a.shape, lambda i: (0,) * a.ndim)
    ut, et = pl.pallas_call(
        functools.partial(_s5_sums_kernel, nchunk=nct),
        grid=(t // tile,),
        in_specs=[pl.BlockSpec((tile, LANES), lambda i: (i, 0)), pl.BlockSpec((tile, LANES), lambda i: (i, 1)),
                  whole(bend)],
        out_specs=[ut_spec, plane_spec],
        out_shape=[jax.ShapeDtypeStruct((ng, L * G, nchunk), BF16), plane_shape],
        compiler_params=_params("parallel"),
        name="s5_chunk_sums",
    )(zs, zs, bend)
    xt = pl.pallas_call(
        functools.partial(_s5_scan_kernel, bsz=bsz, steps=seq // L),
        out_shape=plane_shape,
        scratch_shapes=[pltpu.VMEM((4, nchunk, ng * P), F32), pltpu.VMEM((4, nchunk, ng * P), F32)],
        compiler_params=pltpu.CompilerParams(vmem_limit_bytes=VMEM_LIMIT),
        name="s5_state_scan",
    )(et, laml)
    half_spec = pl.BlockSpec((tile, LANES), lambda i: (i, 0))
    half_shape = jax.ShapeDtypeStruct((t, LANES), F32)
    return pl.pallas_call(
        functools.partial(_s5_out_kernel, nchunk=nct),
        grid=(t // tile,),
        in_specs=[ut_spec, plane_spec, whole(toep), whole(cout)],
        out_specs=[half_spec, half_spec],
        out_shape=[half_shape, half_shape],
        scratch_shapes=[pltpu.VMEM((L, M_S5, nct), F32)],
        compiler_params=_params("parallel"),
        name="s5_outputs",
    )(ut, xt, toep, cout)


def _merge_kernel(x_ref, hf_ref, hb_ref, zo_ref, yf_ref, ys_lo_ref, ys_hi_ref, gpre_ref, wgate_ref, bgate_ref,
                  ng_ref, wm_ref, wf_ref, wglu_ref, bglu_ref, wout_ref, gpost_ref, o_ref):
    x = x_ref[...]
    h = _rms(x, gpre_ref[...]).astype(BF16)
    hs = hf_ref[...] + hb_ref[...]
    parts = []
    for hd in range(HEADS):
        blk = hs[:, hd * HEAD_DIM:(hd + 1) * HEAD_DIM]
        mu = jnp.mean(blk, axis=-1, keepdims=True)
        cen = blk - mu
        var = jnp.mean(cen * cen, axis=-1, keepdims=True)
        parts.append(cen * lax.rsqrt(var + EPS))
    hm = jnp.concatenate(parts, axis=1) * ng_ref[...] * _sigmoid(zo_ref[...].astype(F32))
    y_m = _dot(hm.astype(BF16), wm_ref[...])
    y_f = _dot(yf_ref[...].astype(BF16), wf_ref[...])
    zz = (_dot(ys_lo_ref[...].astype(BF16), wglu_ref[:LANES, :]) + _dot(ys_hi_ref[...].astype(BF16), wglu_ref[LANES:, :])
          + bglu_ref[...])
    y_s = zz[:, :D_MODEL] * _sigmoid(zz[:, D_MODEL:])

    def gate(i):
        lo = i * D_MODEL
        return _sigmoid(_dot(h, wgate_ref[:, lo:lo + D_MODEL]) + bgate_ref[:, lo:lo + D_MODEL])

    mixed = gate(0) * y_m + gate(1) * y_f + gate(2) * y_s
    o_ref[...] = x + _rms(_dot(mixed.astype(BF16), wout_ref[...]), gpost_ref[...])


def _merge(x2, hf, hb, zo, yf, ys_lo, ys_hi, g_pre, w_gate, b_gate, norm_g, w_up_m, w_up_f, w_glu, b_glu, w_out,
           g_post, tile):
    t = x2.shape[0]
    rows = lambda width: pl.BlockSpec((tile, width), lambda i: (i, 0))
    const = lambda a: pl.BlockSpec(a.shape, lambda i: (0, 0))
    args = [g_pre[None, :], w_gate.astype(BF16), b_gate[None, :], norm_g[None, :], w_up_m.astype(BF16),
            w_up_f.astype(BF16), w_glu.astype(BF16), b_glu[None, :], w_out.astype(BF16), g_post[None, :]]
    return pl.pallas_call(
        _merge_kernel,
        grid=(t // tile,),
        in_specs=[rows(D_MODEL), rows(M_MLSTM), rows(M_MLSTM), rows(M_MLSTM), rows(M_FOURIER),
                  rows(LANES), rows(LANES)] + [const(a) for a in args],
        out_specs=rows(D_MODEL),
        out_shape=jax.ShapeDtypeStruct((t, D_MODEL), F32),
        compiler_params=_params("parallel"),
        name="merge",
    )(x2, hf, hb, zo, yf, ys_lo, ys_hi, *args)


def _ffn_kernel(x_ref, gpre_ref, w1_ref, w2_ref, gpost_ref, o_ref, *, n_split):
    x = x_ref[...]
    h = _rms(x, gpre_ref[...]).astype(BF16)
    width = D_FF // n_split
    f = None
    for j in range(n_split):
        a = jnp.maximum(_dot(h, w1_ref[:, j * width:(j + 1) * width]), 0.0)
        part = _dot((a * a).astype(BF16), w2_ref[j * width:(j + 1) * width, :])
        f = part if f is None else f + part
    o_ref[...] = x + _rms(f, gpost_ref[...])


def _ffn(x2, g_pre, w1, w2, g_post, tile):
    t = x2.shape[0]
    rows = pl.BlockSpec((tile, D_MODEL), lambda i: (i, 0))
    const = lambda shape: pl.BlockSpec(shape, lambda i: (0, 0))
    return pl.pallas_call(
        functools.partial(_ffn_kernel, n_split=4),
        grid=(t // tile,),
        in_specs=[rows, const((1, D_MODEL)), const((D_MODEL, D_FF)), const((D_FF, D_MODEL)), const((1, D_MODEL))],
        out_specs=rows,
        out_shape=jax.ShapeDtypeStruct((t, D_MODEL), F32),
        compiler_params=_params("parallel"),
        name="ffn",
    )(x2, g_pre[None, :], w1.astype(BF16), w2.astype(BF16), g_post[None, :])


def kernel(x, g_mix_pre, g_mix_post, g_ffn_pre, g_ffn_post, w_in, b_in, conv_w, conv_b, mlstm_norm_g, w_up_mlstm, w_up_fourier, s5_lam_re, s5_lam_im, s5_log_dt, s5_b_re, s5_b_im, s5_c_re, s5_c_im, s5_d, w_glu, b_glu, w_out, w_ffn1, w_ffn2):
    bsz, seq, _ = x.shape
    depth = w_in.shape[0]
    t = bsz * seq
    tile = min(512, t)
    x2 = x.reshape(t, D_MODEL)
    for l in range(depth):
        qk, zv, zo, zf, zs, zg, zgt = _inproj(x2, g_mix_pre[l], w_in[l], b_in[l], conv_w[l], conv_b[l], tile, seq)
        hf, hb = _mlstm(qk, zv, zg, zgt, bsz, seq)
        yf = _fourier(zf, bsz, seq)
        prep = _s5_prep(s5_lam_re[l], s5_lam_im[l], s5_log_dt[l], s5_b_re[l], s5_b_im[l], s5_c_re[l], s5_c_im[l],
                        s5_d[l])
        ys_lo, ys_hi = _s5(zs, prep, bsz, seq)
        x2 = _merge(x2, hf, hb, zo, yf, ys_lo, ys_hi, g_mix_pre[l], w_in[l][:, OFF_GATE:], b_in[l][OFF_GATE:],
                    mlstm_norm_g[l], w_up_mlstm[l], w_up_fourier[l], w_glu[l], b_glu[l], w_out[l],
                    g_mix_post[l], tile)
        x2 = _ffn(x2, g_ffn_pre[l], w_ffn1[l], w_ffn2[l], g_ffn_post[l], tile)
    return x2.reshape(bsz, seq, D_MODEL)
```

```python
import functools
import itertools
import math

import numpy as np
import jax
import jax.numpy as jnp
from jax import lax
from jax.experimental import pallas as pl
from jax.experimental.pallas import tpu as pltpu

F32 = jnp.float32
BF16 = jnp.bfloat16
HIGHEST = lax.Precision.HIGHEST

LANES = 128
D_MODEL = 1024
M_MLSTM = 512
HEADS = 4
HEAD_DIM = 128
CHUNK = 128
MLSTM_SUB = 4
CONV_WIDTH = 5
CONV_HALO = 8
M_FOURIER = 256
FOURIER_GROUP_DIM = 64
M_S5 = 256
S5_GROUP = 16
S5_GROUPS = 16
S5_STATE = 64
S5_CHUNK = 32
S5_TILE = 4096
N_BRANCHES = 3
MERGE_COLS = 256
D_FF = 4 * D_MODEL
EPS = 1e-6

OFF_Q = 0
OFF_V = 2 * M_MLSTM
OFF_O = 3 * M_MLSTM
OFF_IG = 4 * M_MLSTM
OFF_FOURIER = OFF_IG + 4 * HEADS
OFF_S5 = OFF_FOURIER + M_FOURIER
OFF_GATE = OFF_S5 + M_S5

DFT_N2 = 64
DFT_STEP = 8

VMEM_LIMIT = 56 * 1024 * 1024

NT_DIMS = (((1,), (1,)), ((), ()))
TN_DIMS = (((0,), (0,)), ((), ()))


def _params(*sem):
    return pltpu.CompilerParams(dimension_semantics=sem, vmem_limit_bytes=VMEM_LIMIT)


def _rms(x, g):
    return x * lax.rsqrt(jnp.mean(x * x, axis=-1, keepdims=True) + EPS) * g


def _sigmoid(x):
    return 1.0 / (1.0 + jnp.exp(-x))


def _dot(a, b):
    return jnp.dot(a, b, preferred_element_type=F32)


def _dot_hi(a, b):
    return jnp.dot(a, b, precision=HIGHEST, preferred_element_type=F32)


def _split3(a):
    hi = a.astype(BF16)
    rest = a - hi.astype(F32)
    mid = rest.astype(BF16)
    return hi, mid, (rest - mid.astype(F32)).astype(BF16)


def _dot_sel(a, sel):
    hi, mid, lo = _split3(a)
    return _dot(hi, sel) + _dot(mid, sel) + _dot(lo, sel)


def _sel_dot(sel, a):
    hi, mid, lo = _split3(a)
    return _dot(sel, hi) + _dot(sel, mid) + _dot(sel, lo)


def _inproj_kernel(x_ref, xp_ref, xn_ref, g_ref, wm_ref, bm_ref, wg_ref, bg_ref, wgt_ref, bgt_ref, cw_ref, cb_ref,
                   qk_ref, zv_ref, zo_ref, zf_ref, zs_ref, zg_ref, zgt_ref, ext_sc, *, tiles_per_seq):
    i = pl.program_id(0)
    tile = x_ref.shape[0]
    pad = CONV_WIDTH // 2
    h = _rms(x_ref[...], g_ref[...]).astype(BF16)

    def proj(hh, lo, hi):
        return _dot(hh, wm_ref[:, lo:hi]) + bm_ref[:, lo:hi]

    h_halo = _rms(jnp.concatenate([xp_ref[...], xn_ref[...]], axis=0), g_ref[...]).astype(BF16)
    z_halo = proj(h_halo, OFF_Q, OFF_V)
    first = (i % tiles_per_seq) == 0
    last = (i % tiles_per_seq) == tiles_per_seq - 1
    ext_sc[0:CONV_HALO, :] = jnp.where(first, 0.0, z_halo[:CONV_HALO])
    ext_sc[CONV_HALO:CONV_HALO + tile, :] = proj(h, OFF_Q, OFF_V)
    ext_sc[CONV_HALO + tile:, :] = jnp.where(last, 0.0, z_halo[CONV_HALO:])
    acc = cb_ref[...] + cw_ref[0:1, :] * ext_sc[CONV_HALO - pad:CONV_HALO - pad + tile, :]
    for j in range(1, CONV_WIDTH):
        acc = acc + cw_ref[j:j + 1, :] * ext_sc[CONV_HALO - pad + j:CONV_HALO - pad + j + tile, :]
    qk = acc * _sigmoid(acc)
    qk_ref[:, :M_MLSTM] = (qk[:, :M_MLSTM] * (HEAD_DIM ** -0.5)).astype(BF16)
    qk_ref[:, M_MLSTM:] = qk[:, M_MLSTM:].astype(BF16)

    zv_ref[...] = proj(h, OFF_V, OFF_O).astype(BF16)
    zo_ref[...] = proj(h, OFF_O, OFF_IG).astype(BF16)
    zf_ref[...] = proj(h, OFF_IG, OFF_IG + M_FOURIER)
    zs_ref[...] = proj(h, OFF_IG + M_FOURIER, OFF_IG + M_FOURIER + M_S5)
    zg_ref[...] = _dot(h, wg_ref[...]) + bg_ref[...]
    zgt_ref[...] = lax.dot_general(wgt_ref[...], h, NT_DIMS, preferred_element_type=F32) + bgt_ref[...]


def _inproj(x2, g, w_in, b_in, conv_w, conv_b, tile, seq):
    t = x2.shape[0]
    n_gate = 4 * HEADS
    wm = jnp.concatenate([w_in[:, :OFF_IG], w_in[:, OFF_FOURIER:OFF_GATE]], axis=1).astype(BF16)
    bm = jnp.concatenate([b_in[:OFF_IG], b_in[OFF_FOURIER:OFF_GATE]])[None, :]
    wg = w_in[:, OFF_IG:OFF_FOURIER]
    bg = b_in[OFF_IG:OFF_FOURIER]
    n_main = wm.shape[1]
    hpt = tile // CONV_HALO
    n_halo = t // CONV_HALO
    const = lambda shape: pl.BlockSpec(shape, lambda i: (0, 0))
    rows = lambda width: pl.BlockSpec((tile, width), lambda i: (i, 0))
    cols = lambda height: pl.BlockSpec((height, tile), lambda i: (0, i))
    return pl.pallas_call(
        functools.partial(_inproj_kernel, tiles_per_seq=seq // tile),
        grid=(t // tile,),
        in_specs=[rows(D_MODEL),
                  pl.BlockSpec((CONV_HALO, D_MODEL), lambda i: (jnp.maximum(i * hpt - 1, 0), 0)),
                  pl.BlockSpec((CONV_HALO, D_MODEL), lambda i: (jnp.minimum((i + 1) * hpt, n_halo - 1), 0)),
                  const((1, D_MODEL)), const((D_MODEL, n_main)), const((1, n_main)),
                  const((D_MODEL, n_gate)), const((1, n_gate)), const((n_gate, D_MODEL)), const((n_gate, 1)),
                  const((CONV_WIDTH, 2 * M_MLSTM)), const((1, 2 * M_MLSTM))],
        out_specs=[rows(2 * M_MLSTM), rows(M_MLSTM), rows(M_MLSTM), rows(M_FOURIER), rows(M_S5),
                   rows(n_gate), cols(n_gate)],
        out_shape=[jax.ShapeDtypeStruct((t, 2 * M_MLSTM), BF16), jax.ShapeDtypeStruct((t, M_MLSTM), BF16),
                   jax.ShapeDtypeStruct((t, M_MLSTM), BF16), jax.ShapeDtypeStruct((t, M_FOURIER), F32),
                   jax.ShapeDtypeStruct((t, M_S5), F32), jax.ShapeDtypeStruct((t, n_gate), F32),
                   jax.ShapeDtypeStruct((n_gate, t), F32)],
        scratch_shapes=[pltpu.VMEM((tile + 2 * CONV_HALO, 2 * M_MLSTM), F32)],
        compiler_params=_params("parallel"),
        name="inproj",
    )(x2, x2, x2, g[None, :], wm, bm, wg.astype(BF16), bg[None, :], wg.T.astype(BF16), bg[:, None],
      conv_w, conv_b[None, :])


def _log_sigmoid(x):
    return jnp.minimum(x, 0.0) - jnp.log1p(jnp.exp(-jnp.abs(x)))


def _mlstm_kernel(qk_f, v_f, g_f, gt_f, qk_b, v_b, g_b, gt_b, hf_ref, hb_ref, state_sc, m_sc):
    c = pl.program_id(1)
    L = CHUNK

    @pl.when(c == 0)
    def _():
        state_sc[...] = jnp.zeros_like(state_sc)
        m_sc[...] = jnp.zeros_like(m_sc)

    row = lax.broadcasted_iota(jnp.int32, (L, L), 0)
    col = lax.broadcasted_iota(jnp.int32, (L, L), 1)
    lower = row >= col
    upper = row <= col
    lower_f = jnp.where(lower, 1.0, 0.0).astype(BF16)
    upper_f = jnp.where(upper, 1.0, 0.0).astype(BF16)
    ones_v = jnp.ones((L, HEAD_DIM), F32)

    streams = ((0, qk_f, v_f, g_f, gt_f, hf_ref), (1, qk_b, v_b, g_b, gt_b, hb_ref))
    for sub, (d, qk_ref, v_ref, g_ref, gt_ref, out_ref) in itertools.product(range(MLSTM_SUB), streams):
        r0 = (sub if d == 0 else MLSTM_SUB - 1 - sub) * L
        rows = slice(r0, r0 + L)
        gates = g_ref[rows, :]
        gates_t = gt_ref[:, rows]
        lf_cols = _log_sigmoid(gates[:, 2 * HEADS:])
        lf_rows = _log_sigmoid(gates_t[2 * HEADS:, :])
        if d == 0:
            b_cols = _sel_dot(lower_f, lf_cols)
            b_rows = _dot_sel(lf_rows, upper_f)
            mask = lower
        else:
            b_cols = _sel_dot(upper_f, lf_cols)
            b_rows = _dot_sel(lf_rows, lower_f)
            mask = upper

        for hd in range(HEADS):
            k_idx = d * HEADS + hd
            lo = hd * HEAD_DIM
            q = qk_ref[rows, lo:lo + HEAD_DIM]
            k = qk_ref[rows, M_MLSTM + lo:M_MLSTM + lo + HEAD_DIM]
            v = v_ref[rows, lo:lo + HEAD_DIM]
            b_c = jnp.broadcast_to(b_cols[:, k_idx:k_idx + 1], (L, L))
            ig_c = jnp.broadcast_to(gates[:, k_idx:k_idx + 1], (L, L))
            b_r = b_rows[k_idx:k_idx + 1, :]
            ig_r = gates_t[k_idx:k_idx + 1, :]
            g_tot = b_c[L - 1:L, :] if d == 0 else b_c[0:1, :]
            m_prev = m_sc[k_idx:k_idx + 1, :]

            d_log = jnp.where(mask, b_c - b_r + ig_r, -1e30)
            inter_log = b_c + m_prev
            m_t = jnp.maximum(inter_log, jnp.max(d_log, axis=1, keepdims=True))
            scores = lax.dot_general(q, k, NT_DIMS, preferred_element_type=F32) * jnp.exp(d_log - m_t)
            inter_w = jnp.exp(inter_log - m_t)
            st = state_sc[k_idx]
            v_ext = jnp.concatenate([v, ones_v.astype(BF16)], axis=1)
            res = _dot(scores.astype(BF16), v_ext) + _dot((inter_w * q.astype(F32)).astype(BF16), st.astype(BF16))
            num = res[:, :HEAD_DIM]
            den = res[:, HEAD_DIM:]
            out_ref[rows, lo:lo + HEAD_DIM] = num / jnp.maximum(jnp.abs(den), jnp.exp(-m_t))

            a_c = g_tot - b_c + ig_c
            a_max = jnp.max(a_c, axis=0, keepdims=True)
            w_c = jnp.exp(a_c - a_max)
            vw = jnp.concatenate([v.astype(F32) * w_c, w_c], axis=1).astype(BF16)
            st_loc = lax.dot_general(k, vw, TN_DIMS, preferred_element_type=F32)
            m_new = jnp.maximum(g_tot + m_prev, a_max)
            s_old = jnp.exp(g_tot + m_prev - m_new)
            s_new = jnp.exp(a_max - m_new)
            s_old2 = jnp.concatenate([s_old, s_old], axis=1)
            s_new2 = jnp.concatenate([s_new, s_new], axis=1)
            state_sc[k_idx] = s_old2 * st + s_new2 * st_loc
            m_sc[k_idx:k_idx + 1, :] = m_new


def _mlstm(qk, zv, zg, zgt, bsz, seq):
    t = bsz * seq
    blk = CHUNK * MLSTM_SUB
    nc = seq // blk
    n_gate = 4 * HEADS

    def fwd(b, c):
        return b * nc + c

    def bwd(b, c):
        return b * nc + nc - 1 - c

    def specs(pos):
        return [pl.BlockSpec((blk, 2 * M_MLSTM), lambda b, c: (pos(b, c), 0)),
                pl.BlockSpec((blk, M_MLSTM), lambda b, c: (pos(b, c), 0)),
                pl.BlockSpec((blk, n_gate), lambda b, c: (pos(b, c), 0)),
                pl.BlockSpec((n_gate, blk), lambda b, c: (0, pos(b, c)))]

    out_shape = jax.ShapeDtypeStruct((t, M_MLSTM), F32)
    return pl.pallas_call(
        _mlstm_kernel,
        grid=(bsz, nc),
        in_specs=specs(fwd) + specs(bwd),
        out_specs=[pl.BlockSpec((blk, M_MLSTM), lambda b, c: (fwd(b, c), 0)),
                   pl.BlockSpec((blk, M_MLSTM), lambda b, c: (bwd(b, c), 0))],
        out_shape=[out_shape, out_shape],
        scratch_shapes=[pltpu.VMEM((2 * HEADS, HEAD_DIM, 2 * HEAD_DIM), F32),
                        pltpu.VMEM((2 * HEADS, HEAD_DIM), F32)],
        compiler_params=_params("arbitrary", "arbitrary"),
        name="mlstm",
    )(qk, zv, zg, zgt, qk, zv, zg, zgt)


def _dft_constants(seq):
    n1, n2 = seq // DFT_N2, DFT_N2
    k1 = np.arange(n1)[:, None, None]
    s2 = np.arange(n2)[None, None, :]
    s1 = np.arange(n1)[None, :, None]
    ang = -2.0 * np.pi * ((k1 * (n2 * s1 + s2)) % seq) / seq
    stage1 = np.concatenate([np.cos(ang), np.sin(ang)], axis=0)
    stage1 = np.ascontiguousarray(stage1.transpose(2, 0, 1))
    a2 = 2.0 * np.pi * np.outer(np.arange(n2), np.arange(n2)) / n2
    c2, sn2 = np.cos(a2), np.sin(a2)
    stage2 = np.block([[c2, sn2], [-sn2, c2]])
    ag = 2.0 * np.pi * np.outer(np.arange(FOURIER_GROUP_DIM), np.arange(FOURIER_GROUP_DIM)) / FOURIER_GROUP_DIM
    scale = 1.0 / math.sqrt(seq * FOURIER_GROUP_DIM)
    eye = np.eye(LANES // FOURIER_GROUP_DIM)
    group = np.stack([np.kron(eye, np.cos(ag)), np.kron(eye, np.sin(ag))]) * scale
    as_bf16 = lambda a: jnp.asarray(a, F32).astype(BF16)
    return as_bf16(stage1), as_bf16(stage2), as_bf16(group)


def _fourier_a_kernel(u_ref, w_ref, br_ref, bi_ref, u_sc, br_sc, bi_sc, *, n1):
    rows = n1 * DFT_STEP
    halves = M_FOURIER // LANES
    for h in range(halves):
        u_sc[h] = u_ref[:, :, h * LANES:(h + 1) * LANES].reshape(rows, LANES)
    for j in range(DFT_STEP):
        pick = pl.ds(j, n1, stride=DFT_STEP)
        for h in range(halves):
            res = _dot(w_ref[j], u_sc[h, pick, :].astype(BF16))
            br_sc[h, pick, :] = res[:n1]
            bi_sc[h, pick, :] = res[n1:]
    for h in range(halves):
        br_ref[:, :, h * LANES:(h + 1) * LANES] = br_sc[h].reshape(n1, DFT_STEP, LANES)
        bi_ref[:, :, h * LANES:(h + 1) * LANES] = bi_sc[h].reshape(n1, DFT_STEP, LANES)


def _fourier_c_kernel(br_ref, bi_ref, w2_ref, wg_ref, y_ref, y_sc):
    n2 = DFT_N2
    for h in range(M_FOURIER // LANES):
        cols = slice(h * LANES, (h + 1) * LANES)
        stacked = jnp.concatenate(
            [jnp.concatenate([br_ref[j * n2:(j + 1) * n2, cols], bi_ref[j * n2:(j + 1) * n2, cols]], axis=0)
             for j in range(DFT_STEP)], axis=1).astype(BF16)
        z = _dot(w2_ref[...], stacked).astype(BF16)
        zr = jnp.concatenate([z[:n2, j * LANES:(j + 1) * LANES] for j in range(DFT_STEP)], axis=0)
        zi = jnp.concatenate([z[n2:, j * LANES:(j + 1) * LANES] for j in range(DFT_STEP)], axis=0)
        y = _dot(zr, wg_ref[0]) + _dot(zi, wg_ref[1])
        for j in range(DFT_STEP):
            y_sc[pl.ds(j, n2, stride=DFT_STEP), :] = y[j * n2:(j + 1) * n2, :]
        y_ref[:, :, cols] = y_sc[...].reshape(n2, DFT_STEP, LANES)


def _fourier(zf, bsz, seq):
    n1, n2 = seq // DFT_N2, DFT_N2
    stage1, stage2, group = _dft_constants(seq)
    halves = M_FOURIER // LANES
    blk_a = pl.BlockSpec((None, n1, DFT_STEP, M_FOURIER), lambda b, j: (b, 0, j, 0))
    b_shape = jax.ShapeDtypeStruct((bsz, n1, n2, M_FOURIER), F32)
    br, bi = pl.pallas_call(
        functools.partial(_fourier_a_kernel, n1=n1),
        grid=(bsz, n2 // DFT_STEP),
        in_specs=[blk_a, pl.BlockSpec((DFT_STEP, 2 * n1, n1), lambda b, j: (j, 0, 0))],
        out_specs=[blk_a, blk_a],
        out_shape=[b_shape, b_shape],
        scratch_shapes=[pltpu.VMEM((halves, n1 * DFT_STEP, LANES), F32)] * 3,
        compiler_params=_params("parallel", "parallel"),
        name="fourier_stage1",
    )(zf.reshape(bsz, n1, n2, M_FOURIER), stage1)
    blk_c = pl.BlockSpec((None, DFT_STEP * n2, M_FOURIER), lambda b, j: (b, j, 0))
    y = pl.pallas_call(
        _fourier_c_kernel,
        grid=(bsz, n1 // DFT_STEP),
        in_specs=[blk_c, blk_c, pl.BlockSpec((2 * n2, 2 * n2), lambda b, j: (0, 0)),
                  pl.BlockSpec((2, LANES, LANES), lambda b, j: (0, 0, 0))],
        out_specs=pl.BlockSpec((None, n2, DFT_STEP, M_FOURIER), lambda b, j: (b, 0, j, 0)),
        out_shape=jax.ShapeDtypeStruct((bsz, n2, n1, M_FOURIER), F32),
        scratch_shapes=[pltpu.VMEM((n2 * DFT_STEP, LANES), F32)],
        compiler_params=_params("parallel", "parallel"),
        name="fourier_stage2",
    )(br.reshape(bsz, n1 * n2, M_FOURIER), bi.reshape(bsz, n1 * n2, M_FOURIER), stage2, group)
    return y.reshape(bsz * seq, M_FOURIER)


def _s5_prep_kernel(lamc_re, lamc_im, ldt_ref, lam4_re, lam4_im, ldt4_ref, b_re, b_im, c_re, c_im, c4_re, c4_im,
                    d_ref, toep_ref, bend_ref, cout_ref, laml_ref):
    L, P, G = S5_CHUNK, S5_STATE, S5_GROUP
    W = 2 * L * G
    shift = G.bit_length() - 1

    def cmul(ar, ai, br, bi):
        return ar * br - ai * bi, ar * bi + ai * br

    def power(lre, lim, dt, tau):
        mag = jnp.exp(tau * (lre * dt))
        ang = tau * (lim * dt)
        return mag * jnp.cos(ang), mag * jnp.sin(ang)

    def zoh_factor(lre, lim, dt):
        lbr, lbi = power(lre, lim, dt, 1.0)
        den = lre * lre + lim * lim
        return ((lbr - 1.0) * lre + lbi * lim) / den, (lbi * lre - (lbr - 1.0) * lim) / den

    lane_g = lax.broadcasted_iota(jnp.int32, (G, W), 1)
    row_g = lax.broadcasted_iota(jnp.int32, (G, W), 0)
    rep = jnp.where((lane_g & (G - 1)) == row_g, 1.0, 0.0).astype(BF16)
    expo_rows = lax.broadcasted_iota(jnp.int32, (LANES, W), 0)
    lag_w = (L - 1) - (lax.broadcasted_iota(jnp.int32, (LANES, W), 1) >> shift)
    expo_half = lax.broadcasted_iota(jnp.int32, (LANES, L * G), 0)
    step_half = lax.broadcasted_iota(jnp.int32, (LANES, L * G), 1) >> shift
    expo_tab = jnp.minimum(lax.broadcasted_iota(jnp.int32, (P, LANES), 1), L).astype(F32)

    d_tiled = _dot_sel(jnp.broadcast_to(d_ref[0], (G, G)), rep)
    gen = jnp.where(((lane_g & (G - 1)) == row_g) & ((lane_g >> shift) == L - 1), d_tiled, 0.0)
    for d in range(2):
        dt = jnp.exp(ldt_ref[d, 0])
        lr, li = lamc_re[d, 0], lamc_im[d, 0]
        fr, fi = zoh_factor(lr, li, dt)
        bbr, bbi = cmul(fr, fi, b_re[d, 0], b_im[d, 0])
        btr, bti = _dot_sel(bbr, rep), _dot_sel(bbi, rep)
        tab_r, tab_i = power(lr, li, dt, expo_tab)
        if d == 0:
            pick = (expo_rows == lag_w) & (lag_w >= 0)
        else:
            pick = (expo_rows == -lag_w) & (lag_w <= 0) & (lag_w > -L)
        spread = jnp.where(pick, 1.0, 0.0).astype(BF16)
        mr, mi = cmul(_dot_sel(tab_r, spread), _dot_sel(tab_i, spread), btr, bti)
        gen = gen + _dot_hi(c_re[d, 0], mr) - _dot_hi(c_im[d, 0], mi)
        half = L * G
        spread_s = jnp.where(expo_half == (((L - 1) - step_half) if d == 0 else step_half), 1.0, 0.0).astype(BF16)
        xr, xi = cmul(_dot_sel(tab_r, spread_s), _dot_sel(tab_i, spread_s), btr[:, :half], bti[:, :half])
        bend_ref[0, 2 * d] = xr.astype(BF16)
        bend_ref[0, 2 * d + 1] = xi.astype(BF16)

    for t in range(L):
        a = (L - 1 - t) * G
        toep_ref[0, t * G:(t + 1) * G, :] = gen[:, a:a + L * G].astype(BF16)

    lane4 = lax.broadcasted_iota(jnp.int32, (L, 4 * P), 1)
    step4 = lax.broadcasted_iota(jnp.int32, (L, 4 * P), 0)
    is_fwd = lane4 < 2 * P
    dt4 = jnp.exp(ldt4_ref[0])
    pr, pi = power(lam4_re[0], lam4_im[0], dt4, jnp.where(is_fwd, step4 + 1, L - step4).astype(F32))
    cr, ci = c4_re[0], c4_im[0]
    plane_bit = P.bit_length() - 1
    re_c = ((lax.broadcasted_iota(jnp.int32, (G, 4 * P), 1) >> plane_bit) & 1) == 0
    for t in range(L):
        re_part, im_part = cmul(cr, ci, pr[t:t + 1], pi[t:t + 1])
        cout_ref[0, t * G:(t + 1) * G, :] = jnp.where(re_c, re_part, -im_part).astype(BF16)
    re_1 = ((lax.broadcasted_iota(jnp.int32, (1, 4 * P), 1) >> plane_bit) & 1) == 0
    lr, li = power(lam4_re[0], lam4_im[0], dt4, float(L))
    laml_ref[0] = jnp.where(re_1, lr, li)


def _s5_prep(lam_re, lam_im, log_dt, b_re, b_im, c_re, c_im, d_skip):
    ng, P, G, L = S5_GROUPS, S5_STATE, S5_GROUP, S5_CHUNK
    tile4 = lambda a: jnp.concatenate([a[0], a[0], a[1], a[1]], axis=-1)
    ldt4 = jnp.repeat(tile4(log_dt[..., None]), P, axis=-1)[:, None, :]
    spec = lambda *tail: pl.BlockSpec((2, 1) + tail, lambda g: (0, g) + (0,) * len(tail))
    per_g = lambda *tail: pl.BlockSpec((1,) + tail, lambda g: (g,) + (0,) * len(tail))
    return pl.pallas_call(
        _s5_prep_kernel,
        grid=(ng,),
        in_specs=[spec(P, 1), spec(P, 1), spec(1, 1), per_g(1, 4 * P), per_g(1, 4 * P), per_g(1, 4 * P),
                  spec(P, G), spec(P, G), spec(G, P), spec(G, P), per_g(G, 4 * P), per_g(G, 4 * P), per_g(1, G)],
        out_specs=[per_g(L * G, L * G), per_g(4, P, L * G), per_g(L * G, 4 * P), per_g(1, 4 * P)],
        out_shape=[jax.ShapeDtypeStruct((ng, L * G, L * G), BF16),
                   jax.ShapeDtypeStruct((ng, 4, P, L * G), BF16),
                   jax.ShapeDtypeStruct((ng, L * G, 4 * P), BF16),
                   jax.ShapeDtypeStruct((ng, 1, 4 * P), F32)],
        compiler_params=_params("parallel"),
        name="s5_prep",
    )(lam_re[..., None], lam_im[..., None], log_dt[:, :, None, None],
      tile4(lam_re)[:, None, :], tile4(lam_im)[:, None, :], ldt4,
      b_re, b_im, c_re, c_im, tile4(c_re), tile4(c_im), d_skip[:, None, :])


def _s5_sums_kernel(lo_ref, hi_ref, bend_ref, ut_ref, et_ref, *, nchunk):
    L, P, G = S5_CHUNK, S5_STATE, S5_GROUP
    per_half = LANES // G
    for half, z_ref in enumerate((lo_ref, hi_ref)):
        for s in range(L):
            zt = z_ref[pl.ds(s, nchunk, stride=L), :].T.astype(BF16)
            for gl in range(per_half):
                ut_ref[half * per_half + gl, s * G:(s + 1) * G, :] = zt[gl * G:(gl + 1) * G, :]
    for g in range(S5_GROUPS):
        for plane in range(4):
            row = plane * S5_GROUPS * P + g * P
            et_ref[row:row + P, :] = _dot(bend_ref[g, plane], ut_ref[g])


def _s5_scan_kernel(et_ref, lam_ref, xt_ref, e_sc, x_sc, *, bsz, steps):
    width = S5_GROUPS * S5_STATE
    for plane in range(4):
        e_sc[plane] = et_ref[plane * width:(plane + 1) * width, :].T
    ar_f, ai_f, ar_b, ai_b = lam_ref[0], lam_ref[1], lam_ref[2], lam_ref[3]
    zero = jnp.zeros_like(ar_f)
    for b in range(bsz):
        def body(i, carry):
            xr, xi, yr, yi = carry
            rf = b * steps + i
            rb = b * steps + steps - 1 - i
            x_sc[0, pl.ds(rf, 1), :] = xr
            x_sc[1, pl.ds(rf, 1), :] = xi
            x_sc[2, pl.ds(rb, 1), :] = yr
            x_sc[3, pl.ds(rb, 1), :] = yi
            xr, xi = (ar_f * xr - ai_f * xi + e_sc[0, pl.ds(rf, 1), :],
                      ar_f * xi + ai_f * xr + e_sc[1, pl.ds(rf, 1), :])
            yr, yi = (ar_b * yr - ai_b * yi + e_sc[2, pl.ds(rb, 1), :],
                      ar_b * yi + ai_b * yr + e_sc[3, pl.ds(rb, 1), :])
            return xr, xi, yr, yi
        lax.fori_loop(0, steps, body, (zero, zero, zero, zero))
    for plane in range(4):
        xt_ref[plane * width:(plane + 1) * width, :] = x_sc[plane].T


def _gelu_tanh(x):
    return 0.5 * x * (1.0 + jnp.tanh(math.sqrt(2.0 / math.pi) * (x + 0.044715 * (x * x * x))))


def _s5_out_kernel(ut_ref, xt_ref, toep_ref, cout_ref, lo_ref, hi_ref, yt_sc, *, nchunk):
    L, P, G = S5_CHUNK, S5_STATE, S5_GROUP
    width = S5_GROUPS * P
    for g in range(S5_GROUPS):
        xg = jnp.concatenate([xt_ref[plane * width + g * P:plane * width + (g + 1) * P, :] for plane in range(4)],
                             axis=0).astype(BF16)
        yt = _gelu_tanh(_dot(toep_ref[g], ut_ref[g]) + _dot(cout_ref[g], xg))
        for t in range(L):
            yt_sc[t, g * G:(g + 1) * G, :] = yt[t * G:(t + 1) * G, :]
    for t in range(L):
        y = yt_sc[t].T
        lo_ref[pl.ds(t, nchunk, stride=L), :] = y[:, :LANES]
        hi_ref[pl.ds(t, nchunk, stride=L), :] = y[:, LANES:]


def _s5(zs, prep, bsz, seq):
    ng, P, G, L = S5_GROUPS, S5_STATE, S5_GROUP, S5_CHUNK
    toep, bend, cout, laml = prep
    laml = laml.reshape(ng, 4, P).transpose(1, 0, 2).reshape(4, 1, ng * P)
    t = bsz * seq
    tile = min(S5_TILE, t)
    nct = tile // L
    nchunk = t // L
    ut_spec = pl.BlockSpec((ng, L * G, nct), lambda i: (0, 0, i))
    plane_spec = pl.BlockSpec((4 * ng * P, nct), lambda i: (0, i))
    plane_shape = jax.ShapeDtypeStruct((4 * ng * P, nchunk), F32)
    whole = lambda a: pl.BlockSpec(a.shape, lambda i: (0,) * a.ndim)
    ut, et = pl.pallas_call(
        functools.partial(_s5_sums_kernel, nchunk=nct),
        grid=(t // tile,),
        in_specs=[pl.BlockSpec((tile, LANES), lambda i: (i, 0)), pl.BlockSpec((tile, LANES), lambda i: (i, 1)),
                  whole(bend)],
        out_specs=[ut_spec, plane_spec],
        out_shape=[jax.ShapeDtypeStruct((ng, L * G, nchunk), BF16), plane_shape],
        compiler_params=_params("parallel"),
        name="s5_chunk_sums",
    )(zs, zs, bend)
    xt = pl.pallas_call(
        functools.partial(_s5_scan_kernel, bsz=bsz, steps=seq // L),
        out_shape=plane_shape,
        scratch_shapes=[pltpu.VMEM((4, nchunk, ng * P), F32), pltpu.VMEM((4, nchunk, ng * P), F32)],
        compiler_params=pltpu.CompilerParams(vmem_limit_bytes=VMEM_LIMIT),
        name="s5_state_scan",
    )(et, laml)
    half_spec = pl.BlockSpec((tile, LANES), lambda i: (i, 0))
    half_shape = jax.ShapeDtypeStruct((t, LANES), F32)
    return pl.pallas_call(
        functools.partial(_s5_out_kernel, nchunk=nct),
        grid=(t // tile,),
        in_specs=[ut_spec, plane_spec, whole(toep), whole(cout)],
        out_specs=[half_spec, half_spec],
        out_shape=[half_shape, half_shape],
        scratch_shapes=[pltpu.VMEM((L, M_S5, nct), F32)],
        compiler_params=_params("parallel"),
        name="s5_outputs",
    )(ut, xt, toep, cout)


def _merge_kernel(x_ref, hf_ref, hb_ref, zo_ref, yf_ref, ys_lo_ref, ys_hi_ref, gpre_ref, wgate_ref, bgate_ref,
                  ng_ref, wm_ref, wf_ref, wglu_ref, bglu_ref, wout_ref, gpost_ref, o_ref, mixed_sc):
    x = x_ref[...]
    h = _rms(x, gpre_ref[...]).astype(BF16)
    hs = hf_ref[...] + hb_ref[...]
    parts = []
    for hd in range(HEADS):
        blk = hs[:, hd * HEAD_DIM:(hd + 1) * HEAD_DIM]
        mu = jnp.mean(blk, axis=-1, keepdims=True)
        cen = blk - mu
        var = jnp.mean(cen * cen, axis=-1, keepdims=True)
        parts.append(cen * lax.rsqrt(var + EPS))
    hm = (jnp.concatenate(parts, axis=1) * ng_ref[...] * _sigmoid(zo_ref[...].astype(F32))).astype(BF16)
    yf = yf_ref[...].astype(BF16)
    ys = jnp.concatenate([ys_lo_ref[...].astype(BF16), ys_hi_ref[...].astype(BF16)], axis=1)
    for n in range(D_MODEL // MERGE_COLS):
        lo = n * MERGE_COLS
        cols = slice(lo, lo + MERGE_COLS)

        def gate(i):
            gcols = slice(i * D_MODEL + lo, i * D_MODEL + lo + MERGE_COLS)
            return _sigmoid(_dot(h, wgate_ref[:, gcols]) + bgate_ref[:, gcols])

        lin = _dot(ys, wglu_ref[:, cols]) + bglu_ref[:, cols]
        gcols = slice(D_MODEL + lo, D_MODEL + lo + MERGE_COLS)
        y_s = lin * _sigmoid(_dot(ys, wglu_ref[:, gcols]) + bglu_ref[:, gcols])
        mixed = gate(0) * _dot(hm, wm_ref[:, cols]) + gate(1) * _dot(yf, wf_ref[:, cols]) + gate(2) * y_s
        mixed_sc[:, cols] = mixed.astype(BF16)
    o_ref[...] = x + _rms(_dot(mixed_sc[...], wout_ref[...]), gpost_ref[...])


def _merge(x2, hf, hb, zo, yf, ys_lo, ys_hi, g_pre, w_gate, b_gate, norm_g, w_up_m, w_up_f, w_glu, b_glu, w_out,
           g_post, tile):
    t = x2.shape[0]
    rows = lambda width: pl.BlockSpec((tile, width), lambda i: (i, 0))
    const = lambda a: pl.BlockSpec(a.shape, lambda i: (0, 0))
    args = [g_pre[None, :], w_gate.astype(BF16), b_gate[None, :], norm_g[None, :], w_up_m.astype(BF16),
            w_up_f.astype(BF16), w_glu.astype(BF16), b_glu[None, :], w_out.astype(BF16), g_post[None, :]]
    return pl.pallas_call(
        _merge_kernel,
        grid=(t // tile,),
        in_specs=[rows(D_MODEL), rows(M_MLSTM), rows(M_MLSTM), rows(M_MLSTM), rows(M_FOURIER),
                  rows(LANES), rows(LANES)] + [const(a) for a in args],
        out_specs=rows(D_MODEL),
        out_shape=jax.ShapeDtypeStruct((t, D_MODEL), F32),
        scratch_shapes=[pltpu.VMEM((tile, D_MODEL), BF16)],
        compiler_params=_params("parallel"),
        name="merge",
    )(x2, hf, hb, zo, yf, ys_lo, ys_hi, *args)


def _ffn_kernel(x_ref, gpre_ref, w1_ref, w2_ref, gpost_ref, o_ref, *, n_split):
    x = x_ref[...]
    h = _rms(x, gpre_ref[...]).astype(BF16)
    width = D_FF // n_split
    f = None
    for j in range(n_split):
        a = jnp.maximum(_dot(h, w1_ref[:, j * width:(j + 1) * width]), 0.0)
        part = _dot((a * a).astype(BF16), w2_ref[j * width:(j + 1) * width, :])
        f = part if f is None else f + part
    o_ref[...] = x + _rms(f, gpost_ref[...])


def _ffn(x2, g_pre, w1, w2, g_post, tile):
    t = x2.shape[0]
    rows = pl.BlockSpec((tile, D_MODEL), lambda i: (i, 0))
    const = lambda shape: pl.BlockSpec(shape, lambda i: (0, 0))
    return pl.pallas_call(
        functools.partial(_ffn_kernel, n_split=4),
        grid=(t // tile,),
        in_specs=[rows, const((1, D_MODEL)), const((D_MODEL, D_FF)), const((D_FF, D_MODEL)), const((1, D_MODEL))],
        out_specs=rows,
        out_shape=jax.ShapeDtypeStruct((t, D_MODEL), F32),
        compiler_params=_params("parallel"),
        name="ffn",
    )(x2, g_pre[None, :], w1.astype(BF16), w2.astype(BF16), g_post[None, :])


def kernel(x, g_mix_pre, g_mix_post, g_ffn_pre, g_ffn_post, w_in, b_in, conv_w, conv_b, mlstm_norm_g, w_up_mlstm, w_up_fourier, s5_lam_re, s5_lam_im, s5_log_dt, s5_b_re, s5_b_im, s5_c_re, s5_c_im, s5_d, w_glu, b_glu, w_out, w_ffn1, w_ffn2):
    bsz, seq, _ = x.shape
    depth = w_in.shape[0]
    t = bsz * seq
    tile = min(512, t)
    x2 = x.reshape(t, D_MODEL)
    for l in range(depth):
        qk, zv, zo, zf, zs, zg, zgt = _inproj(x2, g_mix_pre[l], w_in[l], b_in[l], conv_w[l], conv_b[l], tile, seq)
        hf, hb = _mlstm(qk, zv, zg, zgt, bsz, seq)
        yf = _fourier(zf, bsz, seq)
        prep = _s5_prep(s5_lam_re[l], s5_lam_im[l], s5_log_dt[l], s5_b_re[l], s5_b_im[l], s5_c_re[l], s5_c_im[l],
                        s5_d[l])
        ys_lo, ys_hi = _s5(zs, prep, bsz, seq)
        x2 = _merge(x2, hf, hb, zo, yf, ys_lo, ys_hi, g_mix_pre[l], w_in[l][:, OFF_GATE:], b_in[l][OFF_GATE:],
                    mlstm_norm_g[l], w_up_mlstm[l], w_up_fourier[l], w_glu[l], b_glu[l], w_out[l],
                    g_mix_post[l], tile)
        x2 = _ffn(x2, g_ffn_pre[l], w_ffn1[l], w_ffn2[l], g_ffn_post[l], tile)
    return x2.reshape(bsz, seq, D_MODEL)
```

```python
import functools
import itertools
import math

import numpy as np
import jax
import jax.numpy as jnp
from jax import lax
from jax.experimental import pallas as pl
from jax.experimental.pallas import tpu as pltpu

F32 = jnp.float32
BF16 = jnp.bfloat16
HIGHEST = lax.Precision.HIGHEST

LANES = 128
D_MODEL = 1024
M_MLSTM = 512
HEADS = 4
HEAD_DIM = 128
CHUNK = 128
MLSTM_SUB = 4
CONV_WIDTH = 5
CONV_COLS = 256
CONV_HALO = 8
M_FOURIER = 256
FOURIER_GROUP_DIM = 64
M_S5 = 256
S5_GROUP = 16
S5_GROUPS = 16
S5_STATE = 64
S5_CHUNK = 32
S5_TILE = 4096
N_BRANCHES = 3
MERGE_COLS = 256
D_FF = 4 * D_MODEL
EPS = 1e-6

OFF_Q = 0
OFF_V = 2 * M_MLSTM
OFF_O = 3 * M_MLSTM
OFF_IG = 4 * M_MLSTM
OFF_FOURIER = OFF_IG + 4 * HEADS
OFF_S5 = OFF_FOURIER + M_FOURIER
OFF_GATE = OFF_S5 + M_S5

DFT_N2 = 64
DFT_STEP = 8

VMEM_LIMIT = 56 * 1024 * 1024

NT_DIMS = (((1,), (1,)), ((), ()))
TN_DIMS = (((0,), (0,)), ((), ()))


def _params(*sem):
    return pltpu.CompilerParams(dimension_semantics=sem, vmem_limit_bytes=VMEM_LIMIT)


def _layer_spec(a, l):
    return pl.BlockSpec((None,) + a.shape[1:], lambda *_: (l,) + (0,) * (a.ndim - 1))


def _rms(x, g):
    return x * lax.rsqrt(jnp.mean(x * x, axis=-1, keepdims=True) + EPS) * g


def _sigmoid(x):
    return 1.0 / (1.0 + jnp.exp(-x))


def _dot(a, b):
    return jnp.dot(a, b, preferred_element_type=F32)


def _dot_hi(a, b):
    return jnp.dot(a, b, precision=HIGHEST, preferred_element_type=F32)


def _split3(a):
    hi = a.astype(BF16)
    rest = a - hi.astype(F32)
    mid = rest.astype(BF16)
    return hi, mid, (rest - mid.astype(F32)).astype(BF16)


def _dot_sel(a, sel):
    hi, mid, lo = _split3(a)
    return _dot(hi, sel) + _dot(mid, sel) + _dot(lo, sel)


def _sel_dot(sel, a):
    hi, mid, lo = _split3(a)
    return _dot(sel, hi) + _dot(sel, mid) + _dot(sel, lo)


def _inproj_kernel(x_ref, xp_ref, xn_ref, g_ref, wm_ref, bm_ref, wg_ref, bg_ref, wgt_ref, bgt_ref, cw_ref, cb_ref,
                   qk_ref, zv_ref, zo_ref, zf_ref, zs_ref, zg_ref, zgt_ref, ext_sc, *, tiles_per_seq):
    i = pl.program_id(0)
    tile = x_ref.shape[0]
    pad = CONV_WIDTH // 2
    h = _rms(x_ref[...], g_ref[...]).astype(BF16)
    h_halo = _rms(jnp.concatenate([xp_ref[...], xn_ref[...]], axis=0), g_ref[...]).astype(BF16)
    first = (i % tiles_per_seq) == 0
    last = (i % tiles_per_seq) == tiles_per_seq - 1

    def proj(hh, lo, hi):
        return _dot(hh, wm_ref[:, lo:hi]) + bm_ref[:, lo:hi]

    for n in range(2 * M_MLSTM // CONV_COLS):
        lo = n * CONV_COLS
        cols = slice(lo, lo + CONV_COLS)
        z_halo = proj(h_halo, OFF_Q + lo, OFF_Q + lo + CONV_COLS)
        ext_sc[0:CONV_HALO, cols] = jnp.where(first, 0.0, z_halo[:CONV_HALO])
        ext_sc[CONV_HALO:CONV_HALO + tile, cols] = proj(h, OFF_Q + lo, OFF_Q + lo + CONV_COLS)
        ext_sc[CONV_HALO + tile:, cols] = jnp.where(last, 0.0, z_halo[CONV_HALO:])
        acc = cb_ref[:, cols] + cw_ref[0:1, cols] * ext_sc[CONV_HALO - pad:CONV_HALO - pad + tile, cols]
        for j in range(1, CONV_WIDTH):
            acc = acc + cw_ref[j:j + 1, cols] * ext_sc[CONV_HALO - pad + j:CONV_HALO - pad + j + tile, cols]
        qk = acc * _sigmoid(acc)
        if lo < M_MLSTM:
            qk = qk * (HEAD_DIM ** -0.5)
        qk_ref[:, cols] = qk.astype(BF16)

    zv_ref[...] = proj(h, OFF_V, OFF_O).astype(BF16)
    zo_ref[...] = proj(h, OFF_O, OFF_IG).astype(BF16)
    zf_ref[...] = proj(h, OFF_IG, OFF_IG + M_FOURIER)
    zs_ref[...] = proj(h, OFF_IG + M_FOURIER, OFF_IG + M_FOURIER + M_S5)
    zg_ref[...] = _dot(h, wg_ref[...]) + bg_ref[...]
    zgt_ref[...] = lax.dot_general(wgt_ref[...], h, NT_DIMS, preferred_element_type=F32) + bgt_ref[...]


def _inproj(x2, params, l, tile, seq):
    t = x2.shape[0]
    n_gate = 4 * HEADS
    hpt = tile // CONV_HALO
    n_halo = t // CONV_HALO
    rows = lambda width: pl.BlockSpec((tile, width), lambda i: (i, 0))
    cols = lambda height: pl.BlockSpec((height, tile), lambda i: (0, i))
    return pl.pallas_call(
        functools.partial(_inproj_kernel, tiles_per_seq=seq // tile),
        grid=(t // tile,),
        in_specs=[rows(D_MODEL),
                  pl.BlockSpec((CONV_HALO, D_MODEL), lambda i: (jnp.maximum(i * hpt - 1, 0), 0)),
                  pl.BlockSpec((CONV_HALO, D_MODEL), lambda i: (jnp.minimum((i + 1) * hpt, n_halo - 1), 0))]
                 + [_layer_spec(a, l) for a in params],
        out_specs=[rows(2 * M_MLSTM), rows(M_MLSTM), rows(M_MLSTM), rows(M_FOURIER), rows(M_S5),
                   rows(n_gate), cols(n_gate)],
        out_shape=[jax.ShapeDtypeStruct((t, 2 * M_MLSTM), BF16), jax.ShapeDtypeStruct((t, M_MLSTM), BF16),
                   jax.ShapeDtypeStruct((t, M_MLSTM), BF16), jax.ShapeDtypeStruct((t, M_FOURIER), F32),
                   jax.ShapeDtypeStruct((t, M_S5), F32), jax.ShapeDtypeStruct((t, n_gate), F32),
                   jax.ShapeDtypeStruct((n_gate, t), F32)],
        scratch_shapes=[pltpu.VMEM((tile + 2 * CONV_HALO, 2 * M_MLSTM), F32)],
        compiler_params=_params("parallel"),
        name="inproj",
    )(x2, x2, x2, *params)


def _log_sigmoid(x):
    return jnp.minimum(x, 0.0) - jnp.log1p(jnp.exp(-jnp.abs(x)))


def _mlstm_kernel(qk_f, v_f, g_f, gt_f, qk_b, v_b, g_b, gt_b, hf_ref, hb_ref, state_sc, m_sc):
    c = pl.program_id(1)
    L = CHUNK

    @pl.when(c == 0)
    def _():
        state_sc[...] = jnp.zeros_like(state_sc)
        m_sc[...] = jnp.zeros_like(m_sc)

    row = lax.broadcasted_iota(jnp.int32, (L, L), 0)
    col = lax.broadcasted_iota(jnp.int32, (L, L), 1)
    lower = row >= col
    upper = row <= col
    lower_f = jnp.where(lower, 1.0, 0.0).astype(BF16)
    upper_f = jnp.where(upper, 1.0, 0.0).astype(BF16)
    ones_v = jnp.ones((L, HEAD_DIM), F32)

    streams = ((0, qk_f, v_f, g_f, gt_f, hf_ref), (1, qk_b, v_b, g_b, gt_b, hb_ref))
    for sub, (d, qk_ref, v_ref, g_ref, gt_ref, out_ref) in itertools.product(range(MLSTM_SUB), streams):
        r0 = (sub if d == 0 else MLSTM_SUB - 1 - sub) * L
        rows = slice(r0, r0 + L)
        gates = g_ref[rows, :]
        gates_t = gt_ref[:, rows]
        lf_cols = _log_sigmoid(gates[:, 2 * HEADS:])
        lf_rows = _log_sigmoid(gates_t[2 * HEADS:, :])
        if d == 0:
            b_cols = _sel_dot(lower_f, lf_cols)
            b_rows = _dot_sel(lf_rows, upper_f)
            mask = lower
        else:
            b_cols = _sel_dot(upper_f, lf_cols)
            b_rows = _dot_sel(lf_rows, lower_f)
            mask = upper

        for hd in range(HEADS):
            k_idx = d * HEADS + hd
            lo = hd * HEAD_DIM
            q = qk_ref[rows, lo:lo + HEAD_DIM]
            k = qk_ref[rows, M_MLSTM + lo:M_MLSTM + lo + HEAD_DIM]
            v = v_ref[rows, lo:lo + HEAD_DIM]
            b_c = jnp.broadcast_to(b_cols[:, k_idx:k_idx + 1], (L, L))
            ig_c = jnp.broadcast_to(gates[:, k_idx:k_idx + 1], (L, L))
            b_r = b_rows[k_idx:k_idx + 1, :]
            ig_r = gates_t[k_idx:k_idx + 1, :]
            g_tot = b_c[L - 1:L, :] if d == 0 else b_c[0:1, :]
            m_prev = m_sc[k_idx:k_idx + 1, :]

            d_log = jnp.where(mask, b_c - b_r + ig_r, -1e30)
            inter_log = b_c + m_prev
            m_t = jnp.maximum(inter_log, jnp.max(d_log, axis=1, keepdims=True))
            scores = lax.dot_general(q, k, NT_DIMS, preferred_element_type=F32) * jnp.exp(d_log - m_t)
            inter_w = jnp.exp(inter_log - m_t)
            st = state_sc[k_idx]
            v_ext = jnp.concatenate([v, ones_v.astype(BF16)], axis=1)
            res = _dot(scores.astype(BF16), v_ext) + _dot((inter_w * q.astype(F32)).astype(BF16), st.astype(BF16))
            num = res[:, :HEAD_DIM]
            den = res[:, HEAD_DIM:]
            out_ref[rows, lo:lo + HEAD_DIM] = num / jnp.maximum(jnp.abs(den), jnp.exp(-m_t))

            a_c = g_tot - b_c + ig_c
            a_max = jnp.max(a_c, axis=0, keepdims=True)
            w_c = jnp.exp(a_c - a_max)
            vw = jnp.concatenate([v.astype(F32) * w_c, w_c], axis=1).astype(BF16)
            st_loc = lax.dot_general(k, vw, TN_DIMS, preferred_element_type=F32)
            m_new = jnp.maximum(g_tot + m_prev, a_max)
            s_old = jnp.exp(g_tot + m_prev - m_new)
            s_new = jnp.exp(a_max - m_new)
            s_old2 = jnp.concatenate([s_old, s_old], axis=1)
            s_new2 = jnp.concatenate([s_new, s_new], axis=1)
            state_sc[k_idx] = s_old2 * st + s_new2 * st_loc
            m_sc[k_idx:k_idx + 1, :] = m_new


def _mlstm(qk, zv, zg, zgt, bsz, seq):
    t = bsz * seq
    blk = CHUNK * MLSTM_SUB
    nc = seq // blk
    n_gate = 4 * HEADS

    def fwd(b, c):
        return b * nc + c

    def bwd(b, c):
        return b * nc + nc - 1 - c

    def specs(pos):
        return [pl.BlockSpec((blk, 2 * M_MLSTM), lambda b, c: (pos(b, c), 0)),
                pl.BlockSpec((blk, M_MLSTM), lambda b, c: (pos(b, c), 0)),
                pl.BlockSpec((blk, n_gate), lambda b, c: (pos(b, c), 0)),
                pl.BlockSpec((n_gate, blk), lambda b, c: (0, pos(b, c)))]

    out_shape = jax.ShapeDtypeStruct((t, M_MLSTM), F32)
    return pl.pallas_call(
        _mlstm_kernel,
        grid=(bsz, nc),
        in_specs=specs(fwd) + specs(bwd),
        out_specs=[pl.BlockSpec((blk, M_MLSTM), lambda b, c: (fwd(b, c), 0)),
                   pl.BlockSpec((blk, M_MLSTM), lambda b, c: (bwd(b, c), 0))],
        out_shape=[out_shape, out_shape],
        scratch_shapes=[pltpu.VMEM((2 * HEADS, HEAD_DIM, 2 * HEAD_DIM), F32),
                        pltpu.VMEM((2 * HEADS, HEAD_DIM), F32)],
        compiler_params=_params("arbitrary", "arbitrary"),
        name="mlstm",
    )(qk, zv, zg, zgt, qk, zv, zg, zgt)


def _dft_constants(seq):
    n1, n2 = seq // DFT_N2, DFT_N2
    k1 = np.arange(n1)[:, None, None]
    s2 = np.arange(n2)[None, None, :]
    s1 = np.arange(n1)[None, :, None]
    ang = -2.0 * np.pi * ((k1 * (n2 * s1 + s2)) % seq) / seq
    stage1 = np.concatenate([np.cos(ang), np.sin(ang)], axis=0)
    stage1 = np.ascontiguousarray(stage1.transpose(2, 0, 1))
    a2 = 2.0 * np.pi * np.outer(np.arange(n2), np.arange(n2)) / n2
    c2, sn2 = np.cos(a2), np.sin(a2)
    stage2 = np.block([[c2, sn2], [-sn2, c2]])
    ag = 2.0 * np.pi * np.outer(np.arange(FOURIER_GROUP_DIM), np.arange(FOURIER_GROUP_DIM)) / FOURIER_GROUP_DIM
    scale = 1.0 / math.sqrt(seq * FOURIER_GROUP_DIM)
    eye = np.eye(LANES // FOURIER_GROUP_DIM)
    group = np.stack([np.kron(eye, np.cos(ag)), np.kron(eye, np.sin(ag))]) * scale
    as_bf16 = lambda a: jnp.asarray(a, F32).astype(BF16)
    return as_bf16(stage1), as_bf16(stage2), as_bf16(group)


def _fourier_a_kernel(u_ref, w_ref, br_ref, bi_ref, u_sc, br_sc, bi_sc, *, n1):
    rows = n1 * DFT_STEP
    halves = M_FOURIER // LANES
    for h in range(halves):
        u_sc[h] = u_ref[:, :, h * LANES:(h + 1) * LANES].reshape(rows, LANES)
    for j in range(DFT_STEP):
        pick = pl.ds(j, n1, stride=DFT_STEP)
        for h in range(halves):
            res = _dot(w_ref[j], u_sc[h, pick, :].astype(BF16))
            br_sc[h, pick, :] = res[:n1]
            bi_sc[h, pick, :] = res[n1:]
    for h in range(halves):
        br_ref[:, :, h * LANES:(h + 1) * LANES] = br_sc[h].reshape(n1, DFT_STEP, LANES)
        bi_ref[:, :, h * LANES:(h + 1) * LANES] = bi_sc[h].reshape(n1, DFT_STEP, LANES)


def _fourier_c_kernel(br_ref, bi_ref, w2_ref, wg_ref, y_ref, y_sc):
    n2 = DFT_N2
    for h in range(M_FOURIER // LANES):
        cols = slice(h * LANES, (h + 1) * LANES)
        stacked = jnp.concatenate(
            [jnp.concatenate([br_ref[j * n2:(j + 1) * n2, cols], bi_ref[j * n2:(j + 1) * n2, cols]], axis=0)
             for j in range(DFT_STEP)], axis=1).astype(BF16)
        z = _dot(w2_ref[...], stacked).astype(BF16)
        zr = jnp.concatenate([z[:n2, j * LANES:(j + 1) * LANES] for j in range(DFT_STEP)], axis=0)
        zi = jnp.concatenate([z[n2:, j * LANES:(j + 1) * LANES] for j in range(DFT_STEP)], axis=0)
        y = _dot(zr, wg_ref[0]) + _dot(zi, wg_ref[1])
        for j in range(DFT_STEP):
            y_sc[pl.ds(j, n2, stride=DFT_STEP), :] = y[j * n2:(j + 1) * n2, :]
        y_ref[:, :, cols] = y_sc[...].reshape(n2, DFT_STEP, LANES)


def _fourier(zf, bsz, seq):
    n1, n2 = seq // DFT_N2, DFT_N2
    stage1, stage2, group = _dft_constants(seq)
    halves = M_FOURIER // LANES
    blk_a = pl.BlockSpec((None, n1, DFT_STEP, M_FOURIER), lambda b, j: (b, 0, j, 0))
    b_shape = jax.ShapeDtypeStruct((bsz, n1, n2, M_FOURIER), F32)
    br, bi = pl.pallas_call(
        functools.partial(_fourier_a_kernel, n1=n1),
        grid=(bsz, n2 // DFT_STEP),
        in_specs=[blk_a, pl.BlockSpec((DFT_STEP, 2 * n1, n1), lambda b, j: (j, 0, 0))],
        out_specs=[blk_a, blk_a],
        out_shape=[b_shape, b_shape],
        scratch_shapes=[pltpu.VMEM((halves, n1 * DFT_STEP, LANES), F32)] * 3,
        compiler_params=_params("parallel", "parallel"),
        name="fourier_stage1",
    )(zf.reshape(bsz, n1, n2, M_FOURIER), stage1)
    blk_c = pl.BlockSpec((None, DFT_STEP * n2, M_FOURIER), lambda b, j: (b, j, 0))
    y = pl.pallas_call(
        _fourier_c_kernel,
        grid=(bsz, n1 // DFT_STEP),
        in_specs=[blk_c, blk_c, pl.BlockSpec((2 * n2, 2 * n2), lambda b, j: (0, 0)),
                  pl.BlockSpec((2, LANES, LANES), lambda b, j: (0, 0, 0))],
        out_specs=pl.BlockSpec((None, n2, DFT_STEP, M_FOURIER), lambda b, j: (b, 0, j, 0)),
        out_shape=jax.ShapeDtypeStruct((bsz, n2, n1, M_FOURIER), F32),
        scratch_shapes=[pltpu.VMEM((n2 * DFT_STEP, LANES), F32)],
        compiler_params=_params("parallel", "parallel"),
        name="fourier_stage2",
    )(br.reshape(bsz, n1 * n2, M_FOURIER), bi.reshape(bsz, n1 * n2, M_FOURIER), stage2, group)
    return y.reshape(bsz * seq, M_FOURIER)


def _s5_constants():
    L, G = S5_CHUNK, S5_GROUP
    lane = np.arange(2 * L * G)
    expo = np.arange(LANES)[:, None]
    rep = (lane[None, :] % G == np.arange(G)[:, None])
    lag = (L - 1) - lane // G
    spread = np.stack([(expo == lag) & (lag >= 0), (expo == -lag) & (lag <= 0) & (lag > -L)])
    step = lane[:L * G] // G
    spread_s = np.stack([expo == (L - 1) - step, expo == step])
    return tuple(jnp.asarray(m, BF16) for m in (rep, spread, spread_s))


def _s5_prep_kernel(lamc_re, lamc_im, ldt_ref, lam4_re, lam4_im, ldt4_ref, b_re, b_im, c_re, c_im, c4_re, c4_im,
                    d_ref, rep_ref, spread_ref, spread_s_ref, toep_ref, bend_ref, cout_ref, laml_ref):
    L, P, G = S5_CHUNK, S5_STATE, S5_GROUP
    W = 2 * L * G
    shift = G.bit_length() - 1
    nbits = L.bit_length()

    def cmul(ar, ai, br, bi):
        return ar * br - ai * bi, ar * bi + ai * br

    def lam_bar(lre, lim, dt):
        mag = jnp.exp(lre * dt)
        return mag * jnp.cos(lim * dt), mag * jnp.sin(lim * dt)

    def power_table(base_r, base_i, expo):
        tr = jnp.ones(expo.shape, F32)
        ti = jnp.zeros(expo.shape, F32)
        sr, si = base_r, base_i
        for bit in range(nbits):
            nr, ni = cmul(tr, ti, sr, si)
            has = ((expo >> bit) & 1) == 1
            tr, ti = jnp.where(has, nr, tr), jnp.where(has, ni, ti)
            sr, si = cmul(sr, si, sr, si)
        return tr, ti

    def spread_dot(xr, xi, sel):
        n = xr.shape[0]
        out = _dot(jnp.concatenate(_split3(xr) + _split3(xi), axis=0), sel)
        return out[:n] + out[n:2 * n] + out[2 * n:3 * n], out[3 * n:4 * n] + out[4 * n:5 * n] + out[5 * n:]

    def dot3(c, m):
        c_hi, c_mid, _ = _split3(c)
        m_hi, m_mid, _ = _split3(m)
        return _dot(c_hi, m_hi) + _dot(c_mid, m_hi) + _dot(c_hi, m_mid)

    rep = rep_ref[...]
    lane_g = lax.broadcasted_iota(jnp.int32, (G, W), 1)
    row_g = lax.broadcasted_iota(jnp.int32, (G, W), 0)
    expo_tab = jnp.minimum(lax.broadcasted_iota(jnp.int32, (P, LANES), 1), L)

    d_tiled = _dot_sel(jnp.broadcast_to(d_ref[0], (G, G)), rep)
    gen = jnp.where(((lane_g & (G - 1)) == row_g) & ((lane_g >> shift) == L - 1), d_tiled, 0.0)
    for d in range(2):
        dt = jnp.exp(ldt_ref[d, 0])
        lr, li = lamc_re[d, 0], lamc_im[d, 0]
        lbr, lbi = lam_bar(lr, li, dt)
        den = lr * lr + li * li
        fr, fi = ((lbr - 1.0) * lr + lbi * li) / den, (lbi * lr - (lbr - 1.0) * li) / den
        bbr, bbi = cmul(fr, fi, b_re[d, 0], b_im[d, 0])
        btr, bti = spread_dot(bbr, bbi, rep)
        tab_r, tab_i = power_table(lbr, lbi, expo_tab)
        pr, pi = spread_dot(tab_r, tab_i, spread_ref[d])
        mr, mi = cmul(pr, pi, btr, bti)
        gen = gen + dot3(c_re[d, 0], mr) - dot3(c_im[d, 0], mi)
        half = L * G
        qr, qi = spread_dot(tab_r, tab_i, spread_s_ref[d])
        xr, xi = cmul(qr, qi, btr[:, :half], bti[:, :half])
        bend_ref[0, 2 * d] = xr.astype(BF16)
        bend_ref[0, 2 * d + 1] = xi.astype(BF16)

    for t in range(L):
        a = (L - 1 - t) * G
        toep_ref[0, t * G:(t + 1) * G, :] = gen[:, a:a + L * G].astype(BF16)

    lane4 = lax.broadcasted_iota(jnp.int32, (L, 4 * P), 1)
    step4 = lax.broadcasted_iota(jnp.int32, (L, 4 * P), 0)
    lb4r, lb4i = lam_bar(lam4_re[0], lam4_im[0], jnp.exp(ldt4_ref[0]))
    pr, pi = power_table(lb4r, lb4i, jnp.where(lane4 < 2 * P, step4 + 1, L - step4))
    cr, ci = c4_re[0], c4_im[0]
    plane_bit = P.bit_length() - 1
    re_c = ((lax.broadcasted_iota(jnp.int32, (G, 4 * P), 1) >> plane_bit) & 1) == 0
    for t in range(L):
        re_part, im_part = cmul(cr, ci, pr[t:t + 1], pi[t:t + 1])
        cout_ref[0, t * G:(t + 1) * G, :] = jnp.where(re_c, re_part, -im_part).astype(BF16)
    re_1 = ((lax.broadcasted_iota(jnp.int32, (1, 4 * P), 1) >> plane_bit) & 1) == 0
    lr, li = power_table(lb4r, lb4i, jnp.full((1, 4 * P), L, jnp.int32))
    laml_ref[0] = jnp.where(re_1, lr, li)


def _s5_prep(lam_re, lam_im, log_dt, b_re, b_im, c_re, c_im, d_skip):
    ng, P, G, L = S5_GROUPS, S5_STATE, S5_GROUP, S5_CHUNK
    depth = lam_re.shape[0]
    tile4 = lambda a: jnp.concatenate([a[:, 0], a[:, 0], a[:, 1], a[:, 1]], axis=-1)
    ldt4 = jnp.repeat(tile4(log_dt[..., None]), P, axis=-1)[:, :, None, :]
    consts = _s5_constants()
    spec = lambda *tail: pl.BlockSpec((None, 2, 1) + tail, lambda l, g: (l, 0, g) + (0,) * len(tail))
    per_g = lambda *tail: pl.BlockSpec((None, 1) + tail, lambda l, g: (l, g) + (0,) * len(tail))
    whole = lambda a: pl.BlockSpec(a.shape, lambda l, g: (0,) * a.ndim)
    return pl.pallas_call(
        _s5_prep_kernel,
        grid=(depth, ng),
        in_specs=[spec(P, 1), spec(P, 1), spec(1, 1), per_g(1, 4 * P), per_g(1, 4 * P), per_g(1, 4 * P),
                  spec(P, G), spec(P, G), spec(G, P), spec(G, P), per_g(G, 4 * P), per_g(G, 4 * P), per_g(1, G)]
                 + [whole(m) for m in consts],
        out_specs=[per_g(L * G, L * G), per_g(4, P, L * G), per_g(L * G, 4 * P), per_g(1, 4 * P)],
        out_shape=[jax.ShapeDtypeStruct((depth, ng, L * G, L * G), BF16),
                   jax.ShapeDtypeStruct((depth, ng, 4, P, L * G), BF16),
                   jax.ShapeDtypeStruct((depth, ng, L * G, 4 * P), BF16),
                   jax.ShapeDtypeStruct((depth, ng, 1, 4 * P), F32)],
        compiler_params=_params("parallel", "parallel"),
        name="s5_prep",
    )(lam_re[..., None], lam_im[..., None], log_dt[..., None, None],
      tile4(lam_re)[:, :, None, :], tile4(lam_im)[:, :, None, :], ldt4,
      b_re, b_im, c_re, c_im, tile4(c_re), tile4(c_im), d_skip[:, :, None, :], *consts)


def _s5_sums_kernel(lo_ref, hi_ref, bend_ref, ut_ref, et_ref, *, nchunk):
    L, P, G = S5_CHUNK, S5_STATE, S5_GROUP
    per_half = LANES // G
    for half, z_ref in enumerate((lo_ref, hi_ref)):
        for s in range(L):
            zt = z_ref[pl.ds(s, nchunk, stride=L), :].T.astype(BF16)
            for gl in range(per_half):
                ut_ref[half * per_half + gl, s * G:(s + 1) * G, :] = zt[gl * G:(gl + 1) * G, :]
    for g in range(S5_GROUPS):
        for plane in range(4):
            row = plane * S5_GROUPS * P + g * P
            et_ref[row:row + P, :] = _dot(bend_ref[g, plane], ut_ref[g])


def _s5_scan_kernel(et_ref, lam_ref, xt_ref, e_sc, x_sc, *, bsz, steps):
    width = S5_GROUPS * S5_STATE
    for plane in range(4):
        e_sc[plane] = et_ref[plane * width:(plane + 1) * width, :].T
    ar_f, ai_f, ar_b, ai_b = lam_ref[0], lam_ref[1], lam_ref[2], lam_ref[3]
    zero = jnp.zeros_like(ar_f)
    for b in range(bsz):
        def body(i, carry):
            xr, xi, yr, yi = carry
            rf = b * steps + i
            rb = b * steps + steps - 1 - i
            x_sc[0, pl.ds(rf, 1), :] = xr
            x_sc[1, pl.ds(rf, 1), :] = xi
            x_sc[2, pl.ds(rb, 1), :] = yr
            x_sc[3, pl.ds(rb, 1), :] = yi
            xr, xi = (ar_f * xr - ai_f * xi + e_sc[0, pl.ds(rf, 1), :],
                      ar_f * xi + ai_f * xr + e_sc[1, pl.ds(rf, 1), :])
            yr, yi = (ar_b * yr - ai_b * yi + e_sc[2, pl.ds(rb, 1), :],
                      ar_b * yi + ai_b * yr + e_sc[3, pl.ds(rb, 1), :])
            return xr, xi, yr, yi
        lax.fori_loop(0, steps, body, (zero, zero, zero, zero))
    for plane in range(4):
        xt_ref[plane * width:(plane + 1) * width, :] = x_sc[plane].T


def _gelu_tanh(x):
    return 0.5 * x * (1.0 + jnp.tanh(math.sqrt(2.0 / math.pi) * (x + 0.044715 * (x * x * x))))


def _s5_out_kernel(ut_ref, xt_ref, toep_ref, cout_ref, lo_ref, hi_ref, yt_sc, *, nchunk):
    L, P, G = S5_CHUNK, S5_STATE, S5_GROUP
    width = S5_GROUPS * P
    for g in range(S5_GROUPS):
        xg = jnp.concatenate([xt_ref[plane * width + g * P:plane * width + (g + 1) * P, :] for plane in range(4)],
                             axis=0).astype(BF16)
        yt = _gelu_tanh(_dot(toep_ref[g], ut_ref[g]) + _dot(cout_ref[g], xg))
        for t in range(L):
            yt_sc[t, g * G:(g + 1) * G, :] = yt[t * G:(t + 1) * G, :]
    for t in range(L):
        y = yt_sc[t].T
        lo_ref[pl.ds(t, nchunk, stride=L), :] = y[:, :LANES]
        hi_ref[pl.ds(t, nchunk, stride=L), :] = y[:, LANES:]


def _s5(zs, prep, l, bsz, seq):
    ng, P, G, L = S5_GROUPS, S5_STATE, S5_GROUP, S5_CHUNK
    toep, bend, cout, laml = prep
    depth = toep.shape[0]
    laml = laml.reshape(depth, ng, 4, P).transpose(0, 2, 1, 3).reshape(depth, 4, 1, ng * P)
    t = bsz * seq
    tile = min(S5_TILE, t)
    nct = tile // L
    nchunk = t // L
    ut_spec = pl.BlockSpec((ng, L * G, nct), lambda i: (0, 0, i))
    plane_spec = pl.BlockSpec((4 * ng * P, nct), lambda i: (0, i))
    plane_shape = jax.ShapeDtypeStruct((4 * ng * P, nchunk), F32)
    ut, et = pl.pallas_call(
        functools.partial(_s5_sums_kernel, nchunk=nct),
        grid=(t // tile,),
        in_specs=[pl.BlockSpec((tile, LANES), lambda i: (i, 0)), pl.BlockSpec((tile, LANES), lambda i: (i, 1)),
                  _layer_spec(bend, l)],
        out_specs=[ut_spec, plane_spec],
        out_shape=[jax.ShapeDtypeStruct((ng, L * G, nchunk), BF16), plane_shape],
        compiler_params=_params("parallel"),
        name="s5_chunk_sums",
    )(zs, zs, bend)
    whole_planes = pl.BlockSpec((4 * ng * P, nchunk), lambda i: (0, 0))
    xt = pl.pallas_call(
        functools.partial(_s5_scan_kernel, bsz=bsz, steps=seq // L),
        grid=(1,),
        in_specs=[whole_planes, _layer_spec(laml, l)],
        out_specs=whole_planes,
        out_shape=plane_shape,
        scratch_shapes=[pltpu.VMEM((4, nchunk, ng * P), F32), pltpu.VMEM((4, nchunk, ng * P), F32)],
        compiler_params=_params("arbitrary"),
        name="s5_state_scan",
    )(et, laml)
    half_spec = pl.BlockSpec((tile, LANES), lambda i: (i, 0))
    half_shape = jax.ShapeDtypeStruct((t, LANES), F32)
    return pl.pallas_call(
        functools.partial(_s5_out_kernel, nchunk=nct),
        grid=(t // tile,),
        in_specs=[ut_spec, plane_spec, _layer_spec(toep, l), _layer_spec(cout, l)],
        out_specs=[half_spec, half_spec],
        out_shape=[half_shape, half_shape],
        scratch_shapes=[pltpu.VMEM((L, M_S5, nct), F32)],
        compiler_params=_params("parallel"),
        name="s5_outputs",
    )(ut, xt, toep, cout)


def _merge_kernel(x_ref, hf_ref, hb_ref, zo_ref, yf_ref, ys_lo_ref, ys_hi_ref, gpre_ref, wgate_ref, bgate_ref,
                  ng_ref, wm_ref, wf_ref, wglu_ref, bglu_ref, wout_ref, gpost_ref, o_ref, mixed_sc):
    x = x_ref[...]
    h = _rms(x, gpre_ref[...]).astype(BF16)
    hs = hf_ref[...] + hb_ref[...]
    parts = []
    for hd in range(HEADS):
        blk = hs[:, hd * HEAD_DIM:(hd + 1) * HEAD_DIM]
        mu = jnp.mean(blk, axis=-1, keepdims=True)
        cen = blk - mu
        var = jnp.mean(cen * cen, axis=-1, keepdims=True)
        parts.append(cen * lax.rsqrt(var + EPS))
    hm = (jnp.concatenate(parts, axis=1) * ng_ref[...] * _sigmoid(zo_ref[...].astype(F32))).astype(BF16)
    yf = yf_ref[...].astype(BF16)
    ys = jnp.concatenate([ys_lo_ref[...].astype(BF16), ys_hi_ref[...].astype(BF16)], axis=1)
    for n in range(D_MODEL // MERGE_COLS):
        lo = n * MERGE_COLS
        cols = slice(lo, lo + MERGE_COLS)

        def gate(i):
            gcols = slice(i * D_MODEL + lo, i * D_MODEL + lo + MERGE_COLS)
            return _sigmoid(_dot(h, wgate_ref[:, gcols]) + bgate_ref[:, gcols])

        lin = _dot(ys, wglu_ref[:, cols]) + bglu_ref[:, cols]
        gcols = slice(D_MODEL + lo, D_MODEL + lo + MERGE_COLS)
        y_s = lin * _sigmoid(_dot(ys, wglu_ref[:, gcols]) + bglu_ref[:, gcols])
        mixed = gate(0) * _dot(hm, wm_ref[:, cols]) + gate(1) * _dot(yf, wf_ref[:, cols]) + gate(2) * y_s
        mixed_sc[:, cols] = mixed.astype(BF16)
    o_ref[...] = x + _rms(_dot(mixed_sc[...], wout_ref[...]), gpost_ref[...])


def _merge(x2, hf, hb, zo, yf, ys_lo, ys_hi, params, l, tile):
    t = x2.shape[0]
    rows = lambda width: pl.BlockSpec((tile, width), lambda i: (i, 0))
    return pl.pallas_call(
        _merge_kernel,
        grid=(t // tile,),
        in_specs=[rows(D_MODEL), rows(M_MLSTM), rows(M_MLSTM), rows(M_MLSTM), rows(M_FOURIER),
                  rows(LANES), rows(LANES)] + [_layer_spec(a, l) for a in params],
        out_specs=rows(D_MODEL),
        out_shape=jax.ShapeDtypeStruct((t, D_MODEL), F32),
        scratch_shapes=[pltpu.VMEM((tile, D_MODEL), BF16)],
        compiler_params=_params("parallel"),
        name="merge",
    )(x2, hf, hb, zo, yf, ys_lo, ys_hi, *params)


def _ffn_kernel(x_ref, gpre_ref, w1_ref, w2_ref, gpost_ref, o_ref, *, n_split):
    x = x_ref[...]
    h = _rms(x, gpre_ref[...]).astype(BF16)
    width = D_FF // n_split
    f = None
    for j in range(n_split):
        a = jnp.maximum(_dot(h, w1_ref[:, j * width:(j + 1) * width]), 0.0)
        part = _dot((a * a).astype(BF16), w2_ref[j * width:(j + 1) * width, :])
        f = part if f is None else f + part
    o_ref[...] = x + _rms(f, gpost_ref[...])


def _ffn(x2, params, l, tile):
    t = x2.shape[0]
    rows = pl.BlockSpec((tile, D_MODEL), lambda i: (i, 0))
    return pl.pallas_call(
        functools.partial(_ffn_kernel, n_split=4),
        grid=(t // tile,),
        in_specs=[rows] + [_layer_spec(a, l) for a in params],
        out_specs=rows,
        out_shape=jax.ShapeDtypeStruct((t, D_MODEL), F32),
        compiler_params=_params("parallel"),
        name="ffn",
    )(x2, *params)


def kernel(x, g_mix_pre, g_mix_post, g_ffn_pre, g_ffn_post, w_in, b_in, conv_w, conv_b, mlstm_norm_g, w_up_mlstm, w_up_fourier, s5_lam_re, s5_lam_im, s5_log_dt, s5_b_re, s5_b_im, s5_c_re, s5_c_im, s5_d, w_glu, b_glu, w_out, w_ffn1, w_ffn2):
    bsz, seq, _ = x.shape
    depth = w_in.shape[0]
    t = bsz * seq
    tile = min(512, t)
    bf = lambda a: a.astype(BF16)
    row = lambda a: a[:, None, :]
    w_gates = w_in[:, :, OFF_IG:OFF_FOURIER]
    b_gates = b_in[:, OFF_IG:OFF_FOURIER]
    inproj_params = (row(g_mix_pre),
                     bf(jnp.concatenate([w_in[:, :, :OFF_IG], w_in[:, :, OFF_FOURIER:OFF_GATE]], axis=2)),
                     row(jnp.concatenate([b_in[:, :OFF_IG], b_in[:, OFF_FOURIER:OFF_GATE]], axis=1)),
                     bf(w_gates), row(b_gates), bf(jnp.swapaxes(w_gates, 1, 2)), b_gates[:, :, None],
                     conv_w, row(conv_b))
    merge_params = (row(g_mix_pre), bf(w_in[:, :, OFF_GATE:]), row(b_in[:, OFF_GATE:]), row(mlstm_norm_g),
                    bf(w_up_mlstm), bf(w_up_fourier), bf(w_glu), row(b_glu), bf(w_out), row(g_mix_post))
    ffn_params = (row(g_ffn_pre), bf(w_ffn1), bf(w_ffn2), row(g_ffn_post))
    s5_prep = _s5_prep(s5_lam_re, s5_lam_im, s5_log_dt, s5_b_re, s5_b_im, s5_c_re, s5_c_im, s5_d)
    x2 = x.reshape(t, D_MODEL)
    for l in range(depth):
        qk, zv, zo, zf, zs, zg, zgt = _inproj(x2, inproj_params, l, tile, seq)
        hf, hb = _mlstm(qk, zv, zg, zgt, bsz, seq)
        yf = _fourier(zf, bsz, seq)
        ys_lo, ys_hi = _s5(zs, s5_prep, l, bsz, seq)
        x2 = _merge(x2, hf, hb, zo, yf, ys_lo, ys_hi, merge_params, l, tile)
        x2 = _ffn(x2, ffn_params, l, tile)
    return x2.reshape(bsz, seq, D_MODEL)
```

```python
import functools
import itertools
import math

import numpy as np
import jax
import jax.numpy as jnp
from jax import lax
from jax.experimental import pallas as pl
from jax.experimental.pallas import tpu as pltpu

F32 = jnp.float32
BF16 = jnp.bfloat16
HIGHEST = lax.Precision.HIGHEST

LANES = 128
D_MODEL = 1024
M_MLSTM = 512
HEADS = 4
HEAD_DIM = 128
CHUNK = 128
MLSTM_SUB = 4
CONV_WIDTH = 5
CONV_COLS = 256
CONV_HALO = 8
M_FOURIER = 256
FOURIER_GROUP_DIM = 64
M_S5 = 256
S5_GROUP = 16
S5_GROUPS = 16
S5_STATE = 64
S5_CHUNK = 32
S5_TILE = 4096
N_BRANCHES = 3
MERGE_COLS = 256
D_FF = 4 * D_MODEL
EPS = 1e-6

OFF_Q = 0
OFF_V = 2 * M_MLSTM
OFF_O = 3 * M_MLSTM
OFF_IG = 4 * M_MLSTM
OFF_FOURIER = OFF_IG + 4 * HEADS
OFF_S5 = OFF_FOURIER + M_FOURIER
OFF_GATE = OFF_S5 + M_S5

DFT_N2 = 64
DFT_STEP = 8

VMEM_LIMIT = 56 * 1024 * 1024

NT_DIMS = (((1,), (1,)), ((), ()))
TN_DIMS = (((0,), (0,)), ((), ()))


def _params(*sem):
    return pltpu.CompilerParams(dimension_semantics=sem, vmem_limit_bytes=VMEM_LIMIT)


def _layer_spec(a, l, col_block=None):
    width, index = col_block if col_block else (a.shape[-1], 0)
    return pl.BlockSpec((None,) + a.shape[1:-1] + (width,), lambda *_: (l,) + (0,) * (a.ndim - 2) + (index,))


def _rms(x, g):
    return x * lax.rsqrt(jnp.mean(x * x, axis=-1, keepdims=True) + EPS) * g


def _sigmoid(x):
    return 1.0 / (1.0 + jnp.exp(-x))


def _dot(a, b):
    return jnp.dot(a, b, preferred_element_type=F32)


def _dot_hi(a, b):
    return jnp.dot(a, b, precision=HIGHEST, preferred_element_type=F32)


def _split3(a):
    hi = a.astype(BF16)
    rest = a - hi.astype(F32)
    mid = rest.astype(BF16)
    return hi, mid, (rest - mid.astype(F32)).astype(BF16)


def _dot_sel(a, sel):
    hi, mid, lo = _split3(a)
    return _dot(hi, sel) + _dot(mid, sel) + _dot(lo, sel)


def _sel_dot(sel, a):
    hi, mid, lo = _split3(a)
    return _dot(sel, hi) + _dot(sel, mid) + _dot(sel, lo)


def _inproj_kernel(x_ref, xp_ref, xn_ref, g_ref, wm_ref, bm_ref, wg_ref, bg_ref, wgt_ref, bgt_ref, cw_ref, cb_ref,
                   qk_ref, zv_ref, zo_ref, zf_ref, zs_ref, zg_ref, zgt_ref, ext_sc, *, tiles_per_seq):
    i = pl.program_id(0)
    tile = x_ref.shape[0]
    pad = CONV_WIDTH // 2
    h = _rms(x_ref[...], g_ref[...]).astype(BF16)
    h_halo = _rms(jnp.concatenate([xp_ref[...], xn_ref[...]], axis=0), g_ref[...]).astype(BF16)
    first = (i % tiles_per_seq) == 0
    last = (i % tiles_per_seq) == tiles_per_seq - 1

    def proj(hh, lo, hi):
        return _dot(hh, wm_ref[:, lo:hi]) + bm_ref[:, lo:hi]

    for n in range(2 * M_MLSTM // CONV_COLS):
        lo = n * CONV_COLS
        cols = slice(lo, lo + CONV_COLS)
        z_halo = proj(h_halo, OFF_Q + lo, OFF_Q + lo + CONV_COLS)
        ext_sc[0:CONV_HALO, cols] = jnp.where(first, 0.0, z_halo[:CONV_HALO])
        ext_sc[CONV_HALO:CONV_HALO + tile, cols] = proj(h, OFF_Q + lo, OFF_Q + lo + CONV_COLS)
        ext_sc[CONV_HALO + tile:, cols] = jnp.where(last, 0.0, z_halo[CONV_HALO:])
        acc = cb_ref[:, cols] + cw_ref[0:1, cols] * ext_sc[CONV_HALO - pad:CONV_HALO - pad + tile, cols]
        for j in range(1, CONV_WIDTH):
            acc = acc + cw_ref[j:j + 1, cols] * ext_sc[CONV_HALO - pad + j:CONV_HALO - pad + j + tile, cols]
        qk = acc * _sigmoid(acc)
        if lo < M_MLSTM:
            qk = qk * (HEAD_DIM ** -0.5)
        qk_ref[:, cols] = qk.astype(BF16)

    zv_ref[...] = proj(h, OFF_V, OFF_O).astype(BF16)
    zo_ref[...] = proj(h, OFF_O, OFF_IG).astype(BF16)
    zf_ref[...] = proj(h, OFF_IG, OFF_IG + M_FOURIER)
    zs_ref[...] = proj(h, OFF_IG + M_FOURIER, OFF_IG + M_FOURIER + M_S5)
    zg_ref[...] = _dot(h, wg_ref[...]) + bg_ref[...]
    zgt_ref[...] = lax.dot_general(wgt_ref[...], h, NT_DIMS, preferred_element_type=F32) + bgt_ref[...]


def _inproj(x2, params, l, tile, seq):
    t = x2.shape[0]
    n_gate = 4 * HEADS
    hpt = tile // CONV_HALO
    n_halo = t // CONV_HALO
    rows = lambda width: pl.BlockSpec((tile, width), lambda i: (i, 0))
    cols = lambda height: pl.BlockSpec((height, tile), lambda i: (0, i))
    return pl.pallas_call(
        functools.partial(_inproj_kernel, tiles_per_seq=seq // tile),
        grid=(t // tile,),
        in_specs=[rows(D_MODEL),
                  pl.BlockSpec((CONV_HALO, D_MODEL), lambda i: (jnp.maximum(i * hpt - 1, 0), 0)),
                  pl.BlockSpec((CONV_HALO, D_MODEL), lambda i: (jnp.minimum((i + 1) * hpt, n_halo - 1), 0))]
                 + [_layer_spec(a, l, (N_BRANCHES * D_MODEL, 1) if k in (1, 2) else None)
                    for k, a in enumerate(params)],
        out_specs=[rows(2 * M_MLSTM), rows(M_MLSTM), rows(M_MLSTM), rows(M_FOURIER), rows(M_S5),
                   rows(n_gate), cols(n_gate)],
        out_shape=[jax.ShapeDtypeStruct((t, 2 * M_MLSTM), BF16), jax.ShapeDtypeStruct((t, M_MLSTM), BF16),
                   jax.ShapeDtypeStruct((t, M_MLSTM), BF16), jax.ShapeDtypeStruct((t, M_FOURIER), F32),
                   jax.ShapeDtypeStruct((t, M_S5), F32), jax.ShapeDtypeStruct((t, n_gate), F32),
                   jax.ShapeDtypeStruct((n_gate, t), F32)],
        scratch_shapes=[pltpu.VMEM((tile + 2 * CONV_HALO, 2 * M_MLSTM), F32)],
        compiler_params=_params("parallel"),
        name="inproj",
    )(x2, x2, x2, *params)


def _log_sigmoid(x):
    return jnp.minimum(x, 0.0) - jnp.log1p(jnp.exp(-jnp.abs(x)))


def _mlstm_kernel(qk_f, v_f, g_f, gt_f, qk_b, v_b, g_b, gt_b, hf_ref, hb_ref, state_sc, m_sc):
    c = pl.program_id(1)
    L = CHUNK

    @pl.when(c == 0)
    def _():
        state_sc[...] = jnp.zeros_like(state_sc)
        m_sc[...] = jnp.zeros_like(m_sc)

    row = lax.broadcasted_iota(jnp.int32, (L, L), 0)
    col = lax.broadcasted_iota(jnp.int32, (L, L), 1)
    lower = row >= col
    upper = row <= col
    lower_f = jnp.where(lower, 1.0, 0.0).astype(BF16)
    upper_f = jnp.where(upper, 1.0, 0.0).astype(BF16)
    ones_v = jnp.ones((L, HEAD_DIM), F32)

    streams = ((0, qk_f, v_f, g_f, gt_f, hf_ref), (1, qk_b, v_b, g_b, gt_b, hb_ref))
    for sub, (d, qk_ref, v_ref, g_ref, gt_ref, out_ref) in itertools.product(range(MLSTM_SUB), streams):
        r0 = (sub if d == 0 else MLSTM_SUB - 1 - sub) * L
        rows = slice(r0, r0 + L)
        gates = g_ref[rows, :]
        gates_t = gt_ref[:, rows]
        lf_cols = _log_sigmoid(gates[:, 2 * HEADS:])
        lf_rows = _log_sigmoid(gates_t[2 * HEADS:, :])
        if d == 0:
            b_cols = _sel_dot(lower_f, lf_cols)
            b_rows = _dot_sel(lf_rows, upper_f)
            mask = lower
        else:
            b_cols = _sel_dot(upper_f, lf_cols)
            b_rows = _dot_sel(lf_rows, lower_f)
            mask = upper

        for hd in range(HEADS):
            k_idx = d * HEADS + hd
            lo = hd * HEAD_DIM
            q = qk_ref[rows, lo:lo + HEAD_DIM]
            k = qk_ref[rows, M_MLSTM + lo:M_MLSTM + lo + HEAD_DIM]
            v = v_ref[rows, lo:lo + HEAD_DIM]
            b_c = jnp.broadcast_to(b_cols[:, k_idx:k_idx + 1], (L, L))
            ig_c = jnp.broadcast_to(gates[:, k_idx:k_idx + 1], (L, L))
            b_r = b_rows[k_idx:k_idx + 1, :]
            ig_r = gates_t[k_idx:k_idx + 1, :]
            g_tot = b_c[L - 1:L, :] if d == 0 else b_c[0:1, :]
            m_prev = m_sc[k_idx:k_idx + 1, :]

            d_log = jnp.where(mask, b_c - b_r + ig_r, -1e30)
            inter_log = b_c + m_prev
            m_t = jnp.maximum(inter_log, jnp.max(d_log, axis=1, keepdims=True))
            scores = lax.dot_general(q, k, NT_DIMS, preferred_element_type=F32) * jnp.exp(d_log - m_t)
            inter_w = jnp.exp(inter_log - m_t)
            st = state_sc[k_idx]
            v_ext = jnp.concatenate([v, ones_v.astype(BF16)], axis=1)
            res = _dot(scores.astype(BF16), v_ext) + _dot((inter_w * q.astype(F32)).astype(BF16), st.astype(BF16))
            num = res[:, :HEAD_DIM]
            den = res[:, HEAD_DIM:]
            out_ref[rows, lo:lo + HEAD_DIM] = num / jnp.maximum(jnp.abs(den), jnp.exp(-m_t))

            a_c = g_tot - b_c + ig_c
            a_max = jnp.max(a_c, axis=0, keepdims=True)
            w_c = jnp.exp(a_c - a_max)
            vw = jnp.concatenate([v.astype(F32) * w_c, w_c], axis=1).astype(BF16)
            st_loc = lax.dot_general(k, vw, TN_DIMS, preferred_element_type=F32)
            m_new = jnp.maximum(g_tot + m_prev, a_max)
            s_old = jnp.exp(g_tot + m_prev - m_new)
            s_new = jnp.exp(a_max - m_new)
            s_old2 = jnp.concatenate([s_old, s_old], axis=1)
            s_new2 = jnp.concatenate([s_new, s_new], axis=1)
            state_sc[k_idx] = s_old2 * st + s_new2 * st_loc
            m_sc[k_idx:k_idx + 1, :] = m_new


def _mlstm(qk, zv, zg, zgt, bsz, seq):
    t = bsz * seq
    blk = CHUNK * MLSTM_SUB
    nc = seq // blk
    n_gate = 4 * HEADS

    def fwd(b, c):
        return b * nc + c

    def bwd(b, c):
        return b * nc + nc - 1 - c

    def specs(pos):
        return [pl.BlockSpec((blk, 2 * M_MLSTM), lambda b, c: (pos(b, c), 0)),
                pl.BlockSpec((blk, M_MLSTM), lambda b, c: (pos(b, c), 0)),
                pl.BlockSpec((blk, n_gate), lambda b, c: (pos(b, c), 0)),
                pl.BlockSpec((n_gate, blk), lambda b, c: (0, pos(b, c)))]

    out_shape = jax.ShapeDtypeStruct((t, M_MLSTM), F32)
    return pl.pallas_call(
        _mlstm_kernel,
        grid=(bsz, nc),
        in_specs=specs(fwd) + specs(bwd),
        out_specs=[pl.BlockSpec((blk, M_MLSTM), lambda b, c: (fwd(b, c), 0)),
                   pl.BlockSpec((blk, M_MLSTM), lambda b, c: (bwd(b, c), 0))],
        out_shape=[out_shape, out_shape],
        scratch_shapes=[pltpu.VMEM((2 * HEADS, HEAD_DIM, 2 * HEAD_DIM), F32),
                        pltpu.VMEM((2 * HEADS, HEAD_DIM), F32)],
        compiler_params=_params("arbitrary", "arbitrary"),
        name="mlstm",
    )(qk, zv, zg, zgt, qk, zv, zg, zgt)


def _dft_constants(seq):
    n1, n2 = seq // DFT_N2, DFT_N2
    k1 = np.arange(n1)[:, None, None]
    s2 = np.arange(n2)[None, None, :]
    s1 = np.arange(n1)[None, :, None]
    ang = -2.0 * np.pi * ((k1 * (n2 * s1 + s2)) % seq) / seq
    stage1 = np.concatenate([np.cos(ang), np.sin(ang)], axis=0)
    stage1 = np.ascontiguousarray(stage1.transpose(2, 0, 1))
    a2 = 2.0 * np.pi * np.outer(np.arange(n2), np.arange(n2)) / n2
    c2, sn2 = np.cos(a2), np.sin(a2)
    stage2 = np.block([[c2, sn2], [-sn2, c2]])
    ag = 2.0 * np.pi * np.outer(np.arange(FOURIER_GROUP_DIM), np.arange(FOURIER_GROUP_DIM)) / FOURIER_GROUP_DIM
    scale = 1.0 / math.sqrt(seq * FOURIER_GROUP_DIM)
    eye = np.eye(LANES // FOURIER_GROUP_DIM)
    group = np.stack([np.kron(eye, np.cos(ag)), np.kron(eye, np.sin(ag))]) * scale
    as_bf16 = lambda a: jnp.asarray(a, F32).astype(BF16)
    return as_bf16(stage1), as_bf16(stage2), as_bf16(group)


def _fourier_kernel(u_ref, w1_ref, w2_ref, wg_ref, y_ref, u_sc, b_sc, y_sc, *, n1):
    n2 = DFT_N2
    rows = n1 * DFT_STEP
    for sb in range(n2 // DFT_STEP):
        u_sc[sb % 2] = u_ref[:, sb * DFT_STEP:(sb + 1) * DFT_STEP, :].reshape(rows, LANES)
        for j in range(DFT_STEP):
            pick = pl.ds(j, n1, stride=DFT_STEP)
            res = _dot(w1_ref[sb * DFT_STEP + j], u_sc[sb % 2, pick, :].astype(BF16))
            b_sc[sb, 0, pick, :] = res[:n1]
            b_sc[sb, 1, pick, :] = res[n1:]
    for kb in range(n1 // DFT_STEP):
        def gather(part, j):
            r0 = (kb * DFT_STEP + j) * DFT_STEP
            return jnp.concatenate([b_sc[sb, part, r0:r0 + DFT_STEP, :] for sb in range(n2 // DFT_STEP)], axis=0)

        stacked = jnp.concatenate([jnp.concatenate([gather(0, j), gather(1, j)], axis=0) for j in range(DFT_STEP)],
                                  axis=1).astype(BF16)
        z = _dot(w2_ref[...], stacked).astype(BF16)
        zr = jnp.concatenate([z[:n2, j * LANES:(j + 1) * LANES] for j in range(DFT_STEP)], axis=0)
        zi = jnp.concatenate([z[n2:, j * LANES:(j + 1) * LANES] for j in range(DFT_STEP)], axis=0)
        y = _dot(zr, wg_ref[0]) + _dot(zi, wg_ref[1])
        for j in range(DFT_STEP):
            y_sc[kb % 2, pl.ds(j, n2, stride=DFT_STEP), :] = y[j * n2:(j + 1) * n2, :]
        y_ref[:, kb * DFT_STEP:(kb + 1) * DFT_STEP, :] = y_sc[kb % 2].reshape(n2, DFT_STEP, LANES)


def _fourier(zf, bsz, seq):
    n1, n2 = seq // DFT_N2, DFT_N2
    stage1, stage2, group = _dft_constants(seq)
    whole = lambda a: pl.BlockSpec(a.shape, lambda b, h: (0,) * a.ndim)
    y = pl.pallas_call(
        functools.partial(_fourier_kernel, n1=n1),
        grid=(bsz, M_FOURIER // LANES),
        in_specs=[pl.BlockSpec((None, n1, n2, LANES), lambda b, h: (b, 0, 0, h)),
                  whole(stage1), whole(stage2), whole(group)],
        out_specs=pl.BlockSpec((None, n2, n1, LANES), lambda b, h: (b, 0, 0, h)),
        out_shape=jax.ShapeDtypeStruct((bsz, n2, n1, M_FOURIER), F32),
        scratch_shapes=[pltpu.VMEM((2, n1 * DFT_STEP, LANES), F32),
                        pltpu.VMEM((n2 // DFT_STEP, 2, n1 * DFT_STEP, LANES), F32),
                        pltpu.VMEM((2, n2 * DFT_STEP, LANES), F32)],
        compiler_params=_params("parallel", "parallel"),
        name="fourier",
    )(zf.reshape(bsz, n1, n2, M_FOURIER), stage1, stage2, group)
    return y.reshape(bsz * seq, M_FOURIER)


def _s5_constants():
    L, G = S5_CHUNK, S5_GROUP
    lane = np.arange(2 * L * G)
    expo = np.arange(LANES)[:, None]
    rep = (lane[None, :] % G == np.arange(G)[:, None])
    lag = (L - 1) - lane // G
    spread = np.stack([(expo == lag) & (lag >= 0), (expo == -lag) & (lag <= 0) & (lag > -L)])
    step = lane[:L * G] // G
    spread_s = np.stack([expo == (L - 1) - step, expo == step])
    return tuple(jnp.asarray(m, BF16) for m in (rep, spread, spread_s))


def _s5_prep_kernel(lamc_re, lamc_im, ldt_ref, lam4_re, lam4_im, ldt4_ref, b_re, b_im, c_re, c_im, c4_re, c4_im,
                    d_ref, rep_ref, spread_ref, spread_s_ref, toep_ref, bend_ref, cout_ref, laml_ref):
    L, P, G = S5_CHUNK, S5_STATE, S5_GROUP
    W = 2 * L * G
    shift = G.bit_length() - 1
    nbits = L.bit_length()

    def cmul(ar, ai, br, bi):
        return ar * br - ai * bi, ar * bi + ai * br

    def lam_bar(lre, lim, dt):
        mag = jnp.exp(lre * dt)
        return mag * jnp.cos(lim * dt), mag * jnp.sin(lim * dt)

    def power_table(base_r, base_i, expo):
        tr = jnp.ones(expo.shape, F32)
        ti = jnp.zeros(expo.shape, F32)
        sr, si = base_r, base_i
        for bit in range(nbits):
            nr, ni = cmul(tr, ti, sr, si)
            has = ((expo >> bit) & 1) == 1
            tr, ti = jnp.where(has, nr, tr), jnp.where(has, ni, ti)
            sr, si = cmul(sr, si, sr, si)
        return tr, ti

    def spread_dot(xr, xi, sel):
        n = xr.shape[0]
        out = _dot(jnp.concatenate(_split3(xr) + _split3(xi), axis=0), sel)
        return out[:n] + out[n:2 * n] + out[2 * n:3 * n], out[3 * n:4 * n] + out[4 * n:5 * n] + out[5 * n:]

    def dot3(c, m):
        c_hi, c_mid, _ = _split3(c)
        m_hi, m_mid, _ = _split3(m)
        return _dot(c_hi, m_hi) + _dot(c_mid, m_hi) + _dot(c_hi, m_mid)

    rep = rep_ref[...]
    lane_g = lax.broadcasted_iota(jnp.int32, (G, W), 1)
    row_g = lax.broadcasted_iota(jnp.int32, (G, W), 0)
    expo_tab = jnp.minimum(lax.broadcasted_iota(jnp.int32, (P, LANES), 1), L)

    d_tiled = _dot_sel(jnp.broadcast_to(d_ref[0], (G, G)), rep)
    gen = jnp.where(((lane_g & (G - 1)) == row_g) & ((lane_g >> shift) == L - 1), d_tiled, 0.0)
    for d in range(2):
        dt = jnp.exp(ldt_ref[d, 0])
        lr, li = lamc_re[d, 0], lamc_im[d, 0]
        lbr, lbi = lam_bar(lr, li, dt)
        den = lr * lr + li * li
        fr, fi = ((lbr - 1.0) * lr + lbi * li) / den, (lbi * lr - (lbr - 1.0) * li) / den
        bbr, bbi = cmul(fr, fi, b_re[d, 0], b_im[d, 0])
        btr, bti = spread_dot(bbr, bbi, rep)
        tab_r, tab_i = power_table(lbr, lbi, expo_tab)
        pr, pi = spread_dot(tab_r, tab_i, spread_ref[d])
        mr, mi = cmul(pr, pi, btr, bti)
        gen = gen + dot3(c_re[d, 0], mr) - dot3(c_im[d, 0], mi)
        half = L * G
        qr, qi = spread_dot(tab_r, tab_i, spread_s_ref[d])
        xr, xi = cmul(qr, qi, btr[:, :half], bti[:, :half])
        bend_ref[0, 2 * d] = xr.astype(BF16)
        bend_ref[0, 2 * d + 1] = xi.astype(BF16)

    for t in range(L):
        a = (L - 1 - t) * G
        toep_ref[0, t * G:(t + 1) * G, :] = gen[:, a:a + L * G].astype(BF16)

    lane4 = lax.broadcasted_iota(jnp.int32, (L, 4 * P), 1)
    step4 = lax.broadcasted_iota(jnp.int32, (L, 4 * P), 0)
    lb4r, lb4i = lam_bar(lam4_re[0], lam4_im[0], jnp.exp(ldt4_ref[0]))
    pr, pi = power_table(lb4r, lb4i, jnp.where(lane4 < 2 * P, step4 + 1, L - step4))
    cr, ci = c4_re[0], c4_im[0]
    plane_bit = P.bit_length() - 1
    re_c = ((lax.broadcasted_iota(jnp.int32, (G, 4 * P), 1) >> plane_bit) & 1) == 0
    for t in range(L):
        re_part, im_part = cmul(cr, ci, pr[t:t + 1], pi[t:t + 1])
        cout_ref[0, t * G:(t + 1) * G, :] = jnp.where(re_c, re_part, -im_part).astype(BF16)
    re_1 = ((lax.broadcasted_iota(jnp.int32, (1, 4 * P), 1) >> plane_bit) & 1) == 0
    lr, li = power_table(lb4r, lb4i, jnp.full((1, 4 * P), L, jnp.int32))
    laml_ref[0] = jnp.where(re_1, lr, li)


def _s5_prep(lam_re, lam_im, log_dt, b_re, b_im, c_re, c_im, d_skip):
    ng, P, G, L = S5_GROUPS, S5_STATE, S5_GROUP, S5_CHUNK
    depth = lam_re.shape[0]
    tile4 = lambda a: jnp.concatenate([a[:, 0], a[:, 0], a[:, 1], a[:, 1]], axis=-1)
    ldt4 = jnp.repeat(tile4(log_dt[..., None]), P, axis=-1)[:, :, None, :]
    consts = _s5_constants()
    spec = lambda *tail: pl.BlockSpec((None, 2, 1) + tail, lambda l, g: (l, 0, g) + (0,) * len(tail))
    per_g = lambda *tail: pl.BlockSpec((None, 1) + tail, lambda l, g: (l, g) + (0,) * len(tail))
    whole = lambda a: pl.BlockSpec(a.shape, lambda l, g: (0,) * a.ndim)
    return pl.pallas_call(
        _s5_prep_kernel,
        grid=(depth, ng),
        in_specs=[spec(P, 1), spec(P, 1), spec(1, 1), per_g(1, 4 * P), per_g(1, 4 * P), per_g(1, 4 * P),
                  spec(P, G), spec(P, G), spec(G, P), spec(G, P), per_g(G, 4 * P), per_g(G, 4 * P), per_g(1, G)]
                 + [whole(m) for m in consts],
        out_specs=[per_g(L * G, L * G), per_g(4, P, L * G), per_g(L * G, 4 * P), per_g(1, 4 * P)],
        out_shape=[jax.ShapeDtypeStruct((depth, ng, L * G, L * G), BF16),
                   jax.ShapeDtypeStruct((depth, ng, 4, P, L * G), BF16),
                   jax.ShapeDtypeStruct((depth, ng, L * G, 4 * P), BF16),
                   jax.ShapeDtypeStruct((depth, ng, 1, 4 * P), F32)],
        compiler_params=_params("parallel", "parallel"),
        name="s5_prep",
    )(lam_re[..., None], lam_im[..., None], log_dt[..., None, None],
      tile4(lam_re)[:, :, None, :], tile4(lam_im)[:, :, None, :], ldt4,
      b_re, b_im, c_re, c_im, tile4(c_re), tile4(c_im), d_skip[:, :, None, :], *consts)


def _s5_sums_kernel(lo_ref, hi_ref, bend_ref, ut_ref, et_ref, *, nchunk):
    L, P, G = S5_CHUNK, S5_STATE, S5_GROUP
    per_half = LANES // G
    for half, z_ref in enumerate((lo_ref, hi_ref)):
        for s in range(L):
            zt = z_ref[pl.ds(s, nchunk, stride=L), :].T.astype(BF16)
            for gl in range(per_half):
                ut_ref[half * per_half + gl, s * G:(s + 1) * G, :] = zt[gl * G:(gl + 1) * G, :]
    for g in range(S5_GROUPS):
        for plane in range(4):
            row = plane * S5_GROUPS * P + g * P
            et_ref[row:row + P, :] = _dot(bend_ref[g, plane], ut_ref[g])


def _s5_scan_kernel(et_ref, lam_ref, xt_ref, e_sc, x_sc, *, bsz, steps):
    width = S5_GROUPS * S5_STATE
    for plane in range(4):
        e_sc[plane] = et_ref[plane * width:(plane + 1) * width, :].T
    ar_f, ai_f, ar_b, ai_b = lam_ref[0], lam_ref[1], lam_ref[2], lam_ref[3]
    zero = jnp.zeros_like(ar_f)
    for b in range(bsz):
        def body(i, carry):
            xr, xi, yr, yi = carry
            rf = b * steps + i
            rb = b * steps + steps - 1 - i
            x_sc[0, pl.ds(rf, 1), :] = xr
            x_sc[1, pl.ds(rf, 1), :] = xi
            x_sc[2, pl.ds(rb, 1), :] = yr
            x_sc[3, pl.ds(rb, 1), :] = yi
            xr, xi = (ar_f * xr - ai_f * xi + e_sc[0, pl.ds(rf, 1), :],
                      ar_f * xi + ai_f * xr + e_sc[1, pl.ds(rf, 1), :])
            yr, yi = (ar_b * yr - ai_b * yi + e_sc[2, pl.ds(rb, 1), :],
                      ar_b * yi + ai_b * yr + e_sc[3, pl.ds(rb, 1), :])
            return xr, xi, yr, yi
        lax.fori_loop(0, steps, body, (zero, zero, zero, zero))
    for plane in range(4):
        xt_ref[plane * width:(plane + 1) * width, :] = x_sc[plane].T


def _gelu_tanh(x):
    return 0.5 * x * (1.0 + jnp.tanh(math.sqrt(2.0 / math.pi) * (x + 0.044715 * (x * x * x))))


def _s5_out_kernel(ut_ref, xt_ref, toep_ref, cout_ref, lo_ref, hi_ref, yt_sc, *, nchunk):
    L, P, G = S5_CHUNK, S5_STATE, S5_GROUP
    width = S5_GROUPS * P
    for g in range(S5_GROUPS):
        xg = jnp.concatenate([xt_ref[plane * width + g * P:plane * width + (g + 1) * P, :] for plane in range(4)],
                             axis=0).astype(BF16)
        yt = _gelu_tanh(_dot(toep_ref[g], ut_ref[g]) + _dot(cout_ref[g], xg))
        for t in range(L):
            yt_sc[t, g * G:(g + 1) * G, :] = yt[t * G:(t + 1) * G, :]
    for t in range(L):
        y = yt_sc[t].T
        lo_ref[pl.ds(t, nchunk, stride=L), :] = y[:, :LANES]
        hi_ref[pl.ds(t, nchunk, stride=L), :] = y[:, LANES:]


def _s5(zs, prep, l, bsz, seq):
    ng, P, G, L = S5_GROUPS, S5_STATE, S5_GROUP, S5_CHUNK
    toep, bend, cout, laml = prep
    depth = toep.shape[0]
    laml = laml.reshape(depth, ng, 4, P).transpose(0, 2, 1, 3).reshape(depth, 4, 1, ng * P)
    t = bsz * seq
    tile = min(S5_TILE, t)
    nct = tile // L
    nchunk = t // L
    ut_spec = pl.BlockSpec((ng, L * G, nct), lambda i: (0, 0, i))
    plane_spec = pl.BlockSpec((4 * ng * P, nct), lambda i: (0, i))
    plane_shape = jax.ShapeDtypeStruct((4 * ng * P, nchunk), F32)
    ut, et = pl.pallas_call(
        functools.partial(_s5_sums_kernel, nchunk=nct),
        grid=(t // tile,),
        in_specs=[pl.BlockSpec((tile, LANES), lambda i: (i, 0)), pl.BlockSpec((tile, LANES), lambda i: (i, 1)),
                  _layer_spec(bend, l)],
        out_specs=[ut_spec, plane_spec],
        out_shape=[jax.ShapeDtypeStruct((ng, L * G, nchunk), BF16), plane_shape],
        compiler_params=_params("parallel"),
        name="s5_chunk_sums",
    )(zs, zs, bend)
    whole_planes = pl.BlockSpec((4 * ng * P, nchunk), lambda i: (0, 0))
    xt = pl.pallas_call(
        functools.partial(_s5_scan_kernel, bsz=bsz, steps=seq // L),
        grid=(1,),
        in_specs=[whole_planes, _layer_spec(laml, l)],
        out_specs=whole_planes,
        out_shape=plane_shape,
        scratch_shapes=[pltpu.VMEM((4, nchunk, ng * P), F32), pltpu.VMEM((4, nchunk, ng * P), F32)],
        compiler_params=_params("arbitrary"),
        name="s5_state_scan",
    )(et, laml)
    half_spec = pl.BlockSpec((tile, LANES), lambda i: (i, 0))
    half_shape = jax.ShapeDtypeStruct((t, LANES), F32)
    return pl.pallas_call(
        functools.partial(_s5_out_kernel, nchunk=nct),
        grid=(t // tile,),
        in_specs=[ut_spec, plane_spec, _layer_spec(toep, l), _layer_spec(cout, l)],
        out_specs=[half_spec, half_spec],
        out_shape=[half_shape, half_shape],
        scratch_shapes=[pltpu.VMEM((L, M_S5, nct), F32)],
        compiler_params=_params("parallel"),
        name="s5_outputs",
    )(ut, xt, toep, cout)


def _merge_kernel(x_ref, hf_ref, hb_ref, zo_ref, yf_ref, ys_lo_ref, ys_hi_ref, gpre_ref, wgate_ref, bgate_ref,
                  ng_ref, wm_ref, wf_ref, wglu_ref, bglu_ref, wout_ref, gpost_ref, o_ref, mixed_sc):
    x = x_ref[...]
    h = _rms(x, gpre_ref[...]).astype(BF16)
    hs = hf_ref[...] + hb_ref[...]
    parts = []
    for hd in range(HEADS):
        blk = hs[:, hd * HEAD_DIM:(hd + 1) * HEAD_DIM]
        mu = jnp.mean(blk, axis=-1, keepdims=True)
        cen = blk - mu
        var = jnp.mean(cen * cen, axis=-1, keepdims=True)
        parts.append(cen * lax.rsqrt(var + EPS))
    hm = (jnp.concatenate(parts, axis=1) * ng_ref[...] * _sigmoid(zo_ref[...].astype(F32))).astype(BF16)
    yf = yf_ref[...].astype(BF16)
    ys = jnp.concatenate([ys_lo_ref[...].astype(BF16), ys_hi_ref[...].astype(BF16)], axis=1)
    for n in range(D_MODEL // MERGE_COLS):
        lo = n * MERGE_COLS
        cols = slice(lo, lo + MERGE_COLS)

        def gate(i):
            gcols = slice(i * D_MODEL + lo, i * D_MODEL + lo + MERGE_COLS)
            return _sigmoid(_dot(h, wgate_ref[:, gcols]) + bgate_ref[:, gcols])

        lin = _dot(ys, wglu_ref[:, cols]) + bglu_ref[:, cols]
        gcols = slice(D_MODEL + lo, D_MODEL + lo + MERGE_COLS)
        y_s = lin * _sigmoid(_dot(ys, wglu_ref[:, gcols]) + bglu_ref[:, gcols])
        mixed = gate(0) * _dot(hm, wm_ref[:, cols]) + gate(1) * _dot(yf, wf_ref[:, cols]) + gate(2) * y_s
        mixed_sc[:, cols] = mixed.astype(BF16)
    o_ref[...] = x + _rms(_dot(mixed_sc[...], wout_ref[...]), gpost_ref[...])


def _merge(x2, hf, hb, zo, yf, ys_lo, ys_hi, params, l, tile):
    t = x2.shape[0]
    rows = lambda width: pl.BlockSpec((tile, width), lambda i: (i, 0))
    return pl.pallas_call(
        _merge_kernel,
        grid=(t // tile,),
        in_specs=[rows(D_MODEL), rows(M_MLSTM), rows(M_MLSTM), rows(M_MLSTM), rows(M_FOURIER),
                  rows(LANES), rows(LANES)]
                 + [_layer_spec(a, l, (N_BRANCHES * D_MODEL, 0) if k in (1, 2) else None)
                    for k, a in enumerate(params)],
        out_specs=rows(D_MODEL),
        out_shape=jax.ShapeDtypeStruct((t, D_MODEL), F32),
        scratch_shapes=[pltpu.VMEM((tile, D_MODEL), BF16)],
        compiler_params=_params("parallel"),
        name="merge",
    )(x2, hf, hb, zo, yf, ys_lo, ys_hi, *params)


def _ffn_kernel(x_ref, gpre_ref, w1_ref, w2_ref, gpost_ref, o_ref, *, n_split):
    x = x_ref[...]
    h = _rms(x, gpre_ref[...]).astype(BF16)
    width = D_FF // n_split
    f = None
    for j in range(n_split):
        a = jnp.maximum(_dot(h, w1_ref[:, j * width:(j + 1) * width]), 0.0)
        part = _dot((a * a).astype(BF16), w2_ref[j * width:(j + 1) * width, :])
        f = part if f is None else f + part
    o_ref[...] = x + _rms(f, gpost_ref[...])


def _ffn(x2, params, l, tile):
    t = x2.shape[0]
    rows = pl.BlockSpec((tile, D_MODEL), lambda i: (i, 0))
    return pl.pallas_call(
        functools.partial(_ffn_kernel, n_split=4),
        grid=(t // tile,),
        in_specs=[rows] + [_layer_spec(a, l) for a in params],
        out_specs=rows,
        out_shape=jax.ShapeDtypeStruct((t, D_MODEL), F32),
        compiler_params=_params("parallel"),
        name="ffn",
    )(x2, *params)


def kernel(x, g_mix_pre, g_mix_post, g_ffn_pre, g_ffn_post, w_in, b_in, conv_w, conv_b, mlstm_norm_g, w_up_mlstm, w_up_fourier, s5_lam_re, s5_lam_im, s5_log_dt, s5_b_re, s5_b_im, s5_c_re, s5_c_im, s5_d, w_glu, b_glu, w_out, w_ffn1, w_ffn2):
    bsz, seq, _ = x.shape
    depth = w_in.shape[0]
    t = bsz * seq
    tile = min(512, t)
    bf = lambda a: a.astype(BF16)
    row = lambda a: a[:, None, :]
    w_gates = w_in[:, :, OFF_IG:OFF_FOURIER]
    b_gates = b_in[:, OFF_IG:OFF_FOURIER]
    n_gate_cols = N_BRANCHES * D_MODEL
    n_main = OFF_IG + M_FOURIER + M_S5
    regroup = lambda a: jnp.concatenate(
        [a[..., OFF_GATE:], a[..., :OFF_IG], a[..., OFF_FOURIER:OFF_GATE],
         jnp.zeros(a.shape[:-1] + (n_gate_cols - n_main,), a.dtype)], axis=-1)
    w_cols = bf(regroup(w_in))
    b_cols = row(regroup(b_in))
    inproj_params = (row(g_mix_pre), w_cols, b_cols,
                     bf(w_gates), row(b_gates), bf(jnp.swapaxes(w_gates, 1, 2)), b_gates[:, :, None],
                     conv_w, row(conv_b))
    merge_params = (row(g_mix_pre), w_cols, b_cols, row(mlstm_norm_g),
                    bf(w_up_mlstm), bf(w_up_fourier), bf(w_glu), row(b_glu), bf(w_out), row(g_mix_post))
    ffn_params = (row(g_ffn_pre), bf(w_ffn1), bf(w_ffn2), row(g_ffn_post))
    s5_prep = _s5_prep(s5_lam_re, s5_lam_im, s5_log_dt, s5_b_re, s5_b_im, s5_c_re, s5_c_im, s5_d)
    x2 = x.reshape(t, D_MODEL)
    for l in range(depth):
        qk, zv, zo, zf, zs, zg, zgt = _inproj(x2, inproj_params, l, tile, seq)
        hf, hb = _mlstm(qk, zv, zg, zgt, bsz, seq)
        yf = _fourier(zf, bsz, seq)
        ys_lo, ys_hi = _s5(zs, s5_prep, l, bsz, seq)
        x2 = _merge(x2, hf, hb, zo, yf, ys_lo, ys_hi, merge_params, l, tile)
        x2 = _ffn(x2, ffn_params, l, tile)
    return x2.reshape(bsz, seq, D_MODEL)
```

```python
import functools
import itertools
import math

import numpy as np
import jax
import jax.numpy as jnp
from jax import lax
from jax.experimental import pallas as pl
from jax.experimental.pallas import tpu as pltpu

F32 = jnp.float32
BF16 = jnp.bfloat16
HIGHEST = lax.Precision.HIGHEST

LANES = 128
D_MODEL = 1024
M_MLSTM = 512
HEADS = 4
HEAD_DIM = 128
CHUNK = 128
MLSTM_SUB = 4
CONV_WIDTH = 5
CONV_COLS = 256
CONV_HALO = 8
M_FOURIER = 256
FOURIER_GROUP_DIM = 64
M_S5 = 256
S5_GROUP = 16
S5_GROUPS = 16
S5_STATE = 64
S5_CHUNK = 32
S5_TILE = 4096
N_BRANCHES = 3
MERGE_COLS = 256
D_FF = 4 * D_MODEL
EPS = 1e-6

OFF_Q = 0
OFF_V = 2 * M_MLSTM
OFF_O = 3 * M_MLSTM
OFF_IG = 4 * M_MLSTM
OFF_FOURIER = OFF_IG + 4 * HEADS
OFF_S5 = OFF_FOURIER + M_FOURIER
OFF_GATE = OFF_S5 + M_S5

DFT_N2 = 64
DFT_STEP = 8

VMEM_LIMIT = 56 * 1024 * 1024

NT_DIMS = (((1,), (1,)), ((), ()))
TN_DIMS = (((0,), (0,)), ((), ()))


def _params(*sem):
    return pltpu.CompilerParams(dimension_semantics=sem, vmem_limit_bytes=VMEM_LIMIT)


def _layer_spec(a, l, col_block=None):
    width, index = col_block if col_block else (a.shape[-1], 0)
    return pl.BlockSpec((None,) + a.shape[1:-1] + (width,), lambda *_: (l,) + (0,) * (a.ndim - 2) + (index,))


def _rms(x, g):
    return x * lax.rsqrt(jnp.mean(x * x, axis=-1, keepdims=True) + EPS) * g


def _sigmoid(x):
    return 1.0 / (1.0 + jnp.exp(-x))


def _dot(a, b):
    return jnp.dot(a, b, preferred_element_type=F32)


def _dot_hi(a, b):
    return jnp.dot(a, b, precision=HIGHEST, preferred_element_type=F32)


def _split3(a):
    hi = a.astype(BF16)
    rest = a - hi.astype(F32)
    mid = rest.astype(BF16)
    return hi, mid, (rest - mid.astype(F32)).astype(BF16)


def _dot_sel(a, sel):
    hi, mid, lo = _split3(a)
    return _dot(hi, sel) + _dot(mid, sel) + _dot(lo, sel)


def _sel_dot(sel, a):
    hi, mid, lo = _split3(a)
    return _dot(sel, hi) + _dot(sel, mid) + _dot(sel, lo)


W_BLOCK = N_BRANCHES * D_MODEL
N_MAIN = OFF_IG + M_FOURIER + M_S5


def _regroup_kernel(w_ref, o_ref):
    o_ref[:, :W_BLOCK] = w_ref[:, OFF_GATE:].astype(BF16)
    o_ref[:, W_BLOCK:W_BLOCK + OFF_IG] = w_ref[:, :OFF_IG].astype(BF16)
    o_ref[:, W_BLOCK + OFF_IG:W_BLOCK + N_MAIN] = w_ref[:, OFF_FOURIER:OFF_GATE].astype(BF16)
    o_ref[:, W_BLOCK + N_MAIN:] = jnp.zeros((o_ref.shape[0], W_BLOCK - N_MAIN), BF16)


def _regroup_w_in(w_in):
    depth, d_model, n_in = w_in.shape
    rows = 256
    return pl.pallas_call(
        _regroup_kernel,
        grid=(depth, d_model // rows),
        in_specs=[pl.BlockSpec((None, rows, n_in), lambda l, i: (l, i, 0))],
        out_specs=pl.BlockSpec((None, rows, 2 * W_BLOCK), lambda l, i: (l, i, 0)),
        out_shape=jax.ShapeDtypeStruct((depth, d_model, 2 * W_BLOCK), BF16),
        compiler_params=_params("parallel", "parallel"),
        name="regroup_w_in",
    )(w_in)


def _inproj_kernel(x_ref, xp_ref, xn_ref, g_ref, wm_ref, bm_ref, wg_ref, bg_ref, wgt_ref, bgt_ref, cw_ref, cb_ref,
                   qk_ref, zv_ref, zo_ref, zf_ref, zs_ref, zg_ref, zgt_ref, ext_sc, *, tiles_per_seq):
    i = pl.program_id(0)
    tile = x_ref.shape[0]
    pad = CONV_WIDTH // 2
    h = _rms(x_ref[...], g_ref[...]).astype(BF16)
    h_halo = _rms(jnp.concatenate([xp_ref[...], xn_ref[...]], axis=0), g_ref[...]).astype(BF16)
    first = (i % tiles_per_seq) == 0
    last = (i % tiles_per_seq) == tiles_per_seq - 1

    def proj(hh, lo, hi):
        return _dot(hh, wm_ref[:, lo:hi]) + bm_ref[:, lo:hi]

    for n in range(2 * M_MLSTM // CONV_COLS):
        lo = n * CONV_COLS
        cols = slice(lo, lo + CONV_COLS)
        z_halo = proj(h_halo, OFF_Q + lo, OFF_Q + lo + CONV_COLS)
        ext_sc[0:CONV_HALO, cols] = jnp.where(first, 0.0, z_halo[:CONV_HALO])
        ext_sc[CONV_HALO:CONV_HALO + tile, cols] = proj(h, OFF_Q + lo, OFF_Q + lo + CONV_COLS)
        ext_sc[CONV_HALO + tile:, cols] = jnp.where(last, 0.0, z_halo[CONV_HALO:])
        ext = ext_sc[:, cols]
        acc = cb_ref[:, cols]
        for j in range(CONV_WIDTH):
            shifted = ext if j == pad else pltpu.roll(ext, (pad - j) % ext.shape[0], 0)
            acc = acc + cw_ref[j:j + 1, cols] * shifted[CONV_HALO:CONV_HALO + tile]
        half = 0.5 * acc
        qk = half + half * jnp.tanh(half)
        if lo < M_MLSTM:
            qk = qk * (HEAD_DIM ** -0.5)
        qk_ref[:, cols] = qk.astype(BF16)

    zv_ref[...] = proj(h, OFF_V, OFF_O).astype(BF16)
    zo_ref[...] = proj(h, OFF_O, OFF_IG).astype(BF16)
    zf_ref[...] = proj(h, OFF_IG, OFF_IG + M_FOURIER)
    zs_ref[...] = proj(h, OFF_IG + M_FOURIER, OFF_IG + M_FOURIER + M_S5)
    zg_ref[...] = _dot(h, wg_ref[...]) + bg_ref[...]
    zgt_ref[...] = lax.dot_general(wgt_ref[...], h, NT_DIMS, preferred_element_type=F32) + bgt_ref[...]


def _inproj(x2, params, l, tile, seq):
    t = x2.shape[0]
    n_gate = 4 * HEADS
    hpt = tile // CONV_HALO
    n_halo = t // CONV_HALO
    rows = lambda width: pl.BlockSpec((tile, width), lambda i: (i, 0))
    cols = lambda height: pl.BlockSpec((height, tile), lambda i: (0, i))
    return pl.pallas_call(
        functools.partial(_inproj_kernel, tiles_per_seq=seq // tile),
        grid=(t // tile,),
        in_specs=[rows(D_MODEL),
                  pl.BlockSpec((CONV_HALO, D_MODEL), lambda i: (jnp.maximum(i * hpt - 1, 0), 0)),
                  pl.BlockSpec((CONV_HALO, D_MODEL), lambda i: (jnp.minimum((i + 1) * hpt, n_halo - 1), 0))]
                 + [_layer_spec(a, l, (W_BLOCK, 1) if k in (1, 2) else None)
                    for k, a in enumerate(params)],
        out_specs=[rows(2 * M_MLSTM), rows(M_MLSTM), rows(M_MLSTM), rows(M_FOURIER), rows(M_S5),
                   rows(n_gate), cols(n_gate)],
        out_shape=[jax.ShapeDtypeStruct((t, 2 * M_MLSTM), BF16), jax.ShapeDtypeStruct((t, M_MLSTM), BF16),
                   jax.ShapeDtypeStruct((t, M_MLSTM), BF16), jax.ShapeDtypeStruct((t, M_FOURIER), F32),
                   jax.ShapeDtypeStruct((t, M_S5), F32), jax.ShapeDtypeStruct((t, n_gate), F32),
                   jax.ShapeDtypeStruct((n_gate, t), F32)],
        scratch_shapes=[pltpu.VMEM((tile + 2 * CONV_HALO, 2 * M_MLSTM), F32)],
        compiler_params=_params("parallel"),
        name="inproj",
    )(x2, x2, x2, *params)


def _log_sigmoid(x):
    return jnp.minimum(x, 0.0) - jnp.log1p(jnp.exp(-jnp.abs(x)))


def _mlstm_kernel(qk_f, v_f, g_f, gt_f, qk_b, v_b, g_b, gt_b, hf_ref, hb_ref, state_sc, m_sc):
    c = pl.program_id(1)
    L = CHUNK

    @pl.when(c == 0)
    def _():
        state_sc[...] = jnp.zeros_like(state_sc)
        m_sc[...] = jnp.zeros_like(m_sc)

    row = lax.broadcasted_iota(jnp.int32, (L, L), 0)
    col = lax.broadcasted_iota(jnp.int32, (L, L), 1)
    lower = row >= col
    upper = row <= col
    lower_f = jnp.where(lower, 1.0, 0.0).astype(BF16)
    upper_f = jnp.where(upper, 1.0, 0.0).astype(BF16)
    ones_v = jnp.ones((L, HEAD_DIM), F32)

    streams = ((0, qk_f, v_f, g_f, gt_f, hf_ref), (1, qk_b, v_b, g_b, gt_b, hb_ref))
    for sub, (d, qk_ref, v_ref, g_ref, gt_ref, out_ref) in itertools.product(range(MLSTM_SUB), streams):
        r0 = (sub if d == 0 else MLSTM_SUB - 1 - sub) * L
        rows = slice(r0, r0 + L)
        gates = g_ref[rows, :]
        gates_t = gt_ref[:, rows]
        lf_cols = _log_sigmoid(gates[:, 2 * HEADS:])
        lf_rows = _log_sigmoid(gates_t[2 * HEADS:, :])
        if d == 0:
            b_cols = _sel_dot(lower_f, lf_cols)
            b_rows = _dot_sel(lf_rows, upper_f)
            mask = lower
        else:
            b_cols = _sel_dot(upper_f, lf_cols)
            b_rows = _dot_sel(lf_rows, lower_f)
            mask = upper

        for hd in range(HEADS):
            k_idx = d * HEADS + hd
            lo = hd * HEAD_DIM
            q = qk_ref[rows, lo:lo + HEAD_DIM]
            k = qk_ref[rows, M_MLSTM + lo:M_MLSTM + lo + HEAD_DIM]
            v = v_ref[rows, lo:lo + HEAD_DIM]
            b_c = jnp.broadcast_to(b_cols[:, k_idx:k_idx + 1], (L, L))
            ig_c = jnp.broadcast_to(gates[:, k_idx:k_idx + 1], (L, L))
            b_r = b_rows[k_idx:k_idx + 1, :]
            ig_r = gates_t[k_idx:k_idx + 1, :]
            g_tot = b_c[L - 1:L, :] if d == 0 else b_c[0:1, :]
            m_prev = m_sc[k_idx:k_idx + 1, :]

            d_log = jnp.where(mask, b_c - b_r + ig_r, -1e30)
            inter_log = b_c + m_prev
            m_t = jnp.maximum(inter_log, jnp.max(d_log, axis=1, keepdims=True))
            scores = lax.dot_general(q, k, NT_DIMS, preferred_element_type=F32) * jnp.exp(d_log - m_t)
            inter_w = jnp.exp(inter_log - m_t)
            st = state_sc[k_idx]
            v_ext = jnp.concatenate([v, ones_v.astype(BF16)], axis=1)
            res = _dot(scores.astype(BF16), v_ext) + _dot((inter_w * q.astype(F32)).astype(BF16), st.astype(BF16))
            num = res[:, :HEAD_DIM]
            den = res[:, HEAD_DIM:]
            out_ref[rows, lo:lo + HEAD_DIM] = num / jnp.maximum(jnp.abs(den), jnp.exp(-m_t))

            a_c = g_tot - b_c + ig_c
            a_max = jnp.max(a_c, axis=0, keepdims=True)
            w_c = jnp.exp(a_c - a_max)
            vw = jnp.concatenate([v.astype(F32) * w_c, w_c], axis=1).astype(BF16)
            st_loc = lax.dot_general(k, vw, TN_DIMS, preferred_element_type=F32)
            m_new = jnp.maximum(g_tot + m_prev, a_max)
            s_old = jnp.exp(g_tot + m_prev - m_new)
            s_new = jnp.exp(a_max - m_new)
            s_old2 = jnp.concatenate([s_old, s_old], axis=1)
            s_new2 = jnp.concatenate([s_new, s_new], axis=1)
            state_sc[k_idx] = s_old2 * st + s_new2 * st_loc
            m_sc[k_idx:k_idx + 1, :] = m_new


def _mlstm(qk, zv, zg, zgt, bsz, seq):
    t = bsz * seq
    blk = CHUNK * MLSTM_SUB
    nc = seq // blk
    n_gate = 4 * HEADS

    def fwd(b, c):
        return b * nc + c

    def bwd(b, c):
        return b * nc + nc - 1 - c

    def specs(pos):
        return [pl.BlockSpec((blk, 2 * M_MLSTM), lambda b, c: (pos(b, c), 0)),
                pl.BlockSpec((blk, M_MLSTM), lambda b, c: (pos(b, c), 0)),
                pl.BlockSpec((blk, n_gate), lambda b, c: (pos(b, c), 0)),
                pl.BlockSpec((n_gate, blk), lambda b, c: (0, pos(b, c)))]

    out_shape = jax.ShapeDtypeStruct((t, M_MLSTM), F32)
    return pl.pallas_call(
        _mlstm_kernel,
        grid=(bsz, nc),
        in_specs=specs(fwd) + specs(bwd),
        out_specs=[pl.BlockSpec((blk, M_MLSTM), lambda b, c: (fwd(b, c), 0)),
                   pl.BlockSpec((blk, M_MLSTM), lambda b, c: (bwd(b, c), 0))],
        out_shape=[out_shape, out_shape],
        scratch_shapes=[pltpu.VMEM((2 * HEADS, HEAD_DIM, 2 * HEAD_DIM), F32),
                        pltpu.VMEM((2 * HEADS, HEAD_DIM), F32)],
        compiler_params=_params("arbitrary", "arbitrary"),
        name="mlstm",
    )(qk, zv, zg, zgt, qk, zv, zg, zgt)


def _dft_constants(seq):
    n1, n2 = seq // DFT_N2, DFT_N2
    k1 = np.arange(n1)[:, None, None]
    s2 = np.arange(n2)[None, None, :]
    s1 = np.arange(n1)[None, :, None]
    ang = -2.0 * np.pi * ((k1 * (n2 * s1 + s2)) % seq) / seq
    stage1 = np.concatenate([np.cos(ang), np.sin(ang)], axis=0)
    stage1 = np.ascontiguousarray(stage1.transpose(2, 0, 1))
    a2 = 2.0 * np.pi * np.outer(np.arange(n2), np.arange(n2)) / n2
    c2, sn2 = np.cos(a2), np.sin(a2)
    stage2 = np.block([[c2, sn2], [-sn2, c2]])
    ag = 2.0 * np.pi * np.outer(np.arange(FOURIER_GROUP_DIM), np.arange(FOURIER_GROUP_DIM)) / FOURIER_GROUP_DIM
    scale = 1.0 / math.sqrt(seq * FOURIER_GROUP_DIM)
    eye = np.eye(LANES // FOURIER_GROUP_DIM)
    group = np.stack([np.kron(eye, np.cos(ag)), np.kron(eye, np.sin(ag))]) * scale
    as_bf16 = lambda a: jnp.asarray(a, F32).astype(BF16)
    return as_bf16(stage1), as_bf16(stage2), as_bf16(group)


def _fourier_kernel(u_ref, w1_ref, w2_ref, wg_ref, y_ref, u_sc, b_sc, y_sc, *, n1):
    n2 = DFT_N2
    rows = n1 * DFT_STEP
    for sb in range(n2 // DFT_STEP):
        u_sc[sb % 2] = u_ref[:, sb * DFT_STEP:(sb + 1) * DFT_STEP, :].reshape(rows, LANES)
        for j in range(DFT_STEP):
            pick = pl.ds(j, n1, stride=DFT_STEP)
            res = _dot(w1_ref[sb * DFT_STEP + j], u_sc[sb % 2, pick, :].astype(BF16))
            b_sc[sb, 0, pick, :] = res[:n1]
            b_sc[sb, 1, pick, :] = res[n1:]
    for kb in range(n1 // DFT_STEP):
        def gather(part, j):
            r0 = (kb * DFT_STEP + j) * DFT_STEP
            return jnp.concatenate([b_sc[sb, part, r0:r0 + DFT_STEP, :] for sb in range(n2 // DFT_STEP)], axis=0)

        stacked = jnp.concatenate([jnp.concatenate([gather(0, j), gather(1, j)], axis=0) for j in range(DFT_STEP)],
                                  axis=1).astype(BF16)
        z = _dot(w2_ref[...], stacked).astype(BF16)
        zr = jnp.concatenate([z[:n2, j * LANES:(j + 1) * LANES] for j in range(DFT_STEP)], axis=0)
        zi = jnp.concatenate([z[n2:, j * LANES:(j + 1) * LANES] for j in range(DFT_STEP)], axis=0)
        y = _dot(zr, wg_ref[0]) + _dot(zi, wg_ref[1])
        for j in range(DFT_STEP):
            y_sc[kb % 2, pl.ds(j, n2, stride=DFT_STEP), :] = y[j * n2:(j + 1) * n2, :]
        y_ref[:, kb * DFT_STEP:(kb + 1) * DFT_STEP, :] = y_sc[kb % 2].reshape(n2, DFT_STEP, LANES)


def _fourier(zf, bsz, seq):
    n1, n2 = seq // DFT_N2, DFT_N2
    stage1, stage2, group = _dft_constants(seq)
    whole = lambda a: pl.BlockSpec(a.shape, lambda b, h: (0,) * a.ndim)
    y = pl.pallas_call(
        functools.partial(_fourier_kernel, n1=n1),
        grid=(bsz, M_FOURIER // LANES),
        in_specs=[pl.BlockSpec((None, n1, n2, LANES), lambda b, h: (b, 0, 0, h)),
                  whole(stage1), whole(stage2), whole(group)],
        out_specs=pl.BlockSpec((None, n2, n1, LANES), lambda b, h: (b, 0, 0, h)),
        out_shape=jax.ShapeDtypeStruct((bsz, n2, n1, M_FOURIER), F32),
        scratch_shapes=[pltpu.VMEM((2, n1 * DFT_STEP, LANES), F32),
                        pltpu.VMEM((n2 // DFT_STEP, 2, n1 * DFT_STEP, LANES), F32),
                        pltpu.VMEM((2, n2 * DFT_STEP, LANES), F32)],
        compiler_params=_params("parallel", "parallel"),
        name="fourier",
    )(zf.reshape(bsz, n1, n2, M_FOURIER), stage1, stage2, group)
    return y.reshape(bsz * seq, M_FOURIER)


def _s5_constants():
    L, G = S5_CHUNK, S5_GROUP
    lane = np.arange(2 * L * G)
    expo = np.arange(LANES)[:, None]
    rep = (lane[None, :] % G == np.arange(G)[:, None])
    lag = (L - 1) - lane // G
    spread = np.stack([(expo == lag) & (lag >= 0), (expo == -lag) & (lag <= 0) & (lag > -L)])
    step = lane[:L * G] // G
    spread_s = np.stack([expo == (L - 1) - step, expo == step])
    return tuple(jnp.asarray(m, BF16) for m in (rep, spread, spread_s))


def _s5_prep_kernel(lamc_re, lamc_im, ldt_ref, lam4_re, lam4_im, ldt4_ref, b_re, b_im, c_re, c_im, c4_re, c4_im,
                    d_ref, rep_ref, spread_ref, spread_s_ref, toep_ref, bend_ref, cout_ref, laml_ref):
    L, P, G = S5_CHUNK, S5_STATE, S5_GROUP
    W = 2 * L * G
    shift = G.bit_length() - 1
    nbits = L.bit_length()

    def cmul(ar, ai, br, bi):
        return ar * br - ai * bi, ar * bi + ai * br

    def lam_bar(lre, lim, dt):
        mag = jnp.exp(lre * dt)
        return mag * jnp.cos(lim * dt), mag * jnp.sin(lim * dt)

    def power_table(base_r, base_i, expo):
        tr = jnp.ones(expo.shape, F32)
        ti = jnp.zeros(expo.shape, F32)
        sr, si = base_r, base_i
        for bit in range(nbits):
            nr, ni = cmul(tr, ti, sr, si)
            has = ((expo >> bit) & 1) == 1
            tr, ti = jnp.where(has, nr, tr), jnp.where(has, ni, ti)
            sr, si = cmul(sr, si, sr, si)
        return tr, ti

    def spread_dot(xr, xi, sel):
        n = xr.shape[0]
        out = _dot(jnp.concatenate(_split3(xr)[:2] + _split3(xi)[:2], axis=0), sel)
        return out[:n] + out[n:2 * n], out[2 * n:3 * n] + out[3 * n:]

    def dot3(c, m):
        c_hi, c_mid, _ = _split3(c)
        m_hi, m_mid, _ = _split3(m)
        return _dot(c_hi, m_hi) + _dot(c_mid, m_hi) + _dot(c_hi, m_mid)

    rep = rep_ref[...]
    lane_g = lax.broadcasted_iota(jnp.int32, (G, W), 1)
    row_g = lax.broadcasted_iota(jnp.int32, (G, W), 0)
    expo_tab = jnp.minimum(lax.broadcasted_iota(jnp.int32, (P, LANES), 1), L)

    d_tiled = _dot_sel(jnp.broadcast_to(d_ref[0], (G, G)), rep)
    gen = jnp.where(((lane_g & (G - 1)) == row_g) & ((lane_g >> shift) == L - 1), d_tiled, 0.0)
    for d in range(2):
        dt = jnp.exp(ldt_ref[d, 0])
        lr, li = lamc_re[d, 0], lamc_im[d, 0]
        lbr, lbi = lam_bar(lr, li, dt)
        den = lr * lr + li * li
        fr, fi = ((lbr - 1.0) * lr + lbi * li) / den, (lbi * lr - (lbr - 1.0) * li) / den
        bbr, bbi = cmul(fr, fi, b_re[d, 0], b_im[d, 0])
        btr, bti = spread_dot(bbr, bbi, rep)
        tab_r, tab_i = power_table(lbr, lbi, expo_tab)
        pr, pi = spread_dot(tab_r, tab_i, spread_ref[d])
        mr, mi = cmul(pr, pi, btr, bti)
        gen = gen + dot3(c_re[d, 0], mr) - dot3(c_im[d, 0], mi)
        half = L * G
        qr, qi = spread_dot(tab_r, tab_i, spread_s_ref[d])
        xr, xi = cmul(qr, qi, btr[:, :half], bti[:, :half])
        bend_ref[0, 2 * d] = xr.astype(BF16)
        bend_ref[0, 2 * d + 1] = xi.astype(BF16)

    for t in range(L):
        a = (L - 1 - t) * G
        toep_ref[0, t * G:(t + 1) * G, :] = gen[:, a:a + L * G].astype(BF16)

    lane4 = lax.broadcasted_iota(jnp.int32, (L, 4 * P), 1)
    step4 = lax.broadcasted_iota(jnp.int32, (L, 4 * P), 0)
    lb4r, lb4i = lam_bar(lam4_re[0], lam4_im[0], jnp.exp(ldt4_ref[0]))
    pr, pi = power_table(lb4r, lb4i, jnp.where(lane4 < 2 * P, step4 + 1, L - step4))
    cr, ci = c4_re[0], c4_im[0]
    plane_bit = P.bit_length() - 1
    re_c = ((lax.broadcasted_iota(jnp.int32, (G, 4 * P), 1) >> plane_bit) & 1) == 0
    for t in range(L):
        re_part, im_part = cmul(cr, ci, pr[t:t + 1], pi[t:t + 1])
        cout_ref[0, t * G:(t + 1) * G, :] = jnp.where(re_c, re_part, -im_part).astype(BF16)
    re_1 = ((lax.broadcasted_iota(jnp.int32, (1, 4 * P), 1) >> plane_bit) & 1) == 0
    lr, li = power_table(lb4r, lb4i, jnp.full((1, 4 * P), L, jnp.int32))
    laml_ref[0] = jnp.where(re_1, lr, li)


def _s5_prep(lam_re, lam_im, log_dt, b_re, b_im, c_re, c_im, d_skip):
    ng, P, G, L = S5_GROUPS, S5_STATE, S5_GROUP, S5_CHUNK
    depth = lam_re.shape[0]
    tile4 = lambda a: jnp.concatenate([a[:, 0], a[:, 0], a[:, 1], a[:, 1]], axis=-1)
    ldt4 = jnp.repeat(tile4(log_dt[..., None]), P, axis=-1)[:, :, None, :]
    consts = _s5_constants()
    spec = lambda *tail: pl.BlockSpec((None, 2, 1) + tail, lambda l, g: (l, 0, g) + (0,) * len(tail))
    per_g = lambda *tail: pl.BlockSpec((None, 1) + tail, lambda l, g: (l, g) + (0,) * len(tail))
    whole = lambda a: pl.BlockSpec(a.shape, lambda l, g: (0,) * a.ndim)
    return pl.pallas_call(
        _s5_prep_kernel,
        grid=(depth, ng),
        in_specs=[spec(P, 1), spec(P, 1), spec(1, 1), per_g(1, 4 * P), per_g(1, 4 * P), per_g(1, 4 * P),
                  spec(P, G), spec(P, G), spec(G, P), spec(G, P), per_g(G, 4 * P), per_g(G, 4 * P), per_g(1, G)]
                 + [whole(m) for m in consts],
        out_specs=[per_g(L * G, L * G), per_g(4, P, L * G), per_g(L * G, 4 * P), per_g(1, 4 * P)],
        out_shape=[jax.ShapeDtypeStruct((depth, ng, L * G, L * G), BF16),
                   jax.ShapeDtypeStruct((depth, ng, 4, P, L * G), BF16),
                   jax.ShapeDtypeStruct((depth, ng, L * G, 4 * P), BF16),
                   jax.ShapeDtypeStruct((depth, ng, 1, 4 * P), F32)],
        compiler_params=_params("parallel", "parallel"),
        name="s5_prep",
    )(lam_re[..., None], lam_im[..., None], log_dt[..., None, None],
      tile4(lam_re)[:, :, None, :], tile4(lam_im)[:, :, None, :], ldt4,
      b_re, b_im, c_re, c_im, tile4(c_re), tile4(c_im), d_skip[:, :, None, :], *consts)


def _s5_sums_kernel(lo_ref, hi_ref, bend_ref, ut_ref, et_ref, *, nchunk):
    L, P, G = S5_CHUNK, S5_STATE, S5_GROUP
    per_half = LANES // G
    for half, z_ref in enumerate((lo_ref, hi_ref)):
        for s in range(L):
            zt = z_ref[pl.ds(s, nchunk, stride=L), :].T.astype(BF16)
            for gl in range(per_half):
                ut_ref[half * per_half + gl, s * G:(s + 1) * G, :] = zt[gl * G:(gl + 1) * G, :]
    for g in range(S5_GROUPS):
        for plane in range(4):
            row = plane * S5_GROUPS * P + g * P
            et_ref[row:row + P, :] = _dot(bend_ref[g, plane], ut_ref[g])


def _s5_scan_kernel(et_ref, lam_ref, xt_ref, e_sc, x_sc, *, bsz, steps):
    width = S5_GROUPS * S5_STATE
    for plane in range(4):
        e_sc[plane] = et_ref[plane * width:(plane + 1) * width, :].T
    ar_f, ai_f, ar_b, ai_b = lam_ref[0], lam_ref[1], lam_ref[2], lam_ref[3]
    zero = jnp.zeros_like(ar_f)
    for b in range(bsz):
        def body(i, carry):
            xr, xi, yr, yi = carry
            rf = b * steps + i
            rb = b * steps + steps - 1 - i
            x_sc[0, pl.ds(rf, 1), :] = xr
            x_sc[1, pl.ds(rf, 1), :] = xi
            x_sc[2, pl.ds(rb, 1), :] = yr
            x_sc[3, pl.ds(rb, 1), :] = yi
            xr, xi = (ar_f * xr - ai_f * xi + e_sc[0, pl.ds(rf, 1), :],
                      ar_f * xi + ai_f * xr + e_sc[1, pl.ds(rf, 1), :])
            yr, yi = (ar_b * yr - ai_b * yi + e_sc[2, pl.ds(rb, 1), :],
                      ar_b * yi + ai_b * yr + e_sc[3, pl.ds(rb, 1), :])
            return xr, xi, yr, yi
        lax.fori_loop(0, steps, body, (zero, zero, zero, zero))
    for plane in range(4):
        xt_ref[plane * width:(plane + 1) * width, :] = x_sc[plane].T


def _gelu_tanh(x):
    return 0.5 * x * (1.0 + jnp.tanh(math.sqrt(2.0 / math.pi) * (x + 0.044715 * (x * x * x))))


def _s5_out_kernel(ut_ref, xt_ref, toep_ref, cout_ref, lo_ref, hi_ref, yt_sc, *, nchunk):
    L, P, G = S5_CHUNK, S5_STATE, S5_GROUP
    width = S5_GROUPS * P
    for g in range(S5_GROUPS):
        xg = jnp.concatenate([xt_ref[plane * width + g * P:plane * width + (g + 1) * P, :] for plane in range(4)],
                             axis=0).astype(BF16)
        yt = _gelu_tanh(_dot(toep_ref[g], ut_ref[g]) + _dot(cout_ref[g], xg))
        for t in range(L):
            yt_sc[t, g * G:(g + 1) * G, :] = yt[t * G:(t + 1) * G, :]
    for t in range(L):
        y = yt_sc[t].T
        lo_ref[pl.ds(t, nchunk, stride=L), :] = y[:, :LANES]
        hi_ref[pl.ds(t, nchunk, stride=L), :] = y[:, LANES:]


def _s5(zs, prep, l, bsz, seq):
    ng, P, G, L = S5_GROUPS, S5_STATE, S5_GROUP, S5_CHUNK
    toep, bend, cout, laml = prep
    depth = toep.shape[0]
    laml = laml.reshape(depth, ng, 4, P).transpose(0, 2, 1, 3).reshape(depth, 4, 1, ng * P)
    t = bsz * seq
    tile = min(S5_TILE, t)
    nct = tile // L
    nchunk = t // L
    ut_spec = pl.BlockSpec((ng, L * G, nct), lambda i: (0, 0, i))
    plane_spec = pl.BlockSpec((4 * ng * P, nct), lambda i: (0, i))
    plane_shape = jax.ShapeDtypeStruct((4 * ng * P, nchunk), F32)
    ut, et = pl.pallas_call(
        functools.partial(_s5_sums_kernel, nchunk=nct),
        grid=(t // tile,),
        in_specs=[pl.BlockSpec((tile, LANES), lambda i: (i, 0)), pl.BlockSpec((tile, LANES), lambda i: (i, 1)),
                  _layer_spec(bend, l)],
        out_specs=[ut_spec, plane_spec],
        out_shape=[jax.ShapeDtypeStruct((ng, L * G, nchunk), BF16), plane_shape],
        compiler_params=_params("parallel"),
        name="s5_chunk_sums",
    )(zs, zs, bend)
    whole_planes = pl.BlockSpec((4 * ng * P, nchunk), lambda i: (0, 0))
    xt = pl.pallas_call(
        functools.partial(_s5_scan_kernel, bsz=bsz, steps=seq // L),
        grid=(1,),
        in_specs=[whole_planes, _layer_spec(laml, l)],
        out_specs=whole_planes,
        out_shape=plane_shape,
        scratch_shapes=[pltpu.VMEM((4, nchunk, ng * P), F32), pltpu.VMEM((4, nchunk, ng * P), F32)],
        compiler_params=_params("arbitrary"),
        name="s5_state_scan",
    )(et, laml)
    half_spec = pl.BlockSpec((tile, LANES), lambda i: (i, 0))
    half_shape = jax.ShapeDtypeStruct((t, LANES), F32)
    return pl.pallas_call(
        functools.partial(_s5_out_kernel, nchunk=nct),
        grid=(t // tile,),
        in_specs=[ut_spec, plane_spec, _layer_spec(toep, l), _layer_spec(cout, l)],
        out_specs=[half_spec, half_spec],
        out_shape=[half_shape, half_shape],
        scratch_shapes=[pltpu.VMEM((L, M_S5, nct), F32)],
        compiler_params=_params("parallel"),
        name="s5_outputs",
    )(ut, xt, toep, cout)


def _merge_kernel(x_ref, hf_ref, hb_ref, zo_ref, yf_ref, ys_lo_ref, ys_hi_ref, gpre_ref, wgate_ref, bgate_ref,
                  ng_ref, wm_ref, wf_ref, wglu_ref, bglu_ref, wout_ref, gpost_ref, o_ref, mixed_sc):
    x = x_ref[...]
    h = _rms(x, gpre_ref[...]).astype(BF16)
    hs = hf_ref[...] + hb_ref[...]
    parts = []
    for hd in range(HEADS):
        blk = hs[:, hd * HEAD_DIM:(hd + 1) * HEAD_DIM]
        mu = jnp.mean(blk, axis=-1, keepdims=True)
        cen = blk - mu
        var = jnp.mean(cen * cen, axis=-1, keepdims=True)
        parts.append(cen * lax.rsqrt(var + EPS))
    hm = (jnp.concatenate(parts, axis=1) * ng_ref[...] * _sigmoid(zo_ref[...].astype(F32))).astype(BF16)
    yf = yf_ref[...].astype(BF16)
    ys = jnp.concatenate([ys_lo_ref[...].astype(BF16), ys_hi_ref[...].astype(BF16)], axis=1)
    for n in range(D_MODEL // MERGE_COLS):
        lo = n * MERGE_COLS
        cols = slice(lo, lo + MERGE_COLS)

        def gate(i):
            gcols = slice(i * D_MODEL + lo, i * D_MODEL + lo + MERGE_COLS)
            return _sigmoid(_dot(h, wgate_ref[:, gcols]) + bgate_ref[:, gcols])

        lin = _dot(ys, wglu_ref[:, cols]) + bglu_ref[:, cols]
        gcols = slice(D_MODEL + lo, D_MODEL + lo + MERGE_COLS)
        y_s = lin * _sigmoid(_dot(ys, wglu_ref[:, gcols]) + bglu_ref[:, gcols])
        mixed = gate(0) * _dot(hm, wm_ref[:, cols]) + gate(1) * _dot(yf, wf_ref[:, cols]) + gate(2) * y_s
        mixed_sc[:, cols] = mixed.astype(BF16)
    o_ref[...] = x + _rms(_dot(mixed_sc[...], wout_ref[...]), gpost_ref[...])


def _merge(x2, hf, hb, zo, yf, ys_lo, ys_hi, params, l, tile):
    t = x2.shape[0]
    rows = lambda width: pl.BlockSpec((tile, width), lambda i: (i, 0))
    return pl.pallas_call(
        _merge_kernel,
        grid=(t // tile,),
        in_specs=[rows(D_MODEL), rows(M_MLSTM), rows(M_MLSTM), rows(M_MLSTM), rows(M_FOURIER),
                  rows(LANES), rows(LANES)]
                 + [_layer_spec(a, l, (W_BLOCK, 0) if k in (1, 2) else None)
                    for k, a in enumerate(params)],
        out_specs=rows(D_MODEL),
        out_shape=jax.ShapeDtypeStruct((t, D_MODEL), F32),
        scratch_shapes=[pltpu.VMEM((tile, D_MODEL), BF16)],
        compiler_params=_params("parallel"),
        name="merge",
    )(x2, hf, hb, zo, yf, ys_lo, ys_hi, *params)


def _ffn_kernel(x_ref, gpre_ref, w1_ref, w2_ref, gpost_ref, o_ref, *, n_split):
    x = x_ref[...]
    h = _rms(x, gpre_ref[...]).astype(BF16)
    width = D_FF // n_split
    f = None
    for j in range(n_split):
        a = jnp.maximum(_dot(h, w1_ref[:, j * width:(j + 1) * width]), 0.0)
        part = _dot((a * a).astype(BF16), w2_ref[j * width:(j + 1) * width, :])
        f = part if f is None else f + part
    o_ref[...] = x + _rms(f, gpost_ref[...])


def _ffn(x2, params, l, tile):
    t = x2.shape[0]
    rows = pl.BlockSpec((tile, D_MODEL), lambda i: (i, 0))
    return pl.pallas_call(
        functools.partial(_ffn_kernel, n_split=4),
        grid=(t // tile,),
        in_specs=[rows] + [_layer_spec(a, l) for a in params],
        out_specs=rows,
        out_shape=jax.ShapeDtypeStruct((t, D_MODEL), F32),
        compiler_params=_params("parallel"),
        name="ffn",
    )(x2, *params)


def kernel(x, g_mix_pre, g_mix_post, g_ffn_pre, g_ffn_post, w_in, b_in, conv_w, conv_b, mlstm_norm_g, w_up_mlstm, w_up_fourier, s5_lam_re, s5_lam_im, s5_log_dt, s5_b_re, s5_b_im, s5_c_re, s5_c_im, s5_d, w_glu, b_glu, w_out, w_ffn1, w_ffn2):
    bsz, seq, _ = x.shape
    depth = w_in.shape[0]
    t = bsz * seq
    tile = min(512, t)
    bf = lambda a: a.astype(BF16)
    row = lambda a: a[:, None, :]
    w_gates = w_in[:, :, OFF_IG:OFF_FOURIER]
    b_gates = b_in[:, OFF_IG:OFF_FOURIER]
    w_cols = _regroup_w_in(w_in)
    b_cols = row(jnp.concatenate([b_in[:, OFF_GATE:], b_in[:, :OFF_IG], b_in[:, OFF_FOURIER:OFF_GATE],
                                  jnp.zeros((depth, W_BLOCK - N_MAIN), b_in.dtype)], axis=1))
    inproj_params = (row(g_mix_pre), w_cols, b_cols,
                     bf(w_gates), row(b_gates), bf(jnp.swapaxes(w_gates, 1, 2)), b_gates[:, :, None],
                     conv_w, row(conv_b))
    merge_params = (row(g_mix_pre), w_cols, b_cols, row(mlstm_norm_g),
                    bf(w_up_mlstm), bf(w_up_fourier), bf(w_glu), row(b_glu), bf(w_out), row(g_mix_post))
    ffn_params = (row(g_ffn_pre), bf(w_ffn1), bf(w_ffn2), row(g_ffn_post))
    s5_prep = _s5_prep(s5_lam_re, s5_lam_im, s5_log_dt, s5_b_re, s5_b_im, s5_c_re, s5_c_im, s5_d)
    x2 = x.reshape(t, D_MODEL)
    for l in range(depth):
        qk, zv, zo, zf, zs, zg, zgt = _inproj(x2, inproj_params, l, tile, seq)
        hf, hb = _mlstm(qk, zv, zg, zgt, bsz, seq)
        yf = _fourier(zf, bsz, seq)
        ys_lo, ys_hi = _s5(zs, s5_prep, l, bsz, seq)
        x2 = _merge(x2, hf, hb, zo, yf, ys_lo, ys_hi, merge_params, l, tile)
        x2 = _ffn(x2, ffn_params, l, tile)
    return x2.reshape(bsz, seq, D_MODEL)
```

```python
import functools
import itertools
import math

import numpy as np
import jax
import jax.numpy as jnp
from jax import lax
from jax.experimental import pallas as pl
from jax.experimental.pallas import tpu as pltpu

F32 = jnp.float32
BF16 = jnp.bfloat16
HIGHEST = lax.Precision.HIGHEST

LANES = 128
D_MODEL = 1024
M_MLSTM = 512
HEADS = 4
HEAD_DIM = 128
CHUNK = 128
MLSTM_SUB = 4
CONV_WIDTH = 5
CONV_COLS = 256
CONV_HALO = 8
M_FOURIER = 256
FOURIER_GROUP_DIM = 64
M_S5 = 256
S5_GROUP = 16
S5_GROUPS = 16
S5_STATE = 64
S5_CHUNK = 32
S5_TILE = 4096
N_BRANCHES = 3
MERGE_COLS = 256
D_FF = 4 * D_MODEL
EPS = 1e-6

OFF_Q = 0
OFF_V = 2 * M_MLSTM
OFF_O = 3 * M_MLSTM
OFF_IG = 4 * M_MLSTM
OFF_FOURIER = OFF_IG + 4 * HEADS
OFF_S5 = OFF_FOURIER + M_FOURIER
OFF_GATE = OFF_S5 + M_S5

DFT_N2 = 64
DFT_STEP = 8

VMEM_LIMIT = 56 * 1024 * 1024

NT_DIMS = (((1,), (1,)), ((), ()))
TN_DIMS = (((0,), (0,)), ((), ()))


def _params(*sem):
    return pltpu.CompilerParams(dimension_semantics=sem, vmem_limit_bytes=VMEM_LIMIT)


def _layer_spec(a, l, col_block=None):
    width, index = col_block if col_block else (a.shape[-1], 0)
    return pl.BlockSpec((None,) + a.shape[1:-1] + (width,), lambda *_: (l,) + (0,) * (a.ndim - 2) + (index,),
                        pipeline_mode=pl.Buffered(1))


def _rms(x, g):
    return x * lax.rsqrt(jnp.mean(x * x, axis=-1, keepdims=True) + EPS) * g


def _sigmoid(x):
    return 1.0 / (1.0 + jnp.exp(-x))


def _dot(a, b):
    return jnp.dot(a, b, preferred_element_type=F32)


def _dot_hi(a, b):
    return jnp.dot(a, b, precision=HIGHEST, preferred_element_type=F32)


def _split3(a):
    hi = a.astype(BF16)
    rest = a - hi.astype(F32)
    mid = rest.astype(BF16)
    return hi, mid, (rest - mid.astype(F32)).astype(BF16)


def _dot_sel(a, sel):
    hi, mid, lo = _split3(a)
    return _dot(hi, sel) + _dot(mid, sel) + _dot(lo, sel)


def _sel_dot(sel, a):
    hi, mid, lo = _split3(a)
    return _dot(sel, hi) + _dot(sel, mid) + _dot(sel, lo)


W_BLOCK = N_BRANCHES * D_MODEL
N_MAIN = OFF_IG + M_FOURIER + M_S5


def _inproj_kernel(x_ref, xp_ref, xn_ref, g_ref, wm_ref, bm_ref, wg_ref, bg_ref, wgt_ref, bgt_ref, cw_ref, cb_ref,
                   qk_ref, zv_ref, zo_ref, zf_ref, zs_ref, zg_ref, zgt_ref, ext_sc, *, tiles_per_seq):
    i = pl.program_id(0)
    tile = x_ref.shape[0]
    pad = CONV_WIDTH // 2
    h = _rms(x_ref[...], g_ref[...]).astype(BF16)
    h_halo = _rms(jnp.concatenate([xp_ref[...], xn_ref[...]], axis=0), g_ref[...]).astype(BF16)
    first = (i % tiles_per_seq) == 0
    last = (i % tiles_per_seq) == tiles_per_seq - 1

    def proj(hh, lo, hi):
        return _dot(hh, wm_ref[:, lo:hi]) + bm_ref[:, lo:hi]

    for n in range(2 * M_MLSTM // CONV_COLS):
        lo = n * CONV_COLS
        cols = slice(lo, lo + CONV_COLS)
        z_halo = proj(h_halo, OFF_Q + lo, OFF_Q + lo + CONV_COLS)
        ext_sc[0:CONV_HALO, cols] = jnp.where(first, 0.0, z_halo[:CONV_HALO])
        ext_sc[CONV_HALO:CONV_HALO + tile, cols] = proj(h, OFF_Q + lo, OFF_Q + lo + CONV_COLS)
        ext_sc[CONV_HALO + tile:, cols] = jnp.where(last, 0.0, z_halo[CONV_HALO:])
        ext = ext_sc[:, cols]
        acc = cb_ref[:, cols]
        for j in range(CONV_WIDTH):
            shifted = ext if j == pad else pltpu.roll(ext, (pad - j) % ext.shape[0], 0)
            acc = acc + cw_ref[j:j + 1, cols] * shifted[CONV_HALO:CONV_HALO + tile]
        half = 0.5 * acc
        qk = half + half * jnp.tanh(half)
        if lo < M_MLSTM:
            qk = qk * (HEAD_DIM ** -0.5)
        qk_ref[:, cols] = qk.astype(BF16)

    zv_ref[...] = proj(h, OFF_V, OFF_O).astype(BF16)
    zo_ref[...] = proj(h, OFF_O, OFF_IG).astype(BF16)
    zf_ref[...] = proj(h, OFF_IG, OFF_IG + M_FOURIER)
    zs_ref[...] = proj(h, OFF_IG + M_FOURIER, OFF_IG + M_FOURIER + M_S5)
    zg_ref[...] = _dot(h, wg_ref[...]) + bg_ref[...]
    zgt_ref[...] = lax.dot_general(wgt_ref[...], h, NT_DIMS, preferred_element_type=F32) + bgt_ref[...]


def _inproj(x2, params, l, tile, seq):
    t = x2.shape[0]
    n_gate = 4 * HEADS
    hpt = tile // CONV_HALO
    n_halo = t // CONV_HALO
    rows = lambda width: pl.BlockSpec((tile, width), lambda i: (i, 0))
    cols = lambda height: pl.BlockSpec((height, tile), lambda i: (0, i))
    return pl.pallas_call(
        functools.partial(_inproj_kernel, tiles_per_seq=seq // tile),
        grid=(t // tile,),
        in_specs=[rows(D_MODEL),
                  pl.BlockSpec((CONV_HALO, D_MODEL), lambda i: (jnp.maximum(i * hpt - 1, 0), 0)),
                  pl.BlockSpec((CONV_HALO, D_MODEL), lambda i: (jnp.minimum((i + 1) * hpt, n_halo - 1), 0))]
                 + [_layer_spec(a, l, (W_BLOCK, 1) if k in (1, 2) else None)
                    for k, a in enumerate(params)],
        out_specs=[rows(2 * M_MLSTM), rows(M_MLSTM), rows(M_MLSTM), rows(M_FOURIER), rows(M_S5),
                   rows(n_gate), cols(n_gate)],
        out_shape=[jax.ShapeDtypeStruct((t, 2 * M_MLSTM), BF16), jax.ShapeDtypeStruct((t, M_MLSTM), BF16),
                   jax.ShapeDtypeStruct((t, M_MLSTM), BF16), jax.ShapeDtypeStruct((t, M_FOURIER), F32),
                   jax.ShapeDtypeStruct((t, M_S5), F32), jax.ShapeDtypeStruct((t, n_gate), F32),
                   jax.ShapeDtypeStruct((n_gate, t), F32)],
        scratch_shapes=[pltpu.VMEM((tile + 2 * CONV_HALO, 2 * M_MLSTM), F32)],
        compiler_params=_params("parallel"),
        name="inproj",
    )(x2, x2, x2, *params)


def _log_sigmoid(x):
    return jnp.minimum(x, 0.0) - jnp.log1p(jnp.exp(-jnp.abs(x)))


def _mlstm_kernel(qk_f, v_f, g_f, gt_f, qk_b, v_b, g_b, gt_b, hf_ref, hb_ref, state_sc, m_sc):
    c = pl.program_id(1)
    L = CHUNK

    @pl.when(c == 0)
    def _():
        state_sc[...] = jnp.zeros_like(state_sc)
        m_sc[...] = jnp.zeros_like(m_sc)

    row = lax.broadcasted_iota(jnp.int32, (L, L), 0)
    col = lax.broadcasted_iota(jnp.int32, (L, L), 1)
    lower = row >= col
    upper = row <= col
    lower_f = jnp.where(lower, 1.0, 0.0).astype(BF16)
    upper_f = jnp.where(upper, 1.0, 0.0).astype(BF16)
    ones_v = jnp.ones((L, HEAD_DIM), F32)

    streams = ((0, qk_f, v_f, g_f, gt_f, hf_ref), (1, qk_b, v_b, g_b, gt_b, hb_ref))
    for sub, (d, qk_ref, v_ref, g_ref, gt_ref, out_ref) in itertools.product(range(MLSTM_SUB), streams):
        r0 = (sub if d == 0 else MLSTM_SUB - 1 - sub) * L
        rows = slice(r0, r0 + L)
        gates = g_ref[rows, :]
        gates_t = gt_ref[:, rows]
        lf_cols = _log_sigmoid(gates[:, 2 * HEADS:])
        lf_rows = _log_sigmoid(gates_t[2 * HEADS:, :])
        if d == 0:
            b_cols = _sel_dot(lower_f, lf_cols)
            b_rows = _dot_sel(lf_rows, upper_f)
            mask = lower
        else:
            b_cols = _sel_dot(upper_f, lf_cols)
            b_rows = _dot_sel(lf_rows, lower_f)
            mask = upper

        for hd in range(HEADS):
            k_idx = d * HEADS + hd
            lo = hd * HEAD_DIM
            q = qk_ref[rows, lo:lo + HEAD_DIM]
            k = qk_ref[rows, M_MLSTM + lo:M_MLSTM + lo + HEAD_DIM]
            v = v_ref[rows, lo:lo + HEAD_DIM]
            b_c = jnp.broadcast_to(b_cols[:, k_idx:k_idx + 1], (L, L))
            ig_c = jnp.broadcast_to(gates[:, k_idx:k_idx + 1], (L, L))
            b_r = b_rows[k_idx:k_idx + 1, :]
            ig_r = gates_t[k_idx:k_idx + 1, :]
            g_tot = b_c[L - 1:L, :] if d == 0 else b_c[0:1, :]
            m_prev = m_sc[k_idx:k_idx + 1, :]

            d_log = jnp.where(mask, b_c - b_r + ig_r, -1e30)
            inter_log = b_c + m_prev
            m_t = jnp.maximum(inter_log, jnp.max(d_log, axis=1, keepdims=True))
            scores = lax.dot_general(q, k, NT_DIMS, preferred_element_type=F32) * jnp.exp(d_log - m_t)
            inter_w = jnp.exp(inter_log - m_t)
            st = state_sc[k_idx]
            v_ext = jnp.concatenate([v, ones_v.astype(BF16)], axis=1)
            res = _dot(scores.astype(BF16), v_ext) + _dot((inter_w * q.astype(F32)).astype(BF16), st.astype(BF16))
            num = res[:, :HEAD_DIM]
            den = res[:, HEAD_DIM:]
            out_ref[rows, lo:lo + HEAD_DIM] = num / jnp.maximum(jnp.abs(den), jnp.exp(-m_t))

            a_c = g_tot - b_c + ig_c
            a_max = jnp.max(a_c, axis=0, keepdims=True)
            w_c = jnp.exp(a_c - a_max)
            vw = jnp.concatenate([v.astype(F32) * w_c, w_c], axis=1).astype(BF16)
            st_loc = lax.dot_general(k, vw, TN_DIMS, preferred_element_type=F32)
            m_new = jnp.maximum(g_tot + m_prev, a_max)
            s_old = jnp.exp(g_tot + m_prev - m_new)
            s_new = jnp.exp(a_max - m_new)
            s_old2 = jnp.concatenate([s_old, s_old], axis=1)
            s_new2 = jnp.concatenate([s_new, s_new], axis=1)
            state_sc[k_idx] = s_old2 * st + s_new2 * st_loc
            m_sc[k_idx:k_idx + 1, :] = m_new


def _mlstm(qk, zv, zg, zgt, bsz, seq):
    t = bsz * seq
    blk = CHUNK * MLSTM_SUB
    nc = seq // blk
    n_gate = 4 * HEADS

    def fwd(b, c):
        return b * nc + c

    def bwd(b, c):
        return b * nc + nc - 1 - c

    def specs(pos):
        return [pl.BlockSpec((blk, 2 * M_MLSTM), lambda b, c: (pos(b, c), 0)),
                pl.BlockSpec((blk, M_MLSTM), lambda b, c: (pos(b, c), 0)),
                pl.BlockSpec((blk, n_gate), lambda b, c: (pos(b, c), 0)),
                pl.BlockSpec((n_gate, blk), lambda b, c: (0, pos(b, c)))]

    out_shape = jax.ShapeDtypeStruct((t, M_MLSTM), F32)
    return pl.pallas_call(
        _mlstm_kernel,
        grid=(bsz, nc),
        in_specs=specs(fwd) + specs(bwd),
        out_specs=[pl.BlockSpec((blk, M_MLSTM), lambda b, c: (fwd(b, c), 0)),
                   pl.BlockSpec((blk, M_MLSTM), lambda b, c: (bwd(b, c), 0))],
        out_shape=[out_shape, out_shape],
        scratch_shapes=[pltpu.VMEM((2 * HEADS, HEAD_DIM, 2 * HEAD_DIM), F32),
                        pltpu.VMEM((2 * HEADS, HEAD_DIM), F32)],
        compiler_params=_params("arbitrary", "arbitrary"),
        name="mlstm",
    )(qk, zv, zg, zgt, qk, zv, zg, zgt)


def _dft_constants(seq):
    n1, n2 = seq // DFT_N2, DFT_N2
    k1 = np.arange(n1)[:, None, None]
    s2 = np.arange(n2)[None, None, :]
    s1 = np.arange(n1)[None, :, None]
    ang = -2.0 * np.pi * ((k1 * (n2 * s1 + s2)) % seq) / seq
    stage1 = np.concatenate([np.cos(ang), np.sin(ang)], axis=0)
    stage1 = np.ascontiguousarray(stage1.transpose(2, 0, 1))
    a2 = 2.0 * np.pi * np.outer(np.arange(n2), np.arange(n2)) / n2
    c2, sn2 = np.cos(a2), np.sin(a2)
    stage2 = np.block([[c2, sn2], [-sn2, c2]])
    ag = 2.0 * np.pi * np.outer(np.arange(FOURIER_GROUP_DIM), np.arange(FOURIER_GROUP_DIM)) / FOURIER_GROUP_DIM
    scale = 1.0 / math.sqrt(seq * FOURIER_GROUP_DIM)
    eye = np.eye(LANES // FOURIER_GROUP_DIM)
    group = np.stack([np.kron(eye, np.cos(ag)), np.kron(eye, np.sin(ag))]) * scale
    as_bf16 = lambda a: jnp.asarray(a, F32).astype(BF16)
    return as_bf16(stage1), as_bf16(stage2), as_bf16(group)


def _fourier_kernel(u_ref, w1_ref, w2_ref, wg_ref, y_ref, u_sc, b_sc, y_sc, *, n1):
    n2 = DFT_N2
    rows = n1 * DFT_STEP
    for sb in range(n2 // DFT_STEP):
        u_sc[sb % 2] = u_ref[:, sb * DFT_STEP:(sb + 1) * DFT_STEP, :].reshape(rows, LANES)
        for j in range(DFT_STEP):
            pick = pl.ds(j, n1, stride=DFT_STEP)
            res = _dot(w1_ref[sb * DFT_STEP + j], u_sc[sb % 2, pick, :].astype(BF16))
            b_sc[sb, 0, pick, :] = res[:n1]
            b_sc[sb, 1, pick, :] = res[n1:]
    for kb in range(n1 // DFT_STEP):
        def gather(part, j):
            r0 = (kb * DFT_STEP + j) * DFT_STEP
            return jnp.concatenate([b_sc[sb, part, r0:r0 + DFT_STEP, :] for sb in range(n2 // DFT_STEP)], axis=0)

        stacked = jnp.concatenate([jnp.concatenate([gather(0, j), gather(1, j)], axis=0) for j in range(DFT_STEP)],
                                  axis=1).astype(BF16)
        z = _dot(w2_ref[...], stacked).astype(BF16)
        zr = jnp.concatenate([z[:n2, j * LANES:(j + 1) * LANES] for j in range(DFT_STEP)], axis=0)
        zi = jnp.concatenate([z[n2:, j * LANES:(j + 1) * LANES] for j in range(DFT_STEP)], axis=0)
        y = _dot(zr, wg_ref[0]) + _dot(zi, wg_ref[1])
        for j in range(DFT_STEP):
            y_sc[kb % 2, pl.ds(j, n2, stride=DFT_STEP), :] = y[j * n2:(j + 1) * n2, :]
        y_ref[:, kb * DFT_STEP:(kb + 1) * DFT_STEP, :] = y_sc[kb % 2].reshape(n2, DFT_STEP, LANES)


def _fourier(zf, bsz, seq):
    n1, n2 = seq // DFT_N2, DFT_N2
    stage1, stage2, group = _dft_constants(seq)
    whole = lambda a: pl.BlockSpec(a.shape, lambda b, h: (0,) * a.ndim)
    y = pl.pallas_call(
        functools.partial(_fourier_kernel, n1=n1),
        grid=(bsz, M_FOURIER // LANES),
        in_specs=[pl.BlockSpec((None, n1, n2, LANES), lambda b, h: (b, 0, 0, h)),
                  whole(stage1), whole(stage2), whole(group)],
        out_specs=pl.BlockSpec((None, n2, n1, LANES), lambda b, h: (b, 0, 0, h)),
        out_shape=jax.ShapeDtypeStruct((bsz, n2, n1, M_FOURIER), F32),
        scratch_shapes=[pltpu.VMEM((2, n1 * DFT_STEP, LANES), F32),
                        pltpu.VMEM((n2 // DFT_STEP, 2, n1 * DFT_STEP, LANES), F32),
                        pltpu.VMEM((2, n2 * DFT_STEP, LANES), F32)],
        compiler_params=_params("parallel", "parallel"),
        name="fourier",
    )(zf.reshape(bsz, n1, n2, M_FOURIER), stage1, stage2, group)
    return y.reshape(bsz * seq, M_FOURIER)


def _s5_constants():
    L, G = S5_CHUNK, S5_GROUP
    lane = np.arange(2 * L * G)
    expo = np.arange(LANES)[:, None]
    rep = (lane[None, :] % G == np.arange(G)[:, None])
    lag = (L - 1) - lane // G
    spread = np.stack([(expo == lag) & (lag >= 0), (expo == -lag) & (lag <= 0) & (lag > -L)])
    step = lane[:L * G] // G
    spread_s = np.stack([expo == (L - 1) - step, expo == step])
    return tuple(jnp.asarray(m, BF16) for m in (rep, spread, spread_s))


def _s5_prep_kernel(lamc_re, lamc_im, ldt_ref, lam4_re, lam4_im, ldt4_ref, b_re, b_im, c_re, c_im, c4_re, c4_im,
                    d_ref, rep_ref, spread_ref, spread_s_ref, toep_ref, bend_ref, cout_ref, laml_ref):
    L, P, G = S5_CHUNK, S5_STATE, S5_GROUP
    W = 2 * L * G
    shift = G.bit_length() - 1
    nbits = L.bit_length()

    def cmul(ar, ai, br, bi):
        return ar * br - ai * bi, ar * bi + ai * br

    def lam_bar(lre, lim, dt):
        mag = jnp.exp(lre * dt)
        return mag * jnp.cos(lim * dt), mag * jnp.sin(lim * dt)

    def power_table(base_r, base_i, expo):
        tr = jnp.ones(expo.shape, F32)
        ti = jnp.zeros(expo.shape, F32)
        sr, si = base_r, base_i
        for bit in range(nbits):
            nr, ni = cmul(tr, ti, sr, si)
            has = ((expo >> bit) & 1) == 1
            tr, ti = jnp.where(has, nr, tr), jnp.where(has, ni, ti)
            sr, si = cmul(sr, si, sr, si)
        return tr, ti

    def spread_dot(xr, xi, sel):
        n = xr.shape[0]
        out = _dot(jnp.concatenate(_split3(xr)[:2] + _split3(xi)[:2], axis=0), sel)
        return out[:n] + out[n:2 * n], out[2 * n:3 * n] + out[3 * n:]

    def dot3(c, m):
        c_hi, c_mid, _ = _split3(c)
        m_hi, m_mid, _ = _split3(m)
        return _dot(c_hi, m_hi) + _dot(c_mid, m_hi) + _dot(c_hi, m_mid)

    rep = rep_ref[...]
    lane_g = lax.broadcasted_iota(jnp.int32, (G, W), 1)
    row_g = lax.broadcasted_iota(jnp.int32, (G, W), 0)
    expo_tab = jnp.minimum(lax.broadcasted_iota(jnp.int32, (P, LANES), 1), L)

    d_tiled = _dot_sel(jnp.broadcast_to(d_ref[0], (G, G)), rep)
    gen = jnp.where(((lane_g & (G - 1)) == row_g) & ((lane_g >> shift) == L - 1), d_tiled, 0.0)
    for d in range(2):
        dt = jnp.exp(ldt_ref[d, 0])
        lr, li = lamc_re[d, 0], lamc_im[d, 0]
        lbr, lbi = lam_bar(lr, li, dt)
        den = lr * lr + li * li
        fr, fi = ((lbr - 1.0) * lr + lbi * li) / den, (lbi * lr - (lbr - 1.0) * li) / den
        bbr, bbi = cmul(fr, fi, b_re[d, 0], b_im[d, 0])
        btr, bti = spread_dot(bbr, bbi, rep)
        tab_r, tab_i = power_table(lbr, lbi, expo_tab)
        pr, pi = spread_dot(tab_r, tab_i, spread_ref[d])
        mr, mi = cmul(pr, pi, btr, bti)
        gen = gen + dot3(c_re[d, 0], mr) - dot3(c_im[d, 0], mi)
        half = L * G
        qr, qi = spread_dot(tab_r, tab_i, spread_s_ref[d])
        xr, xi = cmul(qr, qi, btr[:, :half], bti[:, :half])
        bend_ref[0, 2 * d] = xr.astype(BF16)
        bend_ref[0, 2 * d + 1] = xi.astype(BF16)

    for t in range(L):
        a = (L - 1 - t) * G
        toep_ref[0, t * G:(t + 1) * G, :] = gen[:, a:a + L * G].astype(BF16)

    lane4 = lax.broadcasted_iota(jnp.int32, (L, 4 * P), 1)
    step4 = lax.broadcasted_iota(jnp.int32, (L, 4 * P), 0)
    lb4r, lb4i = lam_bar(lam4_re[0], lam4_im[0], jnp.exp(ldt4_ref[0]))
    pr, pi = power_table(lb4r, lb4i, jnp.where(lane4 < 2 * P, step4 + 1, L - step4))
    cr, ci = c4_re[0], c4_im[0]
    plane_bit = P.bit_length() - 1
    re_c = ((lax.broadcasted_iota(jnp.int32, (G, 4 * P), 1) >> plane_bit) & 1) == 0
    for t in range(L):
        re_part, im_part = cmul(cr, ci, pr[t:t + 1], pi[t:t + 1])
        cout_ref[0, t * G:(t + 1) * G, :] = jnp.where(re_c, re_part, -im_part).astype(BF16)
    re_1 = ((lax.broadcasted_iota(jnp.int32, (1, 4 * P), 1) >> plane_bit) & 1) == 0
    lr, li = power_table(lb4r, lb4i, jnp.full((1, 4 * P), L, jnp.int32))
    laml_ref[0] = jnp.where(re_1, lr, li)


def _s5_prep(lam_re, lam_im, log_dt, b_re, b_im, c_re, c_im, d_skip):
    ng, P, G, L = S5_GROUPS, S5_STATE, S5_GROUP, S5_CHUNK
    depth = lam_re.shape[0]
    tile4 = lambda a: jnp.concatenate([a[:, 0], a[:, 0], a[:, 1], a[:, 1]], axis=-1)
    ldt4 = jnp.repeat(tile4(log_dt[..., None]), P, axis=-1)[:, :, None, :]
    consts = _s5_constants()
    spec = lambda *tail: pl.BlockSpec((None, 2, 1) + tail, lambda l, g: (l, 0, g) + (0,) * len(tail))
    per_g = lambda *tail: pl.BlockSpec((None, 1) + tail, lambda l, g: (l, g) + (0,) * len(tail))
    whole = lambda a: pl.BlockSpec(a.shape, lambda l, g: (0,) * a.ndim)
    return pl.pallas_call(
        _s5_prep_kernel,
        grid=(depth, ng),
        in_specs=[spec(P, 1), spec(P, 1), spec(1, 1), per_g(1, 4 * P), per_g(1, 4 * P), per_g(1, 4 * P),
                  spec(P, G), spec(P, G), spec(G, P), spec(G, P), per_g(G, 4 * P), per_g(G, 4 * P), per_g(1, G)]
                 + [whole(m) for m in consts],
        out_specs=[per_g(L * G, L * G), per_g(4, P, L * G), per_g(L * G, 4 * P), per_g(1, 4 * P)],
        out_shape=[jax.ShapeDtypeStruct((depth, ng, L * G, L * G), BF16),
                   jax.ShapeDtypeStruct((depth, ng, 4, P, L * G), BF16),
                   jax.ShapeDtypeStruct((depth, ng, L * G, 4 * P), BF16),
                   jax.ShapeDtypeStruct((depth, ng, 1, 4 * P), F32)],
        compiler_params=_params("parallel", "parallel"),
        name="s5_prep",
    )(lam_re[..., None], lam_im[..., None], log_dt[..., None, None],
      tile4(lam_re)[:, :, None, :], tile4(lam_im)[:, :, None, :], ldt4,
      b_re, b_im, c_re, c_im, tile4(c_re), tile4(c_im), d_skip[:, :, None, :], *consts)


def _s5_sums_kernel(lo_ref, hi_ref, bend_ref, ut_ref, et_ref, *, nchunk):
    L, P, G = S5_CHUNK, S5_STATE, S5_GROUP
    per_half = LANES // G
    for half, z_ref in enumerate((lo_ref, hi_ref)):
        for s in range(L):
            zt = z_ref[pl.ds(s, nchunk, stride=L), :].T.astype(BF16)
            for gl in range(per_half):
                ut_ref[half * per_half + gl, s * G:(s + 1) * G, :] = zt[gl * G:(gl + 1) * G, :]
    for g in range(S5_GROUPS):
        for plane in range(4):
            row = plane * S5_GROUPS * P + g * P
            et_ref[row:row + P, :] = _dot(bend_ref[g, plane], ut_ref[g])


def _s5_scan_kernel(et_ref, lam_ref, xt_ref, e_sc, x_sc, *, bsz, steps):
    width = S5_GROUPS * S5_STATE
    for plane in range(4):
        e_sc[plane] = et_ref[plane * width:(plane + 1) * width, :].T
    ar_f, ai_f, ar_b, ai_b = lam_ref[0], lam_ref[1], lam_ref[2], lam_ref[3]
    zero = jnp.zeros_like(ar_f)
    for b in range(bsz):
        def body(i, carry):
            xr, xi, yr, yi = carry
            rf = b * steps + i
            rb = b * steps + steps - 1 - i
            x_sc[0, pl.ds(rf, 1), :] = xr
            x_sc[1, pl.ds(rf, 1), :] = xi
            x_sc[2, pl.ds(rb, 1), :] = yr
            x_sc[3, pl.ds(rb, 1), :] = yi
            xr, xi = (ar_f * xr - ai_f * xi + e_sc[0, pl.ds(rf, 1), :],
                      ar_f * xi + ai_f * xr + e_sc[1, pl.ds(rf, 1), :])
            yr, yi = (ar_b * yr - ai_b * yi + e_sc[2, pl.ds(rb, 1), :],
                      ar_b * yi + ai_b * yr + e_sc[3, pl.ds(rb, 1), :])
            return xr, xi, yr, yi
        lax.fori_loop(0, steps, body, (zero, zero, zero, zero))
    for plane in range(4):
        xt_ref[plane * width:(plane + 1) * width, :] = x_sc[plane].T


def _gelu_tanh(x):
    return 0.5 * x * (1.0 + jnp.tanh(math.sqrt(2.0 / math.pi) * (x + 0.044715 * (x * x * x))))


def _s5_out_kernel(ut_ref, xt_ref, toep_ref, cout_ref, lo_ref, hi_ref, yt_sc, *, nchunk):
    L, P, G = S5_CHUNK, S5_STATE, S5_GROUP
    width = S5_GROUPS * P
    for g in range(S5_GROUPS):
        xg = jnp.concatenate([xt_ref[plane * width + g * P:plane * width + (g + 1) * P, :] for plane in range(4)],
                             axis=0).astype(BF16)
        yt = _gelu_tanh(_dot(toep_ref[g], ut_ref[g]) + _dot(cout_ref[g], xg))
        for t in range(L):
            yt_sc[t, g * G:(g + 1) * G, :] = yt[t * G:(t + 1) * G, :]
    for t in range(L):
        y = yt_sc[t].T
        lo_ref[pl.ds(t, nchunk, stride=L), :] = y[:, :LANES]
        hi_ref[pl.ds(t, nchunk, stride=L), :] = y[:, LANES:]


def _s5(zs, prep, l, bsz, seq):
    ng, P, G, L = S5_GROUPS, S5_STATE, S5_GROUP, S5_CHUNK
    toep, bend, cout, laml = prep
    depth = toep.shape[0]
    laml = laml.reshape(depth, ng, 4, P).transpose(0, 2, 1, 3).reshape(depth, 4, 1, ng * P)
    t = bsz * seq
    tile = min(S5_TILE, t)
    nct = tile // L
    nchunk = t // L
    ut_spec = pl.BlockSpec((ng, L * G, nct), lambda i: (0, 0, i))
    plane_spec = pl.BlockSpec((4 * ng * P, nct), lambda i: (0, i))
    plane_shape = jax.ShapeDtypeStruct((4 * ng * P, nchunk), F32)
    ut, et = pl.pallas_call(
        functools.partial(_s5_sums_kernel, nchunk=nct),
        grid=(t // tile,),
        in_specs=[pl.BlockSpec((tile, LANES), lambda i: (i, 0)), pl.BlockSpec((tile, LANES), lambda i: (i, 1)),
                  _layer_spec(bend, l)],
        out_specs=[ut_spec, plane_spec],
        out_shape=[jax.ShapeDtypeStruct((ng, L * G, nchunk), BF16), plane_shape],
        compiler_params=_params("parallel"),
        name="s5_chunk_sums",
    )(zs, zs, bend)
    whole_planes = pl.BlockSpec((4 * ng * P, nchunk), lambda i: (0, 0))
    xt = pl.pallas_call(
        functools.partial(_s5_scan_kernel, bsz=bsz, steps=seq // L),
        grid=(1,),
        in_specs=[whole_planes, _layer_spec(laml, l)],
        out_specs=whole_planes,
        out_shape=plane_shape,
        scratch_shapes=[pltpu.VMEM((4, nchunk, ng * P), F32), pltpu.VMEM((4, nchunk, ng * P), F32)],
        compiler_params=_params("arbitrary"),
        name="s5_state_scan",
    )(et, laml)
    half_spec = pl.BlockSpec((tile, LANES), lambda i: (i, 0))
    half_shape = jax.ShapeDtypeStruct((t, LANES), F32)
    return pl.pallas_call(
        functools.partial(_s5_out_kernel, nchunk=nct),
        grid=(t // tile,),
        in_specs=[ut_spec, plane_spec, _layer_spec(toep, l), _layer_spec(cout, l)],
        out_specs=[half_spec, half_spec],
        out_shape=[half_shape, half_shape],
        scratch_shapes=[pltpu.VMEM((L, M_S5, nct), F32)],
        compiler_params=_params("parallel"),
        name="s5_outputs",
    )(ut, xt, toep, cout)


def _merge_kernel(x_ref, hf_ref, hb_ref, zo_ref, yf_ref, ys_lo_ref, ys_hi_ref, gpre_ref, wgate_ref, bgate_ref,
                  ng_ref, wm_ref, wf_ref, wglu_ref, bglu_ref, wout_ref, gpost_ref, o_ref, mixed_sc):
    x = x_ref[...]
    h = _rms(x, gpre_ref[...]).astype(BF16)
    hs = hf_ref[...] + hb_ref[...]
    parts = []
    for hd in range(HEADS):
        blk = hs[:, hd * HEAD_DIM:(hd + 1) * HEAD_DIM]
        mu = jnp.mean(blk, axis=-1, keepdims=True)
        cen = blk - mu
        var = jnp.mean(cen * cen, axis=-1, keepdims=True)
        parts.append(cen * lax.rsqrt(var + EPS))
    hm = (jnp.concatenate(parts, axis=1) * ng_ref[...] * _sigmoid(zo_ref[...].astype(F32))).astype(BF16)
    yf = yf_ref[...].astype(BF16)
    ys = jnp.concatenate([ys_lo_ref[...].astype(BF16), ys_hi_ref[...].astype(BF16)], axis=1)
    for n in range(D_MODEL // MERGE_COLS):
        lo = n * MERGE_COLS
        cols = slice(lo, lo + MERGE_COLS)

        def gate(i):
            gcols = slice(i * D_MODEL + lo, i * D_MODEL + lo + MERGE_COLS)
            return _sigmoid(_dot(h, wgate_ref[:, gcols]) + bgate_ref[:, gcols])

        lin = _dot(ys, wglu_ref[:, cols]) + bglu_ref[:, cols]
        gcols = slice(D_MODEL + lo, D_MODEL + lo + MERGE_COLS)
        y_s = lin * _sigmoid(_dot(ys, wglu_ref[:, gcols]) + bglu_ref[:, gcols])
        mixed = gate(0) * _dot(hm, wm_ref[:, cols]) + gate(1) * _dot(yf, wf_ref[:, cols]) + gate(2) * y_s
        mixed_sc[:, cols] = mixed.astype(BF16)
    o_ref[...] = x + _rms(_dot(mixed_sc[...], wout_ref[...]), gpost_ref[...])


def _merge(x2, hf, hb, zo, yf, ys_lo, ys_hi, params, l, tile):
    t = x2.shape[0]
    rows = lambda width: pl.BlockSpec((tile, width), lambda i: (i, 0))
    return pl.pallas_call(
        _merge_kernel,
        grid=(t // tile,),
        in_specs=[rows(D_MODEL), rows(M_MLSTM), rows(M_MLSTM), rows(M_MLSTM), rows(M_FOURIER),
                  rows(LANES), rows(LANES)]
                 + [_layer_spec(a, l, (W_BLOCK, 0) if k in (1, 2) else None)
                    for k, a in enumerate(params)],
        out_specs=rows(D_MODEL),
        out_shape=jax.ShapeDtypeStruct((t, D_MODEL), F32),
        scratch_shapes=[pltpu.VMEM((tile, D_MODEL), BF16)],
        compiler_params=_params("parallel"),
        name="merge",
    )(x2, hf, hb, zo, yf, ys_lo, ys_hi, *params)


def _ffn_kernel(x_ref, gpre_ref, w1_ref, w2_ref, gpost_ref, o_ref, *, n_split):
    x = x_ref[...]
    h = _rms(x, gpre_ref[...]).astype(BF16)
    width = D_FF // n_split
    f = None
    for j in range(n_split):
        a = jnp.maximum(_dot(h, w1_ref[:, j * width:(j + 1) * width]), 0.0)
        part = _dot((a * a).astype(BF16), w2_ref[j * width:(j + 1) * width, :])
        f = part if f is None else f + part
    o_ref[...] = x + _rms(f, gpost_ref[...])


def _ffn(x2, params, l, tile):
    t = x2.shape[0]
    rows = pl.BlockSpec((tile, D_MODEL), lambda i: (i, 0))
    return pl.pallas_call(
        functools.partial(_ffn_kernel, n_split=4),
        grid=(t // tile,),
        in_specs=[rows] + [_layer_spec(a, l) for a in params],
        out_specs=rows,
        out_shape=jax.ShapeDtypeStruct((t, D_MODEL), F32),
        compiler_params=_params("parallel"),
        name="ffn",
    )(x2, *params)


def kernel(x, g_mix_pre, g_mix_post, g_ffn_pre, g_ffn_post, w_in, b_in, conv_w, conv_b, mlstm_norm_g, w_up_mlstm, w_up_fourier, s5_lam_re, s5_lam_im, s5_log_dt, s5_b_re, s5_b_im, s5_c_re, s5_c_im, s5_d, w_glu, b_glu, w_out, w_ffn1, w_ffn2):
    bsz, seq, _ = x.shape
    depth = w_in.shape[0]
    t = bsz * seq
    tile = min(1024, t)
    bf = lambda a: a.astype(BF16)
    row = lambda a: a[:, None, :]
    w_gates = w_in[:, :, OFF_IG:OFF_FOURIER]
    b_gates = b_in[:, OFF_IG:OFF_FOURIER]
    regroup = lambda a: jnp.concatenate(
        [a[..., OFF_GATE:], a[..., :OFF_IG], a[..., OFF_FOURIER:OFF_GATE],
         jnp.zeros(a.shape[:-1] + (W_BLOCK - N_MAIN,), a.dtype)], axis=-1)
    w_cols = bf(regroup(w_in))
    b_cols = row(regroup(b_in))
    inproj_params = (row(g_mix_pre), w_cols, b_cols,
                     bf(w_gates), row(b_gates), bf(jnp.swapaxes(w_gates, 1, 2)), b_gates[:, :, None],
                     conv_w, row(conv_b))
    merge_params = (row(g_mix_pre), w_cols, b_cols, row(mlstm_norm_g),
                    bf(w_up_mlstm), bf(w_up_fourier), bf(w_glu), row(b_glu), bf(w_out), row(g_mix_post))
    ffn_params = (row(g_ffn_pre), bf(w_ffn1), bf(w_ffn2), row(g_ffn_post))
    s5_prep = _s5_prep(s5_lam_re, s5_lam_im, s5_log_dt, s5_b_re, s5_b_im, s5_c_re, s5_c_im, s5_d)
    x2 = x.reshape(t, D_MODEL)
    for l in range(depth):
        qk, zv, zo, zf, zs, zg, zgt = _inproj(x2, inproj_params, l, tile, seq)
        hf, hb = _mlstm(qk, zv, zg, zgt, bsz, seq)
        yf = _fourier(zf, bsz, seq)
        ys_lo, ys_hi = _s5(zs, s5_prep, l, bsz, seq)
        x2 = _merge(x2, hf, hb, zo, yf, ys_lo, ys_hi, merge_params, l, tile)
        x2 = _ffn(x2, ffn_params, l, tile)
    return x2.reshape(bsz, seq, D_MODEL)
```

```python
import functools
import itertools
import math

import numpy as np
import jax
import jax.numpy as jnp
from jax import lax
from jax.experimental import pallas as pl
from jax.experimental.pallas import tpu as pltpu

F32 = jnp.float32
BF16 = jnp.bfloat16
HIGHEST = lax.Precision.HIGHEST

LANES = 128
D_MODEL = 1024
M_MLSTM = 512
HEADS = 4
HEAD_DIM = 128
CHUNK = 128
MLSTM_SUB = 8
CONV_WIDTH = 5
CONV_COLS = 256
CONV_HALO = 8
M_FOURIER = 256
FOURIER_GROUP_DIM = 64
M_S5 = 256
S5_GROUP = 16
S5_GROUPS = 16
S5_STATE = 64
S5_CHUNK = 32
S5_TILE = 4096
N_BRANCHES = 3
MERGE_COLS = 256
D_FF = 4 * D_MODEL
EPS = 1e-6

OFF_Q = 0
OFF_V = 2 * M_MLSTM
OFF_O = 3 * M_MLSTM
OFF_IG = 4 * M_MLSTM
OFF_FOURIER = OFF_IG + 4 * HEADS
OFF_S5 = OFF_FOURIER + M_FOURIER
OFF_GATE = OFF_S5 + M_S5

DFT_N2 = 64
DFT_STEP = 8

VMEM_LIMIT = 56 * 1024 * 1024

NT_DIMS = (((1,), (1,)), ((), ()))
TN_DIMS = (((0,), (0,)), ((), ()))


def _params(*sem):
    return pltpu.CompilerParams(dimension_semantics=sem, vmem_limit_bytes=VMEM_LIMIT)


def _layer_spec(a, l, col_block=None):
    width, index = col_block if col_block else (a.shape[-1], 0)
    return pl.BlockSpec((None,) + a.shape[1:-1] + (width,), lambda *_: (l,) + (0,) * (a.ndim - 2) + (index,),
                        pipeline_mode=pl.Buffered(1))


def _rms(x, g):
    return x * lax.rsqrt(jnp.mean(x * x, axis=-1, keepdims=True) + EPS) * g


def _sigmoid(x):
    return 1.0 / (1.0 + jnp.exp(-x))


def _dot(a, b):
    return jnp.dot(a, b, preferred_element_type=F32)


def _dot_hi(a, b):
    return jnp.dot(a, b, precision=HIGHEST, preferred_element_type=F32)


def _split3(a):
    hi = a.astype(BF16)
    rest = a - hi.astype(F32)
    mid = rest.astype(BF16)
    return hi, mid, (rest - mid.astype(F32)).astype(BF16)


def _dot_sel(a, sel):
    hi, mid, lo = _split3(a)
    return _dot(hi, sel) + _dot(mid, sel) + _dot(lo, sel)


def _sel_dot(sel, a):
    hi, mid, lo = _split3(a)
    return _dot(sel, hi) + _dot(sel, mid) + _dot(sel, lo)


W_BLOCK = N_BRANCHES * D_MODEL
N_MAIN = OFF_IG + M_FOURIER + M_S5


def _inproj_kernel(x_ref, xp_ref, xn_ref, g_ref, wm_ref, bm_ref, wg_ref, bg_ref, wgt_ref, bgt_ref, cw_ref, cb_ref,
                   qk_ref, zv_ref, zo_ref, zf_ref, zs_ref, zg_ref, zgt_ref, ext_sc, *, tiles_per_seq):
    i = pl.program_id(0)
    tile = x_ref.shape[0]
    pad = CONV_WIDTH // 2
    h = _rms(x_ref[...], g_ref[...]).astype(BF16)
    h_halo = _rms(jnp.concatenate([xp_ref[...], xn_ref[...]], axis=0), g_ref[...]).astype(BF16)
    first = (i % tiles_per_seq) == 0
    last = (i % tiles_per_seq) == tiles_per_seq - 1

    def proj(hh, lo, hi):
        return _dot(hh, wm_ref[:, lo:hi]) + bm_ref[:, lo:hi]

    for n in range(2 * M_MLSTM // CONV_COLS):
        lo = n * CONV_COLS
        cols = slice(lo, lo + CONV_COLS)
        z_halo = proj(h_halo, OFF_Q + lo, OFF_Q + lo + CONV_COLS)
        ext_sc[0:CONV_HALO, cols] = jnp.where(first, 0.0, z_halo[:CONV_HALO])
        ext_sc[CONV_HALO:CONV_HALO + tile, cols] = proj(h, OFF_Q + lo, OFF_Q + lo + CONV_COLS)
        ext_sc[CONV_HALO + tile:, cols] = jnp.where(last, 0.0, z_halo[CONV_HALO:])
        ext = ext_sc[:, cols]
        acc = cb_ref[:, cols]
        for j in range(CONV_WIDTH):
            shifted = ext if j == pad else pltpu.roll(ext, (pad - j) % ext.shape[0], 0)
            acc = acc + cw_ref[j:j + 1, cols] * shifted[CONV_HALO:CONV_HALO + tile]
        half = 0.5 * acc
        qk = half + half * jnp.tanh(half)
        if lo < M_MLSTM:
            qk = qk * (HEAD_DIM ** -0.5)
        qk_ref[:, cols] = qk.astype(BF16)

    zv_ref[...] = proj(h, OFF_V, OFF_O).astype(BF16)
    zo_ref[...] = proj(h, OFF_O, OFF_IG).astype(BF16)
    zf_ref[...] = proj(h, OFF_IG, OFF_IG + M_FOURIER)
    zs_ref[...] = proj(h, OFF_IG + M_FOURIER, OFF_IG + M_FOURIER + M_S5)
    zg_ref[...] = _dot(h, wg_ref[...]) + bg_ref[...]
    zgt_ref[...] = lax.dot_general(wgt_ref[...], h, NT_DIMS, preferred_element_type=F32) + bgt_ref[...]


def _inproj(x2, params, l, tile, seq):
    t = x2.shape[0]
    n_gate = 4 * HEADS
    hpt = tile // CONV_HALO
    n_halo = t // CONV_HALO
    rows = lambda width: pl.BlockSpec((tile, width), lambda i: (i, 0))
    cols = lambda height: pl.BlockSpec((height, tile), lambda i: (0, i))
    return pl.pallas_call(
        functools.partial(_inproj_kernel, tiles_per_seq=seq // tile),
        grid=(t // tile,),
        in_specs=[rows(D_MODEL),
                  pl.BlockSpec((CONV_HALO, D_MODEL), lambda i: (jnp.maximum(i * hpt - 1, 0), 0)),
                  pl.BlockSpec((CONV_HALO, D_MODEL), lambda i: (jnp.minimum((i + 1) * hpt, n_halo - 1), 0))]
                 + [_layer_spec(a, l, (W_BLOCK, 1) if k in (1, 2) else None)
                    for k, a in enumerate(params)],
        out_specs=[rows(2 * M_MLSTM), rows(M_MLSTM), rows(M_MLSTM), rows(M_FOURIER), rows(M_S5),
                   rows(n_gate), cols(n_gate)],
        out_shape=[jax.ShapeDtypeStruct((t, 2 * M_MLSTM), BF16), jax.ShapeDtypeStruct((t, M_MLSTM), BF16),
                   jax.ShapeDtypeStruct((t, M_MLSTM), BF16), jax.ShapeDtypeStruct((t, M_FOURIER), F32),
                   jax.ShapeDtypeStruct((t, M_S5), F32), jax.ShapeDtypeStruct((t, n_gate), F32),
                   jax.ShapeDtypeStruct((n_gate, t), F32)],
        scratch_shapes=[pltpu.VMEM((tile + 2 * CONV_HALO, 2 * M_MLSTM), F32)],
        compiler_params=_params("parallel"),
        name="inproj",
    )(x2, x2, x2, *params)


def _log_sigmoid(x):
    return jnp.minimum(x, 0.0) - jnp.log1p(jnp.exp(-jnp.abs(x)))


def _mlstm_kernel(qk_f, v_f, g_f, gt_f, qk_b, v_b, g_b, gt_b, hf_ref, hb_ref, state_sc, m_sc):
    c = pl.program_id(1)
    L = CHUNK

    @pl.when(c == 0)
    def _():
        state_sc[...] = jnp.zeros_like(state_sc)
        m_sc[...] = jnp.zeros_like(m_sc)

    row = lax.broadcasted_iota(jnp.int32, (L, L), 0)
    col = lax.broadcasted_iota(jnp.int32, (L, L), 1)
    lower = row >= col
    upper = row <= col
    lower_f = jnp.where(lower, 1.0, 0.0).astype(BF16)
    upper_f = jnp.where(upper, 1.0, 0.0).astype(BF16)
    ones_v = jnp.ones((L, HEAD_DIM), F32)

    streams = ((0, qk_f, v_f, g_f, gt_f, hf_ref), (1, qk_b, v_b, g_b, gt_b, hb_ref))
    for sub, (d, qk_ref, v_ref, g_ref, gt_ref, out_ref) in itertools.product(range(MLSTM_SUB), streams):
        r0 = (sub if d == 0 else MLSTM_SUB - 1 - sub) * L
        rows = slice(r0, r0 + L)
        gates = g_ref[rows, :]
        gates_t = gt_ref[:, rows]
        lf_cols = _log_sigmoid(gates[:, 2 * HEADS:])
        lf_rows = _log_sigmoid(gates_t[2 * HEADS:, :])
        if d == 0:
            b_cols = _sel_dot(lower_f, lf_cols)
            b_rows = _dot_sel(lf_rows, upper_f)
            mask = lower
        else:
            b_cols = _sel_dot(upper_f, lf_cols)
            b_rows = _dot_sel(lf_rows, lower_f)
            mask = upper

        for hd in range(HEADS):
            k_idx = d * HEADS + hd
            lo = hd * HEAD_DIM
            q = qk_ref[rows, lo:lo + HEAD_DIM]
            k = qk_ref[rows, M_MLSTM + lo:M_MLSTM + lo + HEAD_DIM]
            v = v_ref[rows, lo:lo + HEAD_DIM]
            b_c = jnp.broadcast_to(b_cols[:, k_idx:k_idx + 1], (L, L))
            ig_c = jnp.broadcast_to(gates[:, k_idx:k_idx + 1], (L, L))
            b_r = b_rows[k_idx:k_idx + 1, :]
            ig_r = gates_t[k_idx:k_idx + 1, :]
            g_tot = b_c[L - 1:L, :] if d == 0 else b_c[0:1, :]
            m_prev = m_sc[k_idx:k_idx + 1, :]

            d_log = jnp.where(mask, b_c - b_r + ig_r, -1e30)
            inter_log = b_c + m_prev
            m_t = jnp.maximum(inter_log, jnp.max(d_log, axis=1, keepdims=True))
            scores = lax.dot_general(q, k, NT_DIMS, preferred_element_type=F32) * jnp.exp(d_log - m_t)
            inter_w = jnp.exp(inter_log - m_t)
            st = state_sc[k_idx]
            v_ext = jnp.concatenate([v, ones_v.astype(BF16)], axis=1)
            res = _dot(scores.astype(BF16), v_ext) + _dot((inter_w * q.astype(F32)).astype(BF16), st.astype(BF16))
            num = res[:, :HEAD_DIM]
            den = res[:, HEAD_DIM:]
            out_ref[rows, lo:lo + HEAD_DIM] = num / jnp.maximum(jnp.abs(den), jnp.exp(-m_t))

            a_c = g_tot - b_c + ig_c
            a_max = jnp.max(a_c, axis=0, keepdims=True)
            w_c = jnp.exp(a_c - a_max)
            vw = jnp.concatenate([v.astype(F32) * w_c, w_c], axis=1).astype(BF16)
            st_loc = lax.dot_general(k, vw, TN_DIMS, preferred_element_type=F32)
            m_new = jnp.maximum(g_tot + m_prev, a_max)
            s_old = jnp.exp(g_tot + m_prev - m_new)
            s_new = jnp.exp(a_max - m_new)
            s_old2 = jnp.concatenate([s_old, s_old], axis=1)
            s_new2 = jnp.concatenate([s_new, s_new], axis=1)
            state_sc[k_idx] = s_old2 * st + s_new2 * st_loc
            m_sc[k_idx:k_idx + 1, :] = m_new


def _mlstm(qk, zv, zg, zgt, bsz, seq):
    t = bsz * seq
    blk = CHUNK * MLSTM_SUB
    nc = seq // blk
    n_gate = 4 * HEADS

    def fwd(b, c):
        return b * nc + c

    def bwd(b, c):
        return b * nc + nc - 1 - c

    def specs(pos):
        return [pl.BlockSpec((blk, 2 * M_MLSTM), lambda b, c: (pos(b, c), 0)),
                pl.BlockSpec((blk, M_MLSTM), lambda b, c: (pos(b, c), 0)),
                pl.BlockSpec((blk, n_gate), lambda b, c: (pos(b, c), 0)),
                pl.BlockSpec((n_gate, blk), lambda b, c: (0, pos(b, c)))]

    out_shape = jax.ShapeDtypeStruct((t, M_MLSTM), F32)
    return pl.pallas_call(
        _mlstm_kernel,
        grid=(bsz, nc),
        in_specs=specs(fwd) + specs(bwd),
        out_specs=[pl.BlockSpec((blk, M_MLSTM), lambda b, c: (fwd(b, c), 0)),
                   pl.BlockSpec((blk, M_MLSTM), lambda b, c: (bwd(b, c), 0))],
        out_shape=[out_shape, out_shape],
        scratch_shapes=[pltpu.VMEM((2 * HEADS, HEAD_DIM, 2 * HEAD_DIM), F32),
                        pltpu.VMEM((2 * HEADS, HEAD_DIM), F32)],
        compiler_params=_params("arbitrary", "arbitrary"),
        name="mlstm",
    )(qk, zv, zg, zgt, qk, zv, zg, zgt)


def _dft_constants(seq):
    n1, n2 = seq // DFT_N2, DFT_N2
    k1 = np.arange(n1)[:, None, None]
    s2 = np.arange(n2)[None, None, :]
    s1 = np.arange(n1)[None, :, None]
    ang = -2.0 * np.pi * ((k1 * (n2 * s1 + s2)) % seq) / seq
    stage1 = np.concatenate([np.cos(ang), np.sin(ang)], axis=0)
    stage1 = np.ascontiguousarray(stage1.transpose(2, 0, 1))
    a2 = 2.0 * np.pi * np.outer(np.arange(n2), np.arange(n2)) / n2
    c2, sn2 = np.cos(a2), np.sin(a2)
    stage2 = np.block([[c2, sn2], [-sn2, c2]])
    ag = 2.0 * np.pi * np.outer(np.arange(FOURIER_GROUP_DIM), np.arange(FOURIER_GROUP_DIM)) / FOURIER_GROUP_DIM
    scale = 1.0 / math.sqrt(seq * FOURIER_GROUP_DIM)
    eye = np.eye(LANES // FOURIER_GROUP_DIM)
    group = np.stack([np.kron(eye, np.cos(ag)), np.kron(eye, np.sin(ag))]) * scale
    as_bf16 = lambda a: jnp.asarray(a, F32).astype(BF16)
    return as_bf16(stage1), as_bf16(stage2), as_bf16(group)


def _fourier_kernel(u_ref, w1_ref, w2_ref, wg_ref, y_ref, u_sc, b_sc, y_sc, *, n1):
    n2 = DFT_N2
    rows = n1 * DFT_STEP
    for sb in range(n2 // DFT_STEP):
        u_sc[sb % 2] = u_ref[:, sb * DFT_STEP:(sb + 1) * DFT_STEP, :].reshape(rows, LANES)
        for j in range(DFT_STEP):
            pick = pl.ds(j, n1, stride=DFT_STEP)
            res = _dot(w1_ref[sb * DFT_STEP + j], u_sc[sb % 2, pick, :].astype(BF16))
            b_sc[sb, 0, pick, :] = res[:n1]
            b_sc[sb, 1, pick, :] = res[n1:]
    for kb in range(n1 // DFT_STEP):
        def gather(part, j):
            r0 = (kb * DFT_STEP + j) * DFT_STEP
            return jnp.concatenate([b_sc[sb, part, r0:r0 + DFT_STEP, :] for sb in range(n2 // DFT_STEP)], axis=0)

        stacked = jnp.concatenate([jnp.concatenate([gather(0, j), gather(1, j)], axis=0) for j in range(DFT_STEP)],
                                  axis=1).astype(BF16)
        z = _dot(w2_ref[...], stacked).astype(BF16)
        zr = jnp.concatenate([z[:n2, j * LANES:(j + 1) * LANES] for j in range(DFT_STEP)], axis=0)
        zi = jnp.concatenate([z[n2:, j * LANES:(j + 1) * LANES] for j in range(DFT_STEP)], axis=0)
        y = _dot(zr, wg_ref[0]) + _dot(zi, wg_ref[1])
        for j in range(DFT_STEP):
            y_sc[kb % 2, pl.ds(j, n2, stride=DFT_STEP), :] = y[j * n2:(j + 1) * n2, :]
        y_ref[:, kb * DFT_STEP:(kb + 1) * DFT_STEP, :] = y_sc[kb % 2].reshape(n2, DFT_STEP, LANES)


def _fourier(zf, bsz, seq):
    n1, n2 = seq // DFT_N2, DFT_N2
    stage1, stage2, group = _dft_constants(seq)
    whole = lambda a: pl.BlockSpec(a.shape, lambda b, h: (0,) * a.ndim)
    y = pl.pallas_call(
        functools.partial(_fourier_kernel, n1=n1),
        grid=(bsz, M_FOURIER // LANES),
        in_specs=[pl.BlockSpec((None, n1, n2, LANES), lambda b, h: (b, 0, 0, h)),
                  whole(stage1), whole(stage2), whole(group)],
        out_specs=pl.BlockSpec((None, n2, n1, LANES), lambda b, h: (b, 0, 0, h)),
        out_shape=jax.ShapeDtypeStruct((bsz, n2, n1, M_FOURIER), F32),
        scratch_shapes=[pltpu.VMEM((2, n1 * DFT_STEP, LANES), F32),
                        pltpu.VMEM((n2 // DFT_STEP, 2, n1 * DFT_STEP, LANES), F32),
                        pltpu.VMEM((2, n2 * DFT_STEP, LANES), F32)],
        compiler_params=_params("parallel", "parallel"),
        name="fourier",
    )(zf.reshape(bsz, n1, n2, M_FOURIER), stage1, stage2, group)
    return y.reshape(bsz * seq, M_FOURIER)


def _s5_constants():
    L, G = S5_CHUNK, S5_GROUP
    lane = np.arange(2 * L * G)
    expo = np.arange(LANES)[:, None]
    rep = (lane[None, :] % G == np.arange(G)[:, None])
    lag = (L - 1) - lane // G
    spread = np.stack([(expo == lag) & (lag >= 0), (expo == -lag) & (lag <= 0) & (lag > -L)])
    step = lane[:L * G] // G
    spread_s = np.stack([expo == (L - 1) - step, expo == step])
    return tuple(jnp.asarray(m, BF16) for m in (rep, spread, spread_s))


def _s5_prep_kernel(lamc_re, lamc_im, ldt_ref, lam4_re, lam4_im, ldt4_ref, b_re, b_im, c_re, c_im, c4_re, c4_im,
                    d_ref, rep_ref, spread_ref, spread_s_ref, toep_ref, bend_ref, cout_ref, laml_ref):
    L, P, G = S5_CHUNK, S5_STATE, S5_GROUP
    W = 2 * L * G
    shift = G.bit_length() - 1
    nbits = L.bit_length()

    def cmul(ar, ai, br, bi):
        return ar * br - ai * bi, ar * bi + ai * br

    def lam_bar(lre, lim, dt):
        mag = jnp.exp(lre * dt)
        return mag * jnp.cos(lim * dt), mag * jnp.sin(lim * dt)

    def power_table(base_r, base_i, expo):
        tr = jnp.ones(expo.shape, F32)
        ti = jnp.zeros(expo.shape, F32)
        sr, si = base_r, base_i
        for bit in range(nbits):
            nr, ni = cmul(tr, ti, sr, si)
            has = ((expo >> bit) & 1) == 1
            tr, ti = jnp.where(has, nr, tr), jnp.where(has, ni, ti)
            sr, si = cmul(sr, si, sr, si)
        return tr, ti

    def spread_dot(xr, xi, sel):
        n = xr.shape[0]
        out = _dot(jnp.concatenate(_split3(xr)[:2] + _split3(xi)[:2], axis=0), sel)
        return out[:n] + out[n:2 * n], out[2 * n:3 * n] + out[3 * n:]

    def dot3(c, m):
        c_hi, c_mid, _ = _split3(c)
        m_hi, m_mid, _ = _split3(m)
        return _dot(c_hi, m_hi) + _dot(c_mid, m_hi) + _dot(c_hi, m_mid)

    rep = rep_ref[...]
    lane_g = lax.broadcasted_iota(jnp.int32, (G, W), 1)
    row_g = lax.broadcasted_iota(jnp.int32, (G, W), 0)
    expo_tab = jnp.minimum(lax.broadcasted_iota(jnp.int32, (P, LANES), 1), L)

    lb4r, lb4i = lam_bar(lam4_re[0], lam4_im[0], jnp.exp(ldt4_ref[0]))

    def as_column(row, d):
        block = jnp.broadcast_to(row[:, 2 * d * P:2 * (d + 1) * P], (8, 2 * P))
        return block.T[:P, 0:1]

    lb_cols = [(as_column(lb4r, d), as_column(lb4i, d)) for d in range(2)]

    d_tiled = _dot_sel(jnp.broadcast_to(d_ref[0], (G, G)), rep)
    gen = jnp.where(((lane_g & (G - 1)) == row_g) & ((lane_g >> shift) == L - 1), d_tiled, 0.0)
    for d in range(2):
        dt = jnp.exp(ldt_ref[d, 0])
        lr, li = lamc_re[d, 0], lamc_im[d, 0]
        lbr, lbi = lb_cols[d]
        den = lr * lr + li * li
        fr, fi = ((lbr - 1.0) * lr + lbi * li) / den, (lbi * lr - (lbr - 1.0) * li) / den
        bbr, bbi = cmul(fr, fi, b_re[d, 0], b_im[d, 0])
        btr, bti = spread_dot(bbr, bbi, rep)
        tab_r, tab_i = power_table(lbr, lbi, expo_tab)
        pr, pi = spread_dot(tab_r, tab_i, spread_ref[d])
        mr, mi = cmul(pr, pi, btr, bti)
        gen = gen + dot3(c_re[d, 0], mr) - dot3(c_im[d, 0], mi)
        half = L * G
        qr, qi = spread_dot(tab_r, tab_i, spread_s_ref[d])
        xr, xi = cmul(qr, qi, btr[:, :half], bti[:, :half])
        bend_ref[0, 2 * d] = xr.astype(BF16)
        bend_ref[0, 2 * d + 1] = xi.astype(BF16)

    for t in range(L):
        a = (L - 1 - t) * G
        toep_ref[0, t * G:(t + 1) * G, :] = gen[:, a:a + L * G].astype(BF16)

    lane4 = lax.broadcasted_iota(jnp.int32, (L, 4 * P), 1)
    step4 = lax.broadcasted_iota(jnp.int32, (L, 4 * P), 0)
    pr, pi = power_table(lb4r, lb4i, jnp.where(lane4 < 2 * P, step4 + 1, L - step4))
    cr, ci = c4_re[0], c4_im[0]
    plane_bit = P.bit_length() - 1
    re_c = ((lax.broadcasted_iota(jnp.int32, (G, 4 * P), 1) >> plane_bit) & 1) == 0
    for t in range(L):
        re_part, im_part = cmul(cr, ci, pr[t:t + 1], pi[t:t + 1])
        cout_ref[0, t * G:(t + 1) * G, :] = jnp.where(re_c, re_part, -im_part).astype(BF16)
    re_1 = ((lax.broadcasted_iota(jnp.int32, (1, 4 * P), 1) >> plane_bit) & 1) == 0
    lr, li = power_table(lb4r, lb4i, jnp.full((1, 4 * P), L, jnp.int32))
    laml_ref[0] = jnp.where(re_1, lr, li)


def _s5_prep(lam_re, lam_im, log_dt, b_re, b_im, c_re, c_im, d_skip):
    ng, P, G, L = S5_GROUPS, S5_STATE, S5_GROUP, S5_CHUNK
    depth = lam_re.shape[0]
    tile4 = lambda a: jnp.concatenate([a[:, 0], a[:, 0], a[:, 1], a[:, 1]], axis=-1)
    ldt4 = jnp.repeat(tile4(log_dt[..., None]), P, axis=-1)[:, :, None, :]
    consts = _s5_constants()
    spec = lambda *tail: pl.BlockSpec((None, 2, 1) + tail, lambda l, g: (l, 0, g) + (0,) * len(tail))
    per_g = lambda *tail: pl.BlockSpec((None, 1) + tail, lambda l, g: (l, g) + (0,) * len(tail))
    whole = lambda a: pl.BlockSpec(a.shape, lambda l, g: (0,) * a.ndim)
    return pl.pallas_call(
        _s5_prep_kernel,
        grid=(depth, ng),
        in_specs=[spec(P, 1), spec(P, 1), spec(1, 1), per_g(1, 4 * P), per_g(1, 4 * P), per_g(1, 4 * P),
                  spec(P, G), spec(P, G), spec(G, P), spec(G, P), per_g(G, 4 * P), per_g(G, 4 * P), per_g(1, G)]
                 + [whole(m) for m in consts],
        out_specs=[per_g(L * G, L * G), per_g(4, P, L * G), per_g(L * G, 4 * P), per_g(1, 4 * P)],
        out_shape=[jax.ShapeDtypeStruct((depth, ng, L * G, L * G), BF16),
                   jax.ShapeDtypeStruct((depth, ng, 4, P, L * G), BF16),
                   jax.ShapeDtypeStruct((depth, ng, L * G, 4 * P), BF16),
                   jax.ShapeDtypeStruct((depth, ng, 1, 4 * P), F32)],
        compiler_params=_params("parallel", "parallel"),
        name="s5_prep",
    )(lam_re[..., None], lam_im[..., None], log_dt[..., None, None],
      tile4(lam_re)[:, :, None, :], tile4(lam_im)[:, :, None, :], ldt4,
      b_re, b_im, c_re, c_im, tile4(c_re), tile4(c_im), d_skip[:, :, None, :], *consts)


def _s5_sums_kernel(lo_ref, hi_ref, bend_ref, ut_ref, et_ref, *, nchunk):
    L, P, G = S5_CHUNK, S5_STATE, S5_GROUP
    per_half = LANES // G
    for half, z_ref in enumerate((lo_ref, hi_ref)):
        for s in range(L):
            zt = z_ref[pl.ds(s, nchunk, stride=L), :].T.astype(BF16)
            for gl in range(per_half):
                ut_ref[half * per_half + gl, s * G:(s + 1) * G, :] = zt[gl * G:(gl + 1) * G, :]
    for g in range(S5_GROUPS):
        for plane in range(4):
            row = plane * S5_GROUPS * P + g * P
            et_ref[row:row + P, :] = _dot(bend_ref[g, plane], ut_ref[g])


def _s5_scan_kernel(et_ref, lam_ref, xt_ref, e_sc, x_sc, *, bsz, steps):
    width = S5_GROUPS * S5_STATE
    for plane in range(4):
        e_sc[plane] = et_ref[plane * width:(plane + 1) * width, :].T
    ar_f, ai_f, ar_b, ai_b = lam_ref[0], lam_ref[1], lam_ref[2], lam_ref[3]
    zero = jnp.zeros_like(ar_f)
    for b in range(bsz):
        def body(i, carry):
            xr, xi, yr, yi = carry
            rf = b * steps + i
            rb = b * steps + steps - 1 - i
            x_sc[0, pl.ds(rf, 1), :] = xr
            x_sc[1, pl.ds(rf, 1), :] = xi
            x_sc[2, pl.ds(rb, 1), :] = yr
            x_sc[3, pl.ds(rb, 1), :] = yi
            xr, xi = (ar_f * xr - ai_f * xi + e_sc[0, pl.ds(rf, 1), :],
                      ar_f * xi + ai_f * xr + e_sc[1, pl.ds(rf, 1), :])
            yr, yi = (ar_b * yr - ai_b * yi + e_sc[2, pl.ds(rb, 1), :],
                      ar_b * yi + ai_b * yr + e_sc[3, pl.ds(rb, 1), :])
            return xr, xi, yr, yi
        lax.fori_loop(0, steps, body, (zero, zero, zero, zero))
    for plane in range(4):
        xt_ref[plane * width:(plane + 1) * width, :] = x_sc[plane].T


def _gelu_tanh(x):
    return 0.5 * x * (1.0 + jnp.tanh(math.sqrt(2.0 / math.pi) * (x + 0.044715 * (x * x * x))))


def _s5_out_kernel(ut_ref, xt_ref, toep_ref, cout_ref, lo_ref, hi_ref, yt_sc, *, nchunk):
    L, P, G = S5_CHUNK, S5_STATE, S5_GROUP
    width = S5_GROUPS * P
    for g in range(S5_GROUPS):
        xg = jnp.concatenate([xt_ref[plane * width + g * P:plane * width + (g + 1) * P, :] for plane in range(4)],
                             axis=0).astype(BF16)
        yt = _gelu_tanh(_dot(toep_ref[g], ut_ref[g]) + _dot(cout_ref[g], xg))
        for t in range(L):
            yt_sc[t, g * G:(g + 1) * G, :] = yt[t * G:(t + 1) * G, :]
    for t in range(L):
        y = yt_sc[t].T
        lo_ref[pl.ds(t, nchunk, stride=L), :] = y[:, :LANES]
        hi_ref[pl.ds(t, nchunk, stride=L), :] = y[:, LANES:]


def _s5(zs, prep, l, bsz, seq):
    ng, P, G, L = S5_GROUPS, S5_STATE, S5_GROUP, S5_CHUNK
    toep, bend, cout, laml = prep
    depth = toep.shape[0]
    laml = laml.reshape(depth, ng, 4, P).transpose(0, 2, 1, 3).reshape(depth, 4, 1, ng * P)
    t = bsz * seq
    tile = min(S5_TILE, t)
    nct = tile // L
    nchunk = t // L
    ut_spec = pl.BlockSpec((ng, L * G, nct), lambda i: (0, 0, i))
    plane_spec = pl.BlockSpec((4 * ng * P, nct), lambda i: (0, i))
    plane_shape = jax.ShapeDtypeStruct((4 * ng * P, nchunk), F32)
    ut, et = pl.pallas_call(
        functools.partial(_s5_sums_kernel, nchunk=nct),
        grid=(t // tile,),
        in_specs=[pl.BlockSpec((tile, LANES), lambda i: (i, 0)), pl.BlockSpec((tile, LANES), lambda i: (i, 1)),
                  _layer_spec(bend, l)],
        out_specs=[ut_spec, plane_spec],
        out_shape=[jax.ShapeDtypeStruct((ng, L * G, nchunk), BF16), plane_shape],
        compiler_params=_params("parallel"),
        name="s5_chunk_sums",
    )(zs, zs, bend)
    whole_planes = pl.BlockSpec((4 * ng * P, nchunk), lambda i: (0, 0))
    xt = pl.pallas_call(
        functools.partial(_s5_scan_kernel, bsz=bsz, steps=seq // L),
        grid=(1,),
        in_specs=[whole_planes, _layer_spec(laml, l)],
        out_specs=whole_planes,
        out_shape=plane_shape,
        scratch_shapes=[pltpu.VMEM((4, nchunk, ng * P), F32), pltpu.VMEM((4, nchunk, ng * P), F32)],
        compiler_params=_params("arbitrary"),
        name="s5_state_scan",
    )(et, laml)
    half_spec = pl.BlockSpec((tile, LANES), lambda i: (i, 0))
    half_shape = jax.ShapeDtypeStruct((t, LANES), F32)
    return pl.pallas_call(
        functools.partial(_s5_out_kernel, nchunk=nct),
        grid=(t // tile,),
        in_specs=[ut_spec, plane_spec, _layer_spec(toep, l), _layer_spec(cout, l)],
        out_specs=[half_spec, half_spec],
        out_shape=[half_shape, half_shape],
        scratch_shapes=[pltpu.VMEM((L, M_S5, nct), F32)],
        compiler_params=_params("parallel"),
        name="s5_outputs",
    )(ut, xt, toep, cout)


def _merge_kernel(x_ref, hf_ref, hb_ref, zo_ref, yf_ref, ys_lo_ref, ys_hi_ref, gpre_ref, wgate_ref, bgate_ref,
                  ng_ref, wm_ref, wf_ref, wglu_ref, bglu_ref, wout_ref, gpost_ref, o_ref, mixed_sc):
    x = x_ref[...]
    h = _rms(x, gpre_ref[...]).astype(BF16)
    hs = hf_ref[...] + hb_ref[...]
    parts = []
    for hd in range(HEADS):
        blk = hs[:, hd * HEAD_DIM:(hd + 1) * HEAD_DIM]
        mu = jnp.mean(blk, axis=-1, keepdims=True)
        cen = blk - mu
        var = jnp.mean(cen * cen, axis=-1, keepdims=True)
        parts.append(cen * lax.rsqrt(var + EPS))
    hm = (jnp.concatenate(parts, axis=1) * ng_ref[...] * _sigmoid(zo_ref[...].astype(F32))).astype(BF16)
    yf = yf_ref[...].astype(BF16)
    ys = jnp.concatenate([ys_lo_ref[...].astype(BF16), ys_hi_ref[...].astype(BF16)], axis=1)
    for n in range(D_MODEL // MERGE_COLS):
        lo = n * MERGE_COLS
        cols = slice(lo, lo + MERGE_COLS)

        def gate(i):
            gcols = slice(i * D_MODEL + lo, i * D_MODEL + lo + MERGE_COLS)
            return _sigmoid(_dot(h, wgate_ref[:, gcols]) + bgate_ref[:, gcols])

        lin = _dot(ys, wglu_ref[:, cols]) + bglu_ref[:, cols]
        gcols = slice(D_MODEL + lo, D_MODEL + lo + MERGE_COLS)
        y_s = lin * _sigmoid(_dot(ys, wglu_ref[:, gcols]) + bglu_ref[:, gcols])
        mixed = gate(0) * _dot(hm, wm_ref[:, cols]) + gate(1) * _dot(yf, wf_ref[:, cols]) + gate(2) * y_s
        mixed_sc[:, cols] = mixed.astype(BF16)
    o_ref[...] = x + _rms(_dot(mixed_sc[...], wout_ref[...]), gpost_ref[...])


def _merge(x2, hf, hb, zo, yf, ys_lo, ys_hi, params, l, tile):
    t = x2.shape[0]
    rows = lambda width: pl.BlockSpec((tile, width), lambda i: (i, 0))
    return pl.pallas_call(
        _merge_kernel,
        grid=(t // tile,),
        in_specs=[rows(D_MODEL), rows(M_MLSTM), rows(M_MLSTM), rows(M_MLSTM), rows(M_FOURIER),
                  rows(LANES), rows(LANES)]
                 + [_layer_spec(a, l, (W_BLOCK, 0) if k in (1, 2) else None)
                    for k, a in enumerate(params)],
        out_specs=rows(D_MODEL),
        out_shape=jax.ShapeDtypeStruct((t, D_MODEL), F32),
        scratch_shapes=[pltpu.VMEM((tile, D_MODEL), BF16)],
        compiler_params=_params("parallel"),
        name="merge",
    )(x2, hf, hb, zo, yf, ys_lo, ys_hi, *params)


def _ffn_kernel(x_ref, gpre_ref, w1_ref, w2_ref, gpost_ref, o_ref, *, n_split):
    x = x_ref[...]
    h = _rms(x, gpre_ref[...]).astype(BF16)
    width = D_FF // n_split
    f = None
    for j in range(n_split):
        a = jnp.maximum(_dot(h, w1_ref[:, j * width:(j + 1) * width]), 0.0)
        part = _dot((a * a).astype(BF16), w2_ref[j * width:(j + 1) * width, :])
        f = part if f is None else f + part
    o_ref[...] = x + _rms(f, gpost_ref[...])


def _ffn(x2, params, l, tile):
    t = x2.shape[0]
    rows = pl.BlockSpec((tile, D_MODEL), lambda i: (i, 0))
    return pl.pallas_call(
        functools.partial(_ffn_kernel, n_split=4),
        grid=(t // tile,),
        in_specs=[rows] + [_layer_spec(a, l) for a in params],
        out_specs=rows,
        out_shape=jax.ShapeDtypeStruct((t, D_MODEL), F32),
        compiler_params=_params("parallel"),
        name="ffn",
    )(x2, *params)


def kernel(x, g_mix_pre, g_mix_post, g_ffn_pre, g_ffn_post, w_in, b_in, conv_w, conv_b, mlstm_norm_g, w_up_mlstm, w_up_fourier, s5_lam_re, s5_lam_im, s5_log_dt, s5_b_re, s5_b_im, s5_c_re, s5_c_im, s5_d, w_glu, b_glu, w_out, w_ffn1, w_ffn2):
    bsz, seq, _ = x.shape
    depth = w_in.shape[0]
    t = bsz * seq
    tile = min(1024, t)
    bf = lambda a: a.astype(BF16)
    row = lambda a: a[:, None, :]
    w_gates = w_in[:, :, OFF_IG:OFF_FOURIER]
    b_gates = b_in[:, OFF_IG:OFF_FOURIER]
    regroup = lambda a: jnp.concatenate(
        [a[..., OFF_GATE:], a[..., :OFF_IG], a[..., OFF_FOURIER:OFF_GATE],
         jnp.zeros(a.shape[:-1] + (W_BLOCK - N_MAIN,), a.dtype)], axis=-1)
    w_cols = regroup(bf(w_in))
    b_cols = row(regroup(b_in))
    inproj_params = (row(g_mix_pre), w_cols, b_cols,
                     bf(w_gates), row(b_gates), bf(jnp.swapaxes(w_gates, 1, 2)), b_gates[:, :, None],
                     conv_w, row(conv_b))
    merge_params = (row(g_mix_pre), w_cols, b_cols, row(mlstm_norm_g),
                    bf(w_up_mlstm), bf(w_up_fourier), bf(w_glu), row(b_glu), bf(w_out), row(g_mix_post))
    ffn_params = (row(g_ffn_pre), bf(w_ffn1), bf(w_ffn2), row(g_ffn_post))
    s5_prep = _s5_prep(s5_lam_re, s5_lam_im, s5_log_dt, s5_b_re, s5_b_im, s5_c_re, s5_c_im, s5_d)
    x2 = x.reshape(t, D_MODEL)
    for l in range(depth):
        qk, zv, zo, zf, zs, zg, zgt = _inproj(x2, inproj_params, l, tile, seq)
        hf, hb = _mlstm(qk, zv, zg, zgt, bsz, seq)
        yf = _fourier(zf, bsz, seq)
        ys_lo, ys_hi = _s5(zs, s5_prep, l, bsz, seq)
        x2 = _merge(x2, hf, hb, zo, yf, ys_lo, ys_hi, merge_params, l, tile)
        x2 = _ffn(x2, ffn_params, l, tile)
    return x2.reshape(bsz, seq, D_MODEL)
```

```python
import functools
import itertools
import math

import numpy as np
import jax
import jax.numpy as jnp
from jax import lax
from jax.experimental import pallas as pl
from jax.experimental.pallas import tpu as pltpu

F32 = jnp.float32
BF16 = jnp.bfloat16
HIGHEST = lax.Precision.HIGHEST

LANES = 128
D_MODEL = 1024
M_MLSTM = 512
HEADS = 4
HEAD_DIM = 128
CHUNK = 128
MLSTM_SUB = 8
CONV_WIDTH = 5
CONV_COLS = 256
CONV_HALO = 8
M_FOURIER = 256
FOURIER_GROUP_DIM = 64
M_S5 = 256
S5_GROUP = 16
S5_GROUPS = 16
S5_STATE = 64
S5_CHUNK = 32
S5_TILE = 4096
N_BRANCHES = 3
MERGE_COLS = 256
D_FF = 4 * D_MODEL
EPS = 1e-6

OFF_Q = 0
OFF_V = 2 * M_MLSTM
OFF_O = 3 * M_MLSTM
OFF_IG = 4 * M_MLSTM
OFF_FOURIER = OFF_IG + 4 * HEADS
OFF_S5 = OFF_FOURIER + M_FOURIER
OFF_GATE = OFF_S5 + M_S5

DFT_N2 = 64
DFT_STEP = 8

VMEM_LIMIT = 56 * 1024 * 1024

NT_DIMS = (((1,), (1,)), ((), ()))
TN_DIMS = (((0,), (0,)), ((), ()))


def _params(*sem):
    return pltpu.CompilerParams(dimension_semantics=sem, vmem_limit_bytes=VMEM_LIMIT)


def _layer_spec(a, l, col_block=None):
    width, index = col_block if col_block else (a.shape[-1], 0)
    return pl.BlockSpec((None,) + a.shape[1:-1] + (width,), lambda *_: (l,) + (0,) * (a.ndim - 2) + (index,),
                        pipeline_mode=pl.Buffered(1))


def _rms(x, g):
    return x * lax.rsqrt(jnp.mean(x * x, axis=-1, keepdims=True) + EPS) * g


def _sigmoid(x):
    return 1.0 / (1.0 + jnp.exp(-x))


def _dot(a, b):
    return jnp.dot(a, b, preferred_element_type=F32)


def _dot_hi(a, b):
    return jnp.dot(a, b, precision=HIGHEST, preferred_element_type=F32)


def _split3(a):
    hi = a.astype(BF16)
    rest = a - hi.astype(F32)
    mid = rest.astype(BF16)
    return hi, mid, (rest - mid.astype(F32)).astype(BF16)


def _dot_sel(a, sel):
    hi, mid, lo = _split3(a)
    return _dot(hi, sel) + _dot(mid, sel) + _dot(lo, sel)


def _sel_dot(sel, a):
    hi, mid, lo = _split3(a)
    return _dot(sel, hi) + _dot(sel, mid) + _dot(sel, lo)


W_BLOCK = N_BRANCHES * D_MODEL
N_MAIN = OFF_IG + M_FOURIER + M_S5


def _inproj_kernel(x_ref, xp_ref, xn_ref, g_ref, wm_ref, bm_ref, wg_ref, bg_ref, wgt_ref, bgt_ref, cw_ref, cb_ref,
                   qk_ref, zv_ref, zo_ref, zf_ref, zs_ref, zg_ref, zgt_ref, ext_sc, *, tiles_per_seq):
    i = pl.program_id(0)
    tile = x_ref.shape[0]
    pad = CONV_WIDTH // 2
    h = _rms(x_ref[...], g_ref[...]).astype(BF16)
    h_halo = _rms(jnp.concatenate([xp_ref[...], xn_ref[...]], axis=0), g_ref[...]).astype(BF16)
    first = (i % tiles_per_seq) == 0
    last = (i % tiles_per_seq) == tiles_per_seq - 1

    def proj(hh, lo, hi):
        return _dot(hh, wm_ref[:, lo:hi]) + bm_ref[:, lo:hi]

    for n in range(2 * M_MLSTM // CONV_COLS):
        lo = n * CONV_COLS
        cols = slice(lo, lo + CONV_COLS)
        z_halo = proj(h_halo, OFF_Q + lo, OFF_Q + lo + CONV_COLS)
        ext_sc[0:CONV_HALO, cols] = jnp.where(first, 0.0, z_halo[:CONV_HALO])
        ext_sc[CONV_HALO:CONV_HALO + tile, cols] = proj(h, OFF_Q + lo, OFF_Q + lo + CONV_COLS)
        ext_sc[CONV_HALO + tile:, cols] = jnp.where(last, 0.0, z_halo[CONV_HALO:])
        ext = ext_sc[:, cols]
        acc = cb_ref[:, cols]
        for j in range(CONV_WIDTH):
            shifted = ext if j == pad else pltpu.roll(ext, (pad - j) % ext.shape[0], 0)
            acc = acc + cw_ref[j:j + 1, cols] * shifted[CONV_HALO:CONV_HALO + tile]
        half = 0.5 * acc
        qk = half + half * jnp.tanh(half)
        if lo < M_MLSTM:
            qk = qk * (HEAD_DIM ** -0.5)
        qk_ref[:, cols] = qk.astype(BF16)

    zv_ref[...] = proj(h, OFF_V, OFF_O).astype(BF16)
    zo_ref[...] = proj(h, OFF_O, OFF_IG).astype(BF16)
    zf_ref[...] = proj(h, OFF_IG, OFF_IG + M_FOURIER)
    zs_ref[...] = proj(h, OFF_IG + M_FOURIER, OFF_IG + M_FOURIER + M_S5)
    zg_ref[...] = _dot(h, wg_ref[...]) + bg_ref[...]
    zgt_ref[...] = lax.dot_general(wgt_ref[...], h, NT_DIMS, preferred_element_type=F32) + bgt_ref[...]


def _inproj(x2, params, l, tile, seq):
    t = x2.shape[0]
    n_gate = 4 * HEADS
    hpt = tile // CONV_HALO
    n_halo = t // CONV_HALO
    rows = lambda width: pl.BlockSpec((tile, width), lambda i: (i, 0))
    cols = lambda height: pl.BlockSpec((height, tile), lambda i: (0, i))
    return pl.pallas_call(
        functools.partial(_inproj_kernel, tiles_per_seq=seq // tile),
        grid=(t // tile,),
        in_specs=[rows(D_MODEL),
                  pl.BlockSpec((CONV_HALO, D_MODEL), lambda i: (jnp.maximum(i * hpt - 1, 0), 0)),
                  pl.BlockSpec((CONV_HALO, D_MODEL), lambda i: (jnp.minimum((i + 1) * hpt, n_halo - 1), 0))]
                 + [_layer_spec(a, l, (W_BLOCK, 1) if k in (1, 2) else None)
                    for k, a in enumerate(params)],
        out_specs=[rows(2 * M_MLSTM), rows(M_MLSTM), rows(M_MLSTM), rows(M_FOURIER), rows(M_S5),
                   rows(n_gate), cols(n_gate)],
        out_shape=[jax.ShapeDtypeStruct((t, 2 * M_MLSTM), BF16), jax.ShapeDtypeStruct((t, M_MLSTM), BF16),
                   jax.ShapeDtypeStruct((t, M_MLSTM), BF16), jax.ShapeDtypeStruct((t, M_FOURIER), F32),
                   jax.ShapeDtypeStruct((t, M_S5), F32), jax.ShapeDtypeStruct((t, n_gate), F32),
                   jax.ShapeDtypeStruct((n_gate, t), F32)],
        scratch_shapes=[pltpu.VMEM((tile + 2 * CONV_HALO, 2 * M_MLSTM), F32)],
        compiler_params=_params("parallel"),
        name="inproj",
    )(x2, x2, x2, *params)


def _log_sigmoid(x):
    return jnp.minimum(x, 0.0) - jnp.log1p(jnp.exp(-jnp.abs(x)))


def _mlstm_kernel(qk_f, v_f, g_f, gt_f, qk_b, v_b, g_b, gt_b, hf_ref, hb_ref, state_sc, m_sc):
    c = pl.program_id(1)
    L = CHUNK

    @pl.when(c == 0)
    def _():
        state_sc[...] = jnp.zeros_like(state_sc)
        m_sc[...] = jnp.zeros_like(m_sc)

    row = lax.broadcasted_iota(jnp.int32, (L, L), 0)
    col = lax.broadcasted_iota(jnp.int32, (L, L), 1)
    lower = row >= col
    upper = row <= col
    lower_f = jnp.where(lower, 1.0, 0.0).astype(BF16)
    upper_f = jnp.where(upper, 1.0, 0.0).astype(BF16)
    ones_v = jnp.ones((L, HEAD_DIM), F32)

    streams = ((0, qk_f, v_f, g_f, gt_f, hf_ref), (1, qk_b, v_b, g_b, gt_b, hb_ref))
    for sub, (d, qk_ref, v_ref, g_ref, gt_ref, out_ref) in itertools.product(range(MLSTM_SUB), streams):
        r0 = (sub if d == 0 else MLSTM_SUB - 1 - sub) * L
        rows = slice(r0, r0 + L)
        gates = g_ref[rows, :]
        gates_t = gt_ref[:, rows]
        lf_cols = _log_sigmoid(gates[:, 2 * HEADS:])
        lf_rows = _log_sigmoid(gates_t[2 * HEADS:, :])
        if d == 0:
            b_cols = _sel_dot(lower_f, lf_cols)
            b_rows = _dot_sel(lf_rows, upper_f)
            mask = lower
        else:
            b_cols = _sel_dot(upper_f, lf_cols)
            b_rows = _dot_sel(lf_rows, lower_f)
            mask = upper

        for hd in range(HEADS):
            k_idx = d * HEADS + hd
            lo = hd * HEAD_DIM
            q = qk_ref[rows, lo:lo + HEAD_DIM]
            k = qk_ref[rows, M_MLSTM + lo:M_MLSTM + lo + HEAD_DIM]
            v = v_ref[rows, lo:lo + HEAD_DIM]
            b_c = jnp.broadcast_to(b_cols[:, k_idx:k_idx + 1], (L, L))
            ig_c = jnp.broadcast_to(gates[:, k_idx:k_idx + 1], (L, L))
            b_r = b_rows[k_idx:k_idx + 1, :]
            ig_r = gates_t[k_idx:k_idx + 1, :]
            g_tot = b_c[L - 1:L, :] if d == 0 else b_c[0:1, :]
            m_prev = m_sc[k_idx:k_idx + 1, :]

            d_log = jnp.where(mask, b_c - b_r + ig_r, -1e30)
            inter_log = b_c + m_prev
            m_t = jnp.maximum(inter_log, jnp.max(d_log, axis=1, keepdims=True))
            scores = lax.dot_general(q, k, NT_DIMS, preferred_element_type=F32) * jnp.exp(d_log - m_t)
            inter_w = jnp.exp(inter_log - m_t)
            st = state_sc[k_idx]
            v_ext = jnp.concatenate([v, ones_v.astype(BF16)], axis=1)
            res = _dot(scores.astype(BF16), v_ext) + _dot((inter_w * q.astype(F32)).astype(BF16), st.astype(BF16))
            num = res[:, :HEAD_DIM]
            den = res[:, HEAD_DIM:]
            out_ref[rows, lo:lo + HEAD_DIM] = num / jnp.maximum(jnp.abs(den), jnp.exp(-m_t))

            a_c = g_tot - b_c + ig_c
            a_max = jnp.max(a_c, axis=0, keepdims=True)
            w_c = jnp.exp(a_c - a_max)
            vw = jnp.concatenate([v.astype(F32) * w_c, w_c], axis=1).astype(BF16)
            st_loc = lax.dot_general(k, vw, TN_DIMS, preferred_element_type=F32)
            m_new = jnp.maximum(g_tot + m_prev, a_max)
            s_old = jnp.exp(g_tot + m_prev - m_new)
            s_new = jnp.exp(a_max - m_new)
            s_old2 = jnp.concatenate([s_old, s_old], axis=1)
            s_new2 = jnp.concatenate([s_new, s_new], axis=1)
            state_sc[k_idx] = s_old2 * st + s_new2 * st_loc
            m_sc[k_idx:k_idx + 1, :] = m_new


def _mlstm(qk, zv, zg, zgt, bsz, seq):
    t = bsz * seq
    blk = CHUNK * MLSTM_SUB
    nc = seq // blk
    n_gate = 4 * HEADS

    def fwd(b, c):
        return b * nc + c

    def bwd(b, c):
        return b * nc + nc - 1 - c

    def specs(pos):
        return [pl.BlockSpec((blk, 2 * M_MLSTM), lambda b, c: (pos(b, c), 0)),
                pl.BlockSpec((blk, M_MLSTM), lambda b, c: (pos(b, c), 0)),
                pl.BlockSpec((blk, n_gate), lambda b, c: (pos(b, c), 0)),
                pl.BlockSpec((n_gate, blk), lambda b, c: (0, pos(b, c)))]

    out_shape = jax.ShapeDtypeStruct((t, M_MLSTM), F32)
    return pl.pallas_call(
        _mlstm_kernel,
        grid=(bsz, nc),
        in_specs=specs(fwd) + specs(bwd),
        out_specs=[pl.BlockSpec((blk, M_MLSTM), lambda b, c: (fwd(b, c), 0)),
                   pl.BlockSpec((blk, M_MLSTM), lambda b, c: (bwd(b, c), 0))],
        out_shape=[out_shape, out_shape],
        scratch_shapes=[pltpu.VMEM((2 * HEADS, HEAD_DIM, 2 * HEAD_DIM), F32),
                        pltpu.VMEM((2 * HEADS, HEAD_DIM), F32)],
        compiler_params=_params("arbitrary", "arbitrary"),
        name="mlstm",
    )(qk, zv, zg, zgt, qk, zv, zg, zgt)


def _dft_constants(seq):
    n1, n2 = seq // DFT_N2, DFT_N2
    k1 = np.arange(n1)[:, None, None]
    s2 = np.arange(n2)[None, None, :]
    s1 = np.arange(n1)[None, :, None]
    ang = -2.0 * np.pi * ((k1 * (n2 * s1 + s2)) % seq) / seq
    stage1 = np.concatenate([np.cos(ang), np.sin(ang)], axis=0)
    stage1 = np.ascontiguousarray(stage1.transpose(2, 0, 1))
    a2 = 2.0 * np.pi * np.outer(np.arange(n2), np.arange(n2)) / n2
    c2, sn2 = np.cos(a2), np.sin(a2)
    stage2 = np.block([[c2, sn2], [-sn2, c2]])
    ag = 2.0 * np.pi * np.outer(np.arange(FOURIER_GROUP_DIM), np.arange(FOURIER_GROUP_DIM)) / FOURIER_GROUP_DIM
    scale = 1.0 / math.sqrt(seq * FOURIER_GROUP_DIM)
    eye = np.eye(LANES // FOURIER_GROUP_DIM)
    group = np.stack([np.kron(eye, np.cos(ag)), np.kron(eye, np.sin(ag))]) * scale
    as_bf16 = lambda a: jnp.asarray(a, F32).astype(BF16)
    return as_bf16(stage1), as_bf16(stage2), as_bf16(group)


def _fourier_kernel(u_ref, w1_ref, w2_ref, wg_ref, y_ref, u_sc, b_sc, y_sc, *, n1):
    n2 = DFT_N2
    rows = n1 * DFT_STEP
    for sb in range(n2 // DFT_STEP):
        u_sc[sb % 2] = u_ref[:, sb * DFT_STEP:(sb + 1) * DFT_STEP, :].reshape(rows, LANES)
        for j in range(DFT_STEP):
            pick = pl.ds(j, n1, stride=DFT_STEP)
            res = _dot(w1_ref[sb * DFT_STEP + j], u_sc[sb % 2, pick, :].astype(BF16))
            b_sc[sb, 0, pick, :] = res[:n1]
            b_sc[sb, 1, pick, :] = res[n1:]
    for kb in range(n1 // DFT_STEP):
        def gather(part, j):
            r0 = (kb * DFT_STEP + j) * DFT_STEP
            return jnp.concatenate([b_sc[sb, part, r0:r0 + DFT_STEP, :] for sb in range(n2 // DFT_STEP)], axis=0)

        stacked = jnp.concatenate([jnp.concatenate([gather(0, j), gather(1, j)], axis=0) for j in range(DFT_STEP)],
                                  axis=1).astype(BF16)
        z = _dot(w2_ref[...], stacked).astype(BF16)
        zr = jnp.concatenate([z[:n2, j * LANES:(j + 1) * LANES] for j in range(DFT_STEP)], axis=0)
        zi = jnp.concatenate([z[n2:, j * LANES:(j + 1) * LANES] for j in range(DFT_STEP)], axis=0)
        y = _dot(zr, wg_ref[0]) + _dot(zi, wg_ref[1])
        for j in range(DFT_STEP):
            y_sc[kb % 2, pl.ds(j, n2, stride=DFT_STEP), :] = y[j * n2:(j + 1) * n2, :]
        y_ref[:, kb * DFT_STEP:(kb + 1) * DFT_STEP, :] = y_sc[kb % 2].reshape(n2, DFT_STEP, LANES)


def _fourier(zf, bsz, seq):
    n1, n2 = seq // DFT_N2, DFT_N2
    stage1, stage2, group = _dft_constants(seq)
    whole = lambda a: pl.BlockSpec(a.shape, lambda b, h: (0,) * a.ndim)
    y = pl.pallas_call(
        functools.partial(_fourier_kernel, n1=n1),
        grid=(bsz, M_FOURIER // LANES),
        in_specs=[pl.BlockSpec((None, n1, n2, LANES), lambda b, h: (b, 0, 0, h)),
                  whole(stage1), whole(stage2), whole(group)],
        out_specs=pl.BlockSpec((None, n2, n1, LANES), lambda b, h: (b, 0, 0, h)),
        out_shape=jax.ShapeDtypeStruct((bsz, n2, n1, M_FOURIER), F32),
        scratch_shapes=[pltpu.VMEM((2, n1 * DFT_STEP, LANES), F32),
                        pltpu.VMEM((n2 // DFT_STEP, 2, n1 * DFT_STEP, LANES), F32),
                        pltpu.VMEM((2, n2 * DFT_STEP, LANES), F32)],
        compiler_params=_params("parallel", "parallel"),
        name="fourier",
    )(zf.reshape(bsz, n1, n2, M_FOURIER), stage1, stage2, group)
    return y.reshape(bsz * seq, M_FOURIER)


def _s5_constants():
    L, G = S5_CHUNK, S5_GROUP
    lane = np.arange(L * G)
    expo = np.arange(LANES)[:, None]
    rep = (lane[None, :] % G == np.arange(G)[:, None])
    step = lane // G
    spread = np.stack([expo == (L - 1) - step, expo == step])
    return tuple(jnp.asarray(m, BF16) for m in (rep, spread))


def _s5_prep_kernel(lamc_re, lamc_im, ldt_ref, lam4_re, lam4_im, ldt4_ref, b_re, b_im, c_re, c_im, c4_re, c4_im,
                    d_ref, rep_ref, spread_ref, toep_ref, bend_ref, cout_ref, laml_ref):
    L, P, G = S5_CHUNK, S5_STATE, S5_GROUP
    shift = G.bit_length() - 1
    nbits = L.bit_length()

    def cmul(ar, ai, br, bi):
        return ar * br - ai * bi, ar * bi + ai * br

    def lam_bar(lre, lim, dt):
        mag = jnp.exp(lre * dt)
        return mag * jnp.cos(lim * dt), mag * jnp.sin(lim * dt)

    def power_table(base_r, base_i, expo):
        tr = jnp.ones(expo.shape, F32)
        ti = jnp.zeros(expo.shape, F32)
        sr, si = base_r, base_i
        for bit in range(nbits):
            nr, ni = cmul(tr, ti, sr, si)
            has = ((expo >> bit) & 1) == 1
            tr, ti = jnp.where(has, nr, tr), jnp.where(has, ni, ti)
            sr, si = cmul(sr, si, sr, si)
        return tr, ti

    def spread_dot(xr, xi, sel):
        n = xr.shape[0]
        out = _dot(jnp.concatenate(_split3(xr)[:2] + _split3(xi)[:2], axis=0), sel)
        return out[:n] + out[n:2 * n], out[2 * n:3 * n] + out[3 * n:]

    def dot3(c, m):
        c_hi, c_mid, _ = _split3(c)
        m_hi, m_mid, _ = _split3(m)
        return _dot(c_hi, m_hi) + _dot(c_mid, m_hi) + _dot(c_hi, m_mid)

    rep = rep_ref[...]
    half = L * G
    lane_g = lax.broadcasted_iota(jnp.int32, (G, half), 1)
    row_g = lax.broadcasted_iota(jnp.int32, (G, half), 0)
    expo_tab = jnp.minimum(lax.broadcasted_iota(jnp.int32, (P, LANES), 1), L)

    lb4r, lb4i = lam_bar(lam4_re[0], lam4_im[0], jnp.exp(ldt4_ref[0]))

    def as_column(row, d):
        block = jnp.broadcast_to(row[:, 2 * d * P:2 * (d + 1) * P], (8, 2 * P))
        return block.T[:P, 0:1]

    taps = []
    for d in range(2):
        dt = jnp.exp(ldt_ref[d, 0])
        lr, li = lamc_re[d, 0], lamc_im[d, 0]
        lbr, lbi = as_column(lb4r, d), as_column(lb4i, d)
        den = lr * lr + li * li
        fr, fi = ((lbr - 1.0) * lr + lbi * li) / den, (lbi * lr - (lbr - 1.0) * li) / den
        bbr, bbi = cmul(fr, fi, b_re[d, 0], b_im[d, 0])
        btr, bti = spread_dot(bbr, bbi, rep)
        tab_r, tab_i = power_table(lbr, lbi, expo_tab)
        qr, qi = spread_dot(tab_r, tab_i, spread_ref[d])
        xr, xi = cmul(qr, qi, btr, bti)
        bend_ref[0, 2 * d] = xr.astype(BF16)
        bend_ref[0, 2 * d + 1] = xi.astype(BF16)
        taps.append(dot3(c_re[d, 0], xr) - dot3(c_im[d, 0], xi))

    d_tiled = _dot_sel(jnp.broadcast_to(d_ref[0], (G, G)), rep)
    skip = jnp.where(((lane_g & (G - 1)) == row_g) & ((lane_g >> shift) == L - 1), d_tiled, 0.0)
    pad = jnp.zeros((G, half), F32)
    gen = (jnp.concatenate([taps[0] + skip, pad], axis=1)
           + pltpu.roll(jnp.concatenate([taps[1], pad], axis=1), (L - 1) * G, 1))

    for t in range(L):
        a = (L - 1 - t) * G
        toep_ref[0, t * G:(t + 1) * G, :] = gen[:, a:a + L * G].astype(BF16)

    lane4 = lax.broadcasted_iota(jnp.int32, (L, 4 * P), 1)
    step4 = lax.broadcasted_iota(jnp.int32, (L, 4 * P), 0)
    pr, pi = power_table(lb4r, lb4i, jnp.where(lane4 < 2 * P, step4 + 1, L - step4))
    cr, ci = c4_re[0], c4_im[0]
    plane_bit = P.bit_length() - 1
    re_c = ((lax.broadcasted_iota(jnp.int32, (G, 4 * P), 1) >> plane_bit) & 1) == 0
    for t in range(L):
        re_part, im_part = cmul(cr, ci, pr[t:t + 1], pi[t:t + 1])
        cout_ref[0, t * G:(t + 1) * G, :] = jnp.where(re_c, re_part, -im_part).astype(BF16)
    re_1 = ((lax.broadcasted_iota(jnp.int32, (1, 4 * P), 1) >> plane_bit) & 1) == 0
    lr, li = power_table(lb4r, lb4i, jnp.full((1, 4 * P), L, jnp.int32))
    laml_ref[0] = jnp.where(re_1, lr, li)


def _s5_prep(lam_re, lam_im, log_dt, b_re, b_im, c_re, c_im, d_skip):
    ng, P, G, L = S5_GROUPS, S5_STATE, S5_GROUP, S5_CHUNK
    depth = lam_re.shape[0]
    tile4 = lambda a: jnp.concatenate([a[:, 0], a[:, 0], a[:, 1], a[:, 1]], axis=-1)
    ldt4 = jnp.repeat(tile4(log_dt[..., None]), P, axis=-1)[:, :, None, :]
    consts = _s5_constants()
    spec = lambda *tail: pl.BlockSpec((None, 2, 1) + tail, lambda l, g: (l, 0, g) + (0,) * len(tail))
    per_g = lambda *tail: pl.BlockSpec((None, 1) + tail, lambda l, g: (l, g) + (0,) * len(tail))
    whole = lambda a: pl.BlockSpec(a.shape, lambda l, g: (0,) * a.ndim)
    return pl.pallas_call(
        _s5_prep_kernel,
        grid=(depth, ng),
        in_specs=[spec(P, 1), spec(P, 1), spec(1, 1), per_g(1, 4 * P), per_g(1, 4 * P), per_g(1, 4 * P),
                  spec(P, G), spec(P, G), spec(G, P), spec(G, P), per_g(G, 4 * P), per_g(G, 4 * P), per_g(1, G)]
                 + [whole(m) for m in consts],
        out_specs=[per_g(L * G, L * G), per_g(4, P, L * G), per_g(L * G, 4 * P), per_g(1, 4 * P)],
        out_shape=[jax.ShapeDtypeStruct((depth, ng, L * G, L * G), BF16),
                   jax.ShapeDtypeStruct((depth, ng, 4, P, L * G), BF16),
                   jax.ShapeDtypeStruct((depth, ng, L * G, 4 * P), BF16),
                   jax.ShapeDtypeStruct((depth, ng, 1, 4 * P), F32)],
        compiler_params=_params("parallel", "parallel"),
        name="s5_prep",
    )(lam_re[..., None], lam_im[..., None], log_dt[..., None, None],
      tile4(lam_re)[:, :, None, :], tile4(lam_im)[:, :, None, :], ldt4,
      b_re, b_im, c_re, c_im, tile4(c_re), tile4(c_im), d_skip[:, :, None, :], *consts)


def _s5_sums_kernel(lo_ref, hi_ref, bend_ref, ut_ref, et_ref, *, nchunk):
    L, P, G = S5_CHUNK, S5_STATE, S5_GROUP
    per_half = LANES // G
    for half, z_ref in enumerate((lo_ref, hi_ref)):
        for s in range(L):
            zt = z_ref[pl.ds(s, nchunk, stride=L), :].T.astype(BF16)
            for gl in range(per_half):
                ut_ref[half * per_half + gl, s * G:(s + 1) * G, :] = zt[gl * G:(gl + 1) * G, :]
    for g in range(S5_GROUPS):
        for plane in range(4):
            row = plane * S5_GROUPS * P + g * P
            et_ref[row:row + P, :] = _dot(bend_ref[g, plane], ut_ref[g])


def _s5_scan_kernel(et_ref, lam_ref, xt_ref, e_sc, x_sc, *, bsz, steps):
    width = S5_GROUPS * S5_STATE
    for plane in range(4):
        e_sc[plane] = et_ref[plane * width:(plane + 1) * width, :].T
    ar_f, ai_f, ar_b, ai_b = lam_ref[0], lam_ref[1], lam_ref[2], lam_ref[3]
    zero = jnp.zeros_like(ar_f)
    for b in range(bsz):
        def body(i, carry):
            xr, xi, yr, yi = carry
            rf = b * steps + i
            rb = b * steps + steps - 1 - i
            x_sc[0, pl.ds(rf, 1), :] = xr
            x_sc[1, pl.ds(rf, 1), :] = xi
            x_sc[2, pl.ds(rb, 1), :] = yr
            x_sc[3, pl.ds(rb, 1), :] = yi
            xr, xi = (ar_f * xr - ai_f * xi + e_sc[0, pl.ds(rf, 1), :],
                      ar_f * xi + ai_f * xr + e_sc[1, pl.ds(rf, 1), :])
            yr, yi = (ar_b * yr - ai_b * yi + e_sc[2, pl.ds(rb, 1), :],
                      ar_b * yi + ai_b * yr + e_sc[3, pl.ds(rb, 1), :])
            return xr, xi, yr, yi
        lax.fori_loop(0, steps, body, (zero, zero, zero, zero))
    for plane in range(4):
        xt_ref[plane * width:(plane + 1) * width, :] = x_sc[plane].T


def _gelu_tanh(x):
    return 0.5 * x * (1.0 + jnp.tanh(math.sqrt(2.0 / math.pi) * (x + 0.044715 * (x * x * x))))


def _s5_out_kernel(ut_ref, xt_ref, toep_ref, cout_ref, lo_ref, hi_ref, yt_sc, *, nchunk):
    L, P, G = S5_CHUNK, S5_STATE, S5_GROUP
    width = S5_GROUPS * P
    for g in range(S5_GROUPS):
        xg = jnp.concatenate([xt_ref[plane * width + g * P:plane * width + (g + 1) * P, :] for plane in range(4)],
                             axis=0).astype(BF16)
        yt = _gelu_tanh(_dot(toep_ref[g], ut_ref[g]) + _dot(cout_ref[g], xg))
        for t in range(L):
            yt_sc[t, g * G:(g + 1) * G, :] = yt[t * G:(t + 1) * G, :]
    for t in range(L):
        y = yt_sc[t].T
        lo_ref[pl.ds(t, nchunk, stride=L), :] = y[:, :LANES]
        hi_ref[pl.ds(t, nchunk, stride=L), :] = y[:, LANES:]


def _s5(zs, prep, l, bsz, seq):
    ng, P, G, L = S5_GROUPS, S5_STATE, S5_GROUP, S5_CHUNK
    toep, bend, cout, laml = prep
    depth = toep.shape[0]
    laml = laml.reshape(depth, ng, 4, P).transpose(0, 2, 1, 3).reshape(depth, 4, 1, ng * P)
    t = bsz * seq
    tile = min(S5_TILE, t)
    nct = tile // L
    nchunk = t // L
    ut_spec = pl.BlockSpec((ng, L * G, nct), lambda i: (0, 0, i))
    plane_spec = pl.BlockSpec((4 * ng * P, nct), lambda i: (0, i))
    plane_shape = jax.ShapeDtypeStruct((4 * ng * P, nchunk), F32)
    ut, et = pl.pallas_call(
        functools.partial(_s5_sums_kernel, nchunk=nct),
        grid=(t // tile,),
        in_specs=[pl.BlockSpec((tile, LANES), lambda i: (i, 0)), pl.BlockSpec((tile, LANES), lambda i: (i, 1)),
                  _layer_spec(bend, l)],
        out_specs=[ut_spec, plane_spec],
        out_shape=[jax.ShapeDtypeStruct((ng, L * G, nchunk), BF16), plane_shape],
        compiler_params=_params("parallel"),
        name="s5_chunk_sums",
    )(zs, zs, bend)
    whole_planes = pl.BlockSpec((4 * ng * P, nchunk), lambda i: (0, 0))
    xt = pl.pallas_call(
        functools.partial(_s5_scan_kernel, bsz=bsz, steps=seq // L),
        grid=(1,),
        in_specs=[whole_planes, _layer_spec(laml, l)],
        out_specs=whole_planes,
        out_shape=plane_shape,
        scratch_shapes=[pltpu.VMEM((4, nchunk, ng * P), F32), pltpu.VMEM((4, nchunk, ng * P), F32)],
        compiler_params=_params("arbitrary"),
        name="s5_state_scan",
    )(et, laml)
    half_spec = pl.BlockSpec((tile, LANES), lambda i: (i, 0))
    half_shape = jax.ShapeDtypeStruct((t, LANES), F32)
    return pl.pallas_call(
        functools.partial(_s5_out_kernel, nchunk=nct),
        grid=(t // tile,),
        in_specs=[ut_spec, plane_spec, _layer_spec(toep, l), _layer_spec(cout, l)],
        out_specs=[half_spec, half_spec],
        out_shape=[half_shape, half_shape],
        scratch_shapes=[pltpu.VMEM((L, M_S5, nct), F32)],
        compiler_params=_params("parallel"),
        name="s5_outputs",
    )(ut, xt, toep, cout)


def _merge_kernel(x_ref, hf_ref, hb_ref, zo_ref, yf_ref, ys_lo_ref, ys_hi_ref, gpre_ref, wgate_ref, bgate_ref,
                  ng_ref, wm_ref, wf_ref, wglu_ref, bglu_ref, wout_ref, gpost_ref, o_ref, mixed_sc):
    x = x_ref[...]
    h = _rms(x, gpre_ref[...]).astype(BF16)
    hs = hf_ref[...] + hb_ref[...]
    parts = []
    for hd in range(HEADS):
        blk = hs[:, hd * HEAD_DIM:(hd + 1) * HEAD_DIM]
        mu = jnp.mean(blk, axis=-1, keepdims=True)
        cen = blk - mu
        var = jnp.mean(cen * cen, axis=-1, keepdims=True)
        parts.append(cen * lax.rsqrt(var + EPS))
    hm = (jnp.concatenate(parts, axis=1) * ng_ref[...] * _sigmoid(zo_ref[...].astype(F32))).astype(BF16)
    yf = yf_ref[...].astype(BF16)
    ys = jnp.concatenate([ys_lo_ref[...].astype(BF16), ys_hi_ref[...].astype(BF16)], axis=1)
    for n in range(D_MODEL // MERGE_COLS):
        lo = n * MERGE_COLS
        cols = slice(lo, lo + MERGE_COLS)

        def gate(i):
            gcols = slice(i * D_MODEL + lo, i * D_MODEL + lo + MERGE_COLS)
            return _sigmoid(_dot(h, wgate_ref[:, gcols]) + bgate_ref[:, gcols])

        lin = _dot(ys, wglu_ref[:, cols]) + bglu_ref[:, cols]
        gcols = slice(D_MODEL + lo, D_MODEL + lo + MERGE_COLS)
        y_s = lin * _sigmoid(_dot(ys, wglu_ref[:, gcols]) + bglu_ref[:, gcols])
        mixed = gate(0) * _dot(hm, wm_ref[:, cols]) + gate(1) * _dot(yf, wf_ref[:, cols]) + gate(2) * y_s
        mixed_sc[:, cols] = mixed.astype(BF16)
    o_ref[...] = x + _rms(_dot(mixed_sc[...], wout_ref[...]), gpost_ref[...])


def _merge(x2, hf, hb, zo, yf, ys_lo, ys_hi, params, l, tile):
    t = x2.shape[0]
    rows = lambda width: pl.BlockSpec((tile, width), lambda i: (i, 0))
    return pl.pallas_call(
        _merge_kernel,
        grid=(t // tile,),
        in_specs=[rows(D_MODEL), rows(M_MLSTM), rows(M_MLSTM), rows(M_MLSTM), rows(M_FOURIER),
                  rows(LANES), rows(LANES)]
                 + [_layer_spec(a, l, (W_BLOCK, 0) if k in (1, 2) else None)
                    for k, a in enumerate(params)],
        out_specs=rows(D_MODEL),
        out_shape=jax.ShapeDtypeStruct((t, D_MODEL), F32),
        scratch_shapes=[pltpu.VMEM((tile, D_MODEL), BF16)],
        compiler_params=_params("parallel"),
        name="merge",
    )(x2, hf, hb, zo, yf, ys_lo, ys_hi, *params)


def _ffn_kernel(x_ref, gpre_ref, w1_ref, w2_ref, gpost_ref, o_ref, *, n_split):
    x = x_ref[...]
    h = _rms(x, gpre_ref[...]).astype(BF16)
    width = D_FF // n_split
    f = None
    for j in range(n_split):
        a = jnp.maximum(_dot(h, w1_ref[:, j * width:(j + 1) * width]), 0.0)
        part = _dot((a * a).astype(BF16), w2_ref[j * width:(j + 1) * width, :])
        f = part if f is None else f + part
    o_ref[...] = x + _rms(f, gpost_ref[...])


def _ffn(x2, params, l, tile):
    t = x2.shape[0]
    rows = pl.BlockSpec((tile, D_MODEL), lambda i: (i, 0))
    return pl.pallas_call(
        functools.partial(_ffn_kernel, n_split=4),
        grid=(t // tile,),
        in_specs=[rows] + [_layer_spec(a, l) for a in params],
        out_specs=rows,
        out_shape=jax.ShapeDtypeStruct((t, D_MODEL), F32),
        compiler_params=_params("parallel"),
        name="ffn",
    )(x2, *params)


def kernel(x, g_mix_pre, g_mix_post, g_ffn_pre, g_ffn_post, w_in, b_in, conv_w, conv_b, mlstm_norm_g, w_up_mlstm, w_up_fourier, s5_lam_re, s5_lam_im, s5_log_dt, s5_b_re, s5_b_im, s5_c_re, s5_c_im, s5_d, w_glu, b_glu, w_out, w_ffn1, w_ffn2):
    bsz, seq, _ = x.shape
    depth = w_in.shape[0]
    t = bsz * seq
    tile = min(1024, t)
    bf = lambda a: a.astype(BF16)
    row = lambda a: a[:, None, :]
    w_gates = w_in[:, :, OFF_IG:OFF_FOURIER]
    b_gates = b_in[:, OFF_IG:OFF_FOURIER]
    regroup = lambda a: jnp.concatenate(
        [a[..., OFF_GATE:], a[..., :OFF_IG], a[..., OFF_FOURIER:OFF_GATE],
         jnp.zeros(a.shape[:-1] + (W_BLOCK - N_MAIN,), a.dtype)], axis=-1)
    w_cols = regroup(bf(w_in))
    b_cols = row(regroup(b_in))
    inproj_params = (row(g_mix_pre), w_cols, b_cols,
                     bf(w_gates), row(b_gates), bf(jnp.swapaxes(w_gates, 1, 2)), b_gates[:, :, None],
                     conv_w, row(conv_b))
    merge_params = (row(g_mix_pre), w_cols, b_cols, row(mlstm_norm_g),
                    bf(w_up_mlstm), bf(w_up_fourier), bf(w_glu), row(b_glu), bf(w_out), row(g_mix_post))
    ffn_params = (row(g_ffn_pre), bf(w_ffn1), bf(w_ffn2), row(g_ffn_post))
    s5_prep = _s5_prep(s5_lam_re, s5_lam_im, s5_log_dt, s5_b_re, s5_b_im, s5_c_re, s5_c_im, s5_d)
    x2 = x.reshape(t, D_MODEL)
    for l in range(depth):
        qk, zv, zo, zf, zs, zg, zgt = _inproj(x2, inproj_params, l, tile, seq)
        hf, hb = _mlstm(qk, zv, zg, zgt, bsz, seq)
        yf = _fourier(zf, bsz, seq)
        ys_lo, ys_hi = _s5(zs, s5_prep, l, bsz, seq)
        x2 = _merge(x2, hf, hb, zo, yf, ys_lo, ys_hi, merge_params, l, tile)
        x2 = _ffn(x2, ffn_params, l, tile)
    return x2.reshape(bsz, seq, D_MODEL)
```

```python
import functools
import itertools
import math

import numpy as np
import jax
import jax.numpy as jnp
from jax import lax
from jax.experimental import pallas as pl
from jax.experimental.pallas import tpu as pltpu

F32 = jnp.float32
BF16 = jnp.bfloat16

LANES = 128
D_MODEL = 1024
M_MLSTM = 512
HEADS = 4
HEAD_DIM = 128
CHUNK = 128
MLSTM_SUB = 8
CONV_WIDTH = 5
CONV_COLS = 256
CONV_HALO = 8
M_FOURIER = 256
FOURIER_GROUP_DIM = 64
M_S5 = 256
S5_GROUP = 16
S5_GROUPS = 16
S5_STATE = 64
S5_CHUNK = 32
S5_TILE = 4096
N_BRANCHES = 3
MERGE_COLS = 256
D_FF = 4 * D_MODEL
EPS = 1e-6

OFF_Q = 0
OFF_V = 2 * M_MLSTM
OFF_O = 3 * M_MLSTM
OFF_IG = 4 * M_MLSTM
OFF_FOURIER = OFF_IG + 4 * HEADS
OFF_S5 = OFF_FOURIER + M_FOURIER
OFF_GATE = OFF_S5 + M_S5

DFT_N2 = 64
DFT_STEP = 8

VMEM_LIMIT = 56 * 1024 * 1024

NT_DIMS = (((1,), (1,)), ((), ()))
TN_DIMS = (((0,), (0,)), ((), ()))


def _params(*sem):
    return pltpu.CompilerParams(dimension_semantics=sem, vmem_limit_bytes=VMEM_LIMIT)


def _layer_spec(a, l, col_block=None):
    width, index = col_block if col_block else (a.shape[-1], 0)
    return pl.BlockSpec((None,) + a.shape[1:-1] + (width,), lambda *_: (l,) + (0,) * (a.ndim - 2) + (index,),
                        pipeline_mode=pl.Buffered(1))


def _rms(x, g):
    return x * lax.rsqrt(jnp.mean(x * x, axis=-1, keepdims=True) + EPS) * g


def _sigmoid(x):
    return 1.0 / (1.0 + jnp.exp(-x))


def _dot(a, b):
    return jnp.dot(a, b, preferred_element_type=F32)


def _split3(a):
    hi = a.astype(BF16)
    rest = a - hi.astype(F32)
    mid = rest.astype(BF16)
    return hi, mid, (rest - mid.astype(F32)).astype(BF16)


def _dot_sel(a, sel):
    hi, mid, lo = _split3(a)
    return _dot(hi, sel) + _dot(mid, sel) + _dot(lo, sel)


def _sel_dot(sel, a):
    hi, mid, lo = _split3(a)
    return _dot(sel, hi) + _dot(sel, mid) + _dot(sel, lo)


W_BLOCK = N_BRANCHES * D_MODEL
N_MAIN = OFF_IG + M_FOURIER + M_S5


def _inproj_kernel(x_ref, xp_ref, xn_ref, g_ref, wm_ref, bm_ref, wg_ref, bg_ref, wgt_ref, bgt_ref, cw_ref, cb_ref,
                   qk_ref, zv_ref, zo_ref, zf_ref, zs_ref, zg_ref, zgt_ref, ext_sc, *, tiles_per_seq):
    i = pl.program_id(0)
    tile = x_ref.shape[0]
    pad = CONV_WIDTH // 2
    h = _rms(x_ref[...], g_ref[...]).astype(BF16)
    h_halo = _rms(jnp.concatenate([xp_ref[...], xn_ref[...]], axis=0), g_ref[...]).astype(BF16)
    first = (i % tiles_per_seq) == 0
    last = (i % tiles_per_seq) == tiles_per_seq - 1

    def proj(hh, lo, hi):
        return _dot(hh, wm_ref[:, lo:hi]) + bm_ref[:, lo:hi]

    for n in range(2 * M_MLSTM // CONV_COLS):
        lo = n * CONV_COLS
        cols = slice(lo, lo + CONV_COLS)
        z_halo = proj(h_halo, OFF_Q + lo, OFF_Q + lo + CONV_COLS)
        ext_sc[0:CONV_HALO, cols] = jnp.where(first, 0.0, z_halo[:CONV_HALO])
        ext_sc[CONV_HALO:CONV_HALO + tile, cols] = proj(h, OFF_Q + lo, OFF_Q + lo + CONV_COLS)
        ext_sc[CONV_HALO + tile:, cols] = jnp.where(last, 0.0, z_halo[CONV_HALO:])
        ext = ext_sc[:, cols]
        acc = cb_ref[:, cols]
        for j in range(CONV_WIDTH):
            shifted = ext if j == pad else pltpu.roll(ext, (pad - j) % ext.shape[0], 0)
            acc = acc + cw_ref[j:j + 1, cols] * shifted[CONV_HALO:CONV_HALO + tile]
        half = 0.5 * acc
        qk = half + half * jnp.tanh(half)
        if lo < M_MLSTM:
            qk = qk * (HEAD_DIM ** -0.5)
        qk_ref[:, cols] = qk.astype(BF16)

    zv_ref[...] = proj(h, OFF_V, OFF_O).astype(BF16)
    zo_ref[...] = proj(h, OFF_O, OFF_IG).astype(BF16)
    zf_ref[...] = proj(h, OFF_IG, OFF_IG + M_FOURIER)
    zs_ref[...] = proj(h, OFF_IG + M_FOURIER, OFF_IG + M_FOURIER + M_S5)
    zg_ref[...] = _dot(h, wg_ref[...]) + bg_ref[...]
    zgt_ref[...] = lax.dot_general(wgt_ref[...], h, NT_DIMS, preferred_element_type=F32) + bgt_ref[...]


def _inproj(x2, params, l, tile, seq):
    t = x2.shape[0]
    n_gate = 4 * HEADS
    hpt = tile // CONV_HALO
    n_halo = t // CONV_HALO
    rows = lambda width: pl.BlockSpec((tile, width), lambda i: (i, 0))
    cols = lambda height: pl.BlockSpec((height, tile), lambda i: (0, i))
    return pl.pallas_call(
        functools.partial(_inproj_kernel, tiles_per_seq=seq // tile),
        grid=(t // tile,),
        in_specs=[rows(D_MODEL),
                  pl.BlockSpec((CONV_HALO, D_MODEL), lambda i: (jnp.maximum(i * hpt - 1, 0), 0)),
                  pl.BlockSpec((CONV_HALO, D_MODEL), lambda i: (jnp.minimum((i + 1) * hpt, n_halo - 1), 0))]
                 + [_layer_spec(a, l, (W_BLOCK, 1) if k in (1, 2) else None)
                    for k, a in enumerate(params)],
        out_specs=[rows(2 * M_MLSTM), rows(M_MLSTM), rows(M_MLSTM), rows(M_FOURIER), rows(M_S5),
                   rows(n_gate), cols(n_gate)],
        out_shape=[jax.ShapeDtypeStruct((t, 2 * M_MLSTM), BF16), jax.ShapeDtypeStruct((t, M_MLSTM), BF16),
                   jax.ShapeDtypeStruct((t, M_MLSTM), BF16), jax.ShapeDtypeStruct((t, M_FOURIER), F32),
                   jax.ShapeDtypeStruct((t, M_S5), F32), jax.ShapeDtypeStruct((t, n_gate), F32),
                   jax.ShapeDtypeStruct((n_gate, t), F32)],
        scratch_shapes=[pltpu.VMEM((tile + 2 * CONV_HALO, 2 * M_MLSTM), F32)],
        compiler_params=_params("parallel"),
        name="inproj",
    )(x2, x2, x2, *params)


def _log_sigmoid(x):
    return jnp.minimum(x, 0.0) - jnp.log1p(jnp.exp(-jnp.abs(x)))


def _mlstm_kernel(qk_f, v_f, g_f, gt_f, qk_b, v_b, g_b, gt_b, hf_ref, hb_ref, state_sc, m_sc):
    c = pl.program_id(1)
    L = CHUNK

    @pl.when(c == 0)
    def _():
        state_sc[...] = jnp.zeros_like(state_sc)
        m_sc[...] = jnp.zeros_like(m_sc)

    row = lax.broadcasted_iota(jnp.int32, (L, L), 0)
    col = lax.broadcasted_iota(jnp.int32, (L, L), 1)
    lower = row >= col
    upper = row <= col
    lower_f = jnp.where(lower, 1.0, 0.0).astype(BF16)
    upper_f = jnp.where(upper, 1.0, 0.0).astype(BF16)
    ones_v = jnp.ones((L, HEAD_DIM), F32)

    streams = ((0, qk_f, v_f, g_f, gt_f, hf_ref), (1, qk_b, v_b, g_b, gt_b, hb_ref))
    for sub, (d, qk_ref, v_ref, g_ref, gt_ref, out_ref) in itertools.product(range(MLSTM_SUB), streams):
        r0 = (sub if d == 0 else MLSTM_SUB - 1 - sub) * L
        rows = slice(r0, r0 + L)
        gates = g_ref[rows, :]
        gates_t = gt_ref[:, rows]
        lf_cols = _log_sigmoid(gates[:, 2 * HEADS:])
        lf_rows = _log_sigmoid(gates_t[2 * HEADS:, :])
        if d == 0:
            b_cols = _sel_dot(lower_f, lf_cols)
            b_rows = _dot_sel(lf_rows, upper_f)
            mask = lower
        else:
            b_cols = _sel_dot(upper_f, lf_cols)
            b_rows = _dot_sel(lf_rows, lower_f)
            mask = upper

        for hd in range(HEADS):
            k_idx = d * HEADS + hd
            lo = hd * HEAD_DIM
            q = qk_ref[rows, lo:lo + HEAD_DIM]
            k = qk_ref[rows, M_MLSTM + lo:M_MLSTM + lo + HEAD_DIM]
            v = v_ref[rows, lo:lo + HEAD_DIM]
            b_c = jnp.broadcast_to(b_cols[:, k_idx:k_idx + 1], (L, L))
            ig_c = jnp.broadcast_to(gates[:, k_idx:k_idx + 1], (L, L))
            b_r = b_rows[k_idx:k_idx + 1, :]
            ig_r = gates_t[k_idx:k_idx + 1, :]
            g_tot = b_c[L - 1:L, :] if d == 0 else b_c[0:1, :]
            m_prev = m_sc[k_idx:k_idx + 1, :]

            d_log = jnp.where(mask, b_c - b_r + ig_r, -1e30)
            inter_log = b_c + m_prev
            m_t = jnp.maximum(inter_log, jnp.max(d_log, axis=1, keepdims=True))
            scores = lax.dot_general(q, k, NT_DIMS, preferred_element_type=F32) * jnp.exp(d_log - m_t)
            inter_w = jnp.exp(inter_log - m_t)
            st = state_sc[k_idx]
            v_ext = jnp.concatenate([v, ones_v.astype(BF16)], axis=1)
            res = _dot(scores.astype(BF16), v_ext) + _dot((inter_w * q.astype(F32)).astype(BF16), st.astype(BF16))
            num = res[:, :HEAD_DIM]
            den = res[:, HEAD_DIM:]
            out_ref[rows, lo:lo + HEAD_DIM] = num / jnp.maximum(jnp.abs(den), jnp.exp(-m_t))

            a_c = g_tot - b_c + ig_c
            a_max = jnp.max(a_c, axis=0, keepdims=True)
            w_c = jnp.exp(a_c - a_max)
            vw = jnp.concatenate([v.astype(F32) * w_c, w_c], axis=1).astype(BF16)
            st_loc = lax.dot_general(k, vw, TN_DIMS, preferred_element_type=F32)
            m_new = jnp.maximum(g_tot + m_prev, a_max)
            s_old = jnp.exp(g_tot + m_prev - m_new)
            s_new = jnp.exp(a_max - m_new)
            s_old2 = jnp.concatenate([s_old, s_old], axis=1)
            s_new2 = jnp.concatenate([s_new, s_new], axis=1)
            state_sc[k_idx] = s_old2 * st + s_new2 * st_loc
            m_sc[k_idx:k_idx + 1, :] = m_new


def _mlstm(qk, zv, zg, zgt, bsz, seq):
    t = bsz * seq
    blk = CHUNK * MLSTM_SUB
    nc = seq // blk
    n_gate = 4 * HEADS

    def fwd(b, c):
        return b * nc + c

    def bwd(b, c):
        return b * nc + nc - 1 - c

    def specs(pos):
        return [pl.BlockSpec((blk, 2 * M_MLSTM), lambda b, c: (pos(b, c), 0)),
                pl.BlockSpec((blk, M_MLSTM), lambda b, c: (pos(b, c), 0)),
                pl.BlockSpec((blk, n_gate), lambda b, c: (pos(b, c), 0)),
                pl.BlockSpec((n_gate, blk), lambda b, c: (0, pos(b, c)))]

    out_shape = jax.ShapeDtypeStruct((t, M_MLSTM), F32)
    return pl.pallas_call(
        _mlstm_kernel,
        grid=(bsz, nc),
        in_specs=specs(fwd) + specs(bwd),
        out_specs=[pl.BlockSpec((blk, M_MLSTM), lambda b, c: (fwd(b, c), 0)),
                   pl.BlockSpec((blk, M_MLSTM), lambda b, c: (bwd(b, c), 0))],
        out_shape=[out_shape, out_shape],
        scratch_shapes=[pltpu.VMEM((2 * HEADS, HEAD_DIM, 2 * HEAD_DIM), F32),
                        pltpu.VMEM((2 * HEADS, HEAD_DIM), F32)],
        compiler_params=_params("arbitrary", "arbitrary"),
        name="mlstm",
    )(qk, zv, zg, zgt, qk, zv, zg, zgt)


def _dft_constants(seq):
    n1, n2 = seq // DFT_N2, DFT_N2
    k1 = np.arange(n1)[:, None, None]
    s2 = np.arange(n2)[None, None, :]
    s1 = np.arange(n1)[None, :, None]
    ang = -2.0 * np.pi * ((k1 * (n2 * s1 + s2)) % seq) / seq
    stage1 = np.concatenate([np.cos(ang), np.sin(ang)], axis=0)
    stage1 = np.ascontiguousarray(stage1.transpose(2, 0, 1))
    a2 = 2.0 * np.pi * np.outer(np.arange(n2), np.arange(n2)) / n2
    c2, sn2 = np.cos(a2), np.sin(a2)
    stage2 = np.block([[c2, sn2], [-sn2, c2]])
    ag = 2.0 * np.pi * np.outer(np.arange(FOURIER_GROUP_DIM), np.arange(FOURIER_GROUP_DIM)) / FOURIER_GROUP_DIM
    scale = 1.0 / math.sqrt(seq * FOURIER_GROUP_DIM)
    eye = np.eye(LANES // FOURIER_GROUP_DIM)
    group = np.stack([np.kron(eye, np.cos(ag)), np.kron(eye, np.sin(ag))]) * scale
    as_bf16 = lambda a: jnp.asarray(a, F32).astype(BF16)
    return as_bf16(stage1), as_bf16(stage2), as_bf16(group)


def _fourier_kernel(u_ref, w1_ref, w2_ref, wg_ref, y_ref, u_sc, b_sc, y_sc, *, n1):
    n2 = DFT_N2
    rows = n1 * DFT_STEP
    for sb in range(n2 // DFT_STEP):
        u_sc[sb % 2] = u_ref[:, sb * DFT_STEP:(sb + 1) * DFT_STEP, :].reshape(rows, LANES)
        for j in range(DFT_STEP):
            pick = pl.ds(j, n1, stride=DFT_STEP)
            res = _dot(w1_ref[sb * DFT_STEP + j], u_sc[sb % 2, pick, :].astype(BF16))
            b_sc[sb, 0, pick, :] = res[:n1]
            b_sc[sb, 1, pick, :] = res[n1:]
    for kb in range(n1 // DFT_STEP):
        def gather(part, j):
            r0 = (kb * DFT_STEP + j) * DFT_STEP
            return jnp.concatenate([b_sc[sb, part, r0:r0 + DFT_STEP, :] for sb in range(n2 // DFT_STEP)], axis=0)

        stacked = jnp.concatenate([jnp.concatenate([gather(0, j), gather(1, j)], axis=0) for j in range(DFT_STEP)],
                                  axis=1).astype(BF16)
        z = _dot(w2_ref[...], stacked).astype(BF16)
        zr = jnp.concatenate([z[:n2, j * LANES:(j + 1) * LANES] for j in range(DFT_STEP)], axis=0)
        zi = jnp.concatenate([z[n2:, j * LANES:(j + 1) * LANES] for j in range(DFT_STEP)], axis=0)
        y = _dot(zr, wg_ref[0]) + _dot(zi, wg_ref[1])
        for j in range(DFT_STEP):
            y_sc[kb % 2, pl.ds(j, n2, stride=DFT_STEP), :] = y[j * n2:(j + 1) * n2, :]
        y_ref[:, kb * DFT_STEP:(kb + 1) * DFT_STEP, :] = y_sc[kb % 2].reshape(n2, DFT_STEP, LANES)


def _fourier(zf, bsz, seq):
    n1, n2 = seq // DFT_N2, DFT_N2
    stage1, stage2, group = _dft_constants(seq)
    whole = lambda a: pl.BlockSpec(a.shape, lambda b, h: (0,) * a.ndim)
    y = pl.pallas_call(
        functools.partial(_fourier_kernel, n1=n1),
        grid=(bsz, M_FOURIER // LANES),
        in_specs=[pl.BlockSpec((None, n1, n2, LANES), lambda b, h: (b, 0, 0, h)),
                  whole(stage1), whole(stage2), whole(group)],
        out_specs=pl.BlockSpec((None, n2, n1, LANES), lambda b, h: (b, 0, 0, h)),
        out_shape=jax.ShapeDtypeStruct((bsz, n2, n1, M_FOURIER), F32),
        scratch_shapes=[pltpu.VMEM((2, n1 * DFT_STEP, LANES), F32),
                        pltpu.VMEM((n2 // DFT_STEP, 2, n1 * DFT_STEP, LANES), F32),
                        pltpu.VMEM((2, n2 * DFT_STEP, LANES), F32)],
        compiler_params=_params("parallel", "parallel"),
        name="fourier",
    )(zf.reshape(bsz, n1, n2, M_FOURIER), stage1, stage2, group)
    return y.reshape(bsz * seq, M_FOURIER)


def _s5_constants():
    L, G = S5_CHUNK, S5_GROUP
    lane = np.arange(L * G)
    expo = np.arange(LANES)[:, None]
    rep = (lane[None, :] % G == np.arange(G)[:, None])
    step = lane // G
    spread = np.stack([expo == (L - 1) - step, expo == step])
    return tuple(jnp.asarray(m, BF16) for m in (rep, spread))


def _s5_prep_kernel(lamc_re, lamc_im, lam4_re, lam4_im, ldt4_ref, b_re, b_im, c_re, c_im, c4_re, c4_im,
                    d_ref, rep_ref, spread_ref, toep_ref, bend_ref, cout_ref, laml_ref):
    L, P, G = S5_CHUNK, S5_STATE, S5_GROUP
    shift = G.bit_length() - 1
    nbits = L.bit_length()

    def cmul(ar, ai, br, bi):
        return ar * br - ai * bi, ar * bi + ai * br

    def lam_bar(lre, lim, dt):
        mag = jnp.exp(lre * dt)
        return mag * jnp.cos(lim * dt), mag * jnp.sin(lim * dt)

    def power_table(base_r, base_i, expo):
        tr = jnp.ones(expo.shape, F32)
        ti = jnp.zeros(expo.shape, F32)
        sr, si = base_r, base_i
        for bit in range(nbits):
            nr, ni = cmul(tr, ti, sr, si)
            has = ((expo >> bit) & 1) == 1
            tr, ti = jnp.where(has, nr, tr), jnp.where(has, ni, ti)
            sr, si = cmul(sr, si, sr, si)
        return tr, ti

    def spread_dot(xr, xi, sel):
        n = xr.shape[0]
        out = _dot(jnp.concatenate(_split3(xr)[:2] + _split3(xi)[:2], axis=0), sel)
        return out[:n] + out[n:2 * n], out[2 * n:3 * n] + out[3 * n:]

    def dot3(c, m):
        c_hi, c_mid, _ = _split3(c)
        m_hi, m_mid, _ = _split3(m)
        return _dot(c_hi, m_hi) + _dot(c_mid, m_hi) + _dot(c_hi, m_mid)

    rep = rep_ref[...]
    half = L * G
    lane_g = lax.broadcasted_iota(jnp.int32, (G, half), 1)
    row_g = lax.broadcasted_iota(jnp.int32, (G, half), 0)
    expo_tab = jnp.minimum(lax.broadcasted_iota(jnp.int32, (P, LANES), 1), L)

    lb4r, lb4i = lam_bar(lam4_re[0], lam4_im[0], jnp.exp(ldt4_ref[0]))

    def as_column(row, d):
        block = jnp.broadcast_to(row[:, 2 * d * P:2 * (d + 1) * P], (8, 2 * P))
        return block.T[:P, 0:1]

    taps = []
    for d in range(2):
        lr, li = lamc_re[d, 0], lamc_im[d, 0]
        lbr, lbi = as_column(lb4r, d), as_column(lb4i, d)
        den = lr * lr + li * li
        fr, fi = ((lbr - 1.0) * lr + lbi * li) / den, (lbi * lr - (lbr - 1.0) * li) / den
        bbr, bbi = cmul(fr, fi, b_re[d, 0], b_im[d, 0])
        btr, bti = spread_dot(bbr, bbi, rep)
        tab_r, tab_i = power_table(lbr, lbi, expo_tab)
        qr, qi = spread_dot(tab_r, tab_i, spread_ref[d])
        xr, xi = cmul(qr, qi, btr, bti)
        bend_ref[0, 2 * d] = xr.astype(BF16)
        bend_ref[0, 2 * d + 1] = xi.astype(BF16)
        taps.append(dot3(c_re[d, 0], xr) - dot3(c_im[d, 0], xi))

    d_tiled = _dot_sel(jnp.broadcast_to(d_ref[0], (G, G)), rep)
    skip = jnp.where(((lane_g & (G - 1)) == row_g) & ((lane_g >> shift) == L - 1), d_tiled, 0.0)
    pad = jnp.zeros((G, half), F32)
    gen = (jnp.concatenate([taps[0] + skip, pad], axis=1)
           + pltpu.roll(jnp.concatenate([taps[1], pad], axis=1), (L - 1) * G, 1))

    for t in range(L):
        a = (L - 1 - t) * G
        toep_ref[0, t * G:(t + 1) * G, :] = gen[:, a:a + L * G].astype(BF16)

    lane4 = lax.broadcasted_iota(jnp.int32, (L, 4 * P), 1)
    step4 = lax.broadcasted_iota(jnp.int32, (L, 4 * P), 0)
    pr, pi = power_table(lb4r, lb4i, jnp.where(lane4 < 2 * P, step4 + 1, L - step4))
    cr, ci = c4_re[0], c4_im[0]
    plane_bit = P.bit_length() - 1
    re_c = ((lax.broadcasted_iota(jnp.int32, (G, 4 * P), 1) >> plane_bit) & 1) == 0
    for t in range(L):
        re_part, im_part = cmul(cr, ci, pr[t:t + 1], pi[t:t + 1])
        cout_ref[0, t * G:(t + 1) * G, :] = jnp.where(re_c, re_part, -im_part).astype(BF16)
    re_1 = ((lax.broadcasted_iota(jnp.int32, (1, 4 * P), 1) >> plane_bit) & 1) == 0
    lr, li = power_table(lb4r, lb4i, jnp.full((1, 4 * P), L, jnp.int32))
    laml_ref[0] = jnp.where(re_1, lr, li)


def _s5_prep(lam_re, lam_im, log_dt, b_re, b_im, c_re, c_im, d_skip):
    ng, P, G, L = S5_GROUPS, S5_STATE, S5_GROUP, S5_CHUNK
    depth = lam_re.shape[0]
    tile4 = lambda a: jnp.concatenate([a[:, 0], a[:, 0], a[:, 1], a[:, 1]], axis=-1)
    ldt4 = jnp.repeat(tile4(log_dt[..., None]), P, axis=-1)[:, :, None, :]
    consts = _s5_constants()
    spec = lambda *tail: pl.BlockSpec((None, 2, 1) + tail, lambda l, g: (l, 0, g) + (0,) * len(tail))
    per_g = lambda *tail: pl.BlockSpec((None, 1) + tail, lambda l, g: (l, g) + (0,) * len(tail))
    whole = lambda a: pl.BlockSpec(a.shape, lambda l, g: (0,) * a.ndim)
    return pl.pallas_call(
        _s5_prep_kernel,
        grid=(depth, ng),
        in_specs=[spec(P, 1), spec(P, 1), per_g(1, 4 * P), per_g(1, 4 * P), per_g(1, 4 * P),
                  spec(P, G), spec(P, G), spec(G, P), spec(G, P), per_g(G, 4 * P), per_g(G, 4 * P), per_g(1, G)]
                 + [whole(m) for m in consts],
        out_specs=[per_g(L * G, L * G), per_g(4, P, L * G), per_g(L * G, 4 * P), per_g(1, 4 * P)],
        out_shape=[jax.ShapeDtypeStruct((depth, ng, L * G, L * G), BF16),
                   jax.ShapeDtypeStruct((depth, ng, 4, P, L * G), BF16),
                   jax.ShapeDtypeStruct((depth, ng, L * G, 4 * P), BF16),
                   jax.ShapeDtypeStruct((depth, ng, 1, 4 * P), F32)],
        compiler_params=_params("parallel", "parallel"),
        name="s5_prep",
    )(lam_re[..., None], lam_im[..., None],
      tile4(lam_re)[:, :, None, :], tile4(lam_im)[:, :, None, :], ldt4,
      b_re, b_im, c_re, c_im, tile4(c_re), tile4(c_im), d_skip[:, :, None, :], *consts)


def _s5_sums_kernel(lo_ref, hi_ref, bend_ref, ut_ref, et_ref, *, nchunk):
    L, P, G = S5_CHUNK, S5_STATE, S5_GROUP
    per_half = LANES // G
    for half, z_ref in enumerate((lo_ref, hi_ref)):
        for s in range(L):
            zt = z_ref[pl.ds(s, nchunk, stride=L), :].T.astype(BF16)
            for gl in range(per_half):
                ut_ref[half * per_half + gl, s * G:(s + 1) * G, :] = zt[gl * G:(gl + 1) * G, :]
    for g in range(S5_GROUPS):
        for plane in range(4):
            row = plane * S5_GROUPS * P + g * P
            et_ref[row:row + P, :] = _dot(bend_ref[g, plane], ut_ref[g])


def _s5_scan_kernel(et_ref, lam_ref, xt_ref, e_sc, x_sc, *, bsz, steps):
    width = S5_GROUPS * S5_STATE
    for plane in range(4):
        e_sc[plane] = et_ref[plane * width:(plane + 1) * width, :].T
    ar_f, ai_f, ar_b, ai_b = lam_ref[0], lam_ref[1], lam_ref[2], lam_ref[3]
    zero = jnp.zeros_like(ar_f)
    for b in range(bsz):
        def body(i, carry):
            xr, xi, yr, yi = carry
            rf = b * steps + i
            rb = b * steps + steps - 1 - i
            x_sc[0, pl.ds(rf, 1), :] = xr
            x_sc[1, pl.ds(rf, 1), :] = xi
            x_sc[2, pl.ds(rb, 1), :] = yr
            x_sc[3, pl.ds(rb, 1), :] = yi
            xr, xi = (ar_f * xr - ai_f * xi + e_sc[0, pl.ds(rf, 1), :],
                      ar_f * xi + ai_f * xr + e_sc[1, pl.ds(rf, 1), :])
            yr, yi = (ar_b * yr - ai_b * yi + e_sc[2, pl.ds(rb, 1), :],
                      ar_b * yi + ai_b * yr + e_sc[3, pl.ds(rb, 1), :])
            return xr, xi, yr, yi
        lax.fori_loop(0, steps, body, (zero, zero, zero, zero))
    for plane in range(4):
        xt_ref[plane * width:(plane + 1) * width, :] = x_sc[plane].T


def _gelu_tanh(x):
    return 0.5 * x * (1.0 + jnp.tanh(math.sqrt(2.0 / math.pi) * (x + 0.044715 * (x * x * x))))


def _s5_out_kernel(ut_ref, xt_ref, toep_ref, cout_ref, lo_ref, hi_ref, yt_sc, *, nchunk):
    L, P, G = S5_CHUNK, S5_STATE, S5_GROUP
    width = S5_GROUPS * P
    for g in range(S5_GROUPS):
        xg = jnp.concatenate([xt_ref[plane * width + g * P:plane * width + (g + 1) * P, :] for plane in range(4)],
                             axis=0).astype(BF16)
        yt = _gelu_tanh(_dot(toep_ref[g], ut_ref[g]) + _dot(cout_ref[g], xg))
        for t in range(L):
            yt_sc[t, g * G:(g + 1) * G, :] = yt[t * G:(t + 1) * G, :]
    for t in range(L):
        y = yt_sc[t].T
        lo_ref[pl.ds(t, nchunk, stride=L), :] = y[:, :LANES]
        hi_ref[pl.ds(t, nchunk, stride=L), :] = y[:, LANES:]


def _s5(zs, prep, l, bsz, seq):
    ng, P, G, L = S5_GROUPS, S5_STATE, S5_GROUP, S5_CHUNK
    toep, bend, cout, laml = prep
    depth = toep.shape[0]
    laml = laml.reshape(depth, ng, 4, P).transpose(0, 2, 1, 3).reshape(depth, 4, 1, ng * P)
    t = bsz * seq
    tile = min(S5_TILE, t)
    nct = tile // L
    nchunk = t // L
    ut_spec = pl.BlockSpec((ng, L * G, nct), lambda i: (0, 0, i))
    plane_spec = pl.BlockSpec((4 * ng * P, nct), lambda i: (0, i))
    plane_shape = jax.ShapeDtypeStruct((4 * ng * P, nchunk), F32)
    ut, et = pl.pallas_call(
        functools.partial(_s5_sums_kernel, nchunk=nct),
        grid=(t // tile,),
        in_specs=[pl.BlockSpec((tile, LANES), lambda i: (i, 0)), pl.BlockSpec((tile, LANES), lambda i: (i, 1)),
                  _layer_spec(bend, l)],
        out_specs=[ut_spec, plane_spec],
        out_shape=[jax.ShapeDtypeStruct((ng, L * G, nchunk), BF16), plane_shape],
        compiler_params=_params("parallel"),
        name="s5_chunk_sums",
    )(zs, zs, bend)
    whole_planes = pl.BlockSpec((4 * ng * P, nchunk), lambda i: (0, 0))
    xt = pl.pallas_call(
        functools.partial(_s5_scan_kernel, bsz=bsz, steps=seq // L),
        grid=(1,),
        in_specs=[whole_planes, _layer_spec(laml, l)],
        out_specs=whole_planes,
        out_shape=plane_shape,
        scratch_shapes=[pltpu.VMEM((4, nchunk, ng * P), F32), pltpu.VMEM((4, nchunk, ng * P), F32)],
        compiler_params=_params("arbitrary"),
        name="s5_state_scan",
    )(et, laml)
    half_spec = pl.BlockSpec((tile, LANES), lambda i: (i, 0))
    half_shape = jax.ShapeDtypeStruct((t, LANES), F32)
    return pl.pallas_call(
        functools.partial(_s5_out_kernel, nchunk=nct),
        grid=(t // tile,),
        in_specs=[ut_spec, plane_spec, _layer_spec(toep, l), _layer_spec(cout, l)],
        out_specs=[half_spec, half_spec],
        out_shape=[half_shape, half_shape],
        scratch_shapes=[pltpu.VMEM((L, M_S5, nct), F32)],
        compiler_params=_params("parallel"),
        name="s5_outputs",
    )(ut, xt, toep, cout)


def _merge_kernel(x_ref, hf_ref, hb_ref, zo_ref, yf_ref, ys_lo_ref, ys_hi_ref, gpre_ref, wgate_ref, bgate_ref,
                  ng_ref, wm_ref, wf_ref, wglu_ref, bglu_ref, wout_ref, gpost_ref, o_ref, mixed_sc):
    x = x_ref[...]
    h = _rms(x, gpre_ref[...]).astype(BF16)
    hs = hf_ref[...] + hb_ref[...]
    parts = []
    for hd in range(HEADS):
        blk = hs[:, hd * HEAD_DIM:(hd + 1) * HEAD_DIM]
        mu = jnp.mean(blk, axis=-1, keepdims=True)
        cen = blk - mu
        var = jnp.mean(cen * cen, axis=-1, keepdims=True)
        parts.append(cen * lax.rsqrt(var + EPS))
    hm = (jnp.concatenate(parts, axis=1) * ng_ref[...] * _sigmoid(zo_ref[...].astype(F32))).astype(BF16)
    yf = yf_ref[...].astype(BF16)
    ys = jnp.concatenate([ys_lo_ref[...].astype(BF16), ys_hi_ref[...].astype(BF16)], axis=1)
    for n in range(D_MODEL // MERGE_COLS):
        lo = n * MERGE_COLS
        cols = slice(lo, lo + MERGE_COLS)

        def gate(i):
            gcols = slice(i * D_MODEL + lo, i * D_MODEL + lo + MERGE_COLS)
            return _sigmoid(_dot(h, wgate_ref[:, gcols]) + bgate_ref[:, gcols])

        lin = _dot(ys, wglu_ref[:, cols]) + bglu_ref[:, cols]
        gcols = slice(D_MODEL + lo, D_MODEL + lo + MERGE_COLS)
        y_s = lin * _sigmoid(_dot(ys, wglu_ref[:, gcols]) + bglu_ref[:, gcols])
        mixed = gate(0) * _dot(hm, wm_ref[:, cols]) + gate(1) * _dot(yf, wf_ref[:, cols]) + gate(2) * y_s
        mixed_sc[:, cols] = mixed.astype(BF16)
    o_ref[...] = x + _rms(_dot(mixed_sc[...], wout_ref[...]), gpost_ref[...])


def _merge(x2, hf, hb, zo, yf, ys_lo, ys_hi, params, l, tile):
    t = x2.shape[0]
    rows = lambda width: pl.BlockSpec((tile, width), lambda i: (i, 0))
    return pl.pallas_call(
        _merge_kernel,
        grid=(t // tile,),
        in_specs=[rows(D_MODEL), rows(M_MLSTM), rows(M_MLSTM), rows(M_MLSTM), rows(M_FOURIER),
                  rows(LANES), rows(LANES)]
                 + [_layer_spec(a, l, (W_BLOCK, 0) if k in (1, 2) else None)
                    for k, a in enumerate(params)],
        out_specs=rows(D_MODEL),
        out_shape=jax.ShapeDtypeStruct((t, D_MODEL), F32),
        scratch_shapes=[pltpu.VMEM((tile, D_MODEL), BF16)],
        compiler_params=_params("parallel"),
        name="merge",
    )(x2, hf, hb, zo, yf, ys_lo, ys_hi, *params)


def _ffn_kernel(x_ref, gpre_ref, w1_ref, w2_ref, gpost_ref, o_ref, *, n_split):
    x = x_ref[...]
    h = _rms(x, gpre_ref[...]).astype(BF16)
    width = D_FF // n_split
    f = None
    for j in range(n_split):
        a = jnp.maximum(_dot(h, w1_ref[:, j * width:(j + 1) * width]), 0.0)
        part = _dot((a * a).astype(BF16), w2_ref[j * width:(j + 1) * width, :])
        f = part if f is None else f + part
    o_ref[...] = x + _rms(f, gpost_ref[...])


def _ffn(x2, params, l, tile):
    t = x2.shape[0]
    rows = pl.BlockSpec((tile, D_MODEL), lambda i: (i, 0))
    return pl.pallas_call(
        functools.partial(_ffn_kernel, n_split=4),
        grid=(t // tile,),
        in_specs=[rows] + [_layer_spec(a, l) for a in params],
        out_specs=rows,
        out_shape=jax.ShapeDtypeStruct((t, D_MODEL), F32),
        compiler_params=_params("parallel"),
        name="ffn",
    )(x2, *params)


def kernel(x, g_mix_pre, g_mix_post, g_ffn_pre, g_ffn_post, w_in, b_in, conv_w, conv_b, mlstm_norm_g, w_up_mlstm, w_up_fourier, s5_lam_re, s5_lam_im, s5_log_dt, s5_b_re, s5_b_im, s5_c_re, s5_c_im, s5_d, w_glu, b_glu, w_out, w_ffn1, w_ffn2):
    bsz, seq, _ = x.shape
    depth = w_in.shape[0]
    t = bsz * seq
    tile = min(1024, t)
    bf = lambda a: a.astype(BF16)
    row = lambda a: a[:, None, :]
    w_gates = w_in[:, :, OFF_IG:OFF_FOURIER]
    b_gates = b_in[:, OFF_IG:OFF_FOURIER]
    regroup = lambda a: jnp.concatenate(
        [a[..., OFF_GATE:], a[..., :OFF_IG], a[..., OFF_FOURIER:OFF_GATE],
         jnp.zeros(a.shape[:-1] + (W_BLOCK - N_MAIN,), a.dtype)], axis=-1)
    w_cols = regroup(bf(w_in))
    b_cols = row(regroup(b_in))
    inproj_params = (row(g_mix_pre), w_cols, b_cols,
                     bf(w_gates), row(b_gates), bf(jnp.swapaxes(w_gates, 1, 2)), b_gates[:, :, None],
                     conv_w, row(conv_b))
    merge_params = (row(g_mix_pre), w_cols, b_cols, row(mlstm_norm_g),
                    bf(w_up_mlstm), bf(w_up_fourier), bf(w_glu), row(b_glu), bf(w_out), row(g_mix_post))
    ffn_params = (row(g_ffn_pre), bf(w_ffn1), bf(w_ffn2), row(g_ffn_post))
    s5_prep = _s5_prep(s5_lam_re, s5_lam_im, s5_log_dt, s5_b_re, s5_b_im, s5_c_re, s5_c_im, s5_d)
    x2 = x.reshape(t, D_MODEL)
    for l in range(depth):
        qk, zv, zo, zf, zs, zg, zgt = _inproj(x2, inproj_params, l, tile, seq)
        hf, hb = _mlstm(qk, zv, zg, zgt, bsz, seq)
        yf = _fourier(zf, bsz, seq)
        ys_lo, ys_hi = _s5(zs, s5_prep, l, bsz, seq)
        x2 = _merge(x2, hf, hb, zo, yf, ys_lo, ys_hi, merge_params, l, tile)
        x2 = _ffn(x2, ffn_params, l, tile)
    return x2.reshape(bsz, seq, D_MODEL)
```

```python
import functools
import itertools
import math

import numpy as np
import jax
import jax.numpy as jnp
from jax import lax
from jax.experimental import pallas as pl
from jax.experimental.pallas import tpu as pltpu

F32 = jnp.float32
BF16 = jnp.bfloat16

LANES = 128
D_MODEL = 1024
M_MLSTM = 512
HEADS = 4
HEAD_DIM = 128
CHUNK = 128
MLSTM_SUB = 8
CONV_WIDTH = 5
CONV_COLS = 256
CONV_HALO = 8
M_FOURIER = 256
FOURIER_GROUP_DIM = 64
M_S5 = 256
S5_GROUP = 16
S5_GROUPS = 16
S5_STATE = 64
S5_CHUNK = 32
S5_TILE = 4096
N_BRANCHES = 3
MERGE_COLS = 256
D_FF = 4 * D_MODEL
EPS = 1e-6

OFF_Q = 0
OFF_V = 2 * M_MLSTM
OFF_O = 3 * M_MLSTM
OFF_IG = 4 * M_MLSTM
OFF_FOURIER = OFF_IG + 4 * HEADS
OFF_S5 = OFF_FOURIER + M_FOURIER
OFF_GATE = OFF_S5 + M_S5

DFT_N2 = 64
DFT_STEP = 8

VMEM_LIMIT = 56 * 1024 * 1024

NT_DIMS = (((1,), (1,)), ((), ()))
TN_DIMS = (((0,), (0,)), ((), ()))


def _params(*sem):
    return pltpu.CompilerParams(dimension_semantics=sem, vmem_limit_bytes=VMEM_LIMIT)


def _layer_spec(a, l, col_block=None):
    width, index = col_block if col_block else (a.shape[-1], 0)
    return pl.BlockSpec((None,) + a.shape[1:-1] + (width,), lambda *_: (l,) + (0,) * (a.ndim - 2) + (index,),
                        pipeline_mode=pl.Buffered(1))


def _rms(x, g):
    return x * lax.rsqrt(jnp.mean(x * x, axis=-1, keepdims=True) + EPS) * g


def _sigmoid(x):
    return 1.0 / (1.0 + jnp.exp(-x))


def _dot(a, b):
    return jnp.dot(a, b, preferred_element_type=F32)


def _split3(a):
    hi = a.astype(BF16)
    rest = a - hi.astype(F32)
    mid = rest.astype(BF16)
    return hi, mid, (rest - mid.astype(F32)).astype(BF16)


def _dot_sel(a, sel):
    hi, mid, lo = _split3(a)
    return _dot(hi, sel) + _dot(mid, sel) + _dot(lo, sel)


def _sel_dot(sel, a):
    hi, mid, lo = _split3(a)
    return _dot(sel, hi) + _dot(sel, mid) + _dot(sel, lo)


W_BLOCK = N_BRANCHES * D_MODEL
N_MAIN = OFF_IG + M_FOURIER + M_S5


def _inproj_kernel(x_ref, xp_ref, xn_ref, g_ref, wm_ref, bm_ref, wg_ref, bg_ref, cw_ref, cb_ref,
                   qk_ref, zv_ref, zo_ref, zf_ref, zs_ref, zg_ref, zgt_ref, *, tiles_per_seq):
    i = pl.program_id(0)
    tile = x_ref.shape[0]
    pad = CONV_WIDTH // 2
    body = slice(CONV_HALO, CONV_HALO + tile)
    h = _rms(jnp.concatenate([xp_ref[...], x_ref[...], xn_ref[...]], axis=0), g_ref[...]).astype(BF16)
    row = lax.broadcasted_iota(jnp.int32, (tile + 2 * CONV_HALO, 1), 0)
    first = (i % tiles_per_seq) == 0
    last = (i % tiles_per_seq) == tiles_per_seq - 1
    outside = (first & (row < CONV_HALO)) | (last & (row >= CONV_HALO + tile))

    def proj(lo, hi):
        return _dot(h, wm_ref[:, lo:hi]) + bm_ref[:, lo:hi]

    for n in range(2 * M_MLSTM // CONV_COLS):
        lo = n * CONV_COLS
        cols = slice(lo, lo + CONV_COLS)
        ext = jnp.where(outside, 0.0, proj(OFF_Q + lo, OFF_Q + lo + CONV_COLS))
        acc = cb_ref[:, cols]
        for j in range(CONV_WIDTH):
            shifted = ext if j == pad else pltpu.roll(ext, (pad - j) % ext.shape[0], 0)
            acc = acc + cw_ref[j:j + 1, cols] * shifted[body]
        half = 0.5 * acc
        qk = half + half * jnp.tanh(half)
        if lo < M_MLSTM:
            qk = qk * (HEAD_DIM ** -0.5)
        qk_ref[:, cols] = qk.astype(BF16)

    zv_ref[...] = proj(OFF_V, OFF_O)[body].astype(BF16)
    zo_ref[...] = proj(OFF_O, OFF_IG)[body].astype(BF16)
    zf_ref[...] = proj(OFF_IG, OFF_IG + M_FOURIER)[body]
    zs_ref[...] = proj(OFF_IG + M_FOURIER, OFF_IG + M_FOURIER + M_S5)[body]
    zg = (_dot(h, wg_ref[...]) + bg_ref[...])[body]
    zg_ref[...] = zg
    zgt_ref[...] = zg.T


def _inproj(x2, params, l, tile, seq):
    t = x2.shape[0]
    n_gate = 4 * HEADS
    hpt = tile // CONV_HALO
    n_halo = t // CONV_HALO
    rows = lambda width: pl.BlockSpec((tile, width), lambda i: (i, 0))
    cols = lambda height: pl.BlockSpec((height, tile), lambda i: (0, i))
    return pl.pallas_call(
        functools.partial(_inproj_kernel, tiles_per_seq=seq // tile),
        grid=(t // tile,),
        in_specs=[rows(D_MODEL),
                  pl.BlockSpec((CONV_HALO, D_MODEL), lambda i: (jnp.maximum(i * hpt - 1, 0), 0)),
                  pl.BlockSpec((CONV_HALO, D_MODEL), lambda i: (jnp.minimum((i + 1) * hpt, n_halo - 1), 0))]
                 + [_layer_spec(a, l, (W_BLOCK, 1) if k in (1, 2) else None)
                    for k, a in enumerate(params)],
        out_specs=[rows(2 * M_MLSTM), rows(M_MLSTM), rows(M_MLSTM), rows(M_FOURIER), rows(M_S5),
                   rows(n_gate), cols(n_gate)],
        out_shape=[jax.ShapeDtypeStruct((t, 2 * M_MLSTM), BF16), jax.ShapeDtypeStruct((t, M_MLSTM), BF16),
                   jax.ShapeDtypeStruct((t, M_MLSTM), BF16), jax.ShapeDtypeStruct((t, M_FOURIER), F32),
                   jax.ShapeDtypeStruct((t, M_S5), F32), jax.ShapeDtypeStruct((t, n_gate), F32),
                   jax.ShapeDtypeStruct((n_gate, t), F32)],
        compiler_params=_params("parallel"),
        name="inproj",
    )(x2, x2, x2, *params)


def _log_sigmoid(x):
    return jnp.minimum(x, 0.0) - jnp.log1p(jnp.exp(-jnp.abs(x)))


def _mlstm_kernel(qk_f, v_f, g_f, gt_f, qk_b, v_b, g_b, gt_b, hf_ref, hb_ref, state_sc, m_sc):
    c = pl.program_id(1)
    L = CHUNK

    @pl.when(c == 0)
    def _():
        state_sc[...] = jnp.zeros_like(state_sc)
        m_sc[...] = jnp.zeros_like(m_sc)

    row = lax.broadcasted_iota(jnp.int32, (L, L), 0)
    col = lax.broadcasted_iota(jnp.int32, (L, L), 1)
    lower = row >= col
    upper = row <= col
    lower_f = jnp.where(lower, 1.0, 0.0).astype(BF16)
    upper_f = jnp.where(upper, 1.0, 0.0).astype(BF16)
    ones_v = jnp.ones((L, HEAD_DIM), F32)

    streams = ((0, qk_f, v_f, g_f, gt_f, hf_ref), (1, qk_b, v_b, g_b, gt_b, hb_ref))
    for sub, (d, qk_ref, v_ref, g_ref, gt_ref, out_ref) in itertools.product(range(MLSTM_SUB), streams):
        r0 = (sub if d == 0 else MLSTM_SUB - 1 - sub) * L
        rows = slice(r0, r0 + L)
        gates = g_ref[rows, :]
        gates_t = gt_ref[:, rows]
        lf_cols = _log_sigmoid(gates[:, 2 * HEADS:])
        lf_rows = _log_sigmoid(gates_t[2 * HEADS:, :])
        if d == 0:
            b_cols = _sel_dot(lower_f, lf_cols)
            b_rows = _dot_sel(lf_rows, upper_f)
            mask = lower
        else:
            b_cols = _sel_dot(upper_f, lf_cols)
            b_rows = _dot_sel(lf_rows, lower_f)
            mask = upper

        for hd in range(HEADS):
            k_idx = d * HEADS + hd
            lo = hd * HEAD_DIM
            q = qk_ref[rows, lo:lo + HEAD_DIM]
            k = qk_ref[rows, M_MLSTM + lo:M_MLSTM + lo + HEAD_DIM]
            v = v_ref[rows, lo:lo + HEAD_DIM]
            b_c = jnp.broadcast_to(b_cols[:, k_idx:k_idx + 1], (L, L))
            ig_c = jnp.broadcast_to(gates[:, k_idx:k_idx + 1], (L, L))
            b_r = b_rows[k_idx:k_idx + 1, :]
            ig_r = gates_t[k_idx:k_idx + 1, :]
            g_tot = b_c[L - 1:L, :] if d == 0 else b_c[0:1, :]
            m_prev = m_sc[k_idx:k_idx + 1, :]

            d_log = jnp.where(mask, b_c - b_r + ig_r, -1e30)
            inter_log = b_c + m_prev
            m_t = jnp.maximum(inter_log, jnp.max(d_log, axis=1, keepdims=True))
            scores = lax.dot_general(q, k, NT_DIMS, preferred_element_type=F32) * jnp.exp(d_log - m_t)
            inter_w = jnp.exp(inter_log - m_t)
            st = state_sc[k_idx]
            v_ext = jnp.concatenate([v, ones_v.astype(BF16)], axis=1)
            res = _dot(scores.astype(BF16), v_ext) + _dot((inter_w * q.astype(F32)).astype(BF16), st.astype(BF16))
            num = res[:, :HEAD_DIM]
            den = res[:, HEAD_DIM:]
            out_ref[rows, lo:lo + HEAD_DIM] = num / jnp.maximum(jnp.abs(den), jnp.exp(-m_t))

            a_c = g_tot - b_c + ig_c
            a_max = jnp.max(a_c, axis=0, keepdims=True)
            w_c = jnp.exp(a_c - a_max)
            vw = jnp.concatenate([v.astype(F32) * w_c, w_c], axis=1).astype(BF16)
            st_loc = lax.dot_general(k, vw, TN_DIMS, preferred_element_type=F32)
            m_new = jnp.maximum(g_tot + m_prev, a_max)
            s_old = jnp.exp(g_tot + m_prev - m_new)
            s_new = jnp.exp(a_max - m_new)
            s_old2 = jnp.concatenate([s_old, s_old], axis=1)
            s_new2 = jnp.concatenate([s_new, s_new], axis=1)
            state_sc[k_idx] = s_old2 * st + s_new2 * st_loc
            m_sc[k_idx:k_idx + 1, :] = m_new


def _mlstm(qk, zv, zg, zgt, bsz, seq):
    t = bsz * seq
    blk = CHUNK * MLSTM_SUB
    nc = seq // blk
    n_gate = 4 * HEADS

    def fwd(b, c):
        return b * nc + c

    def bwd(b, c):
        return b * nc + nc - 1 - c

    def specs(pos):
        return [pl.BlockSpec((blk, 2 * M_MLSTM), lambda b, c: (pos(b, c), 0)),
                pl.BlockSpec((blk, M_MLSTM), lambda b, c: (pos(b, c), 0)),
                pl.BlockSpec((blk, n_gate), lambda b, c: (pos(b, c), 0)),
                pl.BlockSpec((n_gate, blk), lambda b, c: (0, pos(b, c)))]

    out_shape = jax.ShapeDtypeStruct((t, M_MLSTM), F32)
    return pl.pallas_call(
        _mlstm_kernel,
        grid=(bsz, nc),
        in_specs=specs(fwd) + specs(bwd),
        out_specs=[pl.BlockSpec((blk, M_MLSTM), lambda b, c: (fwd(b, c), 0)),
                   pl.BlockSpec((blk, M_MLSTM), lambda b, c: (bwd(b, c), 0))],
        out_shape=[out_shape, out_shape],
        scratch_shapes=[pltpu.VMEM((2 * HEADS, HEAD_DIM, 2 * HEAD_DIM), F32),
                        pltpu.VMEM((2 * HEADS, HEAD_DIM), F32)],
        compiler_params=_params("arbitrary", "arbitrary"),
        name="mlstm",
    )(qk, zv, zg, zgt, qk, zv, zg, zgt)


def _dft_constants(seq):
    n1, n2 = seq // DFT_N2, DFT_N2
    k1 = np.arange(n1)[:, None, None]
    s2 = np.arange(n2)[None, None, :]
    s1 = np.arange(n1)[None, :, None]
    ang = -2.0 * np.pi * ((k1 * (n2 * s1 + s2)) % seq) / seq
    stage1 = np.concatenate([np.cos(ang), np.sin(ang)], axis=0)
    stage1 = np.ascontiguousarray(stage1.transpose(2, 0, 1))
    a2 = 2.0 * np.pi * np.outer(np.arange(n2), np.arange(n2)) / n2
    c2, sn2 = np.cos(a2), np.sin(a2)
    stage2 = np.block([[c2, sn2], [-sn2, c2]])
    ag = 2.0 * np.pi * np.outer(np.arange(FOURIER_GROUP_DIM), np.arange(FOURIER_GROUP_DIM)) / FOURIER_GROUP_DIM
    scale = 1.0 / math.sqrt(seq * FOURIER_GROUP_DIM)
    eye = np.eye(LANES // FOURIER_GROUP_DIM)
    group = np.stack([np.kron(eye, np.cos(ag)), np.kron(eye, np.sin(ag))]) * scale
    as_bf16 = lambda a: jnp.asarray(a, F32).astype(BF16)
    return as_bf16(stage1), as_bf16(stage2), as_bf16(group)


def _fourier_kernel(u_ref, w1_ref, w2_ref, wg_ref, y_ref, u_sc, b_sc, y_sc, *, n1):
    n2 = DFT_N2
    rows = n1 * DFT_STEP
    for sb in range(n2 // DFT_STEP):
        u_sc[sb % 2] = u_ref[:, sb * DFT_STEP:(sb + 1) * DFT_STEP, :].reshape(rows, LANES)
        for j in range(DFT_STEP):
            pick = pl.ds(j, n1, stride=DFT_STEP)
            res = _dot(w1_ref[sb * DFT_STEP + j], u_sc[sb % 2, pick, :].astype(BF16))
            b_sc[sb, 0, pick, :] = res[:n1]
            b_sc[sb, 1, pick, :] = res[n1:]
    for kb in range(n1 // DFT_STEP):
        def gather(part, j):
            r0 = (kb * DFT_STEP + j) * DFT_STEP
            return jnp.concatenate([b_sc[sb, part, r0:r0 + DFT_STEP, :] for sb in range(n2 // DFT_STEP)], axis=0)

        stacked = jnp.concatenate([jnp.concatenate([gather(0, j), gather(1, j)], axis=0) for j in range(DFT_STEP)],
                                  axis=1).astype(BF16)
        z = _dot(w2_ref[...], stacked).astype(BF16)
        zr = jnp.concatenate([z[:n2, j * LANES:(j + 1) * LANES] for j in range(DFT_STEP)], axis=0)
        zi = jnp.concatenate([z[n2:, j * LANES:(j + 1) * LANES] for j in range(DFT_STEP)], axis=0)
        y = _dot(zr, wg_ref[0]) + _dot(zi, wg_ref[1])
        for j in range(DFT_STEP):
            y_sc[kb % 2, pl.ds(j, n2, stride=DFT_STEP), :] = y[j * n2:(j + 1) * n2, :]
        y_ref[:, kb * DFT_STEP:(kb + 1) * DFT_STEP, :] = y_sc[kb % 2].reshape(n2, DFT_STEP, LANES)


def _fourier(zf, bsz, seq):
    n1, n2 = seq // DFT_N2, DFT_N2
    stage1, stage2, group = _dft_constants(seq)
    whole = lambda a: pl.BlockSpec(a.shape, lambda b, h: (0,) * a.ndim)
    y = pl.pallas_call(
        functools.partial(_fourier_kernel, n1=n1),
        grid=(bsz, M_FOURIER // LANES),
        in_specs=[pl.BlockSpec((None, n1, n2, LANES), lambda b, h: (b, 0, 0, h)),
                  whole(stage1), whole(stage2), whole(group)],
        out_specs=pl.BlockSpec((None, n2, n1, LANES), lambda b, h: (b, 0, 0, h)),
        out_shape=jax.ShapeDtypeStruct((bsz, n2, n1, M_FOURIER), F32),
        scratch_shapes=[pltpu.VMEM((2, n1 * DFT_STEP, LANES), F32),
                        pltpu.VMEM((n2 // DFT_STEP, 2, n1 * DFT_STEP, LANES), F32),
                        pltpu.VMEM((2, n2 * DFT_STEP, LANES), F32)],
        compiler_params=_params("parallel", "parallel"),
        name="fourier",
    )(zf.reshape(bsz, n1, n2, M_FOURIER), stage1, stage2, group)
    return y.reshape(bsz * seq, M_FOURIER)


def _s5_constants():
    L, G = S5_CHUNK, S5_GROUP
    lane = np.arange(L * G)
    expo = np.arange(LANES)[:, None]
    rep = (lane[None, :] % G == np.arange(G)[:, None])
    step = lane // G
    spread = np.stack([expo == (L - 1) - step, expo == step])
    return tuple(jnp.asarray(m, BF16) for m in (rep, spread))


def _s5_prep_kernel(lamc_re, lamc_im, lam4_re, lam4_im, ldt4_ref, b_re, b_im, c_re, c_im, c4_re, c4_im,
                    d_ref, rep_ref, spread_ref, toep_ref, bend_ref, cout_ref, laml_ref):
    L, P, G = S5_CHUNK, S5_STATE, S5_GROUP
    shift = G.bit_length() - 1
    nbits = L.bit_length()

    def cmul(ar, ai, br, bi):
        return ar * br - ai * bi, ar * bi + ai * br

    def lam_bar(lre, lim, dt):
        mag = jnp.exp(lre * dt)
        return mag * jnp.cos(lim * dt), mag * jnp.sin(lim * dt)

    def power_table(base_r, base_i, expo):
        tr = jnp.ones(expo.shape, F32)
        ti = jnp.zeros(expo.shape, F32)
        sr, si = base_r, base_i
        for bit in range(nbits):
            nr, ni = cmul(tr, ti, sr, si)
            has = ((expo >> bit) & 1) == 1
            tr, ti = jnp.where(has, nr, tr), jnp.where(has, ni, ti)
            sr, si = cmul(sr, si, sr, si)
        return tr, ti

    def spread_dot(xr, xi, sel):
        n = xr.shape[0]
        out = _dot(jnp.concatenate(_split3(xr)[:2] + _split3(xi)[:2], axis=0), sel)
        return out[:n] + out[n:2 * n], out[2 * n:3 * n] + out[3 * n:]

    def dot3(c, m):
        c_hi, c_mid, _ = _split3(c)
        m_hi, m_mid, _ = _split3(m)
        return _dot(c_hi, m_hi) + _dot(c_mid, m_hi) + _dot(c_hi, m_mid)

    rep = rep_ref[...]
    half = L * G
    lane_g = lax.broadcasted_iota(jnp.int32, (G, half), 1)
    row_g = lax.broadcasted_iota(jnp.int32, (G, half), 0)
    expo_tab = jnp.minimum(lax.broadcasted_iota(jnp.int32, (P, LANES), 1), L)

    lb4r, lb4i = lam_bar(lam4_re[0], lam4_im[0], jnp.exp(ldt4_ref[0]))

    def as_column(row, d):
        block = jnp.broadcast_to(row[:, 2 * d * P:2 * (d + 1) * P], (8, 2 * P))
        return block.T[:P, 0:1]

    taps = []
    for d in range(2):
        lr, li = lamc_re[d, 0], lamc_im[d, 0]
        lbr, lbi = as_column(lb4r, d), as_column(lb4i, d)
        den = lr * lr + li * li
        fr, fi = ((lbr - 1.0) * lr + lbi * li) / den, (lbi * lr - (lbr - 1.0) * li) / den
        bbr, bbi = cmul(fr, fi, b_re[d, 0], b_im[d, 0])
        btr, bti = spread_dot(bbr, bbi, rep)
        tab_r, tab_i = power_table(lbr, lbi, expo_tab)
        qr, qi = spread_dot(tab_r, tab_i, spread_ref[d])
        xr, xi = cmul(qr, qi, btr, bti)
        bend_ref[0, 2 * d] = xr.astype(BF16)
        bend_ref[0, 2 * d + 1] = xi.astype(BF16)
        taps.append(dot3(c_re[d, 0], xr) - dot3(c_im[d, 0], xi))

    d_tiled = _dot_sel(jnp.broadcast_to(d_ref[0], (G, G)), rep)
    skip = jnp.where(((lane_g & (G - 1)) == row_g) & ((lane_g >> shift) == L - 1), d_tiled, 0.0)
    pad = jnp.zeros((G, half), F32)
    gen = (jnp.concatenate([taps[0] + skip, pad], axis=1)
           + pltpu.roll(jnp.concatenate([taps[1], pad], axis=1), (L - 1) * G, 1))

    for t in range(L):
        a = (L - 1 - t) * G
        toep_ref[0, t * G:(t + 1) * G, :] = gen[:, a:a + L * G].astype(BF16)

    lane4 = lax.broadcasted_iota(jnp.int32, (L, 4 * P), 1)
    step4 = lax.broadcasted_iota(jnp.int32, (L, 4 * P), 0)
    pr, pi = power_table(lb4r, lb4i, jnp.where(lane4 < 2 * P, step4 + 1, L - step4))
    cr, ci = c4_re[0], c4_im[0]
    plane_bit = P.bit_length() - 1
    re_c = ((lax.broadcasted_iota(jnp.int32, (G, 4 * P), 1) >> plane_bit) & 1) == 0
    for t in range(L):
        re_part, im_part = cmul(cr, ci, pr[t:t + 1], pi[t:t + 1])
        cout_ref[0, t * G:(t + 1) * G, :] = jnp.where(re_c, re_part, -im_part).astype(BF16)
    re_1 = ((lax.broadcasted_iota(jnp.int32, (1, 4 * P), 1) >> plane_bit) & 1) == 0
    lr, li = power_table(lb4r, lb4i, jnp.full((1, 4 * P), L, jnp.int32))
    laml_ref[0] = jnp.where(re_1, lr, li)


def _s5_prep(lam_re, lam_im, log_dt, b_re, b_im, c_re, c_im, d_skip):
    ng, P, G, L = S5_GROUPS, S5_STATE, S5_GROUP, S5_CHUNK
    depth = lam_re.shape[0]
    tile4 = lambda a: jnp.concatenate([a[:, 0], a[:, 0], a[:, 1], a[:, 1]], axis=-1)
    ldt4 = jnp.repeat(tile4(log_dt[..., None]), P, axis=-1)[:, :, None, :]
    consts = _s5_constants()
    spec = lambda *tail: pl.BlockSpec((None, 2, 1) + tail, lambda l, g: (l, 0, g) + (0,) * len(tail))
    per_g = lambda *tail: pl.BlockSpec((None, 1) + tail, lambda l, g: (l, g) + (0,) * len(tail))
    whole = lambda a: pl.BlockSpec(a.shape, lambda l, g: (0,) * a.ndim)
    return pl.pallas_call(
        _s5_prep_kernel,
        grid=(depth, ng),
        in_specs=[spec(P, 1), spec(P, 1), per_g(1, 4 * P), per_g(1, 4 * P), per_g(1, 4 * P),
                  spec(P, G), spec(P, G), spec(G, P), spec(G, P), per_g(G, 4 * P), per_g(G, 4 * P), per_g(1, G)]
                 + [whole(m) for m in consts],
        out_specs=[per_g(L * G, L * G), per_g(4, P, L * G), per_g(L * G, 4 * P), per_g(1, 4 * P)],
        out_shape=[jax.ShapeDtypeStruct((depth, ng, L * G, L * G), BF16),
                   jax.ShapeDtypeStruct((depth, ng, 4, P, L * G), BF16),
                   jax.ShapeDtypeStruct((depth, ng, L * G, 4 * P), BF16),
                   jax.ShapeDtypeStruct((depth, ng, 1, 4 * P), F32)],
        compiler_params=_params("parallel", "parallel"),
        name="s5_prep",
    )(lam_re[..., None], lam_im[..., None],
      tile4(lam_re)[:, :, None, :], tile4(lam_im)[:, :, None, :], ldt4,
      b_re, b_im, c_re, c_im, tile4(c_re), tile4(c_im), d_skip[:, :, None, :], *consts)


def _s5_sums_kernel(lo_ref, hi_ref, bend_ref, ut_ref, et_ref, *, nchunk):
    L, P, G = S5_CHUNK, S5_STATE, S5_GROUP
    per_half = LANES // G
    for half, z_ref in enumerate((lo_ref, hi_ref)):
        for s in range(L):
            zt = z_ref[pl.ds(s, nchunk, stride=L), :].T.astype(BF16)
            for gl in range(per_half):
                ut_ref[half * per_half + gl, s * G:(s + 1) * G, :] = zt[gl * G:(gl + 1) * G, :]
    for g in range(S5_GROUPS):
        for plane in range(4):
            row = plane * S5_GROUPS * P + g * P
            et_ref[row:row + P, :] = _dot(bend_ref[g, plane], ut_ref[g])


def _s5_scan_kernel(et_ref, lam_ref, xt_ref, e_sc, x_sc, *, bsz, steps):
    width = S5_GROUPS * S5_STATE
    for plane in range(4):
        e_sc[plane] = et_ref[plane * width:(plane + 1) * width, :].T
    ar_f, ai_f, ar_b, ai_b = lam_ref[0], lam_ref[1], lam_ref[2], lam_ref[3]
    zero = jnp.zeros_like(ar_f)
    for b in range(bsz):
        def body(i, carry):
            xr, xi, yr, yi = carry
            rf = b * steps + i
            rb = b * steps + steps - 1 - i
            x_sc[0, pl.ds(rf, 1), :] = xr
            x_sc[1, pl.ds(rf, 1), :] = xi
            x_sc[2, pl.ds(rb, 1), :] = yr
            x_sc[3, pl.ds(rb, 1), :] = yi
            xr, xi = (ar_f * xr - ai_f * xi + e_sc[0, pl.ds(rf, 1), :],
                      ar_f * xi + ai_f * xr + e_sc[1, pl.ds(rf, 1), :])
            yr, yi = (ar_b * yr - ai_b * yi + e_sc[2, pl.ds(rb, 1), :],
                      ar_b * yi + ai_b * yr + e_sc[3, pl.ds(rb, 1), :])
            return xr, xi, yr, yi
        lax.fori_loop(0, steps, body, (zero, zero, zero, zero))
    for plane in range(4):
        xt_ref[plane * width:(plane + 1) * width, :] = x_sc[plane].T


def _gelu_tanh(x):
    return 0.5 * x * (1.0 + jnp.tanh(math.sqrt(2.0 / math.pi) * (x + 0.044715 * (x * x * x))))


def _s5_out_kernel(ut_ref, xt_ref, toep_ref, cout_ref, lo_ref, hi_ref, yt_sc, *, nchunk):
    L, P, G = S5_CHUNK, S5_STATE, S5_GROUP
    width = S5_GROUPS * P
    for g in range(S5_GROUPS):
        xg = jnp.concatenate([xt_ref[plane * width + g * P:plane * width + (g + 1) * P, :] for plane in range(4)],
                             axis=0).astype(BF16)
        yt = _gelu_tanh(_dot(toep_ref[g], ut_ref[g]) + _dot(cout_ref[g], xg))
        for t in range(L):
            yt_sc[t, g * G:(g + 1) * G, :] = yt[t * G:(t + 1) * G, :]
    for t in range(L):
        y = yt_sc[t].T
        lo_ref[pl.ds(t, nchunk, stride=L), :] = y[:, :LANES]
        hi_ref[pl.ds(t, nchunk, stride=L), :] = y[:, LANES:]


def _s5(zs, prep, l, bsz, seq):
    ng, P, G, L = S5_GROUPS, S5_STATE, S5_GROUP, S5_CHUNK
    toep, bend, cout, laml = prep
    depth = toep.shape[0]
    laml = laml.reshape(depth, ng, 4, P).transpose(0, 2, 1, 3).reshape(depth, 4, 1, ng * P)
    t = bsz * seq
    tile = min(S5_TILE, t)
    nct = tile // L
    nchunk = t // L
    ut_spec = pl.BlockSpec((ng, L * G, nct), lambda i: (0, 0, i))
    plane_spec = pl.BlockSpec((4 * ng * P, nct), lambda i: (0, i))
    plane_shape = jax.ShapeDtypeStruct((4 * ng * P, nchunk), F32)
    ut, et = pl.pallas_call(
        functools.partial(_s5_sums_kernel, nchunk=nct),
        grid=(t // tile,),
        in_specs=[pl.BlockSpec((tile, LANES), lambda i: (i, 0)), pl.BlockSpec((tile, LANES), lambda i: (i, 1)),
                  _layer_spec(bend, l)],
        out_specs=[ut_spec, plane_spec],
        out_shape=[jax.ShapeDtypeStruct((ng, L * G, nchunk), BF16), plane_shape],
        compiler_params=_params("parallel"),
        name="s5_chunk_sums",
    )(zs, zs, bend)
    whole_planes = pl.BlockSpec((4 * ng * P, nchunk), lambda i: (0, 0))
    xt = pl.pallas_call(
        functools.partial(_s5_scan_kernel, bsz=bsz, steps=seq // L),
        grid=(1,),
        in_specs=[whole_planes, _layer_spec(laml, l)],
        out_specs=whole_planes,
        out_shape=plane_shape,
        scratch_shapes=[pltpu.VMEM((4, nchunk, ng * P), F32), pltpu.VMEM((4, nchunk, ng * P), F32)],
        compiler_params=_params("arbitrary"),
        name="s5_state_scan",
    )(et, laml)
    half_spec = pl.BlockSpec((tile, LANES), lambda i: (i, 0))
    half_shape = jax.ShapeDtypeStruct((t, LANES), F32)
    return pl.pallas_call(
        functools.partial(_s5_out_kernel, nchunk=nct),
        grid=(t // tile,),
        in_specs=[ut_spec, plane_spec, _layer_spec(toep, l), _layer_spec(cout, l)],
        out_specs=[half_spec, half_spec],
        out_shape=[half_shape, half_shape],
        scratch_shapes=[pltpu.VMEM((L, M_S5, nct), F32)],
        compiler_params=_params("parallel"),
        name="s5_outputs",
    )(ut, xt, toep, cout)


def _merge_kernel(x_ref, hf_ref, hb_ref, zo_ref, yf_ref, ys_lo_ref, ys_hi_ref, gpre_ref, wgate_ref, bgate_ref,
                  ng_ref, wm_ref, wf_ref, wglu_ref, bglu_ref, wout_ref, gpost_ref, o_ref, mixed_sc):
    x = x_ref[...]
    h = _rms(x, gpre_ref[...]).astype(BF16)
    hs = hf_ref[...] + hb_ref[...]
    parts = []
    for hd in range(HEADS):
        blk = hs[:, hd * HEAD_DIM:(hd + 1) * HEAD_DIM]
        mu = jnp.mean(blk, axis=-1, keepdims=True)
        cen = blk - mu
        var = jnp.mean(cen * cen, axis=-1, keepdims=True)
        parts.append(cen * lax.rsqrt(var + EPS))
    hm = (jnp.concatenate(parts, axis=1) * ng_ref[...] * _sigmoid(zo_ref[...].astype(F32))).astype(BF16)
    yf = yf_ref[...].astype(BF16)
    ys = jnp.concatenate([ys_lo_ref[...].astype(BF16), ys_hi_ref[...].astype(BF16)], axis=1)
    for n in range(D_MODEL // MERGE_COLS):
        lo = n * MERGE_COLS
        cols = slice(lo, lo + MERGE_COLS)

        def gate(i):
            gcols = slice(i * D_MODEL + lo, i * D_MODEL + lo + MERGE_COLS)
            return _sigmoid(_dot(h, wgate_ref[:, gcols]) + bgate_ref[:, gcols])

        lin = _dot(ys, wglu_ref[:, cols]) + bglu_ref[:, cols]
        gcols = slice(D_MODEL + lo, D_MODEL + lo + MERGE_COLS)
        y_s = lin * _sigmoid(_dot(ys, wglu_ref[:, gcols]) + bglu_ref[:, gcols])
        mixed = gate(0) * _dot(hm, wm_ref[:, cols]) + gate(1) * _dot(yf, wf_ref[:, cols]) + gate(2) * y_s
        mixed_sc[:, cols] = mixed.astype(BF16)
    o_ref[...] = x + _rms(_dot(mixed_sc[...], wout_ref[...]), gpost_ref[...])


def _merge(x2, hf, hb, zo, yf, ys_lo, ys_hi, params, l, tile):
    t = x2.shape[0]
    rows = lambda width: pl.BlockSpec((tile, width), lambda i: (i, 0))
    return pl.pallas_call(
        _merge_kernel,
        grid=(t // tile,),
        in_specs=[rows(D_MODEL), rows(M_MLSTM), rows(M_MLSTM), rows(M_MLSTM), rows(M_FOURIER),
                  rows(LANES), rows(LANES)]
                 + [_layer_spec(a, l, (W_BLOCK, 0) if k in (1, 2) else None)
                    for k, a in enumerate(params)],
        out_specs=rows(D_MODEL),
        out_shape=jax.ShapeDtypeStruct((t, D_MODEL), F32),
        scratch_shapes=[pltpu.VMEM((tile, D_MODEL), BF16)],
        compiler_params=_params("parallel"),
        name="merge",
    )(x2, hf, hb, zo, yf, ys_lo, ys_hi, *params)


def _ffn_kernel(x_ref, gpre_ref, w1_ref, w2_ref, gpost_ref, o_ref, *, n_split):
    x = x_ref[...]
    h = _rms(x, gpre_ref[...]).astype(BF16)
    width = D_FF // n_split
    f = None
    for j in range(n_split):
        a = jnp.maximum(_dot(h, w1_ref[:, j * width:(j + 1) * width]), 0.0)
        part = _dot((a * a).astype(BF16), w2_ref[j * width:(j + 1) * width, :])
        f = part if f is None else f + part
    o_ref[...] = x + _rms(f, gpost_ref[...])


def _ffn(x2, params, l, tile):
    t = x2.shape[0]
    rows = pl.BlockSpec((tile, D_MODEL), lambda i: (i, 0))
    return pl.pallas_call(
        functools.partial(_ffn_kernel, n_split=4),
        grid=(t // tile,),
        in_specs=[rows] + [_layer_spec(a, l) for a in params],
        out_specs=rows,
        out_shape=jax.ShapeDtypeStruct((t, D_MODEL), F32),
        compiler_params=_params("parallel"),
        name="ffn",
    )(x2, *params)


def kernel(x, g_mix_pre, g_mix_post, g_ffn_pre, g_ffn_post, w_in, b_in, conv_w, conv_b, mlstm_norm_g, w_up_mlstm, w_up_fourier, s5_lam_re, s5_lam_im, s5_log_dt, s5_b_re, s5_b_im, s5_c_re, s5_c_im, s5_d, w_glu, b_glu, w_out, w_ffn1, w_ffn2):
    bsz, seq, _ = x.shape
    depth = w_in.shape[0]
    t = bsz * seq
    tile = min(1024, t)
    bf = lambda a: a.astype(BF16)
    row = lambda a: a[:, None, :]
    w_gates = w_in[:, :, OFF_IG:OFF_FOURIER]
    b_gates = b_in[:, OFF_IG:OFF_FOURIER]
    regroup = lambda a: jnp.concatenate(
        [a[..., OFF_GATE:], a[..., :OFF_IG], a[..., OFF_FOURIER:OFF_GATE],
         jnp.zeros(a.shape[:-1] + (W_BLOCK - N_MAIN,), a.dtype)], axis=-1)
    w_cols = regroup(bf(w_in))
    b_cols = row(regroup(b_in))
    inproj_params = (row(g_mix_pre), w_cols, b_cols, bf(w_gates), row(b_gates), conv_w, row(conv_b))
    merge_params = (row(g_mix_pre), w_cols, b_cols, row(mlstm_norm_g),
                    bf(w_up_mlstm), bf(w_up_fourier), bf(w_glu), row(b_glu), bf(w_out), row(g_mix_post))
    ffn_params = (row(g_ffn_pre), bf(w_ffn1), bf(w_ffn2), row(g_ffn_post))
    s5_prep = _s5_prep(s5_lam_re, s5_lam_im, s5_log_dt, s5_b_re, s5_b_im, s5_c_re, s5_c_im, s5_d)
    x2 = x.reshape(t, D_MODEL)
    for l in range(depth):
        qk, zv, zo, zf, zs, zg, zgt = _inproj(x2, inproj_params, l, tile, seq)
        hf, hb = _mlstm(qk, zv, zg, zgt, bsz, seq)
        yf = _fourier(zf, bsz, seq)
        ys_lo, ys_hi = _s5(zs, s5_prep, l, bsz, seq)
        x2 = _merge(x2, hf, hb, zo, yf, ys_lo, ys_hi, merge_params, l, tile)
        x2 = _ffn(x2, ffn_params, l, tile)
    return x2.reshape(bsz, seq, D_MODEL)
```

```python
import functools
import itertools
import math

import numpy as np
import jax
import jax.numpy as jnp
from jax import lax
from jax.experimental import pallas as pl
from jax.experimental.pallas import tpu as pltpu

F32 = jnp.float32
BF16 = jnp.bfloat16

LANES = 128
D_MODEL = 1024
M_MLSTM = 512
HEADS = 4
HEAD_DIM = 128
CHUNK = 128
MLSTM_SUB = 8
CONV_WIDTH = 5
CONV_COLS = 256
CONV_HALO = 8
M_FOURIER = 256
FOURIER_GROUP_DIM = 64
M_S5 = 256
S5_GROUP = 16
S5_GROUPS = 16
S5_STATE = 64
S5_CHUNK = 32
S5_TILE = 4096
N_BRANCHES = 3
MERGE_COLS = 256
D_FF = 4 * D_MODEL
EPS = 1e-6

OFF_Q = 0
OFF_V = 2 * M_MLSTM
OFF_O = 3 * M_MLSTM
OFF_IG = 4 * M_MLSTM
OFF_FOURIER = OFF_IG + 4 * HEADS
OFF_S5 = OFF_FOURIER + M_FOURIER
OFF_GATE = OFF_S5 + M_S5

DFT_N2 = 64
DFT_STEP = 8

VMEM_LIMIT = 56 * 1024 * 1024

NT_DIMS = (((1,), (1,)), ((), ()))
TN_DIMS = (((0,), (0,)), ((), ()))


def _params(*sem):
    return pltpu.CompilerParams(dimension_semantics=sem, vmem_limit_bytes=VMEM_LIMIT)


def _layer_spec(a, l, col_block=None):
    width, index = col_block if col_block else (a.shape[-1], 0)
    return pl.BlockSpec((None,) + a.shape[1:-1] + (width,), lambda *_: (l,) + (0,) * (a.ndim - 2) + (index,),
                        pipeline_mode=pl.Buffered(1))


def _rms(x, g):
    return x * lax.rsqrt(jnp.mean(x * x, axis=-1, keepdims=True) + EPS) * g


def _sigmoid(x):
    return 1.0 / (1.0 + jnp.exp(-x))


def _log_sigmoid(x):
    return jnp.minimum(x, 0.0) - jnp.log1p(jnp.exp(-jnp.abs(x)))


def _dot(a, b):
    return jnp.dot(a, b, preferred_element_type=F32)


def _split3(a):
    hi = a.astype(BF16)
    rest = a - hi.astype(F32)
    mid = rest.astype(BF16)
    return hi, mid, (rest - mid.astype(F32)).astype(BF16)


def _dot_sel(a, sel):
    hi, mid, lo = _split3(a)
    return _dot(hi, sel) + _dot(mid, sel) + _dot(lo, sel)


W_BLOCK = N_BRANCHES * D_MODEL
N_MAIN = OFF_IG + M_FOURIER + M_S5


def _inproj_kernel(x_ref, xp_ref, xn_ref, g_ref, wm_ref, bm_ref, wg_ref, bg_ref, cw_ref, cb_ref,
                   qk_ref, zv_ref, zo_ref, zf_ref, zs_ref, zgt_ref, *, tiles_per_seq):
    i = pl.program_id(0)
    tile = x_ref.shape[0]
    pad = CONV_WIDTH // 2
    body = slice(CONV_HALO, CONV_HALO + tile)
    h = _rms(jnp.concatenate([xp_ref[...], x_ref[...], xn_ref[...]], axis=0), g_ref[...]).astype(BF16)
    row = lax.broadcasted_iota(jnp.int32, (tile + 2 * CONV_HALO, 1), 0)
    first = (i % tiles_per_seq) == 0
    last = (i % tiles_per_seq) == tiles_per_seq - 1
    outside = (first & (row < CONV_HALO)) | (last & (row >= CONV_HALO + tile))

    def proj(lo, hi):
        return _dot(h, wm_ref[:, lo:hi]) + bm_ref[:, lo:hi]

    for n in range(2 * M_MLSTM // CONV_COLS):
        lo = n * CONV_COLS
        cols = slice(lo, lo + CONV_COLS)
        ext = jnp.where(outside, 0.0, proj(OFF_Q + lo, OFF_Q + lo + CONV_COLS))
        acc = cb_ref[:, cols]
        for j in range(CONV_WIDTH):
            shifted = ext if j == pad else pltpu.roll(ext, (pad - j) % ext.shape[0], 0)
            acc = acc + cw_ref[j:j + 1, cols] * shifted[body]
        half = 0.5 * acc
        qk = half + half * jnp.tanh(half)
        if lo < M_MLSTM:
            qk = qk * (HEAD_DIM ** -0.5)
        qk_ref[:, cols] = qk.astype(BF16)

    zv_ref[...] = proj(OFF_V, OFF_O)[body].astype(BF16)
    zo_ref[...] = proj(OFF_O, OFF_IG)[body].astype(BF16)
    zf_ref[...] = proj(OFF_IG, OFF_IG + M_FOURIER)[body]
    zs_ref[...] = proj(OFF_IG + M_FOURIER, OFF_IG + M_FOURIER + M_S5)[body]
    zgt_ref[...] = (_dot(h, wg_ref[...]) + bg_ref[...])[body].T


def _inproj(x2, params, l, tile, seq):
    t = x2.shape[0]
    n_gate = 4 * HEADS
    hpt = tile // CONV_HALO
    n_halo = t // CONV_HALO
    rows = lambda width: pl.BlockSpec((tile, width), lambda i: (i, 0))
    cols = lambda height: pl.BlockSpec((height, tile), lambda i: (0, i))
    return pl.pallas_call(
        functools.partial(_inproj_kernel, tiles_per_seq=seq // tile),
        grid=(t // tile,),
        in_specs=[rows(D_MODEL),
                  pl.BlockSpec((CONV_HALO, D_MODEL), lambda i: (jnp.maximum(i * hpt - 1, 0), 0)),
                  pl.BlockSpec((CONV_HALO, D_MODEL), lambda i: (jnp.minimum((i + 1) * hpt, n_halo - 1), 0))]
                 + [_layer_spec(a, l, (W_BLOCK, 1) if k in (1, 2) else None)
                    for k, a in enumerate(params)],
        out_specs=[rows(2 * M_MLSTM), rows(M_MLSTM), rows(M_MLSTM), rows(M_FOURIER), rows(M_S5), cols(n_gate)],
        out_shape=[jax.ShapeDtypeStruct((t, 2 * M_MLSTM), BF16), jax.ShapeDtypeStruct((t, M_MLSTM), BF16),
                   jax.ShapeDtypeStruct((t, M_MLSTM), BF16), jax.ShapeDtypeStruct((t, M_FOURIER), F32),
                   jax.ShapeDtypeStruct((t, M_S5), F32), jax.ShapeDtypeStruct((n_gate, t), F32)],
        compiler_params=_params("parallel"),
        name="inproj",
    )(x2, x2, x2, *params)


def _mlstm_gates_kernel(zgt_ref, grow_ref, gcol_ref):
    ig_rows = zgt_ref[:2 * HEADS, :]
    lf_rows = _log_sigmoid(zgt_ref[2 * HEADS:, :])
    r = lax.broadcasted_iota(jnp.int32, (CHUNK, CHUNK), 0)
    c = lax.broadcasted_iota(jnp.int32, (CHUNK, CHUNK), 1)
    tri = jnp.concatenate([jnp.where(r <= c, 1.0, 0.0), jnp.where(r >= c, 1.0, 0.0)], axis=1).astype(BF16)
    fwd_row = lax.broadcasted_iota(jnp.int32, (2 * HEADS, CHUNK), 0) < HEADS
    b_parts, w_parts, amax_parts = [], [], []
    for s in range(zgt_ref.shape[1] // CHUNK):
        span = slice(s * CHUNK, (s + 1) * CHUNK)
        both = _dot_sel(lf_rows[:, span], tri)
        b = jnp.where(fwd_row, both[:, :CHUNK], both[:, CHUNK:])
        g_tot = jnp.where(fwd_row, b[:, CHUNK - 1:], b[:, :1])
        a = g_tot - b + ig_rows[:, span]
        a_max = jnp.max(a, axis=1, keepdims=True)
        b_parts.append(b)
        w_parts.append(jnp.exp(a - a_max))
        amax_parts.append(jnp.broadcast_to(a_max, a.shape))
    b_rows = jnp.concatenate(b_parts, axis=1)
    grow_ref[:2 * HEADS, :] = b_rows
    grow_ref[2 * HEADS:, :] = jnp.concatenate(amax_parts, axis=1)
    gcol_ref[:, :2 * HEADS] = b_rows.T
    gcol_ref[:, 2 * HEADS:] = jnp.concatenate(w_parts, axis=1).T


def _mlstm_gates(zgt, tile):
    n_gate, t = zgt.shape
    cols = pl.BlockSpec((n_gate, tile), lambda i: (0, i))
    return pl.pallas_call(
        _mlstm_gates_kernel,
        grid=(t // tile,),
        in_specs=[cols],
        out_specs=[cols, pl.BlockSpec((tile, n_gate), lambda i: (i, 0))],
        out_shape=[jax.ShapeDtypeStruct((n_gate, t), F32), jax.ShapeDtypeStruct((t, n_gate), F32)],
        compiler_params=_params("parallel"),
        name="mlstm_gates",
    )(zgt)


def _mlstm_kernel(qk_f, v_f, zgt_f, grow_f, gcol_f, qk_b, v_b, zgt_b, grow_b, gcol_b, hf_ref, hb_ref, state_sc, m_sc):
    c = pl.program_id(1)
    L = CHUNK

    @pl.when(c == 0)
    def _():
        state_sc[...] = jnp.zeros_like(state_sc)
        m_sc[...] = jnp.zeros_like(m_sc)

    row = lax.broadcasted_iota(jnp.int32, (L, L), 0)
    col = lax.broadcasted_iota(jnp.int32, (L, L), 1)
    lower = row >= col
    upper = row <= col
    ones_v = jnp.ones((L, HEAD_DIM), F32)

    streams = ((0, qk_f, v_f, zgt_f, grow_f, gcol_f, hf_ref), (1, qk_b, v_b, zgt_b, grow_b, gcol_b, hb_ref))
    for sub, (d, qk_ref, v_ref, zgt_ref, grow_ref, gcol_ref, out_ref) in itertools.product(range(MLSTM_SUB), streams):
        r0 = (sub if d == 0 else MLSTM_SUB - 1 - sub) * L
        rows = slice(r0, r0 + L)
        ig_rows = zgt_ref[:2 * HEADS, rows]
        b_rows = grow_ref[:2 * HEADS, rows]
        amax_rows = grow_ref[2 * HEADS:, rows]
        b_cols = gcol_ref[rows, :2 * HEADS]
        w_cols = gcol_ref[rows, 2 * HEADS:]
        mask = lower if d == 0 else upper

        for hd in range(HEADS):
            k_idx = d * HEADS + hd
            lo = hd * HEAD_DIM
            q = qk_ref[rows, lo:lo + HEAD_DIM]
            k = qk_ref[rows, M_MLSTM + lo:M_MLSTM + lo + HEAD_DIM]
            v = v_ref[rows, lo:lo + HEAD_DIM]
            b_c = jnp.broadcast_to(b_cols[:, k_idx:k_idx + 1], (L, L))
            w_c = jnp.broadcast_to(w_cols[:, k_idx:k_idx + 1], (L, L))
            b_r = b_rows[k_idx:k_idx + 1, :]
            ig_r = ig_rows[k_idx:k_idx + 1, :]
            g_tot = b_c[L - 1:L, :] if d == 0 else b_c[0:1, :]
            a_max = amax_rows[k_idx:k_idx + 1, :]
            m_prev = m_sc[k_idx:k_idx + 1, :]

            d_log = jnp.where(mask, b_c - b_r + ig_r, -1e30)
            inter_log = b_c + m_prev
            m_t = jnp.maximum(inter_log, jnp.max(d_log, axis=1, keepdims=True))
            scores = lax.dot_general(q, k, NT_DIMS, preferred_element_type=F32) * jnp.exp(d_log - m_t)
            inter_w = jnp.exp(inter_log - m_t)
            st = state_sc[k_idx]
            v_ext = jnp.concatenate([v, ones_v.astype(BF16)], axis=1)
            res = _dot(scores.astype(BF16), v_ext) + _dot((inter_w * q.astype(F32)).astype(BF16), st.astype(BF16))
            num = res[:, :HEAD_DIM]
            den = res[:, HEAD_DIM:]
            out_ref[rows, lo:lo + HEAD_DIM] = num / jnp.maximum(jnp.abs(den), jnp.exp(-m_t))

            vw = jnp.concatenate([v.astype(F32) * w_c, w_c], axis=1).astype(BF16)
            st_loc = lax.dot_general(k, vw, TN_DIMS, preferred_element_type=F32)
            m_new = jnp.maximum(g_tot + m_prev, a_max)
            s_old = jnp.exp(g_tot + m_prev - m_new)
            s_new = jnp.exp(a_max - m_new)
            s_old2 = jnp.concatenate([s_old, s_old], axis=1)
            s_new2 = jnp.concatenate([s_new, s_new], axis=1)
            state_sc[k_idx] = s_old2 * st + s_new2 * st_loc
            m_sc[k_idx:k_idx + 1, :] = m_new


def _mlstm(qk, zv, zgt, grow, gcol, bsz, seq):
    t = bsz * seq
    blk = CHUNK * MLSTM_SUB
    nc = seq // blk
    n_gate = 4 * HEADS

    def fwd(b, c):
        return b * nc + c

    def bwd(b, c):
        return b * nc + nc - 1 - c

    def specs(pos):
        return [pl.BlockSpec((blk, 2 * M_MLSTM), lambda b, c: (pos(b, c), 0)),
                pl.BlockSpec((blk, M_MLSTM), lambda b, c: (pos(b, c), 0)),
                pl.BlockSpec((n_gate, blk), lambda b, c: (0, pos(b, c))),
                pl.BlockSpec((n_gate, blk), lambda b, c: (0, pos(b, c))),
                pl.BlockSpec((blk, n_gate), lambda b, c: (pos(b, c), 0))]

    out_shape = jax.ShapeDtypeStruct((t, M_MLSTM), F32)
    return pl.pallas_call(
        _mlstm_kernel,
        grid=(bsz, nc),
        in_specs=specs(fwd) + specs(bwd),
        out_specs=[pl.BlockSpec((blk, M_MLSTM), lambda b, c: (fwd(b, c), 0)),
                   pl.BlockSpec((blk, M_MLSTM), lambda b, c: (bwd(b, c), 0))],
        out_shape=[out_shape, out_shape],
        scratch_shapes=[pltpu.VMEM((2 * HEADS, HEAD_DIM, 2 * HEAD_DIM), F32),
                        pltpu.VMEM((2 * HEADS, HEAD_DIM), F32)],
        compiler_params=_params("arbitrary", "arbitrary"),
        name="mlstm",
    )(qk, zv, zgt, grow, gcol, qk, zv, zgt, grow, gcol)


def _dft_constants(seq):
    n1, n2 = seq // DFT_N2, DFT_N2
    k1 = np.arange(n1)[:, None, None]
    s2 = np.arange(n2)[None, None, :]
    s1 = np.arange(n1)[None, :, None]
    ang = -2.0 * np.pi * ((k1 * (n2 * s1 + s2)) % seq) / seq
    stage1 = np.concatenate([np.cos(ang), np.sin(ang)], axis=0)
    stage1 = np.ascontiguousarray(stage1.transpose(2, 0, 1))
    a2 = 2.0 * np.pi * np.outer(np.arange(n2), np.arange(n2)) / n2
    c2, sn2 = np.cos(a2), np.sin(a2)
    stage2 = np.block([[c2, sn2], [-sn2, c2]])
    ag = 2.0 * np.pi * np.outer(np.arange(FOURIER_GROUP_DIM), np.arange(FOURIER_GROUP_DIM)) / FOURIER_GROUP_DIM
    scale = 1.0 / math.sqrt(seq * FOURIER_GROUP_DIM)
    eye = np.eye(LANES // FOURIER_GROUP_DIM)
    group = np.stack([np.kron(eye, np.cos(ag)), np.kron(eye, np.sin(ag))]) * scale
    as_bf16 = lambda a: jnp.asarray(a, F32).astype(BF16)
    return as_bf16(stage1), as_bf16(stage2), as_bf16(group)


def _fourier_kernel(u_ref, w1_ref, w2_ref, wg_ref, y_ref, u_sc, b_sc, y_sc, *, n1):
    n2 = DFT_N2
    rows = n1 * DFT_STEP
    for sb in range(n2 // DFT_STEP):
        u_sc[sb % 2] = u_ref[:, sb * DFT_STEP:(sb + 1) * DFT_STEP, :].reshape(rows, LANES)
        for j in range(DFT_STEP):
            pick = pl.ds(j, n1, stride=DFT_STEP)
            res = _dot(w1_ref[sb * DFT_STEP + j], u_sc[sb % 2, pick, :].astype(BF16))
            b_sc[sb, 0, pick, :] = res[:n1]
            b_sc[sb, 1, pick, :] = res[n1:]
    for kb in range(n1 // DFT_STEP):
        def gather(part, j):
            r0 = (kb * DFT_STEP + j) * DFT_STEP
            return jnp.concatenate([b_sc[sb, part, r0:r0 + DFT_STEP, :] for sb in range(n2 // DFT_STEP)], axis=0)

        stacked = jnp.concatenate([jnp.concatenate([gather(0, j), gather(1, j)], axis=0) for j in range(DFT_STEP)],
                                  axis=1).astype(BF16)
        z = _dot(w2_ref[...], stacked).astype(BF16)
        zr = jnp.concatenate([z[:n2, j * LANES:(j + 1) * LANES] for j in range(DFT_STEP)], axis=0)
        zi = jnp.concatenate([z[n2:, j * LANES:(j + 1) * LANES] for j in range(DFT_STEP)], axis=0)
        y = _dot(zr, wg_ref[0]) + _dot(zi, wg_ref[1])
        for j in range(DFT_STEP):
            y_sc[kb % 2, pl.ds(j, n2, stride=DFT_STEP), :] = y[j * n2:(j + 1) * n2, :]
        y_ref[:, kb * DFT_STEP:(kb + 1) * DFT_STEP, :] = y_sc[kb % 2].reshape(n2, DFT_STEP, LANES)


def _fourier(zf, bsz, seq):
    n1, n2 = seq // DFT_N2, DFT_N2
    stage1, stage2, group = _dft_constants(seq)
    whole = lambda a: pl.BlockSpec(a.shape, lambda b, h: (0,) * a.ndim)
    y = pl.pallas_call(
        functools.partial(_fourier_kernel, n1=n1),
        grid=(bsz, M_FOURIER // LANES),
        in_specs=[pl.BlockSpec((None, n1, n2, LANES), lambda b, h: (b, 0, 0, h)),
                  whole(stage1), whole(stage2), whole(group)],
        out_specs=pl.BlockSpec((None, n2, n1, LANES), lambda b, h: (b, 0, 0, h)),
        out_shape=jax.ShapeDtypeStruct((bsz, n2, n1, M_FOURIER), F32),
        scratch_shapes=[pltpu.VMEM((2, n1 * DFT_STEP, LANES), F32),
                        pltpu.VMEM((n2 // DFT_STEP, 2, n1 * DFT_STEP, LANES), F32),
                        pltpu.VMEM((2, n2 * DFT_STEP, LANES), F32)],
        compiler_params=_params("parallel", "parallel"),
        name="fourier",
    )(zf.reshape(bsz, n1, n2, M_FOURIER), stage1, stage2, group)
    return y.reshape(bsz * seq, M_FOURIER)


def _s5_constants():
    L, G = S5_CHUNK, S5_GROUP
    lane = np.arange(L * G)
    expo = np.arange(LANES)[:, None]
    rep = (lane[None, :] % G == np.arange(G)[:, None])
    step = lane // G
    spread = np.stack([expo == (L - 1) - step, expo == step])
    return tuple(jnp.asarray(m, BF16) for m in (rep, spread))


def _s5_prep_kernel(lamc_re, lamc_im, lam4_re, lam4_im, ldt4_ref, b_re, b_im, c_re, c_im, c4_re, c4_im,
                    d_ref, rep_ref, spread_ref, toep_ref, bend_ref, cout_ref, laml_ref):
    L, P, G = S5_CHUNK, S5_STATE, S5_GROUP
    shift = G.bit_length() - 1
    nbits = L.bit_length()

    def cmul(ar, ai, br, bi):
        return ar * br - ai * bi, ar * bi + ai * br

    def lam_bar(lre, lim, dt):
        mag = jnp.exp(lre * dt)
        return mag * jnp.cos(lim * dt), mag * jnp.sin(lim * dt)

    def power_table(base_r, base_i, expo):
        tr = jnp.ones(expo.shape, F32)
        ti = jnp.zeros(expo.shape, F32)
        sr, si = base_r, base_i
        for bit in range(nbits):
            nr, ni = cmul(tr, ti, sr, si)
            has = ((expo >> bit) & 1) == 1
            tr, ti = jnp.where(has, nr, tr), jnp.where(has, ni, ti)
            sr, si = cmul(sr, si, sr, si)
        return tr, ti

    def spread_dot(xr, xi, sel):
        n = xr.shape[0]
        out = _dot(jnp.concatenate(_split3(xr)[:2] + _split3(xi)[:2], axis=0), sel)
        return out[:n] + out[n:2 * n], out[2 * n:3 * n] + out[3 * n:]

    def dot3(c, m):
        c_hi, c_mid, _ = _split3(c)
        m_hi, m_mid, _ = _split3(m)
        return _dot(c_hi, m_hi) + _dot(c_mid, m_hi) + _dot(c_hi, m_mid)

    rep = rep_ref[...]
    half = L * G
    lane_g = lax.broadcasted_iota(jnp.int32, (G, half), 1)
    row_g = lax.broadcasted_iota(jnp.int32, (G, half), 0)
    expo_tab = jnp.minimum(lax.broadcasted_iota(jnp.int32, (P, LANES), 1), L)

    lb4r, lb4i = lam_bar(lam4_re[0], lam4_im[0], jnp.exp(ldt4_ref[0]))

    def as_column(row, d):
        block = jnp.broadcast_to(row[:, 2 * d * P:2 * (d + 1) * P], (8, 2 * P))
        return block.T[:P, 0:1]

    taps = []
    for d in range(2):
        lr, li = lamc_re[d, 0], lamc_im[d, 0]
        lbr, lbi = as_column(lb4r, d), as_column(lb4i, d)
        den = lr * lr + li * li
        fr, fi = ((lbr - 1.0) * lr + lbi * li) / den, (lbi * lr - (lbr - 1.0) * li) / den
        bbr, bbi = cmul(fr, fi, b_re[d, 0], b_im[d, 0])
        btr, bti = spread_dot(bbr, bbi, rep)
        tab_r, tab_i = power_table(lbr, lbi, expo_tab)
        qr, qi = spread_dot(tab_r, tab_i, spread_ref[d])
        xr, xi = cmul(qr, qi, btr, bti)
        bend_ref[0, 2 * d] = xr.astype(BF16)
        bend_ref[0, 2 * d + 1] = xi.astype(BF16)
        taps.append(dot3(c_re[d, 0], xr) - dot3(c_im[d, 0], xi))

    d_tiled = _dot_sel(jnp.broadcast_to(d_ref[0], (G, G)), rep)
    skip = jnp.where(((lane_g & (G - 1)) == row_g) & ((lane_g >> shift) == L - 1), d_tiled, 0.0)
    pad = jnp.zeros((G, half), F32)
    gen = (jnp.concatenate([taps[0] + skip, pad], axis=1)
           + pltpu.roll(jnp.concatenate([taps[1], pad], axis=1), (L - 1) * G, 1))

    for t in range(L):
        a = (L - 1 - t) * G
        toep_ref[0, t * G:(t + 1) * G, :] = gen[:, a:a + L * G].astype(BF16)

    lane4 = lax.broadcasted_iota(jnp.int32, (L, 4 * P), 1)
    step4 = lax.broadcasted_iota(jnp.int32, (L, 4 * P), 0)
    pr, pi = power_table(lb4r, lb4i, jnp.where(lane4 < 2 * P, step4 + 1, L - step4))
    cr, ci = c4_re[0], c4_im[0]
    plane_bit = P.bit_length() - 1
    re_c = ((lax.broadcasted_iota(jnp.int32, (G, 4 * P), 1) >> plane_bit) & 1) == 0
    for t in range(L):
        re_part, im_part = cmul(cr, ci, pr[t:t + 1], pi[t:t + 1])
        cout_ref[0, t * G:(t + 1) * G, :] = jnp.where(re_c, re_part, -im_part).astype(BF16)
    re_1 = ((lax.broadcasted_iota(jnp.int32, (1, 4 * P), 1) >> plane_bit) & 1) == 0
    lr, li = power_table(lb4r, lb4i, jnp.full((1, 4 * P), L, jnp.int32))
    laml_ref[0] = jnp.where(re_1, lr, li)


def _s5_prep(lam_re, lam_im, log_dt, b_re, b_im, c_re, c_im, d_skip):
    ng, P, G, L = S5_GROUPS, S5_STATE, S5_GROUP, S5_CHUNK
    depth = lam_re.shape[0]
    tile4 = lambda a: jnp.concatenate([a[:, 0], a[:, 0], a[:, 1], a[:, 1]], axis=-1)
    ldt4 = jnp.repeat(tile4(log_dt[..., None]), P, axis=-1)[:, :, None, :]
    consts = _s5_constants()
    spec = lambda *tail: pl.BlockSpec((None, 2, 1) + tail, lambda l, g: (l, 0, g) + (0,) * len(tail))
    per_g = lambda *tail: pl.BlockSpec((None, 1) + tail, lambda l, g: (l, g) + (0,) * len(tail))
    whole = lambda a: pl.BlockSpec(a.shape, lambda l, g: (0,) * a.ndim)
    return pl.pallas_call(
        _s5_prep_kernel,
        grid=(depth, ng),
        in_specs=[spec(P, 1), spec(P, 1), per_g(1, 4 * P), per_g(1, 4 * P), per_g(1, 4 * P),
                  spec(P, G), spec(P, G), spec(G, P), spec(G, P), per_g(G, 4 * P), per_g(G, 4 * P), per_g(1, G)]
                 + [whole(m) for m in consts],
        out_specs=[per_g(L * G, L * G), per_g(4, P, L * G), per_g(L * G, 4 * P), per_g(1, 4 * P)],
        out_shape=[jax.ShapeDtypeStruct((depth, ng, L * G, L * G), BF16),
                   jax.ShapeDtypeStruct((depth, ng, 4, P, L * G), BF16),
                   jax.ShapeDtypeStruct((depth, ng, L * G, 4 * P), BF16),
                   jax.ShapeDtypeStruct((depth, ng, 1, 4 * P), F32)],
        compiler_params=_params("parallel", "parallel"),
        name="s5_prep",
    )(lam_re[..., None], lam_im[..., None],
      tile4(lam_re)[:, :, None, :], tile4(lam_im)[:, :, None, :], ldt4,
      b_re, b_im, c_re, c_im, tile4(c_re), tile4(c_im), d_skip[:, :, None, :], *consts)


def _s5_sums_kernel(lo_ref, hi_ref, bend_ref, ut_ref, et_ref, *, nchunk):
    L, P, G = S5_CHUNK, S5_STATE, S5_GROUP
    per_half = LANES // G
    for half, z_ref in enumerate((lo_ref, hi_ref)):
        for s in range(L):
            zt = z_ref[pl.ds(s, nchunk, stride=L), :].T.astype(BF16)
            for gl in range(per_half):
                ut_ref[half * per_half + gl, s * G:(s + 1) * G, :] = zt[gl * G:(gl + 1) * G, :]
    for g in range(S5_GROUPS):
        for plane in range(4):
            row = plane * S5_GROUPS * P + g * P
            et_ref[row:row + P, :] = _dot(bend_ref[g, plane], ut_ref[g])


def _s5_scan_kernel(et_ref, lam_ref, xt_ref, e_sc, x_sc, *, bsz, steps):
    width = S5_GROUPS * S5_STATE
    for plane in range(4):
        e_sc[plane] = et_ref[plane * width:(plane + 1) * width, :].T
    ar_f, ai_f, ar_b, ai_b = lam_ref[0], lam_ref[1], lam_ref[2], lam_ref[3]
    zero = jnp.zeros_like(ar_f)
    for b in range(bsz):
        def body(i, carry):
            xr, xi, yr, yi = carry
            rf = b * steps + i
            rb = b * steps + steps - 1 - i
            x_sc[0, pl.ds(rf, 1), :] = xr
            x_sc[1, pl.ds(rf, 1), :] = xi
            x_sc[2, pl.ds(rb, 1), :] = yr
            x_sc[3, pl.ds(rb, 1), :] = yi
            xr, xi = (ar_f * xr - ai_f * xi + e_sc[0, pl.ds(rf, 1), :],
                      ar_f * xi + ai_f * xr + e_sc[1, pl.ds(rf, 1), :])
            yr, yi = (ar_b * yr - ai_b * yi + e_sc[2, pl.ds(rb, 1), :],
                      ar_b * yi + ai_b * yr + e_sc[3, pl.ds(rb, 1), :])
            return xr, xi, yr, yi
        lax.fori_loop(0, steps, body, (zero, zero, zero, zero))
    for plane in range(4):
        xt_ref[plane * width:(plane + 1) * width, :] = x_sc[plane].T


def _gelu_tanh(x):
    return 0.5 * x * (1.0 + jnp.tanh(math.sqrt(2.0 / math.pi) * (x + 0.044715 * (x * x * x))))


def _s5_out_kernel(ut_ref, xt_ref, toep_ref, cout_ref, lo_ref, hi_ref, yt_sc, *, nchunk):
    L, P, G = S5_CHUNK, S5_STATE, S5_GROUP
    width = S5_GROUPS * P
    for g in range(S5_GROUPS):
        xg = jnp.concatenate([xt_ref[plane * width + g * P:plane * width + (g + 1) * P, :] for plane in range(4)],
                             axis=0).astype(BF16)
        yt = _gelu_tanh(_dot(toep_ref[g], ut_ref[g]) + _dot(cout_ref[g], xg))
        for t in range(L):
            yt_sc[t, g * G:(g + 1) * G, :] = yt[t * G:(t + 1) * G, :]
    for t in range(L):
        y = yt_sc[t].T
        lo_ref[pl.ds(t, nchunk, stride=L), :] = y[:, :LANES]
        hi_ref[pl.ds(t, nchunk, stride=L), :] = y[:, LANES:]


def _s5(zs, prep, l, bsz, seq):
    ng, P, G, L = S5_GROUPS, S5_STATE, S5_GROUP, S5_CHUNK
    toep, bend, cout, laml = prep
    depth = toep.shape[0]
    laml = laml.reshape(depth, ng, 4, P).transpose(0, 2, 1, 3).reshape(depth, 4, 1, ng * P)
    t = bsz * seq
    tile = min(S5_TILE, t)
    nct = tile // L
    nchunk = t // L
    ut_spec = pl.BlockSpec((ng, L * G, nct), lambda i: (0, 0, i))
    plane_spec = pl.BlockSpec((4 * ng * P, nct), lambda i: (0, i))
    plane_shape = jax.ShapeDtypeStruct((4 * ng * P, nchunk), F32)
    ut, et = pl.pallas_call(
        functools.partial(_s5_sums_kernel, nchunk=nct),
        grid=(t // tile,),
        in_specs=[pl.BlockSpec((tile, LANES), lambda i: (i, 0)), pl.BlockSpec((tile, LANES), lambda i: (i, 1)),
                  _layer_spec(bend, l)],
        out_specs=[ut_spec, plane_spec],
        out_shape=[jax.ShapeDtypeStruct((ng, L * G, nchunk), BF16), plane_shape],
        compiler_params=_params("parallel"),
        name="s5_chunk_sums",
    )(zs, zs, bend)
    whole_planes = pl.BlockSpec((4 * ng * P, nchunk), lambda i: (0, 0))
    xt = pl.pallas_call(
        functools.partial(_s5_scan_kernel, bsz=bsz, steps=seq // L),
        grid=(1,),
        in_specs=[whole_planes, _layer_spec(laml, l)],
        out_specs=whole_planes,
        out_shape=plane_shape,
        scratch_shapes=[pltpu.VMEM((4, nchunk, ng * P), F32), pltpu.VMEM((4, nchunk, ng * P), F32)],
        compiler_params=_params("arbitrary"),
        name="s5_state_scan",
    )(et, laml)
    half_spec = pl.BlockSpec((tile, LANES), lambda i: (i, 0))
    half_shape = jax.ShapeDtypeStruct((t, LANES), F32)
    return pl.pallas_call(
        functools.partial(_s5_out_kernel, nchunk=nct),
        grid=(t // tile,),
        in_specs=[ut_spec, plane_spec, _layer_spec(toep, l), _layer_spec(cout, l)],
        out_specs=[half_spec, half_spec],
        out_shape=[half_shape, half_shape],
        scratch_shapes=[pltpu.VMEM((L, M_S5, nct), F32)],
        compiler_params=_params("parallel"),
        name="s5_outputs",
    )(ut, xt, toep, cout)


def _merge_kernel(x_ref, hf_ref, hb_ref, zo_ref, yf_ref, ys_lo_ref, ys_hi_ref, gpre_ref, wgate_ref, bgate_ref,
                  ng_ref, wm_ref, wf_ref, wglu_ref, bglu_ref, wout_ref, gpost_ref, o_ref, mixed_sc):
    x = x_ref[...]
    h = _rms(x, gpre_ref[...]).astype(BF16)
    hs = hf_ref[...] + hb_ref[...]
    parts = []
    for hd in range(HEADS):
        blk = hs[:, hd * HEAD_DIM:(hd + 1) * HEAD_DIM]
        mu = jnp.mean(blk, axis=-1, keepdims=True)
        cen = blk - mu
        var = jnp.mean(cen * cen, axis=-1, keepdims=True)
        parts.append(cen * lax.rsqrt(var + EPS))
    hm = (jnp.concatenate(parts, axis=1) * ng_ref[...] * _sigmoid(zo_ref[...].astype(F32))).astype(BF16)
    yf = yf_ref[...].astype(BF16)
    ys = jnp.concatenate([ys_lo_ref[...].astype(BF16), ys_hi_ref[...].astype(BF16)], axis=1)
    for n in range(D_MODEL // MERGE_COLS):
        lo = n * MERGE_COLS
        cols = slice(lo, lo + MERGE_COLS)

        def gate(i):
            gcols = slice(i * D_MODEL + lo, i * D_MODEL + lo + MERGE_COLS)
            return _sigmoid(_dot(h, wgate_ref[:, gcols]) + bgate_ref[:, gcols])

        lin = _dot(ys, wglu_ref[:, cols]) + bglu_ref[:, cols]
        gcols = slice(D_MODEL + lo, D_MODEL + lo + MERGE_COLS)
        y_s = lin * _sigmoid(_dot(ys, wglu_ref[:, gcols]) + bglu_ref[:, gcols])
        mixed = gate(0) * _dot(hm, wm_ref[:, cols]) + gate(1) * _dot(yf, wf_ref[:, cols]) + gate(2) * y_s
        mixed_sc[:, cols] = mixed.astype(BF16)
    o_ref[...] = x + _rms(_dot(mixed_sc[...], wout_ref[...]), gpost_ref[...])


def _merge(x2, hf, hb, zo, yf, ys_lo, ys_hi, params, l, tile):
    t = x2.shape[0]
    rows = lambda width: pl.BlockSpec((tile, width), lambda i: (i, 0))
    return pl.pallas_call(
        _merge_kernel,
        grid=(t // tile,),
        in_specs=[rows(D_MODEL), rows(M_MLSTM), rows(M_MLSTM), rows(M_MLSTM), rows(M_FOURIER),
                  rows(LANES), rows(LANES)]
                 + [_layer_spec(a, l, (W_BLOCK, 0) if k in (1, 2) else None)
                    for k, a in enumerate(params)],
        out_specs=rows(D_MODEL),
        out_shape=jax.ShapeDtypeStruct((t, D_MODEL), F32),
        scratch_shapes=[pltpu.VMEM((tile, D_MODEL), BF16)],
        compiler_params=_params("parallel"),
        name="merge",
    )(x2, hf, hb, zo, yf, ys_lo, ys_hi, *params)


def _ffn_kernel(x_ref, gpre_ref, w1_ref, w2_ref, gpost_ref, o_ref, *, n_split):
    x = x_ref[...]
    h = _rms(x, gpre_ref[...]).astype(BF16)
    width = D_FF // n_split
    f = None
    for j in range(n_split):
        a = jnp.maximum(_dot(h, w1_ref[:, j * width:(j + 1) * width]), 0.0)
        part = _dot((a * a).astype(BF16), w2_ref[j * width:(j + 1) * width, :])
        f = part if f is None else f + part
    o_ref[...] = x + _rms(f, gpost_ref[...])


def _ffn(x2, params, l, tile):
    t = x2.shape[0]
    rows = pl.BlockSpec((tile, D_MODEL), lambda i: (i, 0))
    return pl.pallas_call(
        functools.partial(_ffn_kernel, n_split=4),
        grid=(t // tile,),
        in_specs=[rows] + [_layer_spec(a, l) for a in params],
        out_specs=rows,
        out_shape=jax.ShapeDtypeStruct((t, D_MODEL), F32),
        compiler_params=_params("parallel"),
        name="ffn",
    )(x2, *params)


def kernel(x, g_mix_pre, g_mix_post, g_ffn_pre, g_ffn_post, w_in, b_in, conv_w, conv_b, mlstm_norm_g, w_up_mlstm, w_up_fourier, s5_lam_re, s5_lam_im, s5_log_dt, s5_b_re, s5_b_im, s5_c_re, s5_c_im, s5_d, w_glu, b_glu, w_out, w_ffn1, w_ffn2):
    bsz, seq, _ = x.shape
    depth = w_in.shape[0]
    t = bsz * seq
    tile = min(1024, t)
    bf = lambda a: a.astype(BF16)
    row = lambda a: a[:, None, :]
    w_gates = w_in[:, :, OFF_IG:OFF_FOURIER]
    b_gates = b_in[:, OFF_IG:OFF_FOURIER]
    regroup = lambda a: jnp.concatenate(
        [a[..., OFF_GATE:], a[..., :OFF_IG], a[..., OFF_FOURIER:OFF_GATE],
         jnp.zeros(a.shape[:-1] + (W_BLOCK - N_MAIN,), a.dtype)], axis=-1)
    w_cols = regroup(bf(w_in))
    b_cols = row(regroup(b_in))
    inproj_params = (row(g_mix_pre), w_cols, b_cols, bf(w_gates), row(b_gates), conv_w, row(conv_b))
    merge_params = (row(g_mix_pre), w_cols, b_cols, row(mlstm_norm_g),
                    bf(w_up_mlstm), bf(w_up_fourier), bf(w_glu), row(b_glu), bf(w_out), row(g_mix_post))
    ffn_params = (row(g_ffn_pre), bf(w_ffn1), bf(w_ffn2), row(g_ffn_post))
    s5_prep = _s5_prep(s5_lam_re, s5_lam_im, s5_log_dt, s5_b_re, s5_b_im, s5_c_re, s5_c_im, s5_d)
    x2 = x.reshape(t, D_MODEL)
    for l in range(depth):
        qk, zv, zo, zf, zs, zgt = _inproj(x2, inproj_params, l, tile, seq)
        grow, gcol = _mlstm_gates(zgt, tile)
        hf, hb = _mlstm(qk, zv, zgt, grow, gcol, bsz, seq)
        yf = _fourier(zf, bsz, seq)
        ys_lo, ys_hi = _s5(zs, s5_prep, l, bsz, seq)
        x2 = _merge(x2, hf, hb, zo, yf, ys_lo, ys_hi, merge_params, l, tile)
        x2 = _ffn(x2, ffn_params, l, tile)
    return x2.reshape(bsz, seq, D_MODEL)
```

```python
import functools
import itertools
import math

import numpy as np
import jax
import jax.numpy as jnp
from jax import lax
from jax.experimental import pallas as pl
from jax.experimental.pallas import tpu as pltpu

F32 = jnp.float32
BF16 = jnp.bfloat16

LANES = 128
D_MODEL = 1024
M_MLSTM = 512
HEADS = 4
HEAD_DIM = 128
CHUNK = 128
MLSTM_SUB = 8
CONV_WIDTH = 5
CONV_COLS = 256
CONV_HALO = 8
M_FOURIER = 256
FOURIER_GROUP_DIM = 64
M_S5 = 256
S5_GROUP = 16
S5_GROUPS = 16
S5_STATE = 64
S5_CHUNK = 32
S5_TILE = 4096
N_BRANCHES = 3
MERGE_COLS = 256
D_FF = 4 * D_MODEL
EPS = 1e-6

OFF_Q = 0
OFF_V = 2 * M_MLSTM
OFF_O = 3 * M_MLSTM
OFF_IG = 4 * M_MLSTM
OFF_FOURIER = OFF_IG + 4 * HEADS
OFF_S5 = OFF_FOURIER + M_FOURIER
OFF_GATE = OFF_S5 + M_S5

DFT_N2 = 64
DFT_STEP = 8

VMEM_LIMIT = 56 * 1024 * 1024


def _params(*sem):
    return pltpu.CompilerParams(dimension_semantics=sem, vmem_limit_bytes=VMEM_LIMIT)


def _layer_spec(a, l, col_block=None):
    width, index = col_block if col_block else (a.shape[-1], 0)
    return pl.BlockSpec((None,) + a.shape[1:-1] + (width,), lambda *_: (l,) + (0,) * (a.ndim - 2) + (index,),
                        pipeline_mode=pl.Buffered(1))


def _rms(x, g):
    return x * lax.rsqrt(jnp.mean(x * x, axis=-1, keepdims=True) + EPS) * g


def _sigmoid(x):
    return 1.0 / (1.0 + jnp.exp(-x))


def _log_sigmoid(x):
    return jnp.minimum(x, 0.0) - jnp.log1p(jnp.exp(-jnp.abs(x)))


def _dot(a, b):
    return jnp.dot(a, b, preferred_element_type=F32)


def _split3(a):
    hi = a.astype(BF16)
    rest = a - hi.astype(F32)
    mid = rest.astype(BF16)
    return hi, mid, (rest - mid.astype(F32)).astype(BF16)


def _dot_sel(a, sel):
    hi, mid, lo = _split3(a)
    return _dot(hi, sel) + _dot(mid, sel) + _dot(lo, sel)


W_BLOCK = N_BRANCHES * D_MODEL
N_MAIN = OFF_IG + M_FOURIER + M_S5


def _inproj_kernel(x_ref, xp_ref, xn_ref, g_ref, wm_ref, bm_ref, wg_ref, bg_ref, cw_ref, cb_ref,
                   q_ref, kt_ref, zv_ref, zo_ref, zf_ref, zs_ref, zgt_ref, h_ref, *, tiles_per_seq):
    i = pl.program_id(0)
    tile = x_ref.shape[0]
    pad = CONV_WIDTH // 2
    body = slice(CONV_HALO, CONV_HALO + tile)
    h = _rms(jnp.concatenate([xp_ref[...], x_ref[...], xn_ref[...]], axis=0), g_ref[...]).astype(BF16)
    h_ref[...] = h[body]
    row = lax.broadcasted_iota(jnp.int32, (tile + 2 * CONV_HALO, 1), 0)
    first = (i % tiles_per_seq) == 0
    last = (i % tiles_per_seq) == tiles_per_seq - 1
    outside = (first & (row < CONV_HALO)) | (last & (row >= CONV_HALO + tile))

    def proj(lo, hi):
        return _dot(h, wm_ref[:, lo:hi]) + bm_ref[:, lo:hi]

    for n in range(2 * M_MLSTM // CONV_COLS):
        lo = n * CONV_COLS
        cols = slice(lo, lo + CONV_COLS)
        ext = jnp.where(outside, 0.0, proj(OFF_Q + lo, OFF_Q + lo + CONV_COLS))
        acc = cb_ref[:, cols]
        for j in range(CONV_WIDTH):
            shifted = ext if j == pad else pltpu.roll(ext, (pad - j) % ext.shape[0], 0)
            acc = acc + cw_ref[j:j + 1, cols] * shifted[body]
        half = 0.5 * acc
        qk = half + half * jnp.tanh(half)
        if lo < M_MLSTM:
            q_ref[:, cols] = (qk * (HEAD_DIM ** -0.5)).astype(BF16)
        else:
            kt_ref[lo - M_MLSTM:lo - M_MLSTM + CONV_COLS, :] = qk.T.astype(BF16)

    zv_ref[...] = proj(OFF_V, OFF_O)[body].astype(BF16)
    zo_ref[...] = proj(OFF_O, OFF_IG)[body].astype(BF16)
    zf_ref[...] = proj(OFF_IG, OFF_IG + M_FOURIER)[body]
    zs_ref[...] = proj(OFF_IG + M_FOURIER, OFF_IG + M_FOURIER + M_S5)[body]
    zgt_ref[...] = (_dot(h, wg_ref[...]) + bg_ref[...])[body].T


def _inproj(x2, params, l, tile, seq):
    t = x2.shape[0]
    n_gate = 4 * HEADS
    hpt = tile // CONV_HALO
    n_halo = t // CONV_HALO
    rows = lambda width: pl.BlockSpec((tile, width), lambda i: (i, 0))
    cols = lambda height: pl.BlockSpec((height, tile), lambda i: (0, i))
    return pl.pallas_call(
        functools.partial(_inproj_kernel, tiles_per_seq=seq // tile),
        grid=(t // tile,),
        in_specs=[rows(D_MODEL),
                  pl.BlockSpec((CONV_HALO, D_MODEL), lambda i: (jnp.maximum(i * hpt - 1, 0), 0)),
                  pl.BlockSpec((CONV_HALO, D_MODEL), lambda i: (jnp.minimum((i + 1) * hpt, n_halo - 1), 0))]
                 + [_layer_spec(a, l, (W_BLOCK, 1) if k in (1, 2) else None)
                    for k, a in enumerate(params)],
        out_specs=[rows(M_MLSTM), cols(M_MLSTM), rows(M_MLSTM), rows(M_MLSTM), rows(M_FOURIER), rows(M_S5),
                   cols(n_gate), rows(D_MODEL)],
        out_shape=[jax.ShapeDtypeStruct((t, M_MLSTM), BF16), jax.ShapeDtypeStruct((M_MLSTM, t), BF16),
                   jax.ShapeDtypeStruct((t, M_MLSTM), BF16),
                   jax.ShapeDtypeStruct((t, M_MLSTM), BF16), jax.ShapeDtypeStruct((t, M_FOURIER), F32),
                   jax.ShapeDtypeStruct((t, M_S5), F32), jax.ShapeDtypeStruct((n_gate, t), F32),
                   jax.ShapeDtypeStruct((t, D_MODEL), BF16)],
        compiler_params=_params("parallel"),
        name="inproj",
    )(x2, x2, x2, *params)


def _mlstm_gates_kernel(zgt_ref, grow_ref, gcol_ref):
    ig_rows = zgt_ref[:2 * HEADS, :]
    lf_rows = _log_sigmoid(zgt_ref[2 * HEADS:, :])
    r = lax.broadcasted_iota(jnp.int32, (CHUNK, CHUNK), 0)
    c = lax.broadcasted_iota(jnp.int32, (CHUNK, CHUNK), 1)
    tri = jnp.concatenate([jnp.where(r <= c, 1.0, 0.0), jnp.where(r >= c, 1.0, 0.0)], axis=1).astype(BF16)
    fwd_row = lax.broadcasted_iota(jnp.int32, (2 * HEADS, CHUNK), 0) < HEADS
    b_parts, w_parts, amax_parts = [], [], []
    for s in range(zgt_ref.shape[1] // CHUNK):
        span = slice(s * CHUNK, (s + 1) * CHUNK)
        both = _dot_sel(lf_rows[:, span], tri)
        b = jnp.where(fwd_row, both[:, :CHUNK], both[:, CHUNK:])
        g_tot = jnp.where(fwd_row, b[:, CHUNK - 1:], b[:, :1])
        a = g_tot - b + ig_rows[:, span]
        a_max = jnp.max(a, axis=1, keepdims=True)
        b_parts.append(b)
        w_parts.append(jnp.exp(a - a_max))
        amax_parts.append(jnp.broadcast_to(a_max, a.shape))
    b_rows = jnp.concatenate(b_parts, axis=1)
    grow_ref[:2 * HEADS, :] = b_rows
    grow_ref[2 * HEADS:, :] = jnp.concatenate(amax_parts, axis=1)
    gcol_ref[:, :2 * HEADS] = b_rows.T
    gcol_ref[:, 2 * HEADS:] = jnp.concatenate(w_parts, axis=1).T


def _mlstm_gates(zgt, tile):
    n_gate, t = zgt.shape
    cols = pl.BlockSpec((n_gate, tile), lambda i: (0, i))
    return pl.pallas_call(
        _mlstm_gates_kernel,
        grid=(t // tile,),
        in_specs=[cols],
        out_specs=[cols, pl.BlockSpec((tile, n_gate), lambda i: (i, 0))],
        out_shape=[jax.ShapeDtypeStruct((n_gate, t), F32), jax.ShapeDtypeStruct((t, n_gate), F32)],
        compiler_params=_params("parallel"),
        name="mlstm_gates",
    )(zgt)


def _mlstm_kernel(q_f, kt_f, v_f, zgt_f, grow_f, gcol_f, q_b, kt_b, v_b, zgt_b, grow_b, gcol_b, hf_ref, hb_ref,
                  state_sc, m_sc):
    c = pl.program_id(1)
    L = CHUNK

    @pl.when(c == 0)
    def _():
        state_sc[...] = jnp.zeros_like(state_sc)
        m_sc[...] = jnp.zeros_like(m_sc)

    row = lax.broadcasted_iota(jnp.int32, (L, L), 0)
    col = lax.broadcasted_iota(jnp.int32, (L, L), 1)
    lower = row >= col
    upper = row <= col
    ones_v = jnp.ones((L, HEAD_DIM), F32)

    streams = ((0, q_f, kt_f, v_f, zgt_f, grow_f, gcol_f, hf_ref), (1, q_b, kt_b, v_b, zgt_b, grow_b, gcol_b, hb_ref))
    for sub, (d, q_ref, kt_ref, v_ref, zgt_ref, grow_ref, gcol_ref, out_ref) in itertools.product(range(MLSTM_SUB),
                                                                                                   streams):
        r0 = (sub if d == 0 else MLSTM_SUB - 1 - sub) * L
        rows = slice(r0, r0 + L)
        ig_rows = zgt_ref[:2 * HEADS, rows]
        b_rows = grow_ref[:2 * HEADS, rows]
        amax_rows = grow_ref[2 * HEADS:, rows]
        b_cols = gcol_ref[rows, :2 * HEADS]
        w_cols = gcol_ref[rows, 2 * HEADS:]
        mask = lower if d == 0 else upper

        for hd in range(HEADS):
            k_idx = d * HEADS + hd
            lo = hd * HEAD_DIM
            q = q_ref[rows, lo:lo + HEAD_DIM]
            kt = kt_ref[lo:lo + HEAD_DIM, rows]
            v = v_ref[rows, lo:lo + HEAD_DIM]
            b_c = jnp.broadcast_to(b_cols[:, k_idx:k_idx + 1], (L, L))
            w_c = jnp.broadcast_to(w_cols[:, k_idx:k_idx + 1], (L, L))
            b_r = b_rows[k_idx:k_idx + 1, :]
            ig_r = ig_rows[k_idx:k_idx + 1, :]
            g_tot = b_c[L - 1:L, :] if d == 0 else b_c[0:1, :]
            a_max = amax_rows[k_idx:k_idx + 1, :]
            m_prev = m_sc[k_idx:k_idx + 1, :]

            d_log = jnp.where(mask, b_c - b_r + ig_r, -1e30)
            inter_log = b_c + m_prev
            m_t = jnp.maximum(inter_log, jnp.max(d_log, axis=1, keepdims=True))
            scores = _dot(q, kt) * jnp.exp(d_log - m_t)
            inter_w = jnp.exp(inter_log - m_t)
            st = state_sc[k_idx]
            v_ext = jnp.concatenate([v, ones_v.astype(BF16)], axis=1)
            res = _dot(scores.astype(BF16), v_ext) + _dot((inter_w * q.astype(F32)).astype(BF16), st.astype(BF16))
            num = res[:, :HEAD_DIM]
            den = res[:, HEAD_DIM:]
            out_ref[rows, lo:lo + HEAD_DIM] = num / jnp.maximum(jnp.abs(den), jnp.exp(-m_t))

            vw = jnp.concatenate([v.astype(F32) * w_c, w_c], axis=1).astype(BF16)
            st_loc = _dot(kt, vw)
            m_new = jnp.maximum(g_tot + m_prev, a_max)
            s_old = jnp.exp(g_tot + m_prev - m_new)
            s_new = jnp.exp(a_max - m_new)
            s_old2 = jnp.concatenate([s_old, s_old], axis=1)
            s_new2 = jnp.concatenate([s_new, s_new], axis=1)
            state_sc[k_idx] = s_old2 * st + s_new2 * st_loc
            m_sc[k_idx:k_idx + 1, :] = m_new


def _mlstm(q, kt, zv, zgt, grow, gcol, bsz, seq):
    t = bsz * seq
    blk = CHUNK * MLSTM_SUB
    nc = seq // blk
    n_gate = 4 * HEADS

    def fwd(b, c):
        return b * nc + c

    def bwd(b, c):
        return b * nc + nc - 1 - c

    def specs(pos):
        return [pl.BlockSpec((blk, M_MLSTM), lambda b, c: (pos(b, c), 0)),
                pl.BlockSpec((M_MLSTM, blk), lambda b, c: (0, pos(b, c))),
                pl.BlockSpec((blk, M_MLSTM), lambda b, c: (pos(b, c), 0)),
                pl.BlockSpec((n_gate, blk), lambda b, c: (0, pos(b, c))),
                pl.BlockSpec((n_gate, blk), lambda b, c: (0, pos(b, c))),
                pl.BlockSpec((blk, n_gate), lambda b, c: (pos(b, c), 0))]

    out_shape = jax.ShapeDtypeStruct((t, M_MLSTM), F32)
    return pl.pallas_call(
        _mlstm_kernel,
        grid=(bsz, nc),
        in_specs=specs(fwd) + specs(bwd),
        out_specs=[pl.BlockSpec((blk, M_MLSTM), lambda b, c: (fwd(b, c), 0)),
                   pl.BlockSpec((blk, M_MLSTM), lambda b, c: (bwd(b, c), 0))],
        out_shape=[out_shape, out_shape],
        scratch_shapes=[pltpu.VMEM((2 * HEADS, HEAD_DIM, 2 * HEAD_DIM), F32),
                        pltpu.VMEM((2 * HEADS, HEAD_DIM), F32)],
        compiler_params=_params("arbitrary", "arbitrary"),
        name="mlstm",
    )(q, kt, zv, zgt, grow, gcol, q, kt, zv, zgt, grow, gcol)


def _dft_constants(seq):
    n1, n2 = seq // DFT_N2, DFT_N2
    k1 = np.arange(n1)[:, None, None]
    s2 = np.arange(n2)[None, None, :]
    s1 = np.arange(n1)[None, :, None]
    ang = -2.0 * np.pi * ((k1 * (n2 * s1 + s2)) % seq) / seq
    stage1 = np.concatenate([np.cos(ang), np.sin(ang)], axis=0)
    stage1 = np.ascontiguousarray(stage1.transpose(2, 0, 1))
    a2 = 2.0 * np.pi * np.outer(np.arange(n2), np.arange(n2)) / n2
    c2, sn2 = np.cos(a2), np.sin(a2)
    stage2 = np.block([[c2, sn2], [-sn2, c2]])
    ag = 2.0 * np.pi * np.outer(np.arange(FOURIER_GROUP_DIM), np.arange(FOURIER_GROUP_DIM)) / FOURIER_GROUP_DIM
    scale = 1.0 / math.sqrt(seq * FOURIER_GROUP_DIM)
    eye = np.eye(LANES // FOURIER_GROUP_DIM)
    group = np.stack([np.kron(eye, np.cos(ag)), np.kron(eye, np.sin(ag))]) * scale
    as_bf16 = lambda a: jnp.asarray(a, F32).astype(BF16)
    return as_bf16(stage1), as_bf16(stage2), as_bf16(group)


def _fourier_kernel(u_ref, w1_ref, w2_ref, wg_ref, y_ref, u_sc, b_sc, y_sc, *, n1):
    n2 = DFT_N2
    rows = n1 * DFT_STEP
    for sb in range(n2 // DFT_STEP):
        u_sc[sb % 2] = u_ref[:, sb * DFT_STEP:(sb + 1) * DFT_STEP, :].reshape(rows, LANES)
        for j in range(DFT_STEP):
            pick = pl.ds(j, n1, stride=DFT_STEP)
            res = _dot(w1_ref[sb * DFT_STEP + j], u_sc[sb % 2, pick, :].astype(BF16))
            b_sc[sb, 0, pick, :] = res[:n1]
            b_sc[sb, 1, pick, :] = res[n1:]
    for kb in range(n1 // DFT_STEP):
        def gather(part, j):
            r0 = (kb * DFT_STEP + j) * DFT_STEP
            return jnp.concatenate([b_sc[sb, part, r0:r0 + DFT_STEP, :] for sb in range(n2 // DFT_STEP)], axis=0)

        stacked = jnp.concatenate([jnp.concatenate([gather(0, j), gather(1, j)], axis=0) for j in range(DFT_STEP)],
                                  axis=1).astype(BF16)
        z = _dot(w2_ref[...], stacked).astype(BF16)
        zr = jnp.concatenate([z[:n2, j * LANES:(j + 1) * LANES] for j in range(DFT_STEP)], axis=0)
        zi = jnp.concatenate([z[n2:, j * LANES:(j + 1) * LANES] for j in range(DFT_STEP)], axis=0)
        y = _dot(zr, wg_ref[0]) + _dot(zi, wg_ref[1])
        for j in range(DFT_STEP):
            y_sc[kb % 2, pl.ds(j, n2, stride=DFT_STEP), :] = y[j * n2:(j + 1) * n2, :]
        y_ref[:, kb * DFT_STEP:(kb + 1) * DFT_STEP, :] = y_sc[kb % 2].reshape(n2, DFT_STEP, LANES)


def _fourier(zf, bsz, seq):
    n1, n2 = seq // DFT_N2, DFT_N2
    stage1, stage2, group = _dft_constants(seq)
    whole = lambda a: pl.BlockSpec(a.shape, lambda b, h: (0,) * a.ndim)
    y = pl.pallas_call(
        functools.partial(_fourier_kernel, n1=n1),
        grid=(bsz, M_FOURIER // LANES),
        in_specs=[pl.BlockSpec((None, n1, n2, LANES), lambda b, h: (b, 0, 0, h)),
                  whole(stage1), whole(stage2), whole(group)],
        out_specs=pl.BlockSpec((None, n2, n1, LANES), lambda b, h: (b, 0, 0, h)),
        out_shape=jax.ShapeDtypeStruct((bsz, n2, n1, M_FOURIER), F32),
        scratch_shapes=[pltpu.VMEM((2, n1 * DFT_STEP, LANES), F32),
                        pltpu.VMEM((n2 // DFT_STEP, 2, n1 * DFT_STEP, LANES), F32),
                        pltpu.VMEM((2, n2 * DFT_STEP, LANES), F32)],
        compiler_params=_params("parallel", "parallel"),
        name="fourier",
    )(zf.reshape(bsz, n1, n2, M_FOURIER), stage1, stage2, group)
    return y.reshape(bsz * seq, M_FOURIER)


def _s5_constants():
    L, G = S5_CHUNK, S5_GROUP
    lane = np.arange(L * G)
    expo = np.arange(LANES)[:, None]
    rep = (lane[None, :] % G == np.arange(G)[:, None])
    step = lane // G
    spread = np.stack([expo == (L - 1) - step, expo == step])
    return tuple(jnp.asarray(m, BF16) for m in (rep, spread))


def _s5_prep_kernel(lamc_re, lamc_im, lam4_re, lam4_im, ldt4_ref, b_re, b_im, c_re, c_im, c4_re, c4_im,
                    d_ref, rep_ref, spread_ref, toep_ref, bend_ref, cout_ref, laml_ref):
    L, P, G = S5_CHUNK, S5_STATE, S5_GROUP
    shift = G.bit_length() - 1
    nbits = L.bit_length()

    def cmul(ar, ai, br, bi):
        return ar * br - ai * bi, ar * bi + ai * br

    def lam_bar(lre, lim, dt):
        mag = jnp.exp(lre * dt)
        return mag * jnp.cos(lim * dt), mag * jnp.sin(lim * dt)

    def power_table(base_r, base_i, expo):
        tr = jnp.ones(expo.shape, F32)
        ti = jnp.zeros(expo.shape, F32)
        sr, si = base_r, base_i
        for bit in range(nbits):
            nr, ni = cmul(tr, ti, sr, si)
            has = ((expo >> bit) & 1) == 1
            tr, ti = jnp.where(has, nr, tr), jnp.where(has, ni, ti)
            sr, si = cmul(sr, si, sr, si)
        return tr, ti

    def spread_dot(xr, xi, sel):
        n = xr.shape[0]
        out = _dot(jnp.concatenate(_split3(xr)[:2] + _split3(xi)[:2], axis=0), sel)
        return out[:n] + out[n:2 * n], out[2 * n:3 * n] + out[3 * n:]

    def dot3(c, m):
        c_hi, c_mid, _ = _split3(c)
        m_hi, m_mid, _ = _split3(m)
        return _dot(c_hi, m_hi) + _dot(c_mid, m_hi) + _dot(c_hi, m_mid)

    rep = rep_ref[...]
    half = L * G
    lane_g = lax.broadcasted_iota(jnp.int32, (G, half), 1)
    row_g = lax.broadcasted_iota(jnp.int32, (G, half), 0)
    expo_tab = jnp.minimum(lax.broadcasted_iota(jnp.int32, (P, LANES), 1), L)

    lb4r, lb4i = lam_bar(lam4_re[0], lam4_im[0], jnp.exp(ldt4_ref[0]))

    def as_column(row, d):
        block = jnp.broadcast_to(row[:, 2 * d * P:2 * (d + 1) * P], (8, 2 * P))
        return block.T[:P, 0:1]

    taps = []
    for d in range(2):
        lr, li = lamc_re[d, 0], lamc_im[d, 0]
        lbr, lbi = as_column(lb4r, d), as_column(lb4i, d)
        den = lr * lr + li * li
        fr, fi = ((lbr - 1.0) * lr + lbi * li) / den, (lbi * lr - (lbr - 1.0) * li) / den
        bbr, bbi = cmul(fr, fi, b_re[d, 0], b_im[d, 0])
        btr, bti = spread_dot(bbr, bbi, rep)
        tab_r, tab_i = power_table(lbr, lbi, expo_tab)
        qr, qi = spread_dot(tab_r, tab_i, spread_ref[d])
        xr, xi = cmul(qr, qi, btr, bti)
        bend_ref[0, 2 * d] = xr.astype(BF16)
        bend_ref[0, 2 * d + 1] = xi.astype(BF16)
        taps.append(dot3(c_re[d, 0], xr) - dot3(c_im[d, 0], xi))

    d_tiled = _dot_sel(jnp.broadcast_to(d_ref[0], (G, G)), rep)
    skip = jnp.where(((lane_g & (G - 1)) == row_g) & ((lane_g >> shift) == L - 1), d_tiled, 0.0)
    pad = jnp.zeros((G, half), F32)
    gen = (jnp.concatenate([taps[0] + skip, pad], axis=1)
           + pltpu.roll(jnp.concatenate([taps[1], pad], axis=1), (L - 1) * G, 1))

    for t in range(L):
        a = (L - 1 - t) * G
        toep_ref[0, t * G:(t + 1) * G, :] = gen[:, a:a + L * G].astype(BF16)

    lane4 = lax.broadcasted_iota(jnp.int32, (L, 4 * P), 1)
    step4 = lax.broadcasted_iota(jnp.int32, (L, 4 * P), 0)
    pr, pi = power_table(lb4r, lb4i, jnp.where(lane4 < 2 * P, step4 + 1, L - step4))
    cr, ci = c4_re[0], c4_im[0]
    plane_bit = P.bit_length() - 1
    re_c = ((lax.broadcasted_iota(jnp.int32, (G, 4 * P), 1) >> plane_bit) & 1) == 0
    for t in range(L):
        re_part, im_part = cmul(cr, ci, pr[t:t + 1], pi[t:t + 1])
        cout_ref[0, t * G:(t + 1) * G, :] = jnp.where(re_c, re_part, -im_part).astype(BF16)
    re_1 = ((lax.broadcasted_iota(jnp.int32, (1, 4 * P), 1) >> plane_bit) & 1) == 0
    lr, li = power_table(lb4r, lb4i, jnp.full((1, 4 * P), L, jnp.int32))
    laml_ref[0] = jnp.where(re_1, lr, li)


def _s5_prep(lam_re, lam_im, log_dt, b_re, b_im, c_re, c_im, d_skip):
    ng, P, G, L = S5_GROUPS, S5_STATE, S5_GROUP, S5_CHUNK
    depth = lam_re.shape[0]
    tile4 = lambda a: jnp.concatenate([a[:, 0], a[:, 0], a[:, 1], a[:, 1]], axis=-1)
    ldt4 = jnp.repeat(tile4(log_dt[..., None]), P, axis=-1)[:, :, None, :]
    consts = _s5_constants()
    spec = lambda *tail: pl.BlockSpec((None, 2, 1) + tail, lambda l, g: (l, 0, g) + (0,) * len(tail))
    per_g = lambda *tail: pl.BlockSpec((None, 1) + tail, lambda l, g: (l, g) + (0,) * len(tail))
    whole = lambda a: pl.BlockSpec(a.shape, lambda l, g: (0,) * a.ndim)
    return pl.pallas_call(
        _s5_prep_kernel,
        grid=(depth, ng),
        in_specs=[spec(P, 1), spec(P, 1), per_g(1, 4 * P), per_g(1, 4 * P), per_g(1, 4 * P),
                  spec(P, G), spec(P, G), spec(G, P), spec(G, P), per_g(G, 4 * P), per_g(G, 4 * P), per_g(1, G)]
                 + [whole(m) for m in consts],
        out_specs=[per_g(L * G, L * G), per_g(4, P, L * G), per_g(L * G, 4 * P), per_g(1, 4 * P)],
        out_shape=[jax.ShapeDtypeStruct((depth, ng, L * G, L * G), BF16),
                   jax.ShapeDtypeStruct((depth, ng, 4, P, L * G), BF16),
                   jax.ShapeDtypeStruct((depth, ng, L * G, 4 * P), BF16),
                   jax.ShapeDtypeStruct((depth, ng, 1, 4 * P), F32)],
        compiler_params=_params("parallel", "parallel"),
        name="s5_prep",
    )(lam_re[..., None], lam_im[..., None],
      tile4(lam_re)[:, :, None, :], tile4(lam_im)[:, :, None, :], ldt4,
      b_re, b_im, c_re, c_im, tile4(c_re), tile4(c_im), d_skip[:, :, None, :], *consts)


def _s5_sums_kernel(lo_ref, hi_ref, bend_ref, ut_ref, et_ref, *, nchunk):
    L, P, G = S5_CHUNK, S5_STATE, S5_GROUP
    per_half = LANES // G
    for half, z_ref in enumerate((lo_ref, hi_ref)):
        for s in range(L):
            zt = z_ref[pl.ds(s, nchunk, stride=L), :].T.astype(BF16)
            for gl in range(per_half):
                ut_ref[half * per_half + gl, s * G:(s + 1) * G, :] = zt[gl * G:(gl + 1) * G, :]
    for g in range(S5_GROUPS):
        for plane in range(4):
            row = plane * S5_GROUPS * P + g * P
            et_ref[row:row + P, :] = _dot(bend_ref[g, plane], ut_ref[g])


def _s5_scan_kernel(et_ref, lam_ref, xt_ref, e_sc, x_sc, *, bsz, steps):
    width = S5_GROUPS * S5_STATE
    for plane in range(4):
        e_sc[plane] = et_ref[plane * width:(plane + 1) * width, :].T
    ar_f, ai_f, ar_b, ai_b = lam_ref[0], lam_ref[1], lam_ref[2], lam_ref[3]
    zero = jnp.zeros_like(ar_f)
    for b in range(bsz):
        def body(i, carry):
            xr, xi, yr, yi = carry
            rf = b * steps + i
            rb = b * steps + steps - 1 - i
            x_sc[0, pl.ds(rf, 1), :] = xr
            x_sc[1, pl.ds(rf, 1), :] = xi
            x_sc[2, pl.ds(rb, 1), :] = yr
            x_sc[3, pl.ds(rb, 1), :] = yi
            xr, xi = (ar_f * xr - ai_f * xi + e_sc[0, pl.ds(rf, 1), :],
                      ar_f * xi + ai_f * xr + e_sc[1, pl.ds(rf, 1), :])
            yr, yi = (ar_b * yr - ai_b * yi + e_sc[2, pl.ds(rb, 1), :],
                      ar_b * yi + ai_b * yr + e_sc[3, pl.ds(rb, 1), :])
            return xr, xi, yr, yi
        lax.fori_loop(0, steps, body, (zero, zero, zero, zero))
    for plane in range(4):
        xt_ref[plane * width:(plane + 1) * width, :] = x_sc[plane].T


def _gelu_tanh(x):
    return 0.5 * x * (1.0 + jnp.tanh(math.sqrt(2.0 / math.pi) * (x + 0.044715 * (x * x * x))))


def _s5_out_kernel(ut_ref, xt_ref, toep_ref, cout_ref, lo_ref, hi_ref, yt_sc, *, nchunk):
    L, P, G = S5_CHUNK, S5_STATE, S5_GROUP
    width = S5_GROUPS * P
    for g in range(S5_GROUPS):
        xg = jnp.concatenate([xt_ref[plane * width + g * P:plane * width + (g + 1) * P, :] for plane in range(4)],
                             axis=0).astype(BF16)
        yt = _gelu_tanh(_dot(toep_ref[g], ut_ref[g]) + _dot(cout_ref[g], xg))
        for t in range(L):
            yt_sc[t, g * G:(g + 1) * G, :] = yt[t * G:(t + 1) * G, :]
    for t in range(L):
        y = yt_sc[t].T
        lo_ref[pl.ds(t, nchunk, stride=L), :] = y[:, :LANES]
        hi_ref[pl.ds(t, nchunk, stride=L), :] = y[:, LANES:]


def _s5(zs, prep, l, bsz, seq):
    ng, P, G, L = S5_GROUPS, S5_STATE, S5_GROUP, S5_CHUNK
    toep, bend, cout, laml = prep
    depth = toep.shape[0]
    laml = laml.reshape(depth, ng, 4, P).transpose(0, 2, 1, 3).reshape(depth, 4, 1, ng * P)
    t = bsz * seq
    tile = min(S5_TILE, t)
    nct = tile // L
    nchunk = t // L
    ut_spec = pl.BlockSpec((ng, L * G, nct), lambda i: (0, 0, i))
    plane_spec = pl.BlockSpec((4 * ng * P, nct), lambda i: (0, i))
    plane_shape = jax.ShapeDtypeStruct((4 * ng * P, nchunk), F32)
    ut, et = pl.pallas_call(
        functools.partial(_s5_sums_kernel, nchunk=nct),
        grid=(t // tile,),
        in_specs=[pl.BlockSpec((tile, LANES), lambda i: (i, 0)), pl.BlockSpec((tile, LANES), lambda i: (i, 1)),
                  _layer_spec(bend, l)],
        out_specs=[ut_spec, plane_spec],
        out_shape=[jax.ShapeDtypeStruct((ng, L * G, nchunk), BF16), plane_shape],
        compiler_params=_params("parallel"),
        name="s5_chunk_sums",
    )(zs, zs, bend)
    whole_planes = pl.BlockSpec((4 * ng * P, nchunk), lambda i: (0, 0))
    xt = pl.pallas_call(
        functools.partial(_s5_scan_kernel, bsz=bsz, steps=seq // L),
        grid=(1,),
        in_specs=[whole_planes, _layer_spec(laml, l)],
        out_specs=whole_planes,
        out_shape=plane_shape,
        scratch_shapes=[pltpu.VMEM((4, nchunk, ng * P), F32), pltpu.VMEM((4, nchunk, ng * P), F32)],
        compiler_params=_params("arbitrary"),
        name="s5_state_scan",
    )(et, laml)
    half_spec = pl.BlockSpec((tile, LANES), lambda i: (i, 0))
    half_shape = jax.ShapeDtypeStruct((t, LANES), F32)
    return pl.pallas_call(
        functools.partial(_s5_out_kernel, nchunk=nct),
        grid=(t // tile,),
        in_specs=[ut_spec, plane_spec, _layer_spec(toep, l), _layer_spec(cout, l)],
        out_specs=[half_spec, half_spec],
        out_shape=[half_shape, half_shape],
        scratch_shapes=[pltpu.VMEM((L, M_S5, nct), F32)],
        compiler_params=_params("parallel"),
        name="s5_outputs",
    )(ut, xt, toep, cout)


def _merge_kernel(x_ref, h_ref, hf_ref, hb_ref, zo_ref, yf_ref, ys_lo_ref, ys_hi_ref, wgate_ref, bgate_ref,
                  ng_ref, wm_ref, wf_ref, wglu_ref, bglu_ref, wout_ref, gpost_ref, o_ref, mixed_sc):
    h = h_ref[...]
    hs = hf_ref[...] + hb_ref[...]
    parts = []
    for hd in range(HEADS):
        blk = hs[:, hd * HEAD_DIM:(hd + 1) * HEAD_DIM]
        mu = jnp.mean(blk, axis=-1, keepdims=True)
        cen = blk - mu
        var = jnp.mean(cen * cen, axis=-1, keepdims=True)
        parts.append(cen * lax.rsqrt(var + EPS))
    hm = (jnp.concatenate(parts, axis=1) * ng_ref[...] * _sigmoid(zo_ref[...].astype(F32))).astype(BF16)
    yf = yf_ref[...].astype(BF16)
    ys = jnp.concatenate([ys_lo_ref[...].astype(BF16), ys_hi_ref[...].astype(BF16)], axis=1)
    for n in range(D_MODEL // MERGE_COLS):
        lo = n * MERGE_COLS
        cols = slice(lo, lo + MERGE_COLS)

        def gate(i):
            gcols = slice(i * D_MODEL + lo, i * D_MODEL + lo + MERGE_COLS)
            return _sigmoid(_dot(h, wgate_ref[:, gcols]) + bgate_ref[:, gcols])

        lin = _dot(ys, wglu_ref[:, cols]) + bglu_ref[:, cols]
        gcols = slice(D_MODEL + lo, D_MODEL + lo + MERGE_COLS)
        y_s = lin * _sigmoid(_dot(ys, wglu_ref[:, gcols]) + bglu_ref[:, gcols])
        mixed = gate(0) * _dot(hm, wm_ref[:, cols]) + gate(1) * _dot(yf, wf_ref[:, cols]) + gate(2) * y_s
        mixed_sc[:, cols] = mixed.astype(BF16)
    o_ref[...] = x_ref[...] + _rms(_dot(mixed_sc[...], wout_ref[...]), gpost_ref[...])


def _merge(x2, h, hf, hb, zo, yf, ys_lo, ys_hi, params, l, tile):
    t = x2.shape[0]
    rows = lambda width: pl.BlockSpec((tile, width), lambda i: (i, 0))
    return pl.pallas_call(
        _merge_kernel,
        grid=(t // tile,),
        in_specs=[rows(D_MODEL), rows(D_MODEL), rows(M_MLSTM), rows(M_MLSTM), rows(M_MLSTM), rows(M_FOURIER),
                  rows(LANES), rows(LANES)]
                 + [_layer_spec(a, l, (W_BLOCK, 0) if k in (0, 1) else None)
                    for k, a in enumerate(params)],
        out_specs=rows(D_MODEL),
        out_shape=jax.ShapeDtypeStruct((t, D_MODEL), F32),
        scratch_shapes=[pltpu.VMEM((tile, D_MODEL), BF16)],
        compiler_params=_params("parallel"),
        name="merge",
    )(x2, h, hf, hb, zo, yf, ys_lo, ys_hi, *params)


def _ffn_kernel(x_ref, gpre_ref, w1_ref, w2_ref, gpost_ref, o_ref, *, n_split):
    x = x_ref[...]
    h = _rms(x, gpre_ref[...]).astype(BF16)
    width = D_FF // n_split
    f = None
    for j in range(n_split):
        a = jnp.maximum(_dot(h, w1_ref[:, j * width:(j + 1) * width]), 0.0)
        part = _dot((a * a).astype(BF16), w2_ref[j * width:(j + 1) * width, :])
        f = part if f is None else f + part
    o_ref[...] = x + _rms(f, gpost_ref[...])


def _ffn(x2, params, l, tile):
    t = x2.shape[0]
    rows = pl.BlockSpec((tile, D_MODEL), lambda i: (i, 0))
    return pl.pallas_call(
        functools.partial(_ffn_kernel, n_split=4),
        grid=(t // tile,),
        in_specs=[rows] + [_layer_spec(a, l) for a in params],
        out_specs=rows,
        out_shape=jax.ShapeDtypeStruct((t, D_MODEL), F32),
        compiler_params=_params("parallel"),
        name="ffn",
    )(x2, *params)


def kernel(x, g_mix_pre, g_mix_post, g_ffn_pre, g_ffn_post, w_in, b_in, conv_w, conv_b, mlstm_norm_g, w_up_mlstm, w_up_fourier, s5_lam_re, s5_lam_im, s5_log_dt, s5_b_re, s5_b_im, s5_c_re, s5_c_im, s5_d, w_glu, b_glu, w_out, w_ffn1, w_ffn2):
    bsz, seq, _ = x.shape
    depth = w_in.shape[0]
    t = bsz * seq
    tile = min(1024, t)
    bf = lambda a: a.astype(BF16)
    row = lambda a: a[:, None, :]
    w_gates = w_in[:, :, OFF_IG:OFF_FOURIER]
    b_gates = b_in[:, OFF_IG:OFF_FOURIER]
    regroup = lambda a: jnp.concatenate(
        [a[..., OFF_GATE:], a[..., :OFF_IG], a[..., OFF_FOURIER:OFF_GATE],
         jnp.zeros(a.shape[:-1] + (W_BLOCK - N_MAIN,), a.dtype)], axis=-1)
    w_cols = regroup(bf(w_in))
    b_cols = row(regroup(b_in))
    inproj_params = (row(g_mix_pre), w_cols, b_cols, bf(w_gates), row(b_gates), conv_w, row(conv_b))
    merge_params = (w_cols, b_cols, row(mlstm_norm_g),
                    bf(w_up_mlstm), bf(w_up_fourier), bf(w_glu), row(b_glu), bf(w_out), row(g_mix_post))
    ffn_params = (row(g_ffn_pre), bf(w_ffn1), bf(w_ffn2), row(g_ffn_post))
    s5_prep = _s5_prep(s5_lam_re, s5_lam_im, s5_log_dt, s5_b_re, s5_b_im, s5_c_re, s5_c_im, s5_d)
    x2 = x.reshape(t, D_MODEL)
    for l in range(depth):
        q, kt, zv, zo, zf, zs, zgt, h = _inproj(x2, inproj_params, l, tile, seq)
        grow, gcol = _mlstm_gates(zgt, tile)
        hf, hb = _mlstm(q, kt, zv, zgt, grow, gcol, bsz, seq)
        yf = _fourier(zf, bsz, seq)
        ys_lo, ys_hi = _s5(zs, s5_prep, l, bsz, seq)
        x2 = _merge(x2, h, hf, hb, zo, yf, ys_lo, ys_hi, merge_params, l, tile)
        x2 = _ffn(x2, ffn_params, l, tile)
    return x2.reshape(bsz, seq, D_MODEL)
```

```python
import functools
import itertools
import math

import numpy as np
import jax
import jax.numpy as jnp
from jax import lax
from jax.experimental import pallas as pl
from jax.experimental.pallas import tpu as pltpu

F32 = jnp.float32
BF16 = jnp.bfloat16

LANES = 128
D_MODEL = 1024
M_MLSTM = 512
HEADS = 4
HEAD_DIM = 128
NT_DIMS = (((1,), (1,)), ((), ()))
CHUNK = 128
MLSTM_SUB = 8
CONV_WIDTH = 5
CONV_COLS = 256
CONV_HALO = 8
M_FOURIER = 256
FOURIER_GROUP_DIM = 64
M_S5 = 256
S5_GROUP = 16
S5_GROUPS = 16
S5_STATE = 64
S5_CHUNK = 32
S5_TILE = 4096
N_BRANCHES = 3
MERGE_COLS = 256
D_FF = 4 * D_MODEL
EPS = 1e-6

OFF_Q = 0
OFF_V = 2 * M_MLSTM
OFF_O = 3 * M_MLSTM
OFF_IG = 4 * M_MLSTM
OFF_FOURIER = OFF_IG + 4 * HEADS
OFF_S5 = OFF_FOURIER + M_FOURIER
OFF_GATE = OFF_S5 + M_S5

DFT_N2 = 64
DFT_STEP = 8

VMEM_LIMIT = 56 * 1024 * 1024


def _params(*sem):
    return pltpu.CompilerParams(dimension_semantics=sem, vmem_limit_bytes=VMEM_LIMIT)


def _layer_spec(a, l):
    return pl.BlockSpec((None,) + a.shape[1:], lambda *_: (l,) + (0,) * (a.ndim - 1), pipeline_mode=pl.Buffered(1))


def _rms(x, g):
    return x * lax.rsqrt(jnp.mean(x * x, axis=-1, keepdims=True) + EPS) * g


def _sigmoid(x):
    return 1.0 / (1.0 + jnp.exp(-x))


def _log_sigmoid(x):
    return jnp.minimum(x, 0.0) - jnp.log1p(jnp.exp(-jnp.abs(x)))


def _dot(a, b):
    return jnp.dot(a, b, preferred_element_type=F32)


def _split3(a):
    hi = a.astype(BF16)
    rest = a - hi.astype(F32)
    mid = rest.astype(BF16)
    return hi, mid, (rest - mid.astype(F32)).astype(BF16)


def _dot_sel(a, sel):
    hi, mid, lo = _split3(a)
    return _dot(hi, sel) + _dot(mid, sel) + _dot(lo, sel)


def _inproj_kernel(x_ref, xp_ref, xn_ref, g_ref, wt_ref, b_ref, cw_ref, cb_ref,
                   q_ref, kt_ref, zv_ref, zo_ref, zf_ref, zs_ref, zgt_ref, h_ref, *, tiles_per_seq):
    i = pl.program_id(0)
    tile = x_ref.shape[0]
    pad = CONV_WIDTH // 2
    body = slice(CONV_HALO, CONV_HALO + tile)
    h = _rms(jnp.concatenate([xp_ref[...], x_ref[...], xn_ref[...]], axis=0), g_ref[...]).astype(BF16)
    h_ref[...] = h[body]
    row = lax.broadcasted_iota(jnp.int32, (tile + 2 * CONV_HALO, 1), 0)
    first = (i % tiles_per_seq) == 0
    last = (i % tiles_per_seq) == tiles_per_seq - 1
    outside = (first & (row < CONV_HALO)) | (last & (row >= CONV_HALO + tile))

    def proj(lo, hi):
        return lax.dot_general(h, wt_ref[lo:hi, :], NT_DIMS, preferred_element_type=F32) + b_ref[:, lo:hi]

    for n in range(2 * M_MLSTM // CONV_COLS):
        lo = n * CONV_COLS
        cols = slice(lo, lo + CONV_COLS)
        ext = jnp.where(outside, 0.0, proj(OFF_Q + lo, OFF_Q + lo + CONV_COLS))
        acc = cb_ref[:, cols]
        for j in range(CONV_WIDTH):
            shifted = ext if j == pad else pltpu.roll(ext, (pad - j) % ext.shape[0], 0)
            acc = acc + cw_ref[j:j + 1, cols] * shifted[body]
        half = 0.5 * acc
        qk = half + half * jnp.tanh(half)
        if lo < M_MLSTM:
            q_ref[:, cols] = (qk * (HEAD_DIM ** -0.5)).astype(BF16)
        else:
            kt_ref[lo - M_MLSTM:lo - M_MLSTM + CONV_COLS, :] = qk.T.astype(BF16)

    zv_ref[...] = proj(OFF_V, OFF_O)[body].astype(BF16)
    zo_ref[...] = proj(OFF_O, OFF_IG)[body].astype(BF16)
    zf_ref[...] = proj(OFF_FOURIER, OFF_S5)[body]
    zs_ref[...] = proj(OFF_S5, OFF_GATE)[body]
    zgt_ref[...] = proj(OFF_IG, OFF_FOURIER)[body].T


def _inproj(x2, params, l, tile, seq):
    t = x2.shape[0]
    n_gate = 4 * HEADS
    hpt = tile // CONV_HALO
    n_halo = t // CONV_HALO
    rows = lambda width: pl.BlockSpec((tile, width), lambda i: (i, 0))
    cols = lambda height: pl.BlockSpec((height, tile), lambda i: (0, i))
    return pl.pallas_call(
        functools.partial(_inproj_kernel, tiles_per_seq=seq // tile),
        grid=(t // tile,),
        in_specs=[rows(D_MODEL),
                  pl.BlockSpec((CONV_HALO, D_MODEL), lambda i: (jnp.maximum(i * hpt - 1, 0), 0)),
                  pl.BlockSpec((CONV_HALO, D_MODEL), lambda i: (jnp.minimum((i + 1) * hpt, n_halo - 1), 0))]
                 + [_layer_spec(a, l) for a in params],
        out_specs=[rows(M_MLSTM), cols(M_MLSTM), rows(M_MLSTM), rows(M_MLSTM), rows(M_FOURIER), rows(M_S5),
                   cols(n_gate), rows(D_MODEL)],
        out_shape=[jax.ShapeDtypeStruct((t, M_MLSTM), BF16), jax.ShapeDtypeStruct((M_MLSTM, t), BF16),
                   jax.ShapeDtypeStruct((t, M_MLSTM), BF16),
                   jax.ShapeDtypeStruct((t, M_MLSTM), BF16), jax.ShapeDtypeStruct((t, M_FOURIER), F32),
                   jax.ShapeDtypeStruct((t, M_S5), F32), jax.ShapeDtypeStruct((n_gate, t), F32),
                   jax.ShapeDtypeStruct((t, D_MODEL), BF16)],
        compiler_params=_params("parallel"),
        name="inproj",
    )(x2, x2, x2, *params)


def _mlstm_gates_kernel(zgt_ref, grow_ref, gcol_ref):
    ig_rows = zgt_ref[:2 * HEADS, :]
    lf_rows = _log_sigmoid(zgt_ref[2 * HEADS:, :])
    r = lax.broadcasted_iota(jnp.int32, (CHUNK, CHUNK), 0)
    c = lax.broadcasted_iota(jnp.int32, (CHUNK, CHUNK), 1)
    tri = jnp.concatenate([jnp.where(r <= c, 1.0, 0.0), jnp.where(r >= c, 1.0, 0.0)], axis=1).astype(BF16)
    fwd_row = lax.broadcasted_iota(jnp.int32, (2 * HEADS, CHUNK), 0) < HEADS
    b_parts, w_parts, amax_parts = [], [], []
    for s in range(zgt_ref.shape[1] // CHUNK):
        span = slice(s * CHUNK, (s + 1) * CHUNK)
        both = _dot_sel(lf_rows[:, span], tri)
        b = jnp.where(fwd_row, both[:, :CHUNK], both[:, CHUNK:])
        g_tot = jnp.where(fwd_row, b[:, CHUNK - 1:], b[:, :1])
        a = g_tot - b + ig_rows[:, span]
        a_max = jnp.max(a, axis=1, keepdims=True)
        b_parts.append(b)
        w_parts.append(jnp.exp(a - a_max))
        amax_parts.append(jnp.broadcast_to(a_max, a.shape))
    b_rows = jnp.concatenate(b_parts, axis=1)
    grow_ref[:2 * HEADS, :] = b_rows
    grow_ref[2 * HEADS:, :] = jnp.concatenate(amax_parts, axis=1)
    gcol_ref[:, :2 * HEADS] = b_rows.T
    gcol_ref[:, 2 * HEADS:] = jnp.concatenate(w_parts, axis=1).T


def _mlstm_gates(zgt, tile):
    n_gate, t = zgt.shape
    cols = pl.BlockSpec((n_gate, tile), lambda i: (0, i))
    return pl.pallas_call(
        _mlstm_gates_kernel,
        grid=(t // tile,),
        in_specs=[cols],
        out_specs=[cols, pl.BlockSpec((tile, n_gate), lambda i: (i, 0))],
        out_shape=[jax.ShapeDtypeStruct((n_gate, t), F32), jax.ShapeDtypeStruct((t, n_gate), F32)],
        compiler_params=_params("parallel"),
        name="mlstm_gates",
    )(zgt)


def _mlstm_kernel(q_f, kt_f, v_f, zgt_f, grow_f, gcol_f, q_b, kt_b, v_b, zgt_b, grow_b, gcol_b, hf_ref, hb_ref,
                  state_sc, m_sc):
    c = pl.program_id(1)
    L = CHUNK

    @pl.when(c == 0)
    def _():
        state_sc[...] = jnp.zeros_like(state_sc)
        m_sc[...] = jnp.zeros_like(m_sc)

    row = lax.broadcasted_iota(jnp.int32, (L, L), 0)
    col = lax.broadcasted_iota(jnp.int32, (L, L), 1)
    lower = row >= col
    upper = row <= col
    ones_v = jnp.ones((L, HEAD_DIM), F32)

    streams = ((0, q_f, kt_f, v_f, zgt_f, grow_f, gcol_f, hf_ref), (1, q_b, kt_b, v_b, zgt_b, grow_b, gcol_b, hb_ref))
    for sub, (d, q_ref, kt_ref, v_ref, zgt_ref, grow_ref, gcol_ref, out_ref) in itertools.product(range(MLSTM_SUB),
                                                                                                   streams):
        r0 = (sub if d == 0 else MLSTM_SUB - 1 - sub) * L
        rows = slice(r0, r0 + L)
        ig_rows = zgt_ref[:2 * HEADS, rows]
        b_rows = grow_ref[:2 * HEADS, rows]
        amax_rows = grow_ref[2 * HEADS:, rows]
        b_cols = gcol_ref[rows, :2 * HEADS]
        w_cols = gcol_ref[rows, 2 * HEADS:]
        mask = lower if d == 0 else upper

        for hd in range(HEADS):
            k_idx = d * HEADS + hd
            lo = hd * HEAD_DIM
            q = q_ref[rows, lo:lo + HEAD_DIM]
            kt = kt_ref[lo:lo + HEAD_DIM, rows]
            v = v_ref[rows, lo:lo + HEAD_DIM]
            b_c = jnp.broadcast_to(b_cols[:, k_idx:k_idx + 1], (L, L))
            w_c = jnp.broadcast_to(w_cols[:, k_idx:k_idx + 1], (L, L))
            b_r = b_rows[k_idx:k_idx + 1, :]
            ig_r = ig_rows[k_idx:k_idx + 1, :]
            g_tot = b_c[L - 1:L, :] if d == 0 else b_c[0:1, :]
            a_max = amax_rows[k_idx:k_idx + 1, :]
            m_prev = m_sc[k_idx:k_idx + 1, :]

            d_log = jnp.where(mask, b_c - b_r + ig_r, -1e30)
            inter_log = b_c + m_prev
            m_t = jnp.maximum(inter_log, jnp.max(d_log, axis=1, keepdims=True))
            scores = _dot(q, kt) * jnp.exp(d_log - m_t)
            inter_w = jnp.exp(inter_log - m_t)
            st = state_sc[k_idx]
            v_ext = jnp.concatenate([v, ones_v.astype(BF16)], axis=1)
            res = _dot(scores.astype(BF16), v_ext) + _dot((inter_w * q.astype(F32)).astype(BF16), st.astype(BF16))
            num = res[:, :HEAD_DIM]
            den = res[:, HEAD_DIM:]
            out_ref[rows, lo:lo + HEAD_DIM] = num / jnp.maximum(jnp.abs(den), jnp.exp(-m_t))

            vw = jnp.concatenate([v.astype(F32) * w_c, w_c], axis=1).astype(BF16)
            st_loc = _dot(kt, vw)
            m_new = jnp.maximum(g_tot + m_prev, a_max)
            s_old = jnp.exp(g_tot + m_prev - m_new)
            s_new = jnp.exp(a_max - m_new)
            s_old2 = jnp.concatenate([s_old, s_old], axis=1)
            s_new2 = jnp.concatenate([s_new, s_new], axis=1)
            state_sc[k_idx] = s_old2 * st + s_new2 * st_loc
            m_sc[k_idx:k_idx + 1, :] = m_new


def _mlstm(q, kt, zv, zgt, grow, gcol, bsz, seq):
    t = bsz * seq
    blk = CHUNK * MLSTM_SUB
    nc = seq // blk
    n_gate = 4 * HEADS

    def fwd(b, c):
        return b * nc + c

    def bwd(b, c):
        return b * nc + nc - 1 - c

    def specs(pos):
        return [pl.BlockSpec((blk, M_MLSTM), lambda b, c: (pos(b, c), 0)),
                pl.BlockSpec((M_MLSTM, blk), lambda b, c: (0, pos(b, c))),
                pl.BlockSpec((blk, M_MLSTM), lambda b, c: (pos(b, c), 0)),
                pl.BlockSpec((n_gate, blk), lambda b, c: (0, pos(b, c))),
                pl.BlockSpec((n_gate, blk), lambda b, c: (0, pos(b, c))),
                pl.BlockSpec((blk, n_gate), lambda b, c: (pos(b, c), 0))]

    out_shape = jax.ShapeDtypeStruct((t, M_MLSTM), F32)
    return pl.pallas_call(
        _mlstm_kernel,
        grid=(bsz, nc),
        in_specs=specs(fwd) + specs(bwd),
        out_specs=[pl.BlockSpec((blk, M_MLSTM), lambda b, c: (fwd(b, c), 0)),
                   pl.BlockSpec((blk, M_MLSTM), lambda b, c: (bwd(b, c), 0))],
        out_shape=[out_shape, out_shape],
        scratch_shapes=[pltpu.VMEM((2 * HEADS, HEAD_DIM, 2 * HEAD_DIM), F32),
                        pltpu.VMEM((2 * HEADS, HEAD_DIM), F32)],
        compiler_params=_params("arbitrary", "arbitrary"),
        name="mlstm",
    )(q, kt, zv, zgt, grow, gcol, q, kt, zv, zgt, grow, gcol)


def _dft_constants(seq):
    n1, n2 = seq // DFT_N2, DFT_N2
    k1 = np.arange(n1)[:, None, None]
    s2 = np.arange(n2)[None, None, :]
    s1 = np.arange(n1)[None, :, None]
    ang = -2.0 * np.pi * ((k1 * (n2 * s1 + s2)) % seq) / seq
    stage1 = np.concatenate([np.cos(ang), np.sin(ang)], axis=0)
    stage1 = np.ascontiguousarray(stage1.transpose(2, 0, 1))
    a2 = 2.0 * np.pi * np.outer(np.arange(n2), np.arange(n2)) / n2
    c2, sn2 = np.cos(a2), np.sin(a2)
    stage2 = np.block([[c2, sn2], [-sn2, c2]])
    ag = 2.0 * np.pi * np.outer(np.arange(FOURIER_GROUP_DIM), np.arange(FOURIER_GROUP_DIM)) / FOURIER_GROUP_DIM
    scale = 1.0 / math.sqrt(seq * FOURIER_GROUP_DIM)
    eye = np.eye(LANES // FOURIER_GROUP_DIM)
    group = np.stack([np.kron(eye, np.cos(ag)), np.kron(eye, np.sin(ag))]) * scale
    as_bf16 = lambda a: jnp.asarray(a, F32).astype(BF16)
    return as_bf16(stage1), as_bf16(stage2), as_bf16(group)


def _fourier_kernel(u_ref, w1_ref, w2_ref, wg_ref, y_ref, u_sc, b_sc, y_sc, *, n1):
    n2 = DFT_N2
    rows = n1 * DFT_STEP
    for sb in range(n2 // DFT_STEP):
        u_sc[sb % 2] = u_ref[:, sb * DFT_STEP:(sb + 1) * DFT_STEP, :].reshape(rows, LANES)
        for j in range(DFT_STEP):
            pick = pl.ds(j, n1, stride=DFT_STEP)
            res = _dot(w1_ref[sb * DFT_STEP + j], u_sc[sb % 2, pick, :].astype(BF16))
            b_sc[sb, 0, pick, :] = res[:n1]
            b_sc[sb, 1, pick, :] = res[n1:]
    for kb in range(n1 // DFT_STEP):
        def gather(part, j):
            r0 = (kb * DFT_STEP + j) * DFT_STEP
            return jnp.concatenate([b_sc[sb, part, r0:r0 + DFT_STEP, :] for sb in range(n2 // DFT_STEP)], axis=0)

        stacked = jnp.concatenate([jnp.concatenate([gather(0, j), gather(1, j)], axis=0) for j in range(DFT_STEP)],
                                  axis=1).astype(BF16)
        z = _dot(w2_ref[...], stacked).astype(BF16)
        zr = jnp.concatenate([z[:n2, j * LANES:(j + 1) * LANES] for j in range(DFT_STEP)], axis=0)
        zi = jnp.concatenate([z[n2:, j * LANES:(j + 1) * LANES] for j in range(DFT_STEP)], axis=0)
        y = _dot(zr, wg_ref[0]) + _dot(zi, wg_ref[1])
        for j in range(DFT_STEP):
            y_sc[kb % 2, pl.ds(j, n2, stride=DFT_STEP), :] = y[j * n2:(j + 1) * n2, :]
        y_ref[:, kb * DFT_STEP:(kb + 1) * DFT_STEP, :] = y_sc[kb % 2].reshape(n2, DFT_STEP, LANES)


def _fourier(zf, bsz, seq):
    n1, n2 = seq // DFT_N2, DFT_N2
    stage1, stage2, group = _dft_constants(seq)
    whole = lambda a: pl.BlockSpec(a.shape, lambda b, h: (0,) * a.ndim)
    y = pl.pallas_call(
        functools.partial(_fourier_kernel, n1=n1),
        grid=(bsz, M_FOURIER // LANES),
        in_specs=[pl.BlockSpec((None, n1, n2, LANES), lambda b, h: (b, 0, 0, h)),
                  whole(stage1), whole(stage2), whole(group)],
        out_specs=pl.BlockSpec((None, n2, n1, LANES), lambda b, h: (b, 0, 0, h)),
        out_shape=jax.ShapeDtypeStruct((bsz, n2, n1, M_FOURIER), F32),
        scratch_shapes=[pltpu.VMEM((2, n1 * DFT_STEP, LANES), F32),
                        pltpu.VMEM((n2 // DFT_STEP, 2, n1 * DFT_STEP, LANES), F32),
                        pltpu.VMEM((2, n2 * DFT_STEP, LANES), F32)],
        compiler_params=_params("parallel", "parallel"),
        name="fourier",
    )(zf.reshape(bsz, n1, n2, M_FOURIER), stage1, stage2, group)
    return y.reshape(bsz * seq, M_FOURIER)


def _s5_constants():
    L, G = S5_CHUNK, S5_GROUP
    lane = np.arange(L * G)
    expo = np.arange(LANES)[:, None]
    rep = (lane[None, :] % G == np.arange(G)[:, None])
    step = lane // G
    spread = np.stack([expo == (L - 1) - step, expo == step])
    return tuple(jnp.asarray(m, BF16) for m in (rep, spread))


def _s5_prep_kernel(lamc_re, lamc_im, lam4_re, lam4_im, ldt4_ref, b_re, b_im, c_re, c_im, c4_re, c4_im,
                    d_ref, rep_ref, spread_ref, toep_ref, bend_ref, cout_ref, laml_ref):
    L, P, G = S5_CHUNK, S5_STATE, S5_GROUP
    shift = G.bit_length() - 1
    nbits = L.bit_length()

    def cmul(ar, ai, br, bi):
        return ar * br - ai * bi, ar * bi + ai * br

    def lam_bar(lre, lim, dt):
        mag = jnp.exp(lre * dt)
        return mag * jnp.cos(lim * dt), mag * jnp.sin(lim * dt)

    def power_table(base_r, base_i, expo):
        tr = jnp.ones(expo.shape, F32)
        ti = jnp.zeros(expo.shape, F32)
        sr, si = base_r, base_i
        for bit in range(nbits):
            nr, ni = cmul(tr, ti, sr, si)
            has = ((expo >> bit) & 1) == 1
            tr, ti = jnp.where(has, nr, tr), jnp.where(has, ni, ti)
            sr, si = cmul(sr, si, sr, si)
        return tr, ti

    def spread_dot(xr, xi, sel):
        n = xr.shape[0]
        out = _dot(jnp.concatenate(_split3(xr)[:2] + _split3(xi)[:2], axis=0), sel)
        return out[:n] + out[n:2 * n], out[2 * n:3 * n] + out[3 * n:]

    def dot3(c, m):
        c_hi, c_mid, _ = _split3(c)
        m_hi, m_mid, _ = _split3(m)
        return _dot(c_hi, m_hi) + _dot(c_mid, m_hi) + _dot(c_hi, m_mid)

    rep = rep_ref[...]
    half = L * G
    lane_g = lax.broadcasted_iota(jnp.int32, (G, half), 1)
    row_g = lax.broadcasted_iota(jnp.int32, (G, half), 0)
    expo_tab = jnp.minimum(lax.broadcasted_iota(jnp.int32, (P, LANES), 1), L)

    lb4r, lb4i = lam_bar(lam4_re[0], lam4_im[0], jnp.exp(ldt4_ref[0]))

    def as_column(row, d):
        block = jnp.broadcast_to(row[:, 2 * d * P:2 * (d + 1) * P], (8, 2 * P))
        return block.T[:P, 0:1]

    taps = []
    for d in range(2):
        lr, li = lamc_re[d, 0], lamc_im[d, 0]
        lbr, lbi = as_column(lb4r, d), as_column(lb4i, d)
        den = lr * lr + li * li
        fr, fi = ((lbr - 1.0) * lr + lbi * li) / den, (lbi * lr - (lbr - 1.0) * li) / den
        bbr, bbi = cmul(fr, fi, b_re[d, 0], b_im[d, 0])
        btr, bti = spread_dot(bbr, bbi, rep)
        tab_r, tab_i = power_table(lbr, lbi, expo_tab)
        qr, qi = spread_dot(tab_r, tab_i, spread_ref[d])
        xr, xi = cmul(qr, qi, btr, bti)
        bend_ref[0, 2 * d] = xr.astype(BF16)
        bend_ref[0, 2 * d + 1] = xi.astype(BF16)
        taps.append(dot3(c_re[d, 0], xr) - dot3(c_im[d, 0], xi))

    d_tiled = _dot_sel(jnp.broadcast_to(d_ref[0], (G, G)), rep)
    skip = jnp.where(((lane_g & (G - 1)) == row_g) & ((lane_g >> shift) == L - 1), d_tiled, 0.0)
    pad = jnp.zeros((G, half), F32)
    gen = (jnp.concatenate([taps[0] + skip, pad], axis=1)
           + pltpu.roll(jnp.concatenate([taps[1], pad], axis=1), (L - 1) * G, 1))

    for t in range(L):
        a = (L - 1 - t) * G
        toep_ref[0, t * G:(t + 1) * G, :] = gen[:, a:a + L * G].astype(BF16)

    lane4 = lax.broadcasted_iota(jnp.int32, (L, 4 * P), 1)
    step4 = lax.broadcasted_iota(jnp.int32, (L, 4 * P), 0)
    pr, pi = power_table(lb4r, lb4i, jnp.where(lane4 < 2 * P, step4 + 1, L - step4))
    cr, ci = c4_re[0], c4_im[0]
    plane_bit = P.bit_length() - 1
    re_c = ((lax.broadcasted_iota(jnp.int32, (G, 4 * P), 1) >> plane_bit) & 1) == 0
    for t in range(L):
        re_part, im_part = cmul(cr, ci, pr[t:t + 1], pi[t:t + 1])
        cout_ref[0, t * G:(t + 1) * G, :] = jnp.where(re_c, re_part, -im_part).astype(BF16)
    re_1 = ((lax.broadcasted_iota(jnp.int32, (1, 4 * P), 1) >> plane_bit) & 1) == 0
    lr, li = power_table(lb4r, lb4i, jnp.full((1, 4 * P), L, jnp.int32))
    laml_ref[0] = jnp.where(re_1, lr, li)


def _s5_prep(lam_re, lam_im, log_dt, b_re, b_im, c_re, c_im, d_skip):
    ng, P, G, L = S5_GROUPS, S5_STATE, S5_GROUP, S5_CHUNK
    depth = lam_re.shape[0]
    tile4 = lambda a: jnp.concatenate([a[:, 0], a[:, 0], a[:, 1], a[:, 1]], axis=-1)
    ldt4 = jnp.repeat(tile4(log_dt[..., None]), P, axis=-1)[:, :, None, :]
    consts = _s5_constants()
    spec = lambda *tail: pl.BlockSpec((None, 2, 1) + tail, lambda l, g: (l, 0, g) + (0,) * len(tail))
    per_g = lambda *tail: pl.BlockSpec((None, 1) + tail, lambda l, g: (l, g) + (0,) * len(tail))
    whole = lambda a: pl.BlockSpec(a.shape, lambda l, g: (0,) * a.ndim)
    return pl.pallas_call(
        _s5_prep_kernel,
        grid=(depth, ng),
        in_specs=[spec(P, 1), spec(P, 1), per_g(1, 4 * P), per_g(1, 4 * P), per_g(1, 4 * P),
                  spec(P, G), spec(P, G), spec(G, P), spec(G, P), per_g(G, 4 * P), per_g(G, 4 * P), per_g(1, G)]
                 + [whole(m) for m in consts],
        out_specs=[per_g(L * G, L * G), per_g(4, P, L * G), per_g(L * G, 4 * P), per_g(1, 4 * P)],
        out_shape=[jax.ShapeDtypeStruct((depth, ng, L * G, L * G), BF16),
                   jax.ShapeDtypeStruct((depth, ng, 4, P, L * G), BF16),
                   jax.ShapeDtypeStruct((depth, ng, L * G, 4 * P), BF16),
                   jax.ShapeDtypeStruct((depth, ng, 1, 4 * P), F32)],
        compiler_params=_params("parallel", "parallel"),
        name="s5_prep",
    )(lam_re[..., None], lam_im[..., None],
      tile4(lam_re)[:, :, None, :], tile4(lam_im)[:, :, None, :], ldt4,
      b_re, b_im, c_re, c_im, tile4(c_re), tile4(c_im), d_skip[:, :, None, :], *consts)


def _s5_sums_kernel(lo_ref, hi_ref, bend_ref, ut_ref, et_ref, *, nchunk):
    L, P, G = S5_CHUNK, S5_STATE, S5_GROUP
    per_half = LANES // G
    for half, z_ref in enumerate((lo_ref, hi_ref)):
        for s in range(L):
            zt = z_ref[pl.ds(s, nchunk, stride=L), :].T.astype(BF16)
            for gl in range(per_half):
                ut_ref[half * per_half + gl, s * G:(s + 1) * G, :] = zt[gl * G:(gl + 1) * G, :]
    for g in range(S5_GROUPS):
        for plane in range(4):
            row = plane * S5_GROUPS * P + g * P
            et_ref[row:row + P, :] = _dot(bend_ref[g, plane], ut_ref[g])


def _s5_scan_kernel(et_ref, lam_ref, xt_ref, e_sc, x_sc, *, bsz, steps):
    width = S5_GROUPS * S5_STATE
    for plane in range(4):
        e_sc[plane] = et_ref[plane * width:(plane + 1) * width, :].T
    ar_f, ai_f, ar_b, ai_b = lam_ref[0], lam_ref[1], lam_ref[2], lam_ref[3]
    zero = jnp.zeros_like(ar_f)
    for b in range(bsz):
        def body(i, carry):
            xr, xi, yr, yi = carry
            rf = b * steps + i
            rb = b * steps + steps - 1 - i
            x_sc[0, pl.ds(rf, 1), :] = xr
            x_sc[1, pl.ds(rf, 1), :] = xi
            x_sc[2, pl.ds(rb, 1), :] = yr
            x_sc[3, pl.ds(rb, 1), :] = yi
            xr, xi = (ar_f * xr - ai_f * xi + e_sc[0, pl.ds(rf, 1), :],
                      ar_f * xi + ai_f * xr + e_sc[1, pl.ds(rf, 1), :])
            yr, yi = (ar_b * yr - ai_b * yi + e_sc[2, pl.ds(rb, 1), :],
                      ar_b * yi + ai_b * yr + e_sc[3, pl.ds(rb, 1), :])
            return xr, xi, yr, yi
        lax.fori_loop(0, steps, body, (zero, zero, zero, zero))
    for plane in range(4):
        xt_ref[plane * width:(plane + 1) * width, :] = x_sc[plane].T


def _gelu_tanh(x):
    return 0.5 * x * (1.0 + jnp.tanh(math.sqrt(2.0 / math.pi) * (x + 0.044715 * (x * x * x))))


def _s5_out_kernel(ut_ref, xt_ref, toep_ref, cout_ref, lo_ref, hi_ref, yt_sc, *, nchunk):
    L, P, G = S5_CHUNK, S5_STATE, S5_GROUP
    width = S5_GROUPS * P
    for g in range(S5_GROUPS):
        xg = jnp.concatenate([xt_ref[plane * width + g * P:plane * width + (g + 1) * P, :] for plane in range(4)],
                             axis=0).astype(BF16)
        yt = _gelu_tanh(_dot(toep_ref[g], ut_ref[g]) + _dot(cout_ref[g], xg))
        for t in range(L):
            yt_sc[t, g * G:(g + 1) * G, :] = yt[t * G:(t + 1) * G, :]
    for t in range(L):
        y = yt_sc[t].T
        lo_ref[pl.ds(t, nchunk, stride=L), :] = y[:, :LANES]
        hi_ref[pl.ds(t, nchunk, stride=L), :] = y[:, LANES:]


def _s5(zs, prep, l, bsz, seq):
    ng, P, G, L = S5_GROUPS, S5_STATE, S5_GROUP, S5_CHUNK
    toep, bend, cout, laml = prep
    depth = toep.shape[0]
    laml = laml.reshape(depth, ng, 4, P).transpose(0, 2, 1, 3).reshape(depth, 4, 1, ng * P)
    t = bsz * seq
    tile = min(S5_TILE, t)
    nct = tile // L
    nchunk = t // L
    ut_spec = pl.BlockSpec((ng, L * G, nct), lambda i: (0, 0, i))
    plane_spec = pl.BlockSpec((4 * ng * P, nct), lambda i: (0, i))
    plane_shape = jax.ShapeDtypeStruct((4 * ng * P, nchunk), F32)
    ut, et = pl.pallas_call(
        functools.partial(_s5_sums_kernel, nchunk=nct),
        grid=(t // tile,),
        in_specs=[pl.BlockSpec((tile, LANES), lambda i: (i, 0)), pl.BlockSpec((tile, LANES), lambda i: (i, 1)),
                  _layer_spec(bend, l)],
        out_specs=[ut_spec, plane_spec],
        out_shape=[jax.ShapeDtypeStruct((ng, L * G, nchunk), BF16), plane_shape],
        compiler_params=_params("parallel"),
        name="s5_chunk_sums",
    )(zs, zs, bend)
    whole_planes = pl.BlockSpec((4 * ng * P, nchunk), lambda i: (0, 0))
    xt = pl.pallas_call(
        functools.partial(_s5_scan_kernel, bsz=bsz, steps=seq // L),
        grid=(1,),
        in_specs=[whole_planes, _layer_spec(laml, l)],
        out_specs=whole_planes,
        out_shape=plane_shape,
        scratch_shapes=[pltpu.VMEM((4, nchunk, ng * P), F32), pltpu.VMEM((4, nchunk, ng * P), F32)],
        compiler_params=_params("arbitrary"),
        name="s5_state_scan",
    )(et, laml)
    half_spec = pl.BlockSpec((tile, LANES), lambda i: (i, 0))
    half_shape = jax.ShapeDtypeStruct((t, LANES), F32)
    return pl.pallas_call(
        functools.partial(_s5_out_kernel, nchunk=nct),
        grid=(t // tile,),
        in_specs=[ut_spec, plane_spec, _layer_spec(toep, l), _layer_spec(cout, l)],
        out_specs=[half_spec, half_spec],
        out_shape=[half_shape, half_shape],
        scratch_shapes=[pltpu.VMEM((L, M_S5, nct), F32)],
        compiler_params=_params("parallel"),
        name="s5_outputs",
    )(ut, xt, toep, cout)


def _merge_kernel(x_ref, h_ref, hf_ref, hb_ref, zo_ref, yf_ref, ys_lo_ref, ys_hi_ref, wgate_ref, bgate_ref,
                  ng_ref, wm_ref, wf_ref, wglu_ref, bglu_ref, wout_ref, gpost_ref, o_ref, mixed_sc):
    h = h_ref[...]
    hs = hf_ref[...] + hb_ref[...]
    parts = []
    for hd in range(HEADS):
        blk = hs[:, hd * HEAD_DIM:(hd + 1) * HEAD_DIM]
        mu = jnp.mean(blk, axis=-1, keepdims=True)
        cen = blk - mu
        var = jnp.mean(cen * cen, axis=-1, keepdims=True)
        parts.append(cen * lax.rsqrt(var + EPS))
    hm = (jnp.concatenate(parts, axis=1) * ng_ref[...] * _sigmoid(zo_ref[...].astype(F32))).astype(BF16)
    yf = yf_ref[...].astype(BF16)
    ys = jnp.concatenate([ys_lo_ref[...].astype(BF16), ys_hi_ref[...].astype(BF16)], axis=1)
    for n in range(D_MODEL // MERGE_COLS):
        lo = n * MERGE_COLS
        cols = slice(lo, lo + MERGE_COLS)

        def gate(i):
            gcols = slice(i * D_MODEL + lo, i * D_MODEL + lo + MERGE_COLS)
            return _sigmoid(lax.dot_general(h, wgate_ref[gcols, :], NT_DIMS, preferred_element_type=F32)
                            + bgate_ref[:, gcols])

        lin = _dot(ys, wglu_ref[:, cols]) + bglu_ref[:, cols]
        gcols = slice(D_MODEL + lo, D_MODEL + lo + MERGE_COLS)
        y_s = lin * _sigmoid(_dot(ys, wglu_ref[:, gcols]) + bglu_ref[:, gcols])
        mixed = gate(0) * _dot(hm, wm_ref[:, cols]) + gate(1) * _dot(yf, wf_ref[:, cols]) + gate(2) * y_s
        mixed_sc[:, cols] = mixed.astype(BF16)
    o_ref[...] = x_ref[...] + _rms(_dot(mixed_sc[...], wout_ref[...]), gpost_ref[...])


def _merge(x2, h, hf, hb, zo, yf, ys_lo, ys_hi, params, l, tile):
    t = x2.shape[0]
    rows = lambda width: pl.BlockSpec((tile, width), lambda i: (i, 0))
    return pl.pallas_call(
        _merge_kernel,
        grid=(t // tile,),
        in_specs=[rows(D_MODEL), rows(D_MODEL), rows(M_MLSTM), rows(M_MLSTM), rows(M_MLSTM), rows(M_FOURIER),
                  rows(LANES), rows(LANES)]
                 + [_layer_spec(a, l) for a in params],
        out_specs=rows(D_MODEL),
        out_shape=jax.ShapeDtypeStruct((t, D_MODEL), F32),
        scratch_shapes=[pltpu.VMEM((tile, D_MODEL), BF16)],
        compiler_params=_params("parallel"),
        name="merge",
    )(x2, h, hf, hb, zo, yf, ys_lo, ys_hi, *params)


def _ffn_kernel(x_ref, gpre_ref, w1_ref, w2_ref, gpost_ref, o_ref, *, n_split):
    x = x_ref[...]
    h = _rms(x, gpre_ref[...]).astype(BF16)
    width = D_FF // n_split
    f = None
    for j in range(n_split):
        a = jnp.maximum(_dot(h, w1_ref[:, j * width:(j + 1) * width]), 0.0)
        part = _dot((a * a).astype(BF16), w2_ref[j * width:(j + 1) * width, :])
        f = part if f is None else f + part
    o_ref[...] = x + _rms(f, gpost_ref[...])


def _ffn(x2, params, l, tile):
    t = x2.shape[0]
    rows = pl.BlockSpec((tile, D_MODEL), lambda i: (i, 0))
    return pl.pallas_call(
        functools.partial(_ffn_kernel, n_split=4),
        grid=(t // tile,),
        in_specs=[rows] + [_layer_spec(a, l) for a in params],
        out_specs=rows,
        out_shape=jax.ShapeDtypeStruct((t, D_MODEL), F32),
        compiler_params=_params("parallel"),
        name="ffn",
    )(x2, *params)


def kernel(x, g_mix_pre, g_mix_post, g_ffn_pre, g_ffn_post, w_in, b_in, conv_w, conv_b, mlstm_norm_g, w_up_mlstm, w_up_fourier, s5_lam_re, s5_lam_im, s5_log_dt, s5_b_re, s5_b_im, s5_c_re, s5_c_im, s5_d, w_glu, b_glu, w_out, w_ffn1, w_ffn2):
    bsz, seq, _ = x.shape
    depth = w_in.shape[0]
    t = bsz * seq
    tile = min(1024, t)
    bf = lambda a: a.astype(BF16)
    row = lambda a: a[:, None, :]
    w_t = bf(jnp.swapaxes(w_in, 1, 2))
    inproj_params = (row(g_mix_pre), w_t[:, :OFF_GATE], row(b_in[:, :OFF_GATE]), conv_w, row(conv_b))
    merge_params = (w_t[:, OFF_GATE:], row(b_in[:, OFF_GATE:]), row(mlstm_norm_g),
                    bf(w_up_mlstm), bf(w_up_fourier), bf(w_glu), row(b_glu), bf(w_out), row(g_mix_post))
    ffn_params = (row(g_ffn_pre), bf(w_ffn1), bf(w_ffn2), row(g_ffn_post))
    s5_prep = _s5_prep(s5_lam_re, s5_lam_im, s5_log_dt, s5_b_re, s5_b_im, s5_c_re, s5_c_im, s5_d)
    x2 = x.reshape(t, D_MODEL)
    for l in range(depth):
        q, kt, zv, zo, zf, zs, zgt, h = _inproj(x2, inproj_params, l, tile, seq)
        grow, gcol = _mlstm_gates(zgt, tile)
        hf, hb = _mlstm(q, kt, zv, zgt, grow, gcol, bsz, seq)
        yf = _fourier(zf, bsz, seq)
        ys_lo, ys_hi = _s5(zs, s5_prep, l, bsz, seq)
        x2 = _merge(x2, h, hf, hb, zo, yf, ys_lo, ys_hi, merge_params, l, tile)
        x2 = _ffn(x2, ffn_params, l, tile)
    return x2.reshape(bsz, seq, D_MODEL)
```

```python
import functools
import itertools
import math

import numpy as np
import jax
import jax.numpy as jnp
from jax import lax
from jax.experimental import pallas as pl
from jax.experimental.pallas import tpu as pltpu

F32 = jnp.float32
BF16 = jnp.bfloat16

LANES = 128
D_MODEL = 1024
M_MLSTM = 512
HEADS = 4
HEAD_DIM = 128
NT_DIMS = (((1,), (1,)), ((), ()))
CHUNK = 128
MLSTM_SUB = 8
CONV_WIDTH = 5
CONV_COLS = 256
CONV_HALO = 8
M_FOURIER = 256
FOURIER_GROUP_DIM = 64
M_S5 = 256
S5_GROUP = 16
S5_GROUPS = 16
S5_STATE = 64
S5_CHUNK = 32
S5_TILE = 4096
N_BRANCHES = 3
MERGE_COLS = 256
D_FF = 4 * D_MODEL
EPS = 1e-6

OFF_Q = 0
OFF_V = 2 * M_MLSTM
OFF_O = 3 * M_MLSTM
OFF_IG = 4 * M_MLSTM
OFF_FOURIER = OFF_IG + 4 * HEADS
OFF_S5 = OFF_FOURIER + M_FOURIER
OFF_GATE = OFF_S5 + M_S5

DFT_N2 = 64
DFT_STEP = 8

VMEM_LIMIT = 56 * 1024 * 1024


def _params(*sem):
    return pltpu.CompilerParams(dimension_semantics=sem, vmem_limit_bytes=VMEM_LIMIT)


def _layer_spec(a, l, rows=None):
    shape = a.shape[1:] if rows is None else (rows,) + a.shape[2:]
    return pl.BlockSpec((None,) + shape, lambda *_: (l,) + (0,) * (a.ndim - 1), pipeline_mode=pl.Buffered(1))


def _rms(x, g):
    return x * lax.rsqrt(jnp.mean(x * x, axis=-1, keepdims=True) + EPS) * g


def _sigmoid(x):
    return 1.0 / (1.0 + jnp.exp(-x))


def _log_sigmoid(x):
    return jnp.minimum(x, 0.0) - jnp.log1p(jnp.exp(-jnp.abs(x)))


def _dot(a, b):
    return jnp.dot(a, b, preferred_element_type=F32)


def _split3(a):
    hi = a.astype(BF16)
    rest = a - hi.astype(F32)
    mid = rest.astype(BF16)
    return hi, mid, (rest - mid.astype(F32)).astype(BF16)


def _dot_sel(a, sel):
    hi, mid, lo = _split3(a)
    return _dot(hi, sel) + _dot(mid, sel) + _dot(lo, sel)


def _inproj_kernel(x_ref, xp_ref, xn_ref, g_ref, wt_ref, b_ref, cw_ref, cb_ref,
                   q_ref, kt_ref, zv_ref, zo_ref, zf_ref, zs_ref, zgt_ref, h_ref, *, tiles_per_seq):
    i = pl.program_id(0)
    tile = x_ref.shape[0]
    pad = CONV_WIDTH // 2
    body = slice(CONV_HALO, CONV_HALO + tile)
    h = _rms(jnp.concatenate([xp_ref[...], x_ref[...], xn_ref[...]], axis=0), g_ref[...]).astype(BF16)
    h_ref[...] = h[body]
    row = lax.broadcasted_iota(jnp.int32, (tile + 2 * CONV_HALO, 1), 0)
    first = (i % tiles_per_seq) == 0
    last = (i % tiles_per_seq) == tiles_per_seq - 1
    outside = (first & (row < CONV_HALO)) | (last & (row >= CONV_HALO + tile))

    def proj(lo, hi):
        return lax.dot_general(h, wt_ref[lo:hi, :], NT_DIMS, preferred_element_type=F32) + b_ref[:, lo:hi]

    for n in range(2 * M_MLSTM // CONV_COLS):
        lo = n * CONV_COLS
        cols = slice(lo, lo + CONV_COLS)
        ext = jnp.where(outside, 0.0, proj(OFF_Q + lo, OFF_Q + lo + CONV_COLS))
        acc = cb_ref[:, cols]
        for j in range(CONV_WIDTH):
            shifted = ext if j == pad else pltpu.roll(ext, (pad - j) % ext.shape[0], 0)
            acc = acc + cw_ref[j:j + 1, cols] * shifted[body]
        half = 0.5 * acc
        qk = half + half * jnp.tanh(half)
        if lo < M_MLSTM:
            q_ref[:, cols] = (qk * (HEAD_DIM ** -0.5)).astype(BF16)
        else:
            kt_ref[lo - M_MLSTM:lo - M_MLSTM + CONV_COLS, :] = qk.T.astype(BF16)

    zv_ref[...] = proj(OFF_V, OFF_O)[body].astype(BF16)
    zo_ref[...] = proj(OFF_O, OFF_IG)[body].astype(BF16)
    zf_ref[...] = proj(OFF_FOURIER, OFF_S5)[body]
    zs_ref[...] = proj(OFF_S5, OFF_GATE)[body]
    zgt_ref[...] = proj(OFF_IG, OFF_FOURIER)[body].T


def _inproj(x2, params, l, tile, seq):
    t = x2.shape[0]
    n_gate = 4 * HEADS
    hpt = tile // CONV_HALO
    n_halo = t // CONV_HALO
    rows = lambda width: pl.BlockSpec((tile, width), lambda i: (i, 0))
    cols = lambda height: pl.BlockSpec((height, tile), lambda i: (0, i))
    return pl.pallas_call(
        functools.partial(_inproj_kernel, tiles_per_seq=seq // tile),
        grid=(t // tile,),
        in_specs=[rows(D_MODEL),
                  pl.BlockSpec((CONV_HALO, D_MODEL), lambda i: (jnp.maximum(i * hpt - 1, 0), 0)),
                  pl.BlockSpec((CONV_HALO, D_MODEL), lambda i: (jnp.minimum((i + 1) * hpt, n_halo - 1), 0))]
                 + [_layer_spec(a, l, OFF_GATE if k == 1 else None) for k, a in enumerate(params)],
        out_specs=[rows(M_MLSTM), cols(M_MLSTM), rows(M_MLSTM), rows(M_MLSTM), rows(M_FOURIER), rows(M_S5),
                   cols(n_gate), rows(D_MODEL)],
        out_shape=[jax.ShapeDtypeStruct((t, M_MLSTM), BF16), jax.ShapeDtypeStruct((M_MLSTM, t), BF16),
                   jax.ShapeDtypeStruct((t, M_MLSTM), BF16),
                   jax.ShapeDtypeStruct((t, M_MLSTM), BF16), jax.ShapeDtypeStruct((t, M_FOURIER), F32),
                   jax.ShapeDtypeStruct((t, M_S5), F32), jax.ShapeDtypeStruct((n_gate, t), F32),
                   jax.ShapeDtypeStruct((t, D_MODEL), BF16)],
        compiler_params=_params("parallel"),
        name="inproj",
    )(x2, x2, x2, *params)


def _mlstm_gates_kernel(zgt_ref, grow_ref, gcol_ref):
    ig_rows = zgt_ref[:2 * HEADS, :]
    lf_rows = _log_sigmoid(zgt_ref[2 * HEADS:, :])
    r = lax.broadcasted_iota(jnp.int32, (CHUNK, CHUNK), 0)
    c = lax.broadcasted_iota(jnp.int32, (CHUNK, CHUNK), 1)
    tri = jnp.concatenate([jnp.where(r <= c, 1.0, 0.0), jnp.where(r >= c, 1.0, 0.0)], axis=1).astype(BF16)
    fwd_row = lax.broadcasted_iota(jnp.int32, (2 * HEADS, CHUNK), 0) < HEADS
    b_parts, w_parts, amax_parts = [], [], []
    for s in range(zgt_ref.shape[1] // CHUNK):
        span = slice(s * CHUNK, (s + 1) * CHUNK)
        both = _dot_sel(lf_rows[:, span], tri)
        b = jnp.where(fwd_row, both[:, :CHUNK], both[:, CHUNK:])
        g_tot = jnp.where(fwd_row, b[:, CHUNK - 1:], b[:, :1])
        a = g_tot - b + ig_rows[:, span]
        a_max = jnp.max(a, axis=1, keepdims=True)
        b_parts.append(b)
        w_parts.append(jnp.exp(a - a_max))
        amax_parts.append(jnp.broadcast_to(a_max, a.shape))
    b_rows = jnp.concatenate(b_parts, axis=1)
    grow_ref[:2 * HEADS, :] = b_rows
    grow_ref[2 * HEADS:, :] = jnp.concatenate(amax_parts, axis=1)
    gcol_ref[:, :2 * HEADS] = b_rows.T
    gcol_ref[:, 2 * HEADS:] = jnp.concatenate(w_parts, axis=1).T


def _mlstm_gates(zgt, tile):
    n_gate, t = zgt.shape
    cols = pl.BlockSpec((n_gate, tile), lambda i: (0, i))
    return pl.pallas_call(
        _mlstm_gates_kernel,
        grid=(t // tile,),
        in_specs=[cols],
        out_specs=[cols, pl.BlockSpec((tile, n_gate), lambda i: (i, 0))],
        out_shape=[jax.ShapeDtypeStruct((n_gate, t), F32), jax.ShapeDtypeStruct((t, n_gate), F32)],
        compiler_params=_params("parallel"),
        name="mlstm_gates",
    )(zgt)


def _mlstm_kernel(q_f, kt_f, v_f, zgt_f, grow_f, gcol_f, q_b, kt_b, v_b, zgt_b, grow_b, gcol_b, hf_ref, hb_ref,
                  state_sc, m_sc):
    c = pl.program_id(1)
    L = CHUNK

    @pl.when(c == 0)
    def _():
        state_sc[...] = jnp.zeros_like(state_sc)
        m_sc[...] = jnp.zeros_like(m_sc)

    row = lax.broadcasted_iota(jnp.int32, (L, L), 0)
    col = lax.broadcasted_iota(jnp.int32, (L, L), 1)
    lower = row >= col
    upper = row <= col
    ones_v = jnp.ones((L, HEAD_DIM), F32)

    streams = ((0, q_f, kt_f, v_f, zgt_f, grow_f, gcol_f, hf_ref), (1, q_b, kt_b, v_b, zgt_b, grow_b, gcol_b, hb_ref))
    for sub, (d, q_ref, kt_ref, v_ref, zgt_ref, grow_ref, gcol_ref, out_ref) in itertools.product(range(MLSTM_SUB),
                                                                                                   streams):
        r0 = (sub if d == 0 else MLSTM_SUB - 1 - sub) * L
        rows = slice(r0, r0 + L)
        ig_rows = zgt_ref[:2 * HEADS, rows]
        b_rows = grow_ref[:2 * HEADS, rows]
        amax_rows = grow_ref[2 * HEADS:, rows]
        b_cols = gcol_ref[rows, :2 * HEADS]
        w_cols = gcol_ref[rows, 2 * HEADS:]
        mask = lower if d == 0 else upper

        for hd in range(HEADS):
            k_idx = d * HEADS + hd
            lo = hd * HEAD_DIM
            q = q_ref[rows, lo:lo + HEAD_DIM]
            kt = kt_ref[lo:lo + HEAD_DIM, rows]
            v = v_ref[rows, lo:lo + HEAD_DIM]
            b_c = jnp.broadcast_to(b_cols[:, k_idx:k_idx + 1], (L, L))
            w_c = jnp.broadcast_to(w_cols[:, k_idx:k_idx + 1], (L, L))
            b_r = b_rows[k_idx:k_idx + 1, :]
            ig_r = ig_rows[k_idx:k_idx + 1, :]
            g_tot = b_c[L - 1:L, :] if d == 0 else b_c[0:1, :]
            a_max = amax_rows[k_idx:k_idx + 1, :]
            m_prev = m_sc[k_idx:k_idx + 1, :]

            d_log = jnp.where(mask, b_c - b_r + ig_r, -1e30)
            inter_log = b_c + m_prev
            m_t = jnp.maximum(inter_log, jnp.max(d_log, axis=1, keepdims=True))
            scores = _dot(q, kt) * jnp.exp(d_log - m_t)
            inter_w = jnp.exp(inter_log - m_t)
            st = state_sc[k_idx]
            v_ext = jnp.concatenate([v, ones_v.astype(BF16)], axis=1)
            res = _dot(scores.astype(BF16), v_ext) + _dot((inter_w * q.astype(F32)).astype(BF16), st.astype(BF16))
            num = res[:, :HEAD_DIM]
            den = res[:, HEAD_DIM:]
            out_ref[rows, lo:lo + HEAD_DIM] = num / jnp.maximum(jnp.abs(den), jnp.exp(-m_t))

            vw = jnp.concatenate([v.astype(F32) * w_c, w_c], axis=1).astype(BF16)
            st_loc = _dot(kt, vw)
            m_new = jnp.maximum(g_tot + m_prev, a_max)
            s_old = jnp.exp(g_tot + m_prev - m_new)
            s_new = jnp.exp(a_max - m_new)
            s_old2 = jnp.concatenate([s_old, s_old], axis=1)
            s_new2 = jnp.concatenate([s_new, s_new], axis=1)
            state_sc[k_idx] = s_old2 * st + s_new2 * st_loc
            m_sc[k_idx:k_idx + 1, :] = m_new


def _mlstm(q, kt, zv, zgt, grow, gcol, bsz, seq):
    t = bsz * seq
    blk = CHUNK * MLSTM_SUB
    nc = seq // blk
    n_gate = 4 * HEADS

    def fwd(b, c):
        return b * nc + c

    def bwd(b, c):
        return b * nc + nc - 1 - c

    def specs(pos):
        return [pl.BlockSpec((blk, M_MLSTM), lambda b, c: (pos(b, c), 0)),
                pl.BlockSpec((M_MLSTM, blk), lambda b, c: (0, pos(b, c))),
                pl.BlockSpec((blk, M_MLSTM), lambda b, c: (pos(b, c), 0)),
                pl.BlockSpec((n_gate, blk), lambda b, c: (0, pos(b, c))),
                pl.BlockSpec((n_gate, blk), lambda b, c: (0, pos(b, c))),
                pl.BlockSpec((blk, n_gate), lambda b, c: (pos(b, c), 0))]

    out_shape = jax.ShapeDtypeStruct((t, M_MLSTM), F32)
    return pl.pallas_call(
        _mlstm_kernel,
        grid=(bsz, nc),
        in_specs=specs(fwd) + specs(bwd),
        out_specs=[pl.BlockSpec((blk, M_MLSTM), lambda b, c: (fwd(b, c), 0)),
                   pl.BlockSpec((blk, M_MLSTM), lambda b, c: (bwd(b, c), 0))],
        out_shape=[out_shape, out_shape],
        scratch_shapes=[pltpu.VMEM((2 * HEADS, HEAD_DIM, 2 * HEAD_DIM), F32),
                        pltpu.VMEM((2 * HEADS, HEAD_DIM), F32)],
        compiler_params=_params("arbitrary", "arbitrary"),
        name="mlstm",
    )(q, kt, zv, zgt, grow, gcol, q, kt, zv, zgt, grow, gcol)


def _dft_constants(seq):
    n1, n2 = seq // DFT_N2, DFT_N2
    k1 = np.arange(n1)[:, None, None]
    s2 = np.arange(n2)[None, None, :]
    s1 = np.arange(n1)[None, :, None]
    ang = -2.0 * np.pi * ((k1 * (n2 * s1 + s2)) % seq) / seq
    stage1 = np.concatenate([np.cos(ang), np.sin(ang)], axis=0)
    stage1 = np.ascontiguousarray(stage1.transpose(2, 0, 1))
    a2 = 2.0 * np.pi * np.outer(np.arange(n2), np.arange(n2)) / n2
    c2, sn2 = np.cos(a2), np.sin(a2)
    stage2 = np.block([[c2, sn2], [-sn2, c2]])
    ag = 2.0 * np.pi * np.outer(np.arange(FOURIER_GROUP_DIM), np.arange(FOURIER_GROUP_DIM)) / FOURIER_GROUP_DIM
    scale = 1.0 / math.sqrt(seq * FOURIER_GROUP_DIM)
    eye = np.eye(LANES // FOURIER_GROUP_DIM)
    group = np.stack([np.kron(eye, np.cos(ag)), np.kron(eye, np.sin(ag))]) * scale
    as_bf16 = lambda a: jnp.asarray(a, F32).astype(BF16)
    return as_bf16(stage1), as_bf16(stage2), as_bf16(group)


def _fourier_kernel(u_ref, w1_ref, w2_ref, wg_ref, y_ref, u_sc, b_sc, y_sc, *, n1):
    n2 = DFT_N2
    rows = n1 * DFT_STEP
    for sb in range(n2 // DFT_STEP):
        u_sc[sb % 2] = u_ref[:, sb * DFT_STEP:(sb + 1) * DFT_STEP, :].reshape(rows, LANES)
        for j in range(DFT_STEP):
            pick = pl.ds(j, n1, stride=DFT_STEP)
            res = _dot(w1_ref[sb * DFT_STEP + j], u_sc[sb % 2, pick, :].astype(BF16))
            b_sc[sb, 0, pick, :] = res[:n1]
            b_sc[sb, 1, pick, :] = res[n1:]
    for kb in range(n1 // DFT_STEP):
        def gather(part, j):
            r0 = (kb * DFT_STEP + j) * DFT_STEP
            return jnp.concatenate([b_sc[sb, part, r0:r0 + DFT_STEP, :] for sb in range(n2 // DFT_STEP)], axis=0)

        stacked = jnp.concatenate([jnp.concatenate([gather(0, j), gather(1, j)], axis=0) for j in range(DFT_STEP)],
                                  axis=1).astype(BF16)
        z = _dot(w2_ref[...], stacked).astype(BF16)
        zr = jnp.concatenate([z[:n2, j * LANES:(j + 1) * LANES] for j in range(DFT_STEP)], axis=0)
        zi = jnp.concatenate([z[n2:, j * LANES:(j + 1) * LANES] for j in range(DFT_STEP)], axis=0)
        y = _dot(zr, wg_ref[0]) + _dot(zi, wg_ref[1])
        for j in range(DFT_STEP):
            y_sc[kb % 2, pl.ds(j, n2, stride=DFT_STEP), :] = y[j * n2:(j + 1) * n2, :]
        y_ref[:, kb * DFT_STEP:(kb + 1) * DFT_STEP, :] = y_sc[kb % 2].reshape(n2, DFT_STEP, LANES)


def _fourier(zf, bsz, seq):
    n1, n2 = seq // DFT_N2, DFT_N2
    stage1, stage2, group = _dft_constants(seq)
    whole = lambda a: pl.BlockSpec(a.shape, lambda b, h: (0,) * a.ndim)
    y = pl.pallas_call(
        functools.partial(_fourier_kernel, n1=n1),
        grid=(bsz, M_FOURIER // LANES),
        in_specs=[pl.BlockSpec((None, n1, n2, LANES), lambda b, h: (b, 0, 0, h)),
                  whole(stage1), whole(stage2), whole(group)],
        out_specs=pl.BlockSpec((None, n2, n1, LANES), lambda b, h: (b, 0, 0, h)),
        out_shape=jax.ShapeDtypeStruct((bsz, n2, n1, M_FOURIER), F32),
        scratch_shapes=[pltpu.VMEM((2, n1 * DFT_STEP, LANES), F32),
                        pltpu.VMEM((n2 // DFT_STEP, 2, n1 * DFT_STEP, LANES), F32),
                        pltpu.VMEM((2, n2 * DFT_STEP, LANES), F32)],
        compiler_params=_params("parallel", "parallel"),
        name="fourier",
    )(zf.reshape(bsz, n1, n2, M_FOURIER), stage1, stage2, group)
    return y.reshape(bsz * seq, M_FOURIER)


def _s5_constants():
    L, G = S5_CHUNK, S5_GROUP
    lane = np.arange(L * G)
    expo = np.arange(LANES)[:, None]
    rep = (lane[None, :] % G == np.arange(G)[:, None])
    step = lane // G
    spread = np.stack([expo == (L - 1) - step, expo == step])
    return tuple(jnp.asarray(m, BF16) for m in (rep, spread))


def _s5_prep_kernel(lamc_re, lamc_im, lam4_re, lam4_im, ldt4_ref, b_re, b_im, c_re, c_im, c4_re, c4_im,
                    d_ref, rep_ref, spread_ref, toep_ref, bend_ref, cout_ref, laml_ref):
    L, P, G = S5_CHUNK, S5_STATE, S5_GROUP
    shift = G.bit_length() - 1
    nbits = L.bit_length()

    def cmul(ar, ai, br, bi):
        return ar * br - ai * bi, ar * bi + ai * br

    def lam_bar(lre, lim, dt):
        mag = jnp.exp(lre * dt)
        return mag * jnp.cos(lim * dt), mag * jnp.sin(lim * dt)

    def power_table(base_r, base_i, expo):
        tr = jnp.ones(expo.shape, F32)
        ti = jnp.zeros(expo.shape, F32)
        sr, si = base_r, base_i
        for bit in range(nbits):
            nr, ni = cmul(tr, ti, sr, si)
            has = ((expo >> bit) & 1) == 1
            tr, ti = jnp.where(has, nr, tr), jnp.where(has, ni, ti)
            sr, si = cmul(sr, si, sr, si)
        return tr, ti

    def spread_dot(xr, xi, sel):
        n = xr.shape[0]
        out = _dot(jnp.concatenate(_split3(xr)[:2] + _split3(xi)[:2], axis=0), sel)
        return out[:n] + out[n:2 * n], out[2 * n:3 * n] + out[3 * n:]

    def dot3(c, m):
        c_hi, c_mid, _ = _split3(c)
        m_hi, m_mid, _ = _split3(m)
        return _dot(c_hi, m_hi) + _dot(c_mid, m_hi) + _dot(c_hi, m_mid)

    rep = rep_ref[...]
    half = L * G
    lane_g = lax.broadcasted_iota(jnp.int32, (G, half), 1)
    row_g = lax.broadcasted_iota(jnp.int32, (G, half), 0)
    expo_tab = jnp.minimum(lax.broadcasted_iota(jnp.int32, (P, LANES), 1), L)

    lb4r, lb4i = lam_bar(lam4_re[0], lam4_im[0], jnp.exp(ldt4_ref[0]))

    def as_column(row, d):
        block = jnp.broadcast_to(row[:, 2 * d * P:2 * (d + 1) * P], (8, 2 * P))
        return block.T[:P, 0:1]

    taps = []
    for d in range(2):
        lr, li = lamc_re[d, 0], lamc_im[d, 0]
        lbr, lbi = as_column(lb4r, d), as_column(lb4i, d)
        den = lr * lr + li * li
        fr, fi = ((lbr - 1.0) * lr + lbi * li) / den, (lbi * lr - (lbr - 1.0) * li) / den
        bbr, bbi = cmul(fr, fi, b_re[d, 0], b_im[d, 0])
        btr, bti = spread_dot(bbr, bbi, rep)
        tab_r, tab_i = power_table(lbr, lbi, expo_tab)
        qr, qi = spread_dot(tab_r, tab_i, spread_ref[d])
        xr, xi = cmul(qr, qi, btr, bti)
        bend_ref[0, 2 * d] = xr.astype(BF16)
        bend_ref[0, 2 * d + 1] = xi.astype(BF16)
        taps.append(dot3(c_re[d, 0], xr) - dot3(c_im[d, 0], xi))

    d_tiled = _dot_sel(jnp.broadcast_to(d_ref[0], (G, G)), rep)
    skip = jnp.where(((lane_g & (G - 1)) == row_g) & ((lane_g >> shift) == L - 1), d_tiled, 0.0)
    pad = jnp.zeros((G, half), F32)
    gen = (jnp.concatenate([taps[0] + skip, pad], axis=1)
           + pltpu.roll(jnp.concatenate([taps[1], pad], axis=1), (L - 1) * G, 1))

    for t in range(L):
        a = (L - 1 - t) * G
        toep_ref[0, t * G:(t + 1) * G, :] = gen[:, a:a + L * G].astype(BF16)

    lane4 = lax.broadcasted_iota(jnp.int32, (L, 4 * P), 1)
    step4 = lax.broadcasted_iota(jnp.int32, (L, 4 * P), 0)
    pr, pi = power_table(lb4r, lb4i, jnp.where(lane4 < 2 * P, step4 + 1, L - step4))
    cr, ci = c4_re[0], c4_im[0]
    plane_bit = P.bit_length() - 1
    re_c = ((lax.broadcasted_iota(jnp.int32, (G, 4 * P), 1) >> plane_bit) & 1) == 0
    for t in range(L):
        re_part, im_part = cmul(cr, ci, pr[t:t + 1], pi[t:t + 1])
        cout_ref[0, t * G:(t + 1) * G, :] = jnp.where(re_c, re_part, -im_part).astype(BF16)
    re_1 = ((lax.broadcasted_iota(jnp.int32, (1, 4 * P), 1) >> plane_bit) & 1) == 0
    lr, li = power_table(lb4r, lb4i, jnp.full((1, 4 * P), L, jnp.int32))
    laml_ref[0] = jnp.where(re_1, lr, li)


def _s5_prep(lam_re, lam_im, log_dt, b_re, b_im, c_re, c_im, d_skip):
    ng, P, G, L = S5_GROUPS, S5_STATE, S5_GROUP, S5_CHUNK
    depth = lam_re.shape[0]
    tile4 = lambda a: jnp.concatenate([a[:, 0], a[:, 0], a[:, 1], a[:, 1]], axis=-1)
    ldt4 = jnp.repeat(tile4(log_dt[..., None]), P, axis=-1)[:, :, None, :]
    consts = _s5_constants()
    spec = lambda *tail: pl.BlockSpec((None, 2, 1) + tail, lambda l, g: (l, 0, g) + (0,) * len(tail))
    per_g = lambda *tail: pl.BlockSpec((None, 1) + tail, lambda l, g: (l, g) + (0,) * len(tail))
    whole = lambda a: pl.BlockSpec(a.shape, lambda l, g: (0,) * a.ndim)
    return pl.pallas_call(
        _s5_prep_kernel,
        grid=(depth, ng),
        in_specs=[spec(P, 1), spec(P, 1), per_g(1, 4 * P), per_g(1, 4 * P), per_g(1, 4 * P),
                  spec(P, G), spec(P, G), spec(G, P), spec(G, P), per_g(G, 4 * P), per_g(G, 4 * P), per_g(1, G)]
                 + [whole(m) for m in consts],
        out_specs=[per_g(L * G, L * G), per_g(4, P, L * G), per_g(L * G, 4 * P), per_g(1, 4 * P)],
        out_shape=[jax.ShapeDtypeStruct((depth, ng, L * G, L * G), BF16),
                   jax.ShapeDtypeStruct((depth, ng, 4, P, L * G), BF16),
                   jax.ShapeDtypeStruct((depth, ng, L * G, 4 * P), BF16),
                   jax.ShapeDtypeStruct((depth, ng, 1, 4 * P), F32)],
        compiler_params=_params("parallel", "parallel"),
        name="s5_prep",
    )(lam_re[..., None], lam_im[..., None],
      tile4(lam_re)[:, :, None, :], tile4(lam_im)[:, :, None, :], ldt4,
      b_re, b_im, c_re, c_im, tile4(c_re), tile4(c_im), d_skip[:, :, None, :], *consts)


def _s5_sums_kernel(lo_ref, hi_ref, bend_ref, ut_ref, et_ref, *, nchunk):
    L, P, G = S5_CHUNK, S5_STATE, S5_GROUP
    per_half = LANES // G
    for half, z_ref in enumerate((lo_ref, hi_ref)):
        for s in range(L):
            zt = z_ref[pl.ds(s, nchunk, stride=L), :].T.astype(BF16)
            for gl in range(per_half):
                ut_ref[half * per_half + gl, s * G:(s + 1) * G, :] = zt[gl * G:(gl + 1) * G, :]
    for g in range(S5_GROUPS):
        for plane in range(4):
            row = plane * S5_GROUPS * P + g * P
            et_ref[row:row + P, :] = _dot(bend_ref[g, plane], ut_ref[g])


def _s5_scan_kernel(et_ref, lam_ref, xt_ref, e_sc, x_sc, *, bsz, steps):
    width = S5_GROUPS * S5_STATE
    for plane in range(4):
        e_sc[plane] = et_ref[plane * width:(plane + 1) * width, :].T
    ar_f, ai_f, ar_b, ai_b = lam_ref[0], lam_ref[1], lam_ref[2], lam_ref[3]
    zero = jnp.zeros_like(ar_f)
    for b in range(bsz):
        def body(i, carry):
            xr, xi, yr, yi = carry
            rf = b * steps + i
            rb = b * steps + steps - 1 - i
            x_sc[0, pl.ds(rf, 1), :] = xr
            x_sc[1, pl.ds(rf, 1), :] = xi
            x_sc[2, pl.ds(rb, 1), :] = yr
            x_sc[3, pl.ds(rb, 1), :] = yi
            xr, xi = (ar_f * xr - ai_f * xi + e_sc[0, pl.ds(rf, 1), :],
                      ar_f * xi + ai_f * xr + e_sc[1, pl.ds(rf, 1), :])
            yr, yi = (ar_b * yr - ai_b * yi + e_sc[2, pl.ds(rb, 1), :],
                      ar_b * yi + ai_b * yr + e_sc[3, pl.ds(rb, 1), :])
            return xr, xi, yr, yi
        lax.fori_loop(0, steps, body, (zero, zero, zero, zero))
    for plane in range(4):
        xt_ref[plane * width:(plane + 1) * width, :] = x_sc[plane].T


def _gelu_tanh(x):
    return 0.5 * x * (1.0 + jnp.tanh(math.sqrt(2.0 / math.pi) * (x + 0.044715 * (x * x * x))))


def _s5_out_kernel(ut_ref, xt_ref, toep_ref, cout_ref, lo_ref, hi_ref, yt_sc, *, nchunk):
    L, P, G = S5_CHUNK, S5_STATE, S5_GROUP
    width = S5_GROUPS * P
    for g in range(S5_GROUPS):
        xg = jnp.concatenate([xt_ref[plane * width + g * P:plane * width + (g + 1) * P, :] for plane in range(4)],
                             axis=0).astype(BF16)
        yt = _gelu_tanh(_dot(toep_ref[g], ut_ref[g]) + _dot(cout_ref[g], xg))
        for t in range(L):
            yt_sc[t, g * G:(g + 1) * G, :] = yt[t * G:(t + 1) * G, :]
    for t in range(L):
        y = yt_sc[t].T
        lo_ref[pl.ds(t, nchunk, stride=L), :] = y[:, :LANES]
        hi_ref[pl.ds(t, nchunk, stride=L), :] = y[:, LANES:]


def _s5(zs, prep, l, bsz, seq):
    ng, P, G, L = S5_GROUPS, S5_STATE, S5_GROUP, S5_CHUNK
    toep, bend, cout, laml = prep
    depth = toep.shape[0]
    laml = laml.reshape(depth, ng, 4, P).transpose(0, 2, 1, 3).reshape(depth, 4, 1, ng * P)
    t = bsz * seq
    tile = min(S5_TILE, t)
    nct = tile // L
    nchunk = t // L
    ut_spec = pl.BlockSpec((ng, L * G, nct), lambda i: (0, 0, i))
    plane_spec = pl.BlockSpec((4 * ng * P, nct), lambda i: (0, i))
    plane_shape = jax.ShapeDtypeStruct((4 * ng * P, nchunk), F32)
    ut, et = pl.pallas_call(
        functools.partial(_s5_sums_kernel, nchunk=nct),
        grid=(t // tile,),
        in_specs=[pl.BlockSpec((tile, LANES), lambda i: (i, 0)), pl.BlockSpec((tile, LANES), lambda i: (i, 1)),
                  _layer_spec(bend, l)],
        out_specs=[ut_spec, plane_spec],
        out_shape=[jax.ShapeDtypeStruct((ng, L * G, nchunk), BF16), plane_shape],
        compiler_params=_params("parallel"),
        name="s5_chunk_sums",
    )(zs, zs, bend)
    whole_planes = pl.BlockSpec((4 * ng * P, nchunk), lambda i: (0, 0))
    xt = pl.pallas_call(
        functools.partial(_s5_scan_kernel, bsz=bsz, steps=seq // L),
        grid=(1,),
        in_specs=[whole_planes, _layer_spec(laml, l)],
        out_specs=whole_planes,
        out_shape=plane_shape,
        scratch_shapes=[pltpu.VMEM((4, nchunk, ng * P), F32), pltpu.VMEM((4, nchunk, ng * P), F32)],
        compiler_params=_params("arbitrary"),
        name="s5_state_scan",
    )(et, laml)
    half_spec = pl.BlockSpec((tile, LANES), lambda i: (i, 0))
    half_shape = jax.ShapeDtypeStruct((t, LANES), F32)
    return pl.pallas_call(
        functools.partial(_s5_out_kernel, nchunk=nct),
        grid=(t // tile,),
        in_specs=[ut_spec, plane_spec, _layer_spec(toep, l), _layer_spec(cout, l)],
        out_specs=[half_spec, half_spec],
        out_shape=[half_shape, half_shape],
        scratch_shapes=[pltpu.VMEM((L, M_S5, nct), F32)],
        compiler_params=_params("parallel"),
        name="s5_outputs",
    )(ut, xt, toep, cout)


def _merge_kernel(x_ref, h_ref, hf_ref, hb_ref, zo_ref, yf_ref, ys_lo_ref, ys_hi_ref, wgate_ref, bgate_ref,
                  ng_ref, wm_ref, wf_ref, wglu_ref, bglu_ref, wout_ref, gpost_ref, o_ref, mixed_sc):
    h = h_ref[...]
    hs = hf_ref[...] + hb_ref[...]
    parts = []
    for hd in range(HEADS):
        blk = hs[:, hd * HEAD_DIM:(hd + 1) * HEAD_DIM]
        mu = jnp.mean(blk, axis=-1, keepdims=True)
        cen = blk - mu
        var = jnp.mean(cen * cen, axis=-1, keepdims=True)
        parts.append(cen * lax.rsqrt(var + EPS))
    hm = (jnp.concatenate(parts, axis=1) * ng_ref[...] * _sigmoid(zo_ref[...].astype(F32))).astype(BF16)
    yf = yf_ref[...].astype(BF16)
    ys = jnp.concatenate([ys_lo_ref[...].astype(BF16), ys_hi_ref[...].astype(BF16)], axis=1)
    for n in range(D_MODEL // MERGE_COLS):
        lo = n * MERGE_COLS
        cols = slice(lo, lo + MERGE_COLS)

        def gate(i):
            gcols = slice(i * D_MODEL + lo, i * D_MODEL + lo + MERGE_COLS)
            return _sigmoid(lax.dot_general(h, wgate_ref[gcols, :], NT_DIMS, preferred_element_type=F32)
                            + bgate_ref[:, gcols])

        lin = _dot(ys, wglu_ref[:, cols]) + bglu_ref[:, cols]
        gcols = slice(D_MODEL + lo, D_MODEL + lo + MERGE_COLS)
        y_s = lin * _sigmoid(_dot(ys, wglu_ref[:, gcols]) + bglu_ref[:, gcols])
        mixed = gate(0) * _dot(hm, wm_ref[:, cols]) + gate(1) * _dot(yf, wf_ref[:, cols]) + gate(2) * y_s
        mixed_sc[:, cols] = mixed.astype(BF16)
    o_ref[...] = x_ref[...] + _rms(_dot(mixed_sc[...], wout_ref[...]), gpost_ref[...])


def _merge(x2, h, hf, hb, zo, yf, ys_lo, ys_hi, params, l, tile):
    t = x2.shape[0]
    rows = lambda width: pl.BlockSpec((tile, width), lambda i: (i, 0))
    return pl.pallas_call(
        _merge_kernel,
        grid=(t // tile,),
        in_specs=[rows(D_MODEL), rows(D_MODEL), rows(M_MLSTM), rows(M_MLSTM), rows(M_MLSTM), rows(M_FOURIER),
                  rows(LANES), rows(LANES)]
                 + [_layer_spec(a, l) for a in params],
        out_specs=rows(D_MODEL),
        out_shape=jax.ShapeDtypeStruct((t, D_MODEL), F32),
        scratch_shapes=[pltpu.VMEM((tile, D_MODEL), BF16)],
        compiler_params=_params("parallel"),
        name="merge",
    )(x2, h, hf, hb, zo, yf, ys_lo, ys_hi, *params)


def _ffn_kernel(x_ref, gpre_ref, w1_ref, w2_ref, gpost_ref, o_ref, *, n_split):
    x = x_ref[...]
    h = _rms(x, gpre_ref[...]).astype(BF16)
    width = D_FF // n_split
    f = None
    for j in range(n_split):
        a = jnp.maximum(_dot(h, w1_ref[:, j * width:(j + 1) * width]), 0.0)
        part = _dot((a * a).astype(BF16), w2_ref[j * width:(j + 1) * width, :])
        f = part if f is None else f + part
    o_ref[...] = x + _rms(f, gpost_ref[...])


def _ffn(x2, params, l, tile):
    t = x2.shape[0]
    rows = pl.BlockSpec((tile, D_MODEL), lambda i: (i, 0))
    return pl.pallas_call(
        functools.partial(_ffn_kernel, n_split=4),
        grid=(t // tile,),
        in_specs=[rows] + [_layer_spec(a, l) for a in params],
        out_specs=rows,
        out_shape=jax.ShapeDtypeStruct((t, D_MODEL), F32),
        compiler_params=_params("parallel"),
        name="ffn",
    )(x2, *params)


def kernel(x, g_mix_pre, g_mix_post, g_ffn_pre, g_ffn_post, w_in, b_in, conv_w, conv_b, mlstm_norm_g, w_up_mlstm, w_up_fourier, s5_lam_re, s5_lam_im, s5_log_dt, s5_b_re, s5_b_im, s5_c_re, s5_c_im, s5_d, w_glu, b_glu, w_out, w_ffn1, w_ffn2):
    bsz, seq, _ = x.shape
    depth = w_in.shape[0]
    t = bsz * seq
    tile = min(1024, t)
    bf = lambda a: a.astype(BF16)
    row = lambda a: a[:, None, :]
    w_t = bf(jnp.swapaxes(w_in, 1, 2))
    inproj_params = (row(g_mix_pre), w_t, row(b_in), conv_w, row(conv_b))
    merge_params = (w_t[:, OFF_GATE:], row(b_in[:, OFF_GATE:]), row(mlstm_norm_g),
                    bf(w_up_mlstm), bf(w_up_fourier), bf(w_glu), row(b_glu), bf(w_out), row(g_mix_post))
    ffn_params = (row(g_ffn_pre), bf(w_ffn1), bf(w_ffn2), row(g_ffn_post))
    s5_prep = _s5_prep(s5_lam_re, s5_lam_im, s5_log_dt, s5_b_re, s5_b_im, s5_c_re, s5_c_im, s5_d)
    x2 = x.reshape(t, D_MODEL)
    for l in range(depth):
        q, kt, zv, zo, zf, zs, zgt, h = _inproj(x2, inproj_params, l, tile, seq)
        grow, gcol = _mlstm_gates(zgt, tile)
        hf, hb = _mlstm(q, kt, zv, zgt, grow, gcol, bsz, seq)
        yf = _fourier(zf, bsz, seq)
        ys_lo, ys_hi = _s5(zs, s5_prep, l, bsz, seq)
        x2 = _merge(x2, h, hf, hb, zo, yf, ys_lo, ys_hi, merge_params, l, tile)
        x2 = _ffn(x2, ffn_params, l, tile)
    return x2.reshape(bsz, seq, D_MODEL)
```
